```python
import jax, jax.numpy as jnp
from jax import lax
import numpy as np

D_MODEL = 1024
BATCH = 16
SEQ = 4096
DEPTH = 2

CHUNK = 64
MEM_LEN = 256
RET_HEADS = 8
RET_DK = 64
RET_DV = 128
RET_QK = RET_HEADS * RET_DK
RET_V = RET_HEADS * RET_DV
ROPE_BASE = 10000.0
ATT_HEADS = 8
ATT_DH = 64
ATT_W = ATT_HEADS * ATT_DH
LEFT_CHUNKS = 8
MAX_REL = 256
MEM_HEADS = 4
MEM_DH = D_MODEL // MEM_HEADS
N_GROUPS = 4
EXPERTS_PER_GROUP = 8
N_EXPERTS = N_GROUPS * EXPERTS_PER_GROUP
TOP_K = 2
D_EXPERT = 512
ROUTE_BLOCK = 256
LN_EPS = 1e-5
DEEPNORM_ALPHA = (2.0 * DEPTH) ** 0.25
DEEPNORM_BETA = (8.0 * DEPTH) ** -0.25
IN_WIDTHS = (RET_QK, RET_QK, RET_V, RET_V, ATT_W, ATT_W, ATT_W, D_MODEL, D_MODEL)
IN_TOTAL = sum(IN_WIDTHS)

kernel_name = "hybrid_retention_chunkattn_hmoe_deepnorm"


def layer_norm(x, g, b):
    xf = x.astype(jnp.float32)
    mu = jnp.mean(xf, axis=-1, keepdims=True)
    var = jnp.mean(jnp.square(xf - mu), axis=-1, keepdims=True)
    return ((xf - mu) * lax.rsqrt(var + LN_EPS) * g.astype(jnp.float32) + b.astype(jnp.float32)).astype(x.dtype)


def head_group_norm(y):
    yf = y.astype(jnp.float32)
    mu = jnp.mean(yf, axis=-1, keepdims=True)
    var = jnp.mean(jnp.square(yf - mu), axis=-1, keepdims=True)
    return ((yf - mu) * lax.rsqrt(var + LN_EPS)).astype(y.dtype)


def rotary(t, pos):
    half = t.shape[-1] // 2
    inv_freq = 1.0 / (ROPE_BASE ** jnp.linspace(0.0, 1.0, half, dtype=jnp.float32))
    ang = pos.astype(jnp.float32)[:, :, None, None] * inv_freq
    cos, sin = jnp.cos(ang), jnp.sin(ang)
    t1 = t[..., :half].astype(jnp.float32)
    t2 = t[..., half:].astype(jnp.float32)
    return jnp.concatenate([t1 * cos - t2 * sin, t1 * sin + t2 * cos], axis=-1).astype(t.dtype)


def retention(q, k, v, pos):
    B, S = q.shape[:2]
    nc = S // CHUNK
    dt = q.dtype
    q = rotary(q, pos)
    k = rotary(k, pos) * (RET_DK ** -0.5)
    log_g = jnp.log1p(-jnp.exp2(-5.0 - jnp.arange(RET_HEADS, dtype=jnp.float32)))
    i = jnp.arange(CHUNK, dtype=jnp.float32)
    intra_decay = jnp.exp(log_g[:, None, None] * jnp.abs(i[:, None] - i[None, :])).astype(dt)
    q_decay = jnp.exp(log_g[:, None] * (i + 1.0)).astype(dt)
    k_decay = jnp.exp(log_g[:, None] * (CHUNK - 1.0 - i)).astype(dt)
    chunk_decay = jnp.exp(log_g * CHUNK).astype(dt)
    qc = q.reshape(B, nc, CHUNK, RET_HEADS, RET_DK)
    kc = k.reshape(B, nc, CHUNK, RET_HEADS, RET_DK)
    vc = v.reshape(B, nc, CHUNK, RET_HEADS, RET_DV)
    scores = jnp.einsum('bnihd,bnjhd->bnhij', qc, kc) * intra_decay
    intra = jnp.einsum('bnhij,bnjhe->bnihe', scores, vc)

    def step(state, inp):
        q_n, k_n, v_n = inp
        cross = jnp.einsum('bihd,hi,bhde->bihe', q_n, q_decay, state)
        state = state * chunk_decay[None, :, None, None] + jnp.einsum('bjhd,hj,bjhe->bhde', k_n, k_decay, v_n)
        return state, cross

    state0 = jnp.zeros((B, RET_HEADS, RET_DK, RET_DV), dt)
    _, cross = lax.scan(step, state0, (qc.swapaxes(0, 1), kc.swapaxes(0, 1), vc.swapaxes(0, 1)))
    out = intra + cross.swapaxes(0, 1)
    return out.reshape(B, S, RET_HEADS, RET_DV)


def chunked_attention(q, k, v, rel_bias):
    B, S, H, dh = q.shape
    nc = S // CHUNK
    pad = LEFT_CHUNKS * CHUNK
    band = pad + CHUNK
    kp = jnp.pad(k, ((0, 0), (pad, 0), (0, 0), (0, 0)))
    vp = jnp.pad(v, ((0, 0), (pad, 0), (0, 0), (0, 0)))
    qi = jnp.arange(CHUNK)
    kj = jnp.arange(band)
    dist = (pad + qi)[:, None] - kj[None, :]
    bias = rel_bias[:, jnp.clip(dist, -MAX_REL, MAX_REL) + MAX_REL].astype(jnp.float32)
    scale = dh ** -0.5

    def one_chunk(c):
        start = c * CHUNK
        q_c = lax.dynamic_slice_in_dim(q, start, CHUNK, axis=1)
        k_b = lax.dynamic_slice_in_dim(kp, start, band, axis=1)
        v_b = lax.dynamic_slice_in_dim(vp, start, band, axis=1)
        s = jnp.einsum('bihd,bjhd->bhij', q_c, k_b).astype(jnp.float32) * scale + bias
        valid = (start - pad + kj) >= 0
        s = jnp.where(valid[None, None, None, :], s, -1e30)
        p = jax.nn.softmax(s, axis=-1).astype(v.dtype)
        return jnp.einsum('bhij,bjhd->bihd', p, v_b)

    out = lax.map(one_chunk, jnp.arange(nc))
    return out.swapaxes(0, 1).reshape(B, S, H * dh)


def hybrid_mixer(x, pos, w_in, rel_bias, w_proj_ret, w_proj_att, w_out):
    B, S, _ = x.shape
    z = x @ w_in
    splits = np.cumsum(IN_WIDTHS)[:-1].tolist()
    q_r, k_r, v_r, g_r, q_a, k_a, v_a, gate_r, gate_a = jnp.split(z, splits, axis=-1)
    ret = retention(q_r.reshape(B, S, RET_HEADS, RET_DK), k_r.reshape(B, S, RET_HEADS, RET_DK),
                    v_r.reshape(B, S, RET_HEADS, RET_DV), pos)
    y_r = jax.nn.silu(g_r) * head_group_norm(ret).reshape(B, S, RET_V)
    y_a = chunked_attention(q_a.reshape(B, S, ATT_HEADS, ATT_DH), k_a.reshape(B, S, ATT_HEADS, ATT_DH),
                            v_a.reshape(B, S, ATT_HEADS, ATT_DH), rel_bias)
    merged = jax.nn.sigmoid(gate_r) * (y_r @ w_proj_ret) + jax.nn.sigmoid(gate_a) * (y_a @ w_proj_att)
    return merged @ w_out


def memory_attention(x, mem, w_q, w_kv, w_o):
    B, S, _ = x.shape
    q = (x @ w_q).reshape(B, S, MEM_HEADS, MEM_DH)
    k, v = jnp.split(mem @ w_kv, 2, axis=-1)
    k = k.reshape(B, -1, MEM_HEADS, MEM_DH)
    v = v.reshape(B, -1, MEM_HEADS, MEM_DH)
    s = jnp.einsum('bshd,bmhd->bhsm', q, k).astype(jnp.float32) * (MEM_DH ** -0.5)
    p = jax.nn.softmax(s, axis=-1).astype(v.dtype)
    o = jnp.einsum('bhsm,bmhd->bshd', p, v).reshape(B, S, D_MODEL)
    return o @ w_o


def hierarchical_moe(x, w_group, b_group, w_route, b_route, w_gate, w_up, w_down):
    B, S, D = x.shape
    T = B * S
    xt = x.reshape(T, D)
    g_prob = jax.nn.softmax((xt @ w_group).astype(jnp.float32) + b_group.astype(jnp.float32), axis=-1)
    g_w, g_idx = lax.top_k(g_prob, 1)
    g_w, g_idx = g_w[:, 0], g_idx[:, 0]
    e_all = (xt @ w_route).astype(jnp.float32).reshape(T, N_GROUPS, EXPERTS_PER_GROUP) + b_route.astype(jnp.float32)
    e_logits = jnp.take_along_axis(e_all, g_idx[:, None, None], axis=1)[:, 0]
    e_w, e_idx = lax.top_k(jax.nn.softmax(e_logits, axis=-1), TOP_K)
    e_w = e_w / jnp.sum(e_w, axis=-1, keepdims=True)
    weights = g_w[:, None] * e_w
    expert = g_idx[:, None] * EXPERTS_PER_GROUP + e_idx

    A = T * TOP_K
    flat_e = expert.reshape(A).astype(jnp.int32)
    flat_tok = jnp.repeat(jnp.arange(T, dtype=jnp.int32), TOP_K)
    flat_w = weights.reshape(A)
    order = jnp.argsort(flat_e)
    sorted_e = flat_e[order]
    counts = jnp.zeros((N_EXPERTS,), jnp.int32).at[flat_e].add(1)
    padded = ((counts + ROUTE_BLOCK - 1) // ROUTE_BLOCK) * ROUTE_BLOCK
    start = jnp.cumsum(counts) - counts
    pend = jnp.cumsum(padded)
    pstart = pend - padded
    dest = pstart[sorted_e] + (jnp.arange(A, dtype=jnp.int32) - start[sorted_e])
    n_blocks = -(-(A + N_EXPERTS * (ROUTE_BLOCK - 1)) // ROUTE_BLOCK)
    P = n_blocks * ROUTE_BLOCK
    buf_tok = jnp.full((P,), T, jnp.int32).at[dest].set(flat_tok[order])
    buf_w = jnp.zeros((P,), x.dtype).at[dest].set(flat_w[order].astype(x.dtype))
    block_expert = jnp.clip(jnp.searchsorted(pend, jnp.arange(n_blocks, dtype=jnp.int32) * ROUTE_BLOCK, side='right'),
                            0, N_EXPERTS - 1)
    xpad = jnp.concatenate([xt, jnp.zeros((1, D), x.dtype)], axis=0)
    xb = xpad[buf_tok].reshape(n_blocks, ROUTE_BLOCK, D)

    def run_block(args):
        xblk, e = args
        h = jax.nn.silu(xblk @ w_gate[e]) * (xblk @ w_up[e])
        return h @ w_down[e]

    yb = lax.map(run_block, (xb, block_expert)).reshape(P, D)
    y = jax.ops.segment_sum(yb * buf_w[:, None], buf_tok, num_segments=T + 1)[:T]
    return y.reshape(B, S, D)


def setup_inputs(seed: int = 0) -> dict:
    key = jax.random.key(seed)
    ks = jax.random.split(key, 26)
    f32 = jnp.float32
    L, D = DEPTH, D_MODEL

    def nrm(k, shape, scale):
        return jax.random.normal(k, shape, f32) * scale

    x = jax.random.normal(ks[0], (BATCH, SEQ, D), f32)
    mem = jax.random.normal(ks[1], (BATCH, MEM_LEN, D), f32)
    offsets = jax.random.randint(ks[2], (BATCH, 1), 0, 64, dtype=jnp.int32) * CHUNK
    positions = (offsets + jnp.arange(SEQ, dtype=jnp.int32)[None, :]).astype(jnp.int32)
    return {
        "x": x,
        "mem": mem,
        "positions": positions,
        "ln_in_g": 1.0 + nrm(ks[3], (D,), 0.01),
        "ln_in_b": nrm(ks[4], (D,), 0.01),
        "w_in": nrm(ks[5], (L, D, IN_TOTAL), D ** -0.5),
        "rel_bias": nrm(ks[6], (L, ATT_HEADS, 2 * MAX_REL + 1), 0.1),
        "w_proj_ret": nrm(ks[7], (L, RET_V, D), RET_V ** -0.5),
        "w_proj_att": nrm(ks[8], (L, ATT_W, D), ATT_W ** -0.5),
        "w_out": nrm(ks[9], (L, D, D), DEEPNORM_BETA * D ** -0.5),
        "ln1_g": 1.0 + nrm(ks[10], (L, D), 0.01),
        "ln1_b": nrm(ks[11], (L, D), 0.01),
        "w_q_mem": nrm(ks[12], (L, D, D), D ** -0.5),
        "w_kv_mem": nrm(ks[13], (L, D, 2 * D), D ** -0.5),
        "w_o_mem": nrm(ks[14], (L, D, D), DEEPNORM_BETA * D ** -0.5),
        "ln2_g": 1.0 + nrm(ks[15], (L, D), 0.01),
        "ln2_b": nrm(ks[16], (L, D), 0.01),
        "w_group": nrm(ks[17], (L, D, N_GROUPS), D ** -0.5),
        "b_group": nrm(ks[18], (L, N_GROUPS), 0.01),
        "w_route": nrm(ks[19], (L, D, N_GROUPS * EXPERTS_PER_GROUP), D ** -0.5),
        "b_route": nrm(ks[20], (L, N_GROUPS, EXPERTS_PER_GROUP), 0.01),
        "w_gate": nrm(ks[21], (L, N_EXPERTS, D, D_EXPERT), D ** -0.5),
        "w_up": nrm(ks[22], (L, N_EXPERTS, D, D_EXPERT), D ** -0.5),
        "w_down": nrm(ks[23], (L, N_EXPERTS, D_EXPERT, D), DEEPNORM_BETA * D_EXPERT ** -0.5),
        "ln3_g": 1.0 + nrm(ks[24], (L, D), 0.01),
        "ln3_b": nrm(ks[25], (L, D), 0.01),
    }


def reference(x, mem, positions, ln_in_g, ln_in_b, w_in, rel_bias, w_proj_ret, w_proj_att, w_out,
              ln1_g, ln1_b, w_q_mem, w_kv_mem, w_o_mem, ln2_g, ln2_b,
              w_group, b_group, w_route, b_route, w_gate, w_up, w_down, ln3_g, ln3_b):
    h = layer_norm(x, ln_in_g, ln_in_b)
    for l in range(DEPTH):
        mix = hybrid_mixer(h, positions, w_in[l], rel_bias[l], w_proj_ret[l], w_proj_att[l], w_out[l])
        h = layer_norm(DEEPNORM_ALPHA * h + mix, ln1_g[l], ln1_b[l])
        cross = memory_attention(h, mem, w_q_mem[l], w_kv_mem[l], w_o_mem[l])
        h = layer_norm(DEEPNORM_ALPHA * h + cross, ln2_g[l], ln2_b[l])
        ffn = hierarchical_moe(h, w_group[l], b_group[l], w_route[l], b_route[l], w_gate[l], w_up[l], w_down[l])
        h = layer_norm(DEEPNORM_ALPHA * h + ffn, ln3_g[l], ln3_b[l])
    return h
```

```python
import functools

import jax
import jax.numpy as jnp
import numpy as np
from jax import lax
from jax.experimental import pallas as pl
from jax.experimental.pallas import tpu as pltpu

D_MODEL = 1024
DEPTH = 2
CHUNK = 64
RET_HEADS = 8
RET_DK = 64
RET_DV = 128
RET_QK = RET_HEADS * RET_DK
RET_V = RET_HEADS * RET_DV
ROPE_BASE = 10000.0
ATT_HEADS = 8
ATT_DH = 64
ATT_W = ATT_HEADS * ATT_DH
LEFT_CHUNKS = 8
MAX_REL = 256
MEM_HEADS = 4
MEM_DH = D_MODEL // MEM_HEADS
N_GROUPS = 4
EXPERTS_PER_GROUP = 8
N_EXPERTS = N_GROUPS * EXPERTS_PER_GROUP
TOP_K = 2
D_EXPERT = 512
ROUTE_BLOCK = 256
LN_EPS = 1e-5
DEEPNORM_ALPHA = (2.0 * DEPTH) ** 0.25
IN_TOTAL = 2 * RET_QK + 2 * RET_V + 3 * ATT_W + 2 * D_MODEL

LANES = 128
SUBLANES = 8
VMEM_LIMIT_BYTES = 56 * 1024 * 1024

SEQ_TILE = 256
PROJ_TILE = 512
PROJ_COLS = 512
HEAD_PAIR = 2 * RET_DK
NEG_BIG = -1e30

F32 = jnp.float32
MXU_DTYPE = jnp.bfloat16
PACK_COLS = D_MODEL // 2

Z_QK = 0
Z_VR = 1024
Z_GR = 2048
Z_GATE_R = 3072
Z_GATE_A = 4096
Z_QA = 5120
Z_KA = 5632
Z_VA = 6144


def _layer_norm(x, g, b):
    mu = jnp.mean(x, axis=-1, keepdims=True)
    xc = x - mu
    var = jnp.mean(xc * xc, axis=-1, keepdims=True)
    return xc * lax.rsqrt(var + LN_EPS) * g + b


def _sigmoid(x):
    return 1.0 / (1.0 + jnp.exp(-x))


def _dot(a, b):
    return jnp.dot(a, b, preferred_element_type=F32)


def _dot_nt(a, b):
    return lax.dot_general(a, b, (((1,), (1,)), ((), ())), preferred_element_type=F32)


def _dot_tn(a, b):
    return lax.dot_general(a, b, (((0,), (0,)), ((), ())), preferred_element_type=F32)


def _params(*semantics):
    return pltpu.CompilerParams(dimension_semantics=semantics, vmem_limit_bytes=VMEM_LIMIT_BYTES)


def _rope_kernel(pos_ref, invf_ref, cos_ref, sin_ref):
    ang = pos_ref[...] * invf_ref[...]
    lane = lax.broadcasted_iota(jnp.int32, ang.shape, 1)
    s = jnp.sin(ang)
    cos_ref[...] = jnp.cos(ang)
    sin_ref[...] = jnp.where((lane % RET_DK) < RET_DK // 2, -s, s)


def _rope_tables(positions):
    T = positions.size
    half = RET_DK // 2
    inv_freq = 1.0 / (ROPE_BASE ** jnp.linspace(0.0, 1.0, half, dtype=F32))
    invf = jnp.tile(inv_freq, LANES // half).reshape(1, LANES)
    pos = jnp.broadcast_to(positions.reshape(T, 1).astype(F32), (T, LANES))
    tm = 1024
    spec = pl.BlockSpec((tm, LANES), lambda i: (i, 0))
    return pl.pallas_call(
        _rope_kernel,
        grid=(T // tm,),
        in_specs=[spec, pl.BlockSpec((1, LANES), lambda i: (0, 0))],
        out_specs=[spec, spec],
        out_shape=[jax.ShapeDtypeStruct((T, LANES), F32)] * 2,
        compiler_params=_params("parallel"),
        name="rope_tables",
    )(pos, invf)


def _rotary(a, cos, sin_signed):
    outs = []
    lane = lax.broadcasted_iota(jnp.int32, cos.shape, 1)
    first_half = (lane % RET_DK) < RET_DK // 2
    for g in range(a.shape[1] // LANES):
        x = a[:, g * LANES:(g + 1) * LANES]
        rot = jnp.where(first_half, pltpu.roll(x, LANES - RET_DK // 2, 1), pltpu.roll(x, RET_DK // 2, 1))
        outs.append(x * cos + rot * sin_signed)
    return jnp.concatenate(outs, axis=1)


def _inproj_kernel(apply_ln, h_ref, g_ref, b_ref, w_ref, cos_ref, sin_ref, z_ref, *h_out):
    x = h_ref[...]
    if apply_ln:
        x = _layer_norm(x, g_ref[...], b_ref[...])
        h_out[0][...] = x
    xb = x.astype(MXU_DTYPE)
    cos = cos_ref[...]
    sin = sin_ref[...]
    for c in range(IN_TOTAL // PROJ_COLS):
        lo = c * PROJ_COLS
        acc = _dot(xb, w_ref[:, lo:lo + PROJ_COLS])
        if lo < 2 * RET_QK:
            acc = _rotary(acc, cos, sin)
            if lo >= RET_QK:
                acc = acc * (RET_DK ** -0.5)
        z_ref[:, lo:lo + PROJ_COLS] = acc.astype(z_ref.dtype)


def _inproj(h, ln_g, ln_b, w_in_b, cos_t, sin_t, apply_ln):
    T = h.shape[0]
    tm = PROJ_TILE
    row = lambda i: (i, 0)
    const = lambda i: (0, 0)
    out_shape = [jax.ShapeDtypeStruct((T, IN_TOTAL), MXU_DTYPE)]
    out_specs = [pl.BlockSpec((tm, IN_TOTAL), row)]
    if apply_ln:
        out_shape.append(jax.ShapeDtypeStruct((T, D_MODEL), F32))
        out_specs.append(pl.BlockSpec((tm, D_MODEL), row))
    res = pl.pallas_call(
        functools.partial(_inproj_kernel, apply_ln),
        grid=(T // tm,),
        in_specs=[
            pl.BlockSpec((tm, D_MODEL), row),
            pl.BlockSpec((1, D_MODEL), const),
            pl.BlockSpec((1, D_MODEL), const),
            pl.BlockSpec((D_MODEL, IN_TOTAL), const, pipeline_mode=pl.Buffered(1)),
            pl.BlockSpec((tm, LANES), row),
            pl.BlockSpec((tm, LANES), row),
        ],
        out_specs=out_specs,
        out_shape=out_shape,
        compiler_params=_params("parallel"),
        name="inproj",
    )(h, ln_g, ln_b, w_in_b, cos_t, sin_t)
    return res


def _retention_tables():
    n = SEQ_TILE
    log_g = np.log1p(-np.exp2(-5.0 - np.arange(RET_HEADS, dtype=np.float64)))
    i = np.arange(n)
    same = (i[:, None] // CHUNK) == (i[None, :] // CHUNK)
    earlier = (i[None, :] // CHUNK) < (i[:, None] // CHUNK)
    diff = i[:, None] - i[None, :]
    expo = np.where(same, np.abs(diff), np.where(earlier, diff, 0)).astype(np.float64)
    decay = np.exp(log_g[:, None, None] * expo) * (same | earlier)
    q_decay = np.exp(log_g[:, None] * (i + 1.0))
    k_decay = np.exp(log_g[:, None] * (n - 1.0 - i))
    tile_decay = np.exp(log_g * n)
    wide = lambda t: jnp.asarray(np.broadcast_to(t[:, :, None], t.shape + (HEAD_PAIR,)), F32)
    return jnp.asarray(decay, F32), wide(q_decay), wide(k_decay), tuple(float(v) for v in tile_decay)


def _attention_bias(rel_bias):
    n = SEQ_TILE
    pad = LEFT_CHUNKS * CHUNK
    assert pad == 2 * n
    qi = np.arange(n)
    kj = np.arange(3 * n)
    dist = (pad + qi)[:, None] - kj[None, :]
    idx = np.clip(dist, -MAX_REL, MAX_REL) + MAX_REL
    q_chunk = (pad + qi) // CHUNK
    k_chunk = kj // CHUNK
    in_band = (k_chunk[None, :] <= q_chunk[:, None]) & (k_chunk[None, :] >= q_chunk[:, None] - LEFT_CHUNKS)
    bias = rel_bias[:, idx].astype(F32)
    return jnp.where(jnp.asarray(in_band)[None], bias, NEG_BIG)


def _mixer_kernel(tile_decay,
                  qk_ref, vr_ref, gr_ref, gate_r_ref, gate_a_ref, qa_ref,
                  k2_ref, k1_ref, k0_ref, v2_ref, v1_ref, v0_ref,
                  h_ref, decay_ref, qdec_ref, kdec_ref, bias_ref,
                  wpr_ref, wpa_ref, wout_ref, g_ref, b_ref,
                  o_ref, state_ref, yr_ref, ya_ref):
    i = pl.program_id(1)

    @pl.when(i == 0)
    def _():
        state_ref[...] = jnp.zeros_like(state_ref)

    n = SEQ_TILE
    lane = lax.broadcasted_iota(jnp.int32, (n, HEAD_PAIR), 1)
    low_half = lane < RET_DK

    for hd in range(RET_HEADS):
        p, odd = divmod(hd, 2)
        mine = low_half if odd == 0 else jnp.logical_not(low_half)
        q2 = qk_ref[:, p * HEAD_PAIR:(p + 1) * HEAD_PAIR]
        k2 = qk_ref[:, RET_QK + p * HEAD_PAIR:RET_QK + (p + 1) * HEAD_PAIR]
        qm = jnp.where(mine, q2, jnp.zeros_like(q2))
        km = jnp.where(mine, k2, jnp.zeros_like(k2))
        v = vr_ref[:, hd * RET_DV:(hd + 1) * RET_DV]
        scores = (_dot_nt(qm, k2) * decay_ref[hd]).astype(MXU_DTYPE)
        intra = _dot(scores, v)
        q_dec = (qm.astype(F32) * qdec_ref[hd]).astype(MXU_DTYPE)
        state = state_ref[hd]
        cross = _dot(q_dec, state.astype(MXU_DTYPE))
        k_dec = (km.astype(F32) * kdec_ref[hd]).astype(MXU_DTYPE)
        state_ref[hd] = state * tile_decay[hd] + _dot_tn(k_dec, v)
        ret = intra + cross
        mu = jnp.mean(ret, axis=-1, keepdims=True)
        rc = ret - mu
        var = jnp.mean(rc * rc, axis=-1, keepdims=True)
        gn = rc * lax.rsqrt(var + LN_EPS)
        gate = gr_ref[:, hd * RET_DV:(hd + 1) * RET_DV].astype(F32)
        yr_ref[:, hd * RET_DV:(hd + 1) * RET_DV] = (gate * _sigmoid(gate) * gn).astype(MXU_DTYPE)

    pen2 = jnp.where(i >= 2, 0.0, NEG_BIG).astype(F32)
    pen1 = jnp.where(i >= 1, 0.0, NEG_BIG).astype(F32)
    k_refs = (k2_ref, k1_ref, k0_ref)
    v_refs = (v2_ref, v1_ref, v0_ref)
    pens = (pen2, pen1, None)
    for p in range(ATT_HEADS // 2):
        cols = slice(p * HEAD_PAIR, (p + 1) * HEAD_PAIR)
        q2 = qa_ref[:, cols]
        pair_out = []
        for odd in range(2):
            hd = 2 * p + odd
            mine = low_half if odd == 0 else jnp.logical_not(low_half)
            qm = jnp.where(mine, q2 * (ATT_DH ** -0.5), jnp.zeros_like(q2))
            s = []
            for kb in range(3):
                sk = _dot_nt(qm, k_refs[kb][:, cols]) + bias_ref[hd, :, kb * n:(kb + 1) * n]
                if pens[kb] is not None:
                    sk = sk + pens[kb]
                s.append(sk)
            m = jnp.maximum(jnp.maximum(jnp.max(s[0], axis=-1, keepdims=True),
                                        jnp.max(s[1], axis=-1, keepdims=True)),
                            jnp.max(s[2], axis=-1, keepdims=True))
            e = [jnp.exp(sk - m) for sk in s]
            denom = (jnp.sum(e[0], axis=-1, keepdims=True) + jnp.sum(e[1], axis=-1, keepdims=True)
                     + jnp.sum(e[2], axis=-1, keepdims=True))
            pv = (_dot(e[0].astype(MXU_DTYPE), v_refs[0][:, cols])
                  + _dot(e[1].astype(MXU_DTYPE), v_refs[1][:, cols])
                  + _dot(e[2].astype(MXU_DTYPE), v_refs[2][:, cols]))
            pair_out.append(pv / denom)
        ya_ref[:, cols] = jnp.where(low_half, pair_out[0], pair_out[1]).astype(MXU_DTYPE)

    pr = _dot(yr_ref[...], wpr_ref[...])
    pa = _dot(ya_ref[...], wpa_ref[...])
    merged = (_sigmoid(gate_r_ref[...].astype(F32)) * pr + _sigmoid(gate_a_ref[...].astype(F32)) * pa)
    mix = _dot(merged.astype(MXU_DTYPE), wout_ref[...])
    o_ref[...] = _layer_norm(DEEPNORM_ALPHA * h_ref[...] + mix, g_ref[...], b_ref[...])


def _mixer(z, h, B, S, tables, bias, wpr, wpa, wout, ln_g, ln_b):
    decay, q_decay, k_decay, tile_decay = tables
    T = B * S
    n = SEQ_TILE
    nt = S // n

    def zspec(width, col_off, back=0):
        cb = col_off // width
        return pl.BlockSpec((n, width), lambda b, i: (b * nt + jnp.maximum(i - back, 0), cb))

    row = lambda b, i: (b * nt + i, 0)
    c2 = lambda b, i: (0, 0)
    c3 = lambda b, i: (0, 0, 0)
    one = pl.Buffered(1)
    in_specs = [
        zspec(2 * RET_QK, Z_QK), zspec(RET_V, Z_VR), zspec(RET_V, Z_GR),
        zspec(D_MODEL, Z_GATE_R), zspec(D_MODEL, Z_GATE_A), zspec(ATT_W, Z_QA),
        zspec(ATT_W, Z_KA, 2), zspec(ATT_W, Z_KA, 1), zspec(ATT_W, Z_KA, 0),
        zspec(ATT_W, Z_VA, 2), zspec(ATT_W, Z_VA, 1), zspec(ATT_W, Z_VA, 0),
        pl.BlockSpec((n, D_MODEL), row),
        pl.BlockSpec((RET_HEADS, n, n), c3, pipeline_mode=one),
        pl.BlockSpec((RET_HEADS, n, HEAD_PAIR), c3, pipeline_mode=one),
        pl.BlockSpec((RET_HEADS, n, HEAD_PAIR), c3, pipeline_mode=one),
        pl.BlockSpec((ATT_HEADS, n, 3 * n), c3, pipeline_mode=one),
        pl.BlockSpec((RET_V, D_MODEL), c2, pipeline_mode=one),
        pl.BlockSpec((ATT_W, D_MODEL), c2, pipeline_mode=one),
        pl.BlockSpec((D_MODEL, D_MODEL), c2, pipeline_mode=one),
        pl.BlockSpec((1, D_MODEL), c2),
        pl.BlockSpec((1, D_MODEL), c2),
    ]
    return pl.pallas_call(
        functools.partial(_mixer_kernel, tile_decay),
        grid=(B, nt),
        in_specs=in_specs,
        out_specs=pl.BlockSpec((n, D_MODEL), row),
        out_shape=jax.ShapeDtypeStruct((T, D_MODEL), F32),
        scratch_shapes=[
            pltpu.VMEM((RET_HEADS, HEAD_PAIR, RET_DV), F32),
            pltpu.VMEM((n, RET_V), MXU_DTYPE),
            pltpu.VMEM((n, ATT_W), MXU_DTYPE),
        ],
        compiler_params=_params("arbitrary", "arbitrary"),
        name="mixer",
    )(*([z] * 12), h, decay, q_decay, k_decay, bias, wpr, wpa, wout, ln_g, ln_b)


def _kv_kernel(m_ref, w_ref, o_ref):
    o_ref[...] = _dot(m_ref[...].astype(MXU_DTYPE), w_ref[...]).astype(o_ref.dtype)


def _mem_kv(mem2d, w_kv):
    M = mem2d.shape[0]
    tm = 256
    return pl.pallas_call(
        _kv_kernel,
        grid=(M // tm,),
        in_specs=[pl.BlockSpec((tm, D_MODEL), lambda i: (i, 0)),
                  pl.BlockSpec((D_MODEL, 2 * D_MODEL), lambda i: (0, 0), pipeline_mode=pl.Buffered(1))],
        out_specs=pl.BlockSpec((tm, 2 * D_MODEL), lambda i: (i, 0)),
        out_shape=jax.ShapeDtypeStruct((M, 2 * D_MODEL), MXU_DTYPE),
        compiler_params=_params("parallel"),
        name="mem_kv",
    )(mem2d, w_kv)


ROUTE_ROWS = 8
GROUP_ROW0 = 0
EXPERT_ROW0 = 8
ROUTER_ROWS = 128


def _memattn_kernel(h_ref, kv_ref, wq_ref, wo_ref, g_ref, b_ref, wr_ref, br_ref, tri_ref,
                    h2_ref, hp_ref, rl_ref, rr_ref, cnt_ref, o_scr, carry_ref):
    first = jnp.logical_and(pl.program_id(0) == 0, pl.program_id(1) == 0)

    @pl.when(first)
    def _():
        carry_ref[...] = jnp.zeros_like(carry_ref)

    n = SEQ_TILE
    h1 = h_ref[...]
    q = (_dot(h1.astype(MXU_DTYPE), wq_ref[...]) * (MEM_DH ** -0.5)).astype(MXU_DTYPE)
    for hd in range(MEM_HEADS):
        cols = slice(hd * MEM_DH, (hd + 1) * MEM_DH)
        s = _dot_nt(q[:, cols], kv_ref[:, cols])
        m = jnp.max(s, axis=-1, keepdims=True)
        e = jnp.exp(s - m)
        denom = jnp.sum(e, axis=-1, keepdims=True)
        v = kv_ref[:, D_MODEL + hd * MEM_DH:D_MODEL + (hd + 1) * MEM_DH]
        o_scr[:, cols] = (_dot(e.astype(MXU_DTYPE), v) / denom).astype(MXU_DTYPE)
    cross = _dot(o_scr[...], wo_ref[...])
    h2 = _layer_norm(DEEPNORM_ALPHA * h1 + cross, g_ref[...], b_ref[...])
    h2_ref[...] = h2

    hp_ref[...] = _pack_rows(h2)

    logits = _dot_nt(wr_ref[...], h2.astype(MXU_DTYPE)) + br_ref[...]
    glog = logits[GROUP_ROW0:GROUP_ROW0 + N_GROUPS]
    gmax = jnp.max(glog, axis=0, keepdims=True)
    g_w = 1.0 / jnp.sum(jnp.exp(glog - gmax), axis=0, keepdims=True)
    giota = lax.broadcasted_iota(jnp.int32, glog.shape, 0)
    g_idx = jnp.min(jnp.where(glog == gmax, giota, N_GROUPS), axis=0, keepdims=True)
    el = jnp.zeros((EXPERTS_PER_GROUP, n), F32)
    for g in range(N_GROUPS):
        r0 = EXPERT_ROW0 + g * EXPERTS_PER_GROUP
        el = jnp.where(g_idx == g, logits[r0:r0 + EXPERTS_PER_GROUP], el)
    eiota = lax.broadcasted_iota(jnp.int32, el.shape, 0)
    m1 = jnp.max(el, axis=0, keepdims=True)
    i1 = jnp.min(jnp.where(el == m1, eiota, EXPERTS_PER_GROUP), axis=0, keepdims=True)
    el2 = jnp.where(eiota == i1, -jnp.inf, el)
    m2 = jnp.max(el2, axis=0, keepdims=True)
    i2 = jnp.min(jnp.where(el2 == m2, eiota, EXPERTS_PER_GROUP), axis=0, keepdims=True)
    r = jnp.exp(m2 - m1)
    w1 = g_w / (1.0 + r)
    w2 = g_w * r / (1.0 + r)
    e1 = g_idx * EXPERTS_PER_GROUP + i1
    e2 = g_idx * EXPERTS_PER_GROUP + i2

    xiota = lax.broadcasted_iota(jnp.int32, (N_EXPERTS, n), 0)
    oh1 = xiota == e1
    oh2 = xiota == e2
    cnt = jnp.where(oh1, 1.0, 0.0) + jnp.where(oh2, 1.0, 0.0)
    before = _dot(cnt.astype(MXU_DTYPE), tri_ref[...]) + carry_ref[:, 0:1]
    rank1 = jnp.sum(jnp.where(oh1, before, 0.0), axis=0, keepdims=True)
    rank2 = jnp.sum(jnp.where(oh2, before, 0.0), axis=0, keepdims=True)
    carry_ref[...] = carry_ref[...] + jnp.sum(cnt, axis=1, keepdims=True)
    cnt_ref[...] = carry_ref[...]

    zero = jnp.zeros((1, n), F32)
    rec = jnp.concatenate([e1.astype(F32), e2.astype(F32), w1, w2, rank1, rank2, zero, zero], axis=0)
    rl_ref[...] = rec
    rec_full = jnp.concatenate([rec, jnp.zeros((LANES - ROUTE_ROWS, n), F32)], axis=0)
    rr_ref[...] = rec_full.T


def _memattn(h1, kv, B, S, wq, wo, ln_g, ln_b, wr_t, br_col, tri):
    T = B * S
    n = SEQ_TILE
    nt = S // n
    row = lambda b, i: (b * nt + i, 0)
    c2 = lambda b, i: (0, 0)
    one = pl.Buffered(1)
    return pl.pallas_call(
        _memattn_kernel,
        grid=(B, nt),
        in_specs=[
            pl.BlockSpec((n, D_MODEL), row),
            pl.BlockSpec((kv.shape[0] // B, 2 * D_MODEL), lambda b, i: (b, 0)),
            pl.BlockSpec((D_MODEL, D_MODEL), c2, pipeline_mode=one),
            pl.BlockSpec((D_MODEL, D_MODEL), c2, pipeline_mode=one),
            pl.BlockSpec((1, D_MODEL), c2),
            pl.BlockSpec((1, D_MODEL), c2),
            pl.BlockSpec((ROUTER_ROWS, D_MODEL), c2, pipeline_mode=one),
            pl.BlockSpec((ROUTER_ROWS, 1), c2),
            pl.BlockSpec((n, n), c2, pipeline_mode=one),
        ],
        out_specs=[
            pl.BlockSpec((n, D_MODEL), row),
            pl.BlockSpec((n, PACK_COLS), row),
            pl.BlockSpec((ROUTE_ROWS, n), lambda b, i: (0, b * nt + i)),
            pl.BlockSpec((n, LANES), row),
            pl.BlockSpec((N_EXPERTS, LANES), c2),
        ],
        out_shape=[
            jax.ShapeDtypeStruct((T, D_MODEL), F32),
            jax.ShapeDtypeStruct((T, PACK_COLS), jnp.uint32),
            jax.ShapeDtypeStruct((ROUTE_ROWS, T), F32),
            jax.ShapeDtypeStruct((T, LANES), F32),
            jax.ShapeDtypeStruct((N_EXPERTS, LANES), F32),
        ],
        scratch_shapes=[pltpu.VMEM((n, D_MODEL), MXU_DTYPE), pltpu.VMEM((N_EXPERTS, LANES), F32)],
        compiler_params=_params("arbitrary", "arbitrary"),
        name="memattn_router",
    )(h1, kv, wq, wo, ln_g, ln_b, wr_t, br_col, tri)


def _dest_kernel(rl_ref, pstart_ref, d_ref):
    e1 = rl_ref[0:1, :].astype(jnp.int32)
    e2 = rl_ref[1:2, :].astype(jnp.int32)
    xiota = lax.broadcasted_iota(jnp.int32, (N_EXPERTS, rl_ref.shape[1]), 0)
    ps = pstart_ref[:, 0:1]
    s1 = jnp.sum(jnp.where(xiota == e1, ps, 0.0), axis=0, keepdims=True)
    s2 = jnp.sum(jnp.where(xiota == e2, ps, 0.0), axis=0, keepdims=True)
    d_ref[0:1, :] = (s1 + rl_ref[4:5, :]).astype(jnp.int32)
    d_ref[1:2, :] = (s2 + rl_ref[5:6, :]).astype(jnp.int32)


def _dest_slots(rl, pstart_b):
    T = rl.shape[1]
    tn = 2048
    return pl.pallas_call(
        _dest_kernel,
        grid=(T // tn,),
        in_specs=[pl.BlockSpec((ROUTE_ROWS, tn), lambda i: (0, i)),
                  pl.BlockSpec((N_EXPERTS, LANES), lambda i: (0, 0))],
        out_specs=pl.BlockSpec((TOP_K, tn), lambda i: (0, i)),
        out_shape=jax.ShapeDtypeStruct((TOP_K, T), jnp.int32),
        compiler_params=_params("parallel"),
        name="dest_slots",
    )(rl, pstart_b)


def _row_copy_wait(src, dst, sem):
    pltpu.make_async_copy(src, dst, sem).wait()


def _dispatch_kernel(dest_ref, hp_ref, xb_in_ref, xb_ref, sem):
    del xb_in_ref
    n = SEQ_TILE

    def body(r, carry):
        for k in range(TOP_K):
            d = dest_ref[0, k, r]
            pltpu.make_async_copy(hp_ref.at[pl.ds(r, 1)], xb_ref.at[pl.ds(d, 1)], sem).start()
        return carry

    lax.fori_loop(0, n, body, 0, unroll=8)
    for _ in range(TOP_K):
        _row_copy_wait(hp_ref, xb_ref.at[pl.ds(0, n)], sem)


def _dispatch(dest3, hp, n_slots):
    T = hp.shape[0]
    n = SEQ_TILE
    xb0 = jnp.zeros((n_slots, PACK_COLS), jnp.uint32)
    return pl.pallas_call(
        _dispatch_kernel,
        grid=(T // n,),
        in_specs=[
            pl.BlockSpec((1, TOP_K, n), lambda i: (i, 0, 0), memory_space=pltpu.SMEM),
            pl.BlockSpec((n, PACK_COLS), lambda i: (i, 0)),
            pl.BlockSpec(memory_space=pl.ANY),
        ],
        out_specs=pl.BlockSpec(memory_space=pl.ANY),
        out_shape=jax.ShapeDtypeStruct((n_slots, PACK_COLS), jnp.uint32),
        scratch_shapes=[pltpu.SemaphoreType.DMA],
        input_output_aliases={2: 0},
        compiler_params=_params("arbitrary"),
        name="dispatch",
    )(dest3, hp, xb0)


def _unpack_rows(packed):
    lo = lax.bitcast_convert_type(packed << 16, F32)
    hi = lax.bitcast_convert_type(packed & jnp.uint32(0xFFFF0000), F32)
    return lo, hi


def _pack_rows(y):
    bits = lax.bitcast_convert_type(y.astype(jnp.bfloat16).astype(F32), jnp.uint32)
    return (bits[:, :PACK_COLS] >> 16) | bits[:, PACK_COLS:]


def _expert_kernel(be_ref, nused_ref, x_ref, wg_ref, wu_ref, wd_ref, y_ref):
    del be_ref
    p = pl.program_id(0)

    @pl.when(p < nused_ref[0])
    def _():
        lo, hi = _unpack_rows(x_ref[...])
        x = jnp.concatenate([lo.astype(MXU_DTYPE), hi.astype(MXU_DTYPE)], axis=1)
        gate = _dot(x, wg_ref[...])
        up = _dot(x, wu_ref[...])
        hid = (gate * _sigmoid(gate) * up).astype(MXU_DTYPE)
        y_ref[...] = _pack_rows(_dot(hid, wd_ref[...]))

    @pl.when(p >= nused_ref[0])
    def _():
        y_ref[...] = jnp.zeros_like(y_ref)


def _experts(block_expert, n_used, xb, wg, wu, wd):
    n_slots = xb.shape[0]
    nb = n_slots // ROUTE_BLOCK
    grid_spec = pltpu.PrefetchScalarGridSpec(
        num_scalar_prefetch=2,
        grid=(nb,),
        in_specs=[
            pl.BlockSpec((ROUTE_BLOCK, PACK_COLS), lambda p, be, nu: (p, 0)),
            pl.BlockSpec((None, D_MODEL, D_EXPERT), lambda p, be, nu: (be[p], 0, 0)),
            pl.BlockSpec((None, D_MODEL, D_EXPERT), lambda p, be, nu: (be[p], 0, 0)),
            pl.BlockSpec((None, D_EXPERT, D_MODEL), lambda p, be, nu: (be[p], 0, 0)),
        ],
        out_specs=pl.BlockSpec((ROUTE_BLOCK, PACK_COLS), lambda p, be, nu: (p, 0)),
    )
    return pl.pallas_call(
        _expert_kernel,
        grid_spec=grid_spec,
        out_shape=jax.ShapeDtypeStruct((n_slots, PACK_COLS), jnp.uint32),
        compiler_params=_params("arbitrary"),
        name="experts",
    )(block_expert, n_used, xb, wg, wu, wd)


def _combine_kernel(dest_ref, yb_ref, rr_ref, h_ref, g_ref, b_ref, o_ref, buf, sem):
    n = SEQ_TILE

    def body(r, carry):
        for k in range(TOP_K):
            d = dest_ref[0, k, r]
            pltpu.make_async_copy(yb_ref.at[pl.ds(d, 1)], buf.at[k, pl.ds(r, 1)], sem).start()
        return carry

    lax.fori_loop(0, n, body, 0, unroll=8)
    for k in range(TOP_K):
        _row_copy_wait(yb_ref.at[pl.ds(0, n)], buf.at[k], sem)

    rr = rr_ref[...]
    y_lo = jnp.zeros((n, PACK_COLS), F32)
    y_hi = jnp.zeros((n, PACK_COLS), F32)
    for k in range(TOP_K):
        w = rr[:, 2 + k:3 + k]
        lo, hi = _unpack_rows(buf[k])
        y_lo = y_lo + w * lo
        y_hi = y_hi + w * hi
    y = jnp.concatenate([y_lo, y_hi], axis=1)
    o_ref[...] = _layer_norm(DEEPNORM_ALPHA * h_ref[...] + y, g_ref[...], b_ref[...])


def _combine(dest3, yb, rr, h2, ln_g, ln_b):
    T = h2.shape[0]
    n = SEQ_TILE
    row = lambda i: (i, 0)
    return pl.pallas_call(
        _combine_kernel,
        grid=(T // n,),
        in_specs=[
            pl.BlockSpec((1, TOP_K, n), lambda i: (i, 0, 0), memory_space=pltpu.SMEM),
            pl.BlockSpec(memory_space=pl.ANY),
            pl.BlockSpec((n, LANES), row),
            pl.BlockSpec((n, D_MODEL), row),
            pl.BlockSpec((1, D_MODEL), lambda i: (0, 0)),
            pl.BlockSpec((1, D_MODEL), lambda i: (0, 0)),
        ],
        out_specs=pl.BlockSpec((n, D_MODEL), row),
        out_shape=jax.ShapeDtypeStruct((T, D_MODEL), F32),
        scratch_shapes=[pltpu.VMEM((TOP_K, n, PACK_COLS), jnp.uint32), pltpu.SemaphoreType.DMA],
        compiler_params=_params("arbitrary"),
        name="combine",
    )(dest3, yb, rr, h2, ln_g, ln_b)


def _router_weights(w_group, b_group, w_route, b_route):
    wr = jnp.zeros((ROUTER_ROWS, D_MODEL), F32)
    wr = wr.at[GROUP_ROW0:GROUP_ROW0 + N_GROUPS].set(w_group.T)
    wr = wr.at[EXPERT_ROW0:EXPERT_ROW0 + N_EXPERTS].set(w_route.T)
    br = jnp.zeros((ROUTER_ROWS,), F32)
    br = br.at[GROUP_ROW0:GROUP_ROW0 + N_GROUPS].set(b_group)
    br = br.at[EXPERT_ROW0:EXPERT_ROW0 + N_EXPERTS].set(b_route.reshape(-1))
    return wr.astype(MXU_DTYPE), br.reshape(ROUTER_ROWS, 1)


def kernel(x, mem, positions, ln_in_g, ln_in_b, w_in, rel_bias, w_proj_ret, w_proj_att, w_out, ln1_g, ln1_b, w_q_mem, w_kv_mem, w_o_mem, ln2_g, ln2_b, w_group, b_group, w_route, b_route, w_gate, w_up, w_down, ln3_g, ln3_b):
    B, S, D = x.shape
    assert D == D_MODEL and S % PROJ_TILE == 0 and S % SEQ_TILE == 0
    T = B * S
    A = T * TOP_K
    n_blocks = -(-(A + N_EXPERTS * (ROUTE_BLOCK - 1)) // ROUTE_BLOCK)
    n_slots = n_blocks * ROUTE_BLOCK
    bf = MXU_DTYPE

    cos_t, sin_t = _rope_tables(positions)
    tables = _retention_tables()
    tri = jnp.asarray(np.triu(np.ones((SEQ_TILE, SEQ_TILE), np.float32), 1), bf)
    mem2d = mem.reshape(-1, D)
    row2 = lambda v: v.reshape(1, D)

    h = x.reshape(T, D)
    for l in range(DEPTH):
        w = w_in[l]
        w_re = jnp.concatenate([w[:, :Z_GATE_R], w[:, Z_GATE_R + 3 * ATT_W:], w[:, Z_GATE_R:Z_GATE_R + 3 * ATT_W]],
                               axis=1).astype(bf)
        if l == 0:
            z, h = _inproj(h, row2(ln_in_g), row2(ln_in_b), w_re, cos_t, sin_t, True)
        else:
            (z,) = _inproj(h, row2(ln_in_g), row2(ln_in_b), w_re, cos_t, sin_t, False)
        bias = _attention_bias(rel_bias[l])
        h = _mixer(z, h, B, S, tables, bias, w_proj_ret[l].astype(bf), w_proj_att[l].astype(bf),
                   w_out[l].astype(bf), row2(ln1_g[l]), row2(ln1_b[l]))

        kv = _mem_kv(mem2d, w_kv_mem[l].astype(bf))
        wr_t, br_col = _router_weights(w_group[l], b_group[l], w_route[l], b_route[l])
        h2, hp, rl, rr, counts = _memattn(h, kv, B, S, w_q_mem[l].astype(bf), w_o_mem[l].astype(bf),
                                          row2(ln2_g[l]), row2(ln2_b[l]), wr_t, br_col, tri)

        cnt = counts[:, 0].astype(jnp.int32)
        padded = ((cnt + ROUTE_BLOCK - 1) // ROUTE_BLOCK) * ROUTE_BLOCK
        pend = jnp.cumsum(padded)
        pstart = pend - padded
        block_expert = jnp.clip(
            jnp.searchsorted(pend, jnp.arange(n_blocks, dtype=jnp.int32) * ROUTE_BLOCK, side='right'),
            0, N_EXPERTS - 1).astype(jnp.int32)
        n_used = (pend[-1:] // ROUTE_BLOCK).astype(jnp.int32)
        pstart_b = jnp.broadcast_to(pstart.astype(F32)[:, None], (N_EXPERTS, LANES))

        dest = _dest_slots(rl, pstart_b)
        dest3 = dest.reshape(TOP_K, T // SEQ_TILE, SEQ_TILE).transpose(1, 0, 2)
        xb = _dispatch(dest3, hp, n_slots)
        yb = _experts(block_expert, n_used, xb, w_gate[l].astype(bf), w_up[l].astype(bf), w_down[l].astype(bf))
        h = _combine(dest3, yb, rr, h2, row2(ln3_g[l]), row2(ln3_b[l]))
    return h.reshape(B, S, D)
```

```python
import functools

import jax
import jax.numpy as jnp
import numpy as np
from jax import lax
from jax.experimental import pallas as pl
from jax.experimental.pallas import tpu as pltpu

D_MODEL = 1024
DEPTH = 2
CHUNK = 64
RET_HEADS = 8
RET_DK = 64
RET_DV = 128
RET_QK = RET_HEADS * RET_DK
RET_V = RET_HEADS * RET_DV
ROPE_BASE = 10000.0
ATT_HEADS = 8
ATT_DH = 64
ATT_W = ATT_HEADS * ATT_DH
LEFT_CHUNKS = 8
MAX_REL = 256
MEM_HEADS = 4
MEM_DH = D_MODEL // MEM_HEADS
N_GROUPS = 4
EXPERTS_PER_GROUP = 8
N_EXPERTS = N_GROUPS * EXPERTS_PER_GROUP
TOP_K = 2
D_EXPERT = 512
ROUTE_BLOCK = 256
LN_EPS = 1e-5
DEEPNORM_ALPHA = (2.0 * DEPTH) ** 0.25
IN_TOTAL = 2 * RET_QK + 2 * RET_V + 3 * ATT_W + 2 * D_MODEL

LANES = 128
SUBLANES = 8
VMEM_LIMIT_BYTES = 56 * 1024 * 1024

SEQ_TILE = 256
PROJ_TILE = 512
PROJ_COLS = 512
HEAD_PAIR = 2 * RET_DK
NEG_BIG = -1e30

F32 = jnp.float32
MXU_DTYPE = jnp.bfloat16
PACK_COLS = D_MODEL // 2

Z_QK = 0
Z_VR = Z_QK + 2 * RET_QK
Z_GR = Z_VR + RET_V
Z_QA = Z_GR + RET_V
Z_KA = Z_QA + ATT_W
Z_VA = Z_KA + ATT_W
Z_GATE_R = Z_VA + ATT_W
Z_GATE_A = Z_GATE_R + D_MODEL
GATE_HALF = D_MODEL // 2


def _layer_norm(x, g, b):
    mu = jnp.mean(x, axis=-1, keepdims=True)
    xc = x - mu
    var = jnp.mean(xc * xc, axis=-1, keepdims=True)
    return xc * lax.rsqrt(var + LN_EPS) * g + b


def _sigmoid(x):
    return 1.0 / (1.0 + jnp.exp(-x))


def _dot(a, b):
    return jnp.dot(a, b, preferred_element_type=F32)


def _dot_nt(a, b):
    return lax.dot_general(a, b, (((1,), (1,)), ((), ())), preferred_element_type=F32)


def _dot_tn(a, b):
    return lax.dot_general(a, b, (((0,), (0,)), ((), ())), preferred_element_type=F32)


def _params(*semantics):
    return pltpu.CompilerParams(dimension_semantics=semantics, vmem_limit_bytes=VMEM_LIMIT_BYTES)


def _rope_kernel(pos_ref, invf_ref, cos_ref, sin_ref):
    ang = pos_ref[...] * invf_ref[...]
    lane = lax.broadcasted_iota(jnp.int32, ang.shape, 1)
    s = jnp.sin(ang)
    cos_ref[...] = jnp.cos(ang)
    sin_ref[...] = jnp.where((lane % RET_DK) < RET_DK // 2, -s, s)


def _rope_tables(positions):
    T = positions.size
    half = RET_DK // 2
    inv_freq = 1.0 / (ROPE_BASE ** jnp.linspace(0.0, 1.0, half, dtype=F32))
    invf = jnp.tile(inv_freq, LANES // half).reshape(1, LANES)
    pos = jnp.broadcast_to(positions.reshape(T, 1).astype(F32), (T, LANES))
    tm = 1024
    spec = pl.BlockSpec((tm, LANES), lambda i: (i, 0))
    return pl.pallas_call(
        _rope_kernel,
        grid=(T // tm,),
        in_specs=[spec, pl.BlockSpec((1, LANES), lambda i: (0, 0))],
        out_specs=[spec, spec],
        out_shape=[jax.ShapeDtypeStruct((T, LANES), F32)] * 2,
        compiler_params=_params("parallel"),
        name="rope_tables",
    )(pos, invf)


def _rotary(a, cos, sin_signed):
    outs = []
    lane = lax.broadcasted_iota(jnp.int32, cos.shape, 1)
    first_half = (lane % RET_DK) < RET_DK // 2
    for g in range(a.shape[1] // LANES):
        x = a[:, g * LANES:(g + 1) * LANES]
        rot = jnp.where(first_half, pltpu.roll(x, LANES - RET_DK // 2, 1), pltpu.roll(x, RET_DK // 2, 1))
        outs.append(x * cos + rot * sin_signed)
    return jnp.concatenate(outs, axis=1)


def _inproj_kernel(apply_ln, h_ref, g_ref, b_ref, w_ref, cos_ref, sin_ref, z_ref, *h_out):
    x = h_ref[...]
    if apply_ln:
        x = _layer_norm(x, g_ref[...], b_ref[...])
        h_out[0][...] = x
    xb = x.astype(MXU_DTYPE)
    cos = cos_ref[...]
    sin = sin_ref[...]
    for c in range(IN_TOTAL // PROJ_COLS):
        lo = c * PROJ_COLS
        acc = _dot(xb, w_ref[:, lo:lo + PROJ_COLS])
        if lo < 2 * RET_QK:
            acc = _rotary(acc, cos, sin)
            if lo >= RET_QK:
                acc = acc * (RET_DK ** -0.5)
        z_ref[:, lo:lo + PROJ_COLS] = acc.astype(z_ref.dtype)


def _inproj(h, ln_g, ln_b, w_in_b, cos_t, sin_t, apply_ln):
    T = h.shape[0]
    tm = PROJ_TILE
    row = lambda i: (i, 0)
    const = lambda i: (0, 0)
    out_shape = [jax.ShapeDtypeStruct((T, IN_TOTAL), MXU_DTYPE)]
    out_specs = [pl.BlockSpec((tm, IN_TOTAL), row)]
    if apply_ln:
        out_shape.append(jax.ShapeDtypeStruct((T, D_MODEL), F32))
        out_specs.append(pl.BlockSpec((tm, D_MODEL), row))
    res = pl.pallas_call(
        functools.partial(_inproj_kernel, apply_ln),
        grid=(T // tm,),
        in_specs=[
            pl.BlockSpec((tm, D_MODEL), row),
            pl.BlockSpec((1, D_MODEL), const),
            pl.BlockSpec((1, D_MODEL), const),
            pl.BlockSpec((D_MODEL, IN_TOTAL), const, pipeline_mode=pl.Buffered(1)),
            pl.BlockSpec((tm, LANES), row),
            pl.BlockSpec((tm, LANES), row),
        ],
        out_specs=out_specs,
        out_shape=out_shape,
        compiler_params=_params("parallel"),
        name="inproj",
    )(h, ln_g, ln_b, w_in_b, cos_t, sin_t)
    return res


def _retention_tables():
    n = SEQ_TILE
    log_g = np.log1p(-np.exp2(-5.0 - np.arange(RET_HEADS, dtype=np.float64)))
    i = np.arange(n)
    same = (i[:, None] // CHUNK) == (i[None, :] // CHUNK)
    earlier = (i[None, :] // CHUNK) < (i[:, None] // CHUNK)
    diff = i[:, None] - i[None, :]
    expo = np.where(same, np.abs(diff), np.where(earlier, diff, 0)).astype(np.float64)
    decay = np.exp(log_g[:, None, None] * expo) * (same | earlier)
    q_decay = np.exp(log_g[:, None] * (i + 1.0))
    k_decay = np.exp(log_g[:, None] * (n - 1.0 - i))
    tile_decay = np.exp(log_g * n)
    wide = lambda t: jnp.asarray(np.broadcast_to(t[:, :, None], t.shape + (HEAD_PAIR,)), F32)
    return jnp.asarray(decay, F32), wide(q_decay), wide(k_decay), tuple(float(v) for v in tile_decay)


def _attention_bias(rel_bias):
    n = SEQ_TILE
    width = 3 * n
    pad = LEFT_CHUNKS * CHUNK
    assert pad == 2 * n and n - 1 <= MAX_REL
    heads = rel_bias.shape[0]
    length = width + n
    near = rel_bias[:, MAX_REL - (n - 1):].astype(F32)
    far = jnp.broadcast_to(rel_bias[:, -1:].astype(F32), (heads, length - near.shape[1]))
    g = jnp.concatenate([near, far], axis=1)
    flat = jnp.tile(g, (1, n + 1))[:, :n * (length + 1)]
    hank = flat.reshape(heads, n, length + 1)[:, :, :width]
    bias = jnp.flip(hank, axis=2)
    qi = np.arange(n)
    kj = np.arange(width)
    q_chunk = (pad + qi) // CHUNK
    k_chunk = kj // CHUNK
    in_band = (k_chunk[None, :] <= q_chunk[:, None]) & (k_chunk[None, :] >= q_chunk[:, None] - LEFT_CHUNKS)
    return jnp.where(jnp.asarray(in_band)[None], bias, NEG_BIG)


def _mixer_kernel(tile_decay,
                  qk_ref, vr_ref, gr_ref, gate_r0_ref, gate_r1_ref, gate_a0_ref, gate_a1_ref, qa_ref,
                  k2_ref, k1_ref, k0_ref, v2_ref, v1_ref, v0_ref,
                  h_ref, decay_ref, qdec_ref, kdec_ref, bias_ref,
                  wpr_ref, wpa_ref, wout_ref, g_ref, b_ref,
                  o_ref, state_ref, yr_ref, ya_ref):
    i = pl.program_id(1)

    @pl.when(i == 0)
    def _():
        state_ref[...] = jnp.zeros_like(state_ref)

    n = SEQ_TILE
    lane = lax.broadcasted_iota(jnp.int32, (n, HEAD_PAIR), 1)
    low_half = lane < RET_DK

    for hd in range(RET_HEADS):
        p, odd = divmod(hd, 2)
        mine = low_half if odd == 0 else jnp.logical_not(low_half)
        q2 = qk_ref[:, p * HEAD_PAIR:(p + 1) * HEAD_PAIR]
        k2 = qk_ref[:, RET_QK + p * HEAD_PAIR:RET_QK + (p + 1) * HEAD_PAIR]
        qm = jnp.where(mine, q2, jnp.zeros_like(q2))
        km = jnp.where(mine, k2, jnp.zeros_like(k2))
        v = vr_ref[:, hd * RET_DV:(hd + 1) * RET_DV]
        scores = (_dot_nt(qm, k2) * decay_ref[hd]).astype(MXU_DTYPE)
        intra = _dot(scores, v)
        q_dec = (qm.astype(F32) * qdec_ref[hd]).astype(MXU_DTYPE)
        state = state_ref[hd]
        cross = _dot(q_dec, state.astype(MXU_DTYPE))
        k_dec = (km.astype(F32) * kdec_ref[hd]).astype(MXU_DTYPE)
        state_ref[hd] = state * tile_decay[hd] + _dot_tn(k_dec, v)
        ret = intra + cross
        mu = jnp.mean(ret, axis=-1, keepdims=True)
        rc = ret - mu
        var = jnp.mean(rc * rc, axis=-1, keepdims=True)
        gn = rc * lax.rsqrt(var + LN_EPS)
        gate = gr_ref[:, hd * RET_DV:(hd + 1) * RET_DV].astype(F32)
        yr_ref[:, hd * RET_DV:(hd + 1) * RET_DV] = (gate * _sigmoid(gate) * gn).astype(MXU_DTYPE)

    pen2 = jnp.where(i >= 2, 0.0, NEG_BIG).astype(F32)
    pen1 = jnp.where(i >= 1, 0.0, NEG_BIG).astype(F32)
    k_refs = (k2_ref, k1_ref, k0_ref)
    v_refs = (v2_ref, v1_ref, v0_ref)
    pens = (pen2, pen1, None)
    for p in range(ATT_HEADS // 2):
        cols = slice(p * HEAD_PAIR, (p + 1) * HEAD_PAIR)
        q2 = qa_ref[:, cols]
        pair_out = []
        for odd in range(2):
            hd = 2 * p + odd
            mine = low_half if odd == 0 else jnp.logical_not(low_half)
            qm = jnp.where(mine, q2 * (ATT_DH ** -0.5), jnp.zeros_like(q2))
            s = []
            for kb in range(3):
                sk = _dot_nt(qm, k_refs[kb][:, cols]) + bias_ref[hd, :, kb * n:(kb + 1) * n]
                if pens[kb] is not None:
                    sk = sk + pens[kb]
                s.append(sk)
            m = jnp.maximum(jnp.maximum(jnp.max(s[0], axis=-1, keepdims=True),
                                        jnp.max(s[1], axis=-1, keepdims=True)),
                            jnp.max(s[2], axis=-1, keepdims=True))
            e = [jnp.exp(sk - m) for sk in s]
            denom = (jnp.sum(e[0], axis=-1, keepdims=True) + jnp.sum(e[1], axis=-1, keepdims=True)
                     + jnp.sum(e[2], axis=-1, keepdims=True))
            pv = (_dot(e[0].astype(MXU_DTYPE), v_refs[0][:, cols])
                  + _dot(e[1].astype(MXU_DTYPE), v_refs[1][:, cols])
                  + _dot(e[2].astype(MXU_DTYPE), v_refs[2][:, cols]))
            pair_out.append(pv / denom)
        ya_ref[:, cols] = jnp.where(low_half, pair_out[0], pair_out[1]).astype(MXU_DTYPE)

    pr = _dot(yr_ref[...], wpr_ref[...])
    pa = _dot(ya_ref[...], wpa_ref[...])
    gate_r = jnp.concatenate([gate_r0_ref[...], gate_r1_ref[...]], axis=1).astype(F32)
    gate_a = jnp.concatenate([gate_a0_ref[...], gate_a1_ref[...]], axis=1).astype(F32)
    merged = _sigmoid(gate_r) * pr + _sigmoid(gate_a) * pa
    mix = _dot(merged.astype(MXU_DTYPE), wout_ref[...])
    o_ref[...] = _layer_norm(DEEPNORM_ALPHA * h_ref[...] + mix, g_ref[...], b_ref[...])


def _mixer(z, h, B, S, tables, bias, wpr, wpa, wout, ln_g, ln_b):
    decay, q_decay, k_decay, tile_decay = tables
    T = B * S
    n = SEQ_TILE
    nt = S // n

    def zspec(width, col_off, back=0):
        cb = col_off // width
        return pl.BlockSpec((n, width), lambda b, i: (b * nt + jnp.maximum(i - back, 0), cb))

    row = lambda b, i: (b * nt + i, 0)
    c2 = lambda b, i: (0, 0)
    c3 = lambda b, i: (0, 0, 0)
    one = pl.Buffered(1)
    in_specs = [
        zspec(2 * RET_QK, Z_QK), zspec(RET_V, Z_VR), zspec(RET_V, Z_GR),
        zspec(GATE_HALF, Z_GATE_R), zspec(GATE_HALF, Z_GATE_R + GATE_HALF),
        zspec(GATE_HALF, Z_GATE_A), zspec(GATE_HALF, Z_GATE_A + GATE_HALF), zspec(ATT_W, Z_QA),
        zspec(ATT_W, Z_KA, 2), zspec(ATT_W, Z_KA, 1), zspec(ATT_W, Z_KA, 0),
        zspec(ATT_W, Z_VA, 2), zspec(ATT_W, Z_VA, 1), zspec(ATT_W, Z_VA, 0),
        pl.BlockSpec((n, D_MODEL), row),
        pl.BlockSpec((RET_HEADS, n, n), c3, pipeline_mode=one),
        pl.BlockSpec((RET_HEADS, n, HEAD_PAIR), c3, pipeline_mode=one),
        pl.BlockSpec((RET_HEADS, n, HEAD_PAIR), c3, pipeline_mode=one),
        pl.BlockSpec((ATT_HEADS, n, 3 * n), c3, pipeline_mode=one),
        pl.BlockSpec((RET_V, D_MODEL), c2, pipeline_mode=one),
        pl.BlockSpec((ATT_W, D_MODEL), c2, pipeline_mode=one),
        pl.BlockSpec((D_MODEL, D_MODEL), c2, pipeline_mode=one),
        pl.BlockSpec((1, D_MODEL), c2),
        pl.BlockSpec((1, D_MODEL), c2),
    ]
    return pl.pallas_call(
        functools.partial(_mixer_kernel, tile_decay),
        grid=(B, nt),
        in_specs=in_specs,
        out_specs=pl.BlockSpec((n, D_MODEL), row),
        out_shape=jax.ShapeDtypeStruct((T, D_MODEL), F32),
        scratch_shapes=[
            pltpu.VMEM((RET_HEADS, HEAD_PAIR, RET_DV), F32),
            pltpu.VMEM((n, RET_V), MXU_DTYPE),
            pltpu.VMEM((n, ATT_W), MXU_DTYPE),
        ],
        compiler_params=_params("arbitrary", "arbitrary"),
        name="mixer",
    )(*([z] * 14), h, decay, q_decay, k_decay, bias, wpr, wpa, wout, ln_g, ln_b)


def _kv_kernel(m_ref, w_ref, o_ref):
    o_ref[...] = _dot(m_ref[...].astype(MXU_DTYPE), w_ref[...]).astype(o_ref.dtype)


def _mem_kv(mem2d, w_kv):
    M = mem2d.shape[0]
    tm = 256
    return pl.pallas_call(
        _kv_kernel,
        grid=(M // tm,),
        in_specs=[pl.BlockSpec((tm, D_MODEL), lambda i: (i, 0)),
                  pl.BlockSpec((D_MODEL, 2 * D_MODEL), lambda i: (0, 0), pipeline_mode=pl.Buffered(1))],
        out_specs=pl.BlockSpec((tm, 2 * D_MODEL), lambda i: (i, 0)),
        out_shape=jax.ShapeDtypeStruct((M, 2 * D_MODEL), MXU_DTYPE),
        compiler_params=_params("parallel"),
        name="mem_kv",
    )(mem2d, w_kv)


ROUTE_ROWS = 8
GROUP_ROW0 = 0
EXPERT_ROW0 = 8
ROUTER_ROWS = 128


def _memattn_kernel(h_ref, kv_ref, wq_ref, wo_ref, g_ref, b_ref, wr_ref, br_ref, tri_ref,
                    h2_ref, hp_ref, rl_ref, rr_ref, cnt_ref, o_scr, carry_ref):
    first = jnp.logical_and(pl.program_id(0) == 0, pl.program_id(1) == 0)

    @pl.when(first)
    def _():
        carry_ref[...] = jnp.zeros_like(carry_ref)

    n = SEQ_TILE
    h1 = h_ref[...]
    q = (_dot(h1.astype(MXU_DTYPE), wq_ref[...]) * (MEM_DH ** -0.5)).astype(MXU_DTYPE)
    for hd in range(MEM_HEADS):
        cols = slice(hd * MEM_DH, (hd + 1) * MEM_DH)
        s = _dot_nt(q[:, cols], kv_ref[:, cols])
        m = jnp.max(s, axis=-1, keepdims=True)
        e = jnp.exp(s - m)
        denom = jnp.sum(e, axis=-1, keepdims=True)
        v = kv_ref[:, D_MODEL + hd * MEM_DH:D_MODEL + (hd + 1) * MEM_DH]
        o_scr[:, cols] = (_dot(e.astype(MXU_DTYPE), v) / denom).astype(MXU_DTYPE)
    cross = _dot(o_scr[...], wo_ref[...])
    h2 = _layer_norm(DEEPNORM_ALPHA * h1 + cross, g_ref[...], b_ref[...])
    h2_ref[...] = h2

    hp_ref[...] = _pack_rows(h2)

    logits = _dot_nt(wr_ref[...], h2.astype(MXU_DTYPE)) + br_ref[...]
    glog = logits[GROUP_ROW0:GROUP_ROW0 + N_GROUPS]
    gmax = jnp.max(glog, axis=0, keepdims=True)
    g_w = 1.0 / jnp.sum(jnp.exp(glog - gmax), axis=0, keepdims=True)
    giota = lax.broadcasted_iota(jnp.int32, glog.shape, 0)
    g_idx = jnp.min(jnp.where(glog == gmax, giota, N_GROUPS), axis=0, keepdims=True)
    el = jnp.zeros((EXPERTS_PER_GROUP, n), F32)
    for g in range(N_GROUPS):
        r0 = EXPERT_ROW0 + g * EXPERTS_PER_GROUP
        el = jnp.where(g_idx == g, logits[r0:r0 + EXPERTS_PER_GROUP], el)
    eiota = lax.broadcasted_iota(jnp.int32, el.shape, 0)
    m1 = jnp.max(el, axis=0, keepdims=True)
    i1 = jnp.min(jnp.where(el == m1, eiota, EXPERTS_PER_GROUP), axis=0, keepdims=True)
    el2 = jnp.where(eiota == i1, -jnp.inf, el)
    m2 = jnp.max(el2, axis=0, keepdims=True)
    i2 = jnp.min(jnp.where(el2 == m2, eiota, EXPERTS_PER_GROUP), axis=0, keepdims=True)
    r = jnp.exp(m2 - m1)
    w1 = g_w / (1.0 + r)
    w2 = g_w * r / (1.0 + r)
    e1 = g_idx * EXPERTS_PER_GROUP + i1
    e2 = g_idx * EXPERTS_PER_GROUP + i2

    xiota = lax.broadcasted_iota(jnp.int32, (N_EXPERTS, n), 0)
    oh1 = xiota == e1
    oh2 = xiota == e2
    cnt = jnp.where(oh1, 1.0, 0.0) + jnp.where(oh2, 1.0, 0.0)
    before = _dot(cnt.astype(MXU_DTYPE), tri_ref[...]) + carry_ref[:, 0:1]
    rank1 = jnp.sum(jnp.where(oh1, before, 0.0), axis=0, keepdims=True)
    rank2 = jnp.sum(jnp.where(oh2, before, 0.0), axis=0, keepdims=True)
    carry_ref[...] = carry_ref[...] + jnp.sum(cnt, axis=1, keepdims=True)
    cnt_ref[...] = carry_ref[...]

    zero = jnp.zeros((1, n), F32)
    rec = jnp.concatenate([e1.astype(F32), e2.astype(F32), w1, w2, rank1, rank2, zero, zero], axis=0)
    rl_ref[...] = rec
    rec_full = jnp.concatenate([rec, jnp.zeros((LANES - ROUTE_ROWS, n), F32)], axis=0)
    rr_ref[...] = rec_full.T


def _memattn(h1, kv, B, S, wq, wo, ln_g, ln_b, wr_t, br_col, tri):
    T = B * S
    n = SEQ_TILE
    nt = S // n
    row = lambda b, i: (b * nt + i, 0)
    c2 = lambda b, i: (0, 0)
    one = pl.Buffered(1)
    return pl.pallas_call(
        _memattn_kernel,
        grid=(B, nt),
        in_specs=[
            pl.BlockSpec((n, D_MODEL), row),
            pl.BlockSpec((kv.shape[0] // B, 2 * D_MODEL), lambda b, i: (b, 0)),
            pl.BlockSpec((D_MODEL, D_MODEL), c2, pipeline_mode=one),
            pl.BlockSpec((D_MODEL, D_MODEL), c2, pipeline_mode=one),
            pl.BlockSpec((1, D_MODEL), c2),
            pl.BlockSpec((1, D_MODEL), c2),
            pl.BlockSpec((ROUTER_ROWS, D_MODEL), c2, pipeline_mode=one),
            pl.BlockSpec((ROUTER_ROWS, 1), c2),
            pl.BlockSpec((n, n), c2, pipeline_mode=one),
        ],
        out_specs=[
            pl.BlockSpec((n, D_MODEL), row),
            pl.BlockSpec((n, PACK_COLS), row),
            pl.BlockSpec((ROUTE_ROWS, n), lambda b, i: (0, b * nt + i)),
            pl.BlockSpec((n, LANES), row),
            pl.BlockSpec((N_EXPERTS, LANES), c2),
        ],
        out_shape=[
            jax.ShapeDtypeStruct((T, D_MODEL), F32),
            jax.ShapeDtypeStruct((T, PACK_COLS), jnp.uint32),
            jax.ShapeDtypeStruct((ROUTE_ROWS, T), F32),
            jax.ShapeDtypeStruct((T, LANES), F32),
            jax.ShapeDtypeStruct((N_EXPERTS, LANES), F32),
        ],
        scratch_shapes=[pltpu.VMEM((n, D_MODEL), MXU_DTYPE), pltpu.VMEM((N_EXPERTS, LANES), F32)],
        compiler_params=_params("arbitrary", "arbitrary"),
        name="memattn_router",
    )(h1, kv, wq, wo, ln_g, ln_b, wr_t, br_col, tri)


def _dest_kernel(rl_ref, pstart_ref, d_ref):
    e1 = rl_ref[0:1, :].astype(jnp.int32)
    e2 = rl_ref[1:2, :].astype(jnp.int32)
    xiota = lax.broadcasted_iota(jnp.int32, (N_EXPERTS, rl_ref.shape[1]), 0)
    ps = pstart_ref[:, 0:1]
    s1 = jnp.sum(jnp.where(xiota == e1, ps, 0.0), axis=0, keepdims=True)
    s2 = jnp.sum(jnp.where(xiota == e2, ps, 0.0), axis=0, keepdims=True)
    d_ref[0:1, :] = (s1 + rl_ref[4:5, :]).astype(jnp.int32)
    d_ref[1:2, :] = (s2 + rl_ref[5:6, :]).astype(jnp.int32)


def _dest_slots(rl, pstart_b):
    T = rl.shape[1]
    tn = 2048
    return pl.pallas_call(
        _dest_kernel,
        grid=(T // tn,),
        in_specs=[pl.BlockSpec((ROUTE_ROWS, tn), lambda i: (0, i)),
                  pl.BlockSpec((N_EXPERTS, LANES), lambda i: (0, 0))],
        out_specs=pl.BlockSpec((TOP_K, tn), lambda i: (0, i)),
        out_shape=jax.ShapeDtypeStruct((TOP_K, T), jnp.int32),
        compiler_params=_params("parallel"),
        name="dest_slots",
    )(rl, pstart_b)


def _row_copy_wait(src, dst, sem):
    pltpu.make_async_copy(src, dst, sem).wait()


def _dispatch_kernel(dest_ref, hp_ref, xb_in_ref, xb_ref, sem):
    del xb_in_ref
    n = SEQ_TILE

    def body(r, carry):
        for k in range(TOP_K):
            d = dest_ref[0, k, r]
            pltpu.make_async_copy(hp_ref.at[pl.ds(r, 1)], xb_ref.at[pl.ds(d, 1)], sem).start()
        return carry

    lax.fori_loop(0, n, body, 0, unroll=8)
    for _ in range(TOP_K):
        _row_copy_wait(hp_ref, xb_ref.at[pl.ds(0, n)], sem)


def _dispatch(dest3, hp, n_slots):
    T = hp.shape[0]
    n = SEQ_TILE
    xb0 = jnp.zeros((n_slots, PACK_COLS), jnp.uint32)
    return pl.pallas_call(
        _dispatch_kernel,
        grid=(T // n,),
        in_specs=[
            pl.BlockSpec((1, TOP_K, n), lambda i: (i, 0, 0), memory_space=pltpu.SMEM),
            pl.BlockSpec((n, PACK_COLS), lambda i: (i, 0)),
            pl.BlockSpec(memory_space=pl.ANY),
        ],
        out_specs=pl.BlockSpec(memory_space=pl.ANY),
        out_shape=jax.ShapeDtypeStruct((n_slots, PACK_COLS), jnp.uint32),
        scratch_shapes=[pltpu.SemaphoreType.DMA],
        input_output_aliases={2: 0},
        compiler_params=_params("arbitrary"),
        name="dispatch",
    )(dest3, hp, xb0)


def _unpack_rows(packed):
    lo = lax.bitcast_convert_type(packed << 16, F32)
    hi = lax.bitcast_convert_type(packed & jnp.uint32(0xFFFF0000), F32)
    return lo, hi


def _pack_rows(y):
    bits = lax.bitcast_convert_type(y.astype(jnp.bfloat16).astype(F32), jnp.uint32)
    return (bits[:, :PACK_COLS] >> 16) | bits[:, PACK_COLS:]


def _expert_kernel(be_ref, nused_ref, x_ref, wg_ref, wu_ref, wd_ref, y_ref, wg_b, wu_b, wd_b):
    p = pl.program_id(0)
    new_expert = jnp.logical_or(p == 0, be_ref[p] != be_ref[jnp.maximum(p - 1, 0)])

    @pl.when(new_expert)
    def _():
        wg_b[...] = wg_ref[...].astype(MXU_DTYPE)
        wu_b[...] = wu_ref[...].astype(MXU_DTYPE)
        wd_b[...] = wd_ref[...].astype(MXU_DTYPE)

    @pl.when(p < nused_ref[0])
    def _():
        lo, hi = _unpack_rows(x_ref[...])
        x = jnp.concatenate([lo.astype(MXU_DTYPE), hi.astype(MXU_DTYPE)], axis=1)
        gate = _dot(x, wg_b[...])
        up = _dot(x, wu_b[...])
        hid = (gate * _sigmoid(gate) * up).astype(MXU_DTYPE)
        y_ref[...] = _pack_rows(_dot(hid, wd_b[...]))

    @pl.when(p >= nused_ref[0])
    def _():
        y_ref[...] = jnp.zeros_like(y_ref)


def _experts(block_expert, n_used, xb, wg, wu, wd):
    n_slots = xb.shape[0]
    nb = n_slots // ROUTE_BLOCK
    grid_spec = pltpu.PrefetchScalarGridSpec(
        num_scalar_prefetch=2,
        grid=(nb,),
        in_specs=[
            pl.BlockSpec((ROUTE_BLOCK, PACK_COLS), lambda p, be, nu: (p, 0)),
            pl.BlockSpec((None, D_MODEL, D_EXPERT), lambda p, be, nu: (be[p], 0, 0)),
            pl.BlockSpec((None, D_MODEL, D_EXPERT), lambda p, be, nu: (be[p], 0, 0)),
            pl.BlockSpec((None, D_EXPERT, D_MODEL), lambda p, be, nu: (be[p], 0, 0)),
        ],
        out_specs=pl.BlockSpec((ROUTE_BLOCK, PACK_COLS), lambda p, be, nu: (p, 0)),
        scratch_shapes=[pltpu.VMEM((D_MODEL, D_EXPERT), MXU_DTYPE), pltpu.VMEM((D_MODEL, D_EXPERT), MXU_DTYPE),
                        pltpu.VMEM((D_EXPERT, D_MODEL), MXU_DTYPE)],
    )
    return pl.pallas_call(
        _expert_kernel,
        grid_spec=grid_spec,
        out_shape=jax.ShapeDtypeStruct((n_slots, PACK_COLS), jnp.uint32),
        compiler_params=_params("arbitrary"),
        name="experts",
    )(block_expert, n_used, xb, wg, wu, wd)


def _combine_kernel(dest_ref, yb_ref, rr_ref, h_ref, g_ref, b_ref, o_ref, buf, sem):
    n = SEQ_TILE

    def body(r, carry):
        for k in range(TOP_K):
            d = dest_ref[0, k, r]
            pltpu.make_async_copy(yb_ref.at[pl.ds(d, 1)], buf.at[k, pl.ds(r, 1)], sem).start()
        return carry

    lax.fori_loop(0, n, body, 0, unroll=8)
    for k in range(TOP_K):
        _row_copy_wait(yb_ref.at[pl.ds(0, n)], buf.at[k], sem)

    rr = rr_ref[...]
    y_lo = jnp.zeros((n, PACK_COLS), F32)
    y_hi = jnp.zeros((n, PACK_COLS), F32)
    for k in range(TOP_K):
        w = rr[:, 2 + k:3 + k]
        lo, hi = _unpack_rows(buf[k])
        y_lo = y_lo + w * lo
        y_hi = y_hi + w * hi
    y = jnp.concatenate([y_lo, y_hi], axis=1)
    o_ref[...] = _layer_norm(DEEPNORM_ALPHA * h_ref[...] + y, g_ref[...], b_ref[...])


def _combine(dest3, yb, rr, h2, ln_g, ln_b):
    T = h2.shape[0]
    n = SEQ_TILE
    row = lambda i: (i, 0)
    return pl.pallas_call(
        _combine_kernel,
        grid=(T // n,),
        in_specs=[
            pl.BlockSpec((1, TOP_K, n), lambda i: (i, 0, 0), memory_space=pltpu.SMEM),
            pl.BlockSpec(memory_space=pl.ANY),
            pl.BlockSpec((n, LANES), row),
            pl.BlockSpec((n, D_MODEL), row),
            pl.BlockSpec((1, D_MODEL), lambda i: (0, 0)),
            pl.BlockSpec((1, D_MODEL), lambda i: (0, 0)),
        ],
        out_specs=pl.BlockSpec((n, D_MODEL), row),
        out_shape=jax.ShapeDtypeStruct((T, D_MODEL), F32),
        scratch_shapes=[pltpu.VMEM((TOP_K, n, PACK_COLS), jnp.uint32), pltpu.SemaphoreType.DMA],
        compiler_params=_params("arbitrary"),
        name="combine",
    )(dest3, yb, rr, h2, ln_g, ln_b)


def _router_weights(w_group, b_group, w_route, b_route):
    wr = jnp.zeros((ROUTER_ROWS, D_MODEL), F32)
    wr = wr.at[GROUP_ROW0:GROUP_ROW0 + N_GROUPS].set(w_group.T)
    wr = wr.at[EXPERT_ROW0:EXPERT_ROW0 + N_EXPERTS].set(w_route.T)
    br = jnp.zeros((ROUTER_ROWS,), F32)
    br = br.at[GROUP_ROW0:GROUP_ROW0 + N_GROUPS].set(b_group)
    br = br.at[EXPERT_ROW0:EXPERT_ROW0 + N_EXPERTS].set(b_route.reshape(-1))
    return wr.astype(MXU_DTYPE), br.reshape(ROUTER_ROWS, 1)


def kernel(x, mem, positions, ln_in_g, ln_in_b, w_in, rel_bias, w_proj_ret, w_proj_att, w_out, ln1_g, ln1_b, w_q_mem, w_kv_mem, w_o_mem, ln2_g, ln2_b, w_group, b_group, w_route, b_route, w_gate, w_up, w_down, ln3_g, ln3_b):
    B, S, D = x.shape
    assert D == D_MODEL and S % PROJ_TILE == 0 and S % SEQ_TILE == 0
    T = B * S
    A = T * TOP_K
    n_blocks = -(-(A + N_EXPERTS * (ROUTE_BLOCK - 1)) // ROUTE_BLOCK)
    n_slots = n_blocks * ROUTE_BLOCK
    bf = MXU_DTYPE

    cos_t, sin_t = _rope_tables(positions)
    tables = _retention_tables()
    tri = jnp.asarray(np.triu(np.ones((SEQ_TILE, SEQ_TILE), np.float32), 1), bf)
    incl = jnp.asarray(np.tril(np.ones((N_EXPERTS, N_EXPERTS), bool)))
    mem2d = mem.reshape(-1, D)
    row2 = lambda v: v.reshape(1, D)

    h = x.reshape(T, D)
    for l in range(DEPTH):
        w_re = w_in[l].astype(bf)
        if l == 0:
            z, h = _inproj(h, row2(ln_in_g), row2(ln_in_b), w_re, cos_t, sin_t, True)
        else:
            (z,) = _inproj(h, row2(ln_in_g), row2(ln_in_b), w_re, cos_t, sin_t, False)
        bias = _attention_bias(rel_bias[l])
        h = _mixer(z, h, B, S, tables, bias, w_proj_ret[l].astype(bf), w_proj_att[l].astype(bf),
                   w_out[l].astype(bf), row2(ln1_g[l]), row2(ln1_b[l]))

        kv = _mem_kv(mem2d, w_kv_mem[l].astype(bf))
        wr_t, br_col = _router_weights(w_group[l], b_group[l], w_route[l], b_route[l])
        h2, hp, rl, rr, counts = _memattn(h, kv, B, S, w_q_mem[l].astype(bf), w_o_mem[l].astype(bf),
                                          row2(ln2_g[l]), row2(ln2_b[l]), wr_t, br_col, tri)

        cnt = counts[:, 0].astype(jnp.int32)
        padded = ((cnt + ROUTE_BLOCK - 1) // ROUTE_BLOCK) * ROUTE_BLOCK
        pend = jnp.sum(jnp.where(incl, padded[None, :], 0), axis=1)
        pstart = pend - padded
        block_first = jnp.arange(n_blocks, dtype=jnp.int32) * ROUTE_BLOCK
        block_expert = jnp.minimum(jnp.sum((pend[None, :] <= block_first[:, None]).astype(jnp.int32), axis=1),
                                   N_EXPERTS - 1)
        n_used = (pend[-1:] // ROUTE_BLOCK).astype(jnp.int32)
        pstart_b = jnp.broadcast_to(pstart.astype(F32)[:, None], (N_EXPERTS, LANES))

        dest = _dest_slots(rl, pstart_b)
        dest3 = dest.reshape(TOP_K, T // SEQ_TILE, SEQ_TILE).transpose(1, 0, 2)
        xb = _dispatch(dest3, hp, n_slots)
        yb = _experts(block_expert, n_used, xb, w_gate[l], w_up[l], w_down[l])
        h = _combine(dest3, yb, rr, h2, row2(ln3_g[l]), row2(ln3_b[l]))
    return h.reshape(B, S, D)
```

```python
import functools

import jax
import jax.numpy as jnp
import numpy as np
from jax import lax
from jax.experimental import pallas as pl
from jax.experimental.pallas import tpu as pltpu

D_MODEL = 1024
DEPTH = 2
CHUNK = 64
RET_HEADS = 8
RET_DK = 64
RET_DV = 128
RET_QK = RET_HEADS * RET_DK
RET_V = RET_HEADS * RET_DV
ROPE_BASE = 10000.0
ATT_HEADS = 8
ATT_DH = 64
ATT_W = ATT_HEADS * ATT_DH
LEFT_CHUNKS = 8
MAX_REL = 256
MEM_HEADS = 4
MEM_DH = D_MODEL // MEM_HEADS
N_GROUPS = 4
EXPERTS_PER_GROUP = 8
N_EXPERTS = N_GROUPS * EXPERTS_PER_GROUP
TOP_K = 2
D_EXPERT = 512
ROUTE_BLOCK = 256
LN_EPS = 1e-5
DEEPNORM_ALPHA = (2.0 * DEPTH) ** 0.25
IN_TOTAL = 2 * RET_QK + 2 * RET_V + 3 * ATT_W + 2 * D_MODEL

LANES = 128
SUBLANES = 8
VMEM_LIMIT_BYTES = 56 * 1024 * 1024

SEQ_TILE = 256
PROJ_TILE = 512
PROJ_COLS = 512
HEAD_PAIR = 2 * RET_DK
NEG_BIG = -1e30

F32 = jnp.float32
MXU_DTYPE = jnp.bfloat16
PACK_COLS = D_MODEL // 2

Z_QK = 0
Z_VR = Z_QK + 2 * RET_QK
Z_GR = Z_VR + RET_V
Z_QA = Z_GR + RET_V
Z_KA = Z_QA + ATT_W
Z_VA = Z_KA + ATT_W
Z_GATE_R = Z_VA + ATT_W
Z_GATE_A = Z_GATE_R + D_MODEL
GATE_HALF = D_MODEL // 2


def _layer_norm(x, g, b):
    mu = jnp.mean(x, axis=-1, keepdims=True)
    xc = x - mu
    var = jnp.mean(xc * xc, axis=-1, keepdims=True)
    return xc * lax.rsqrt(var + LN_EPS) * g + b


def _sigmoid(x):
    return 1.0 / (1.0 + jnp.exp(-x))


def _dot(a, b):
    return jnp.dot(a, b, preferred_element_type=F32)


def _dot_nt(a, b):
    return lax.dot_general(a, b, (((1,), (1,)), ((), ())), preferred_element_type=F32)


def _dot_tn(a, b):
    return lax.dot_general(a, b, (((0,), (0,)), ((), ())), preferred_element_type=F32)


def _params(*semantics):
    return pltpu.CompilerParams(dimension_semantics=semantics, vmem_limit_bytes=VMEM_LIMIT_BYTES)


def _cast_kernel(x_ref, o_ref):
    o_ref[...] = x_ref[...].astype(o_ref.dtype)


def _to_mxu(w_stack, layer):
    _, r, c = w_stack.shape
    tr = 256
    return pl.pallas_call(
        _cast_kernel,
        grid=(r // tr,),
        in_specs=[pl.BlockSpec((None, tr, c), lambda i: (layer, i, 0))],
        out_specs=pl.BlockSpec((tr, c), lambda i: (i, 0)),
        out_shape=jax.ShapeDtypeStruct((r, c), MXU_DTYPE),
        compiler_params=_params("parallel"),
        name="cast_weight",
    )(w_stack)


def _rope_kernel(pos_ref, invf_ref, cos_ref, sin_ref):
    ang = pos_ref[...] * invf_ref[...]
    lane = lax.broadcasted_iota(jnp.int32, ang.shape, 1)
    s = jnp.sin(ang)
    cos_ref[...] = jnp.cos(ang)
    sin_ref[...] = jnp.where((lane % RET_DK) < RET_DK // 2, -s, s)


def _rope_tables(positions):
    T = positions.size
    half = RET_DK // 2
    inv_freq = 1.0 / (ROPE_BASE ** jnp.linspace(0.0, 1.0, half, dtype=F32))
    invf = jnp.tile(inv_freq, LANES // half).reshape(1, LANES)
    pos = jnp.broadcast_to(positions.reshape(T, 1).astype(F32), (T, LANES))
    tm = 1024
    spec = pl.BlockSpec((tm, LANES), lambda i: (i, 0))
    return pl.pallas_call(
        _rope_kernel,
        grid=(T // tm,),
        in_specs=[spec, pl.BlockSpec((1, LANES), lambda i: (0, 0))],
        out_specs=[spec, spec],
        out_shape=[jax.ShapeDtypeStruct((T, LANES), F32)] * 2,
        compiler_params=_params("parallel"),
        name="rope_tables",
    )(pos, invf)


def _rotary(a, cos, sin_signed):
    outs = []
    lane = lax.broadcasted_iota(jnp.int32, cos.shape, 1)
    first_half = (lane % RET_DK) < RET_DK // 2
    for g in range(a.shape[1] // LANES):
        x = a[:, g * LANES:(g + 1) * LANES]
        rot = jnp.where(first_half, pltpu.roll(x, LANES - RET_DK // 2, 1), pltpu.roll(x, RET_DK // 2, 1))
        outs.append(x * cos + rot * sin_signed)
    return jnp.concatenate(outs, axis=1)


def _inproj_kernel(apply_ln, h_ref, g_ref, b_ref, w_ref, cos_ref, sin_ref, z_ref, *h_out):
    x = h_ref[...]
    if apply_ln:
        x = _layer_norm(x, g_ref[...], b_ref[...])
        h_out[0][...] = x
    xb = x.astype(MXU_DTYPE)
    cos = cos_ref[...]
    sin = sin_ref[...]
    for c in range(IN_TOTAL // PROJ_COLS):
        lo = c * PROJ_COLS
        acc = _dot(xb, w_ref[:, lo:lo + PROJ_COLS])
        if lo < 2 * RET_QK:
            acc = _rotary(acc, cos, sin)
            if lo >= RET_QK:
                acc = acc * (RET_DK ** -0.5)
        z_ref[:, lo:lo + PROJ_COLS] = acc.astype(z_ref.dtype)


def _inproj(h, ln_g, ln_b, w_in_b, cos_t, sin_t, apply_ln):
    T = h.shape[0]
    tm = PROJ_TILE
    row = lambda i: (i, 0)
    const = lambda i: (0, 0)
    out_shape = [jax.ShapeDtypeStruct((T, IN_TOTAL), MXU_DTYPE)]
    out_specs = [pl.BlockSpec((tm, IN_TOTAL), row)]
    if apply_ln:
        out_shape.append(jax.ShapeDtypeStruct((T, D_MODEL), F32))
        out_specs.append(pl.BlockSpec((tm, D_MODEL), row))
    res = pl.pallas_call(
        functools.partial(_inproj_kernel, apply_ln),
        grid=(T // tm,),
        in_specs=[
            pl.BlockSpec((tm, D_MODEL), row),
            pl.BlockSpec((1, D_MODEL), const),
            pl.BlockSpec((1, D_MODEL), const),
            pl.BlockSpec((D_MODEL, IN_TOTAL), const, pipeline_mode=pl.Buffered(1)),
            pl.BlockSpec((tm, LANES), row),
            pl.BlockSpec((tm, LANES), row),
        ],
        out_specs=out_specs,
        out_shape=out_shape,
        compiler_params=_params("parallel"),
        name="inproj",
    )(h, ln_g, ln_b, w_in_b, cos_t, sin_t)
    return res


def _retention_tables():
    n = SEQ_TILE
    log_g = np.log1p(-np.exp2(-5.0 - np.arange(RET_HEADS, dtype=np.float64)))
    i = np.arange(n)
    same = (i[:, None] // CHUNK) == (i[None, :] // CHUNK)
    earlier = (i[None, :] // CHUNK) < (i[:, None] // CHUNK)
    diff = i[:, None] - i[None, :]
    expo = np.where(same, np.abs(diff), np.where(earlier, diff, 0)).astype(np.float64)
    decay = np.exp(log_g[:, None, None] * expo) * (same | earlier)
    q_decay = np.exp(log_g[:, None] * (i + 1.0))
    k_decay = np.exp(log_g[:, None] * (n - 1.0 - i))
    tile_decay = np.exp(log_g * n)
    wide = lambda t: jnp.asarray(np.broadcast_to(t[:, :, None], t.shape + (HEAD_PAIR,)), F32)
    return jnp.asarray(decay, F32), wide(q_decay), wide(k_decay), tuple(float(v) for v in tile_decay)


def _attention_bias(rel_bias):
    n = SEQ_TILE
    width = 3 * n
    pad = LEFT_CHUNKS * CHUNK
    assert pad == 2 * n and n <= MAX_REL
    heads = rel_bias.shape[0]
    length = width + n
    near = rel_bias[:, MAX_REL - n:].astype(F32)
    far = jnp.broadcast_to(rel_bias[:, -1:].astype(F32), (heads, length - near.shape[1]))
    g_rev = jnp.concatenate([near, far], axis=1)[:, ::-1]
    rows = jnp.tile(g_rev, (1, n + 1))[:, :n * (length - 1)].reshape(heads, n, length - 1)
    bias = rows[:, :, n - 1:n - 1 + width]
    qi = np.arange(n)
    kj = np.arange(width)
    q_chunk = (pad + qi) // CHUNK
    k_chunk = kj // CHUNK
    in_band = (k_chunk[None, :] <= q_chunk[:, None]) & (k_chunk[None, :] >= q_chunk[:, None] - LEFT_CHUNKS)
    return jnp.where(jnp.asarray(in_band)[None], bias, NEG_BIG)


def _mixer_kernel(tile_decay,
                  qk_ref, vr_ref, gr_ref, gate_r0_ref, gate_r1_ref, gate_a0_ref, gate_a1_ref, qa_ref,
                  k2_ref, k1_ref, k0_ref, v2_ref, v1_ref, v0_ref,
                  h_ref, decay_ref, qdec_ref, kdec_ref, bias_ref,
                  wpr_ref, wpa_ref, wout_ref, g_ref, b_ref,
                  o_ref, state_ref, yr_ref, ya_ref):
    i = pl.program_id(1)

    @pl.when(i == 0)
    def _():
        state_ref[...] = jnp.zeros_like(state_ref)

    n = SEQ_TILE
    lane = lax.broadcasted_iota(jnp.int32, (n, HEAD_PAIR), 1)
    low_half = lane < RET_DK

    for hd in range(RET_HEADS):
        p, odd = divmod(hd, 2)
        mine = low_half if odd == 0 else jnp.logical_not(low_half)
        q2 = qk_ref[:, p * HEAD_PAIR:(p + 1) * HEAD_PAIR]
        k2 = qk_ref[:, RET_QK + p * HEAD_PAIR:RET_QK + (p + 1) * HEAD_PAIR]
        qm = jnp.where(mine, q2, jnp.zeros_like(q2))
        km = jnp.where(mine, k2, jnp.zeros_like(k2))
        v = vr_ref[:, hd * RET_DV:(hd + 1) * RET_DV]
        scores = (_dot_nt(qm, k2) * decay_ref[hd]).astype(MXU_DTYPE)
        intra = _dot(scores, v)
        q_dec = (qm.astype(F32) * qdec_ref[hd]).astype(MXU_DTYPE)
        state = state_ref[hd]
        cross = _dot(q_dec, state.astype(MXU_DTYPE))
        k_dec = (km.astype(F32) * kdec_ref[hd]).astype(MXU_DTYPE)
        state_ref[hd] = state * tile_decay[hd] + _dot_tn(k_dec, v)
        ret = intra + cross
        mu = jnp.mean(ret, axis=-1, keepdims=True)
        rc = ret - mu
        var = jnp.mean(rc * rc, axis=-1, keepdims=True)
        gn = rc * lax.rsqrt(var + LN_EPS)
        gate = gr_ref[:, hd * RET_DV:(hd + 1) * RET_DV].astype(F32)
        yr_ref[:, hd * RET_DV:(hd + 1) * RET_DV] = (gate * _sigmoid(gate) * gn).astype(MXU_DTYPE)

    pen2 = jnp.where(i >= 2, 0.0, NEG_BIG).astype(F32)
    pen1 = jnp.where(i >= 1, 0.0, NEG_BIG).astype(F32)
    k_refs = (k2_ref, k1_ref, k0_ref)
    v_refs = (v2_ref, v1_ref, v0_ref)
    pens = (pen2, pen1, None)
    for p in range(ATT_HEADS // 2):
        cols = slice(p * HEAD_PAIR, (p + 1) * HEAD_PAIR)
        q2 = qa_ref[:, cols]
        pair_out = []
        for odd in range(2):
            hd = 2 * p + odd
            mine = low_half if odd == 0 else jnp.logical_not(low_half)
            qm = jnp.where(mine, q2 * (ATT_DH ** -0.5), jnp.zeros_like(q2))
            s = []
            for kb in range(3):
                sk = _dot_nt(qm, k_refs[kb][:, cols]) + bias_ref[hd, :, kb * n:(kb + 1) * n]
                if pens[kb] is not None:
                    sk = sk + pens[kb]
                s.append(sk)
            m = jnp.maximum(jnp.maximum(jnp.max(s[0], axis=-1, keepdims=True),
                                        jnp.max(s[1], axis=-1, keepdims=True)),
                            jnp.max(s[2], axis=-1, keepdims=True))
            e = [jnp.exp(sk - m) for sk in s]
            denom = (jnp.sum(e[0], axis=-1, keepdims=True) + jnp.sum(e[1], axis=-1, keepdims=True)
                     + jnp.sum(e[2], axis=-1, keepdims=True))
            pv = (_dot(e[0].astype(MXU_DTYPE), v_refs[0][:, cols])
                  + _dot(e[1].astype(MXU_DTYPE), v_refs[1][:, cols])
                  + _dot(e[2].astype(MXU_DTYPE), v_refs[2][:, cols]))
            pair_out.append(pv / denom)
        ya_ref[:, cols] = jnp.where(low_half, pair_out[0], pair_out[1]).astype(MXU_DTYPE)

    pr = _dot(yr_ref[...], wpr_ref[...])
    pa = _dot(ya_ref[...], wpa_ref[...])
    gate_r = jnp.concatenate([gate_r0_ref[...], gate_r1_ref[...]], axis=1).astype(F32)
    gate_a = jnp.concatenate([gate_a0_ref[...], gate_a1_ref[...]], axis=1).astype(F32)
    merged = _sigmoid(gate_r) * pr + _sigmoid(gate_a) * pa
    mix = _dot(merged.astype(MXU_DTYPE), wout_ref[...])
    o_ref[...] = _layer_norm(DEEPNORM_ALPHA * h_ref[...] + mix, g_ref[...], b_ref[...])


def _mixer(z, h, B, S, tables, bias, wpr, wpa, wout, ln_g, ln_b):
    decay, q_decay, k_decay, tile_decay = tables
    T = B * S
    n = SEQ_TILE
    nt = S // n

    def zspec(width, col_off, back=0):
        cb = col_off // width
        return pl.BlockSpec((n, width), lambda b, i: (b * nt + jnp.maximum(i - back, 0), cb))

    row = lambda b, i: (b * nt + i, 0)
    c2 = lambda b, i: (0, 0)
    c3 = lambda b, i: (0, 0, 0)
    one = pl.Buffered(1)
    in_specs = [
        zspec(2 * RET_QK, Z_QK), zspec(RET_V, Z_VR), zspec(RET_V, Z_GR),
        zspec(GATE_HALF, Z_GATE_R), zspec(GATE_HALF, Z_GATE_R + GATE_HALF),
        zspec(GATE_HALF, Z_GATE_A), zspec(GATE_HALF, Z_GATE_A + GATE_HALF), zspec(ATT_W, Z_QA),
        zspec(ATT_W, Z_KA, 2), zspec(ATT_W, Z_KA, 1), zspec(ATT_W, Z_KA, 0),
        zspec(ATT_W, Z_VA, 2), zspec(ATT_W, Z_VA, 1), zspec(ATT_W, Z_VA, 0),
        pl.BlockSpec((n, D_MODEL), row),
        pl.BlockSpec((RET_HEADS, n, n), c3, pipeline_mode=one),
        pl.BlockSpec((RET_HEADS, n, HEAD_PAIR), c3, pipeline_mode=one),
        pl.BlockSpec((RET_HEADS, n, HEAD_PAIR), c3, pipeline_mode=one),
        pl.BlockSpec((ATT_HEADS, n, 3 * n), c3, pipeline_mode=one),
        pl.BlockSpec((RET_V, D_MODEL), c2, pipeline_mode=one),
        pl.BlockSpec((ATT_W, D_MODEL), c2, pipeline_mode=one),
        pl.BlockSpec((D_MODEL, D_MODEL), c2, pipeline_mode=one),
        pl.BlockSpec((1, D_MODEL), c2),
        pl.BlockSpec((1, D_MODEL), c2),
    ]
    return pl.pallas_call(
        functools.partial(_mixer_kernel, tile_decay),
        grid=(B, nt),
        in_specs=in_specs,
        out_specs=pl.BlockSpec((n, D_MODEL), row),
        out_shape=jax.ShapeDtypeStruct((T, D_MODEL), F32),
        scratch_shapes=[
            pltpu.VMEM((RET_HEADS, HEAD_PAIR, RET_DV), F32),
            pltpu.VMEM((n, RET_V), MXU_DTYPE),
            pltpu.VMEM((n, ATT_W), MXU_DTYPE),
        ],
        compiler_params=_params("arbitrary", "arbitrary"),
        name="mixer",
    )(*([z] * 14), h, decay, q_decay, k_decay, bias, wpr, wpa, wout, ln_g, ln_b)


def _kv_kernel(m_ref, w_ref, o_ref):
    o_ref[...] = _dot(m_ref[...].astype(MXU_DTYPE), w_ref[...]).astype(o_ref.dtype)


def _mem_kv(mem2d, w_kv):
    M = mem2d.shape[0]
    tm = 256
    return pl.pallas_call(
        _kv_kernel,
        grid=(M // tm,),
        in_specs=[pl.BlockSpec((tm, D_MODEL), lambda i: (i, 0)),
                  pl.BlockSpec((D_MODEL, 2 * D_MODEL), lambda i: (0, 0), pipeline_mode=pl.Buffered(1))],
        out_specs=pl.BlockSpec((tm, 2 * D_MODEL), lambda i: (i, 0)),
        out_shape=jax.ShapeDtypeStruct((M, 2 * D_MODEL), MXU_DTYPE),
        compiler_params=_params("parallel"),
        name="mem_kv",
    )(mem2d, w_kv)


ROUTE_ROWS = 8
GROUP_ROW0 = 0
EXPERT_ROW0 = 8
ROUTER_ROWS = 128


def _memattn_kernel(h_ref, kv_ref, wq_ref, wo_ref, g_ref, b_ref, wr_ref, br_ref, tri_ref,
                    h2_ref, rl_ref, rr_ref, cnt_ref, o_scr):
    n = SEQ_TILE
    h1 = h_ref[...]
    q = (_dot(h1.astype(MXU_DTYPE), wq_ref[...]) * (MEM_DH ** -0.5)).astype(MXU_DTYPE)
    for hd in range(MEM_HEADS):
        cols = slice(hd * MEM_DH, (hd + 1) * MEM_DH)
        s = _dot_nt(q[:, cols], kv_ref[:, cols])
        m = jnp.max(s, axis=-1, keepdims=True)
        e = jnp.exp(s - m)
        denom = jnp.sum(e, axis=-1, keepdims=True)
        v = kv_ref[:, D_MODEL + hd * MEM_DH:D_MODEL + (hd + 1) * MEM_DH]
        o_scr[:, cols] = (_dot(e.astype(MXU_DTYPE), v) / denom).astype(MXU_DTYPE)
    cross = _dot(o_scr[...], wo_ref[...])
    h2 = _layer_norm(DEEPNORM_ALPHA * h1 + cross, g_ref[...], b_ref[...])
    h2_ref[...] = h2

    logits = _dot_nt(wr_ref[...], h2.astype(MXU_DTYPE)) + br_ref[...]
    glog = logits[GROUP_ROW0:GROUP_ROW0 + N_GROUPS]
    gmax = jnp.max(glog, axis=0, keepdims=True)
    g_w = 1.0 / jnp.sum(jnp.exp(glog - gmax), axis=0, keepdims=True)
    giota = lax.broadcasted_iota(jnp.int32, glog.shape, 0)
    g_idx = jnp.min(jnp.where(glog == gmax, giota, N_GROUPS), axis=0, keepdims=True)
    el = jnp.zeros((EXPERTS_PER_GROUP, n), F32)
    for g in range(N_GROUPS):
        r0 = EXPERT_ROW0 + g * EXPERTS_PER_GROUP
        el = jnp.where(g_idx == g, logits[r0:r0 + EXPERTS_PER_GROUP], el)
    eiota = lax.broadcasted_iota(jnp.int32, el.shape, 0)
    m1 = jnp.max(el, axis=0, keepdims=True)
    i1 = jnp.min(jnp.where(el == m1, eiota, EXPERTS_PER_GROUP), axis=0, keepdims=True)
    el2 = jnp.where(eiota == i1, -jnp.inf, el)
    m2 = jnp.max(el2, axis=0, keepdims=True)
    i2 = jnp.min(jnp.where(el2 == m2, eiota, EXPERTS_PER_GROUP), axis=0, keepdims=True)
    r = jnp.exp(m2 - m1)
    w1 = g_w / (1.0 + r)
    w2 = g_w * r / (1.0 + r)
    e1 = g_idx * EXPERTS_PER_GROUP + i1
    e2 = g_idx * EXPERTS_PER_GROUP + i2

    xiota = lax.broadcasted_iota(jnp.int32, (N_EXPERTS, n), 0)
    oh1 = xiota == e1
    oh2 = xiota == e2
    cnt = jnp.where(oh1, 1.0, 0.0) + jnp.where(oh2, 1.0, 0.0)
    before = _dot(cnt.astype(MXU_DTYPE), tri_ref[...])
    rank1 = jnp.sum(jnp.where(oh1, before, 0.0), axis=0, keepdims=True)
    rank2 = jnp.sum(jnp.where(oh2, before, 0.0), axis=0, keepdims=True)
    cnt_ref[0] = jnp.broadcast_to(jnp.sum(cnt, axis=1, keepdims=True), (N_EXPERTS, LANES))

    zero = jnp.zeros((1, n), F32)
    rec = jnp.concatenate([e1.astype(F32), e2.astype(F32), w1, w2, rank1, rank2, zero, zero], axis=0)
    rl_ref[...] = rec
    rec_full = jnp.concatenate([rec, jnp.zeros((LANES - ROUTE_ROWS, n), F32)], axis=0)
    rr_ref[...] = rec_full.T


def _memattn(h1, kv, B, S, wq, wo, ln_g, ln_b, wr_t, br_col, tri):
    T = B * S
    n = SEQ_TILE
    nt = S // n
    row = lambda b, i: (b * nt + i, 0)
    c2 = lambda b, i: (0, 0)
    one = pl.Buffered(1)
    return pl.pallas_call(
        _memattn_kernel,
        grid=(B, nt),
        in_specs=[
            pl.BlockSpec((n, D_MODEL), row),
            pl.BlockSpec((kv.shape[0] // B, 2 * D_MODEL), lambda b, i: (b, 0)),
            pl.BlockSpec((D_MODEL, D_MODEL), c2, pipeline_mode=one),
            pl.BlockSpec((D_MODEL, D_MODEL), c2, pipeline_mode=one),
            pl.BlockSpec((1, D_MODEL), c2),
            pl.BlockSpec((1, D_MODEL), c2),
            pl.BlockSpec((ROUTER_ROWS, D_MODEL), c2, pipeline_mode=one),
            pl.BlockSpec((ROUTER_ROWS, 1), c2),
            pl.BlockSpec((n, n), c2, pipeline_mode=one),
        ],
        out_specs=[
            pl.BlockSpec((n, D_MODEL), row),
            pl.BlockSpec((ROUTE_ROWS, n), lambda b, i: (0, b * nt + i)),
            pl.BlockSpec((n, LANES), row),
            pl.BlockSpec((1, N_EXPERTS, LANES), lambda b, i: (b * nt + i, 0, 0)),
        ],
        out_shape=[
            jax.ShapeDtypeStruct((T, D_MODEL), F32),
            jax.ShapeDtypeStruct((ROUTE_ROWS, T), F32),
            jax.ShapeDtypeStruct((T, LANES), F32),
            jax.ShapeDtypeStruct((T // n, N_EXPERTS, LANES), F32),
        ],
        scratch_shapes=[pltpu.VMEM((n, D_MODEL), MXU_DTYPE)],
        compiler_params=_params("parallel", "parallel"),
        name="memattn_router",
    )(h1, kv, wq, wo, ln_g, ln_b, wr_t, br_col, tri)


CHUNK_ROWS = SUBLANES
LOCAL_ROWS = 768
N_CHUNKS = LOCAL_ROWS // CHUNK_ROWS
assert TOP_K * SEQ_TILE + N_EXPERTS * (CHUNK_ROWS - 1) <= LOCAL_ROWS


def _unpack_rows(packed):
    lo = lax.bitcast_convert_type(packed << 16, F32)
    hi = lax.bitcast_convert_type(packed & jnp.uint32(0xFFFF0000), F32)
    return lo, hi


def _pack_rows(y):
    bits = lax.bitcast_convert_type(y.astype(jnp.bfloat16).astype(F32), jnp.uint32)
    return (bits[:, :PACK_COLS] >> 16) | bits[:, PACK_COLS:]


def _chunk_plan(counts, n_blocks):
    nt = counts.shape[0]
    c = counts.astype(jnp.int32)
    incl = jnp.asarray(np.tril(np.ones((N_EXPERTS, N_EXPERTS), bool)))
    padc = (c + CHUNK_ROWS - 1) // CHUNK_ROWS * CHUNK_ROWS
    lend = jnp.sum(jnp.where(incl[None], padc[:, None, :], 0), axis=2)
    loff = lend - padc
    earlier = jnp.asarray(np.tril(np.ones((nt, nt), bool), -1))
    cbefore = jnp.sum(jnp.where(earlier[:, :, None], padc[None, :, :], 0), axis=1)
    total = jnp.sum(padc, axis=0)
    region = (total + ROUTE_BLOCK - 1) // ROUTE_BLOCK * ROUTE_BLOCK
    gend = jnp.sum(jnp.where(incl, region[None, :], 0), axis=1)
    gstart = gend - region
    delta = gstart[None, :] + cbefore - loff
    k_row = jnp.arange(N_CHUNKS, dtype=jnp.int32) * CHUNK_ROWS
    e_of_chunk = jnp.sum((lend[:, None, :] <= k_row[None, :, None]).astype(jnp.int32), axis=2)
    used = e_of_chunk < N_EXPERTS
    onehot = e_of_chunk[:, :, None] == jnp.arange(N_EXPERTS, dtype=jnp.int32)[None, None, :]
    grow = jnp.sum(jnp.where(onehot, delta[:, None, :], 0), axis=2) + k_row[None, :]
    dump = n_blocks * ROUTE_BLOCK + k_row[None, :]
    dst = jnp.where(used, grow, dump).reshape(-1)
    src = jnp.where(used, grow, 0).reshape(-1)
    block_first = jnp.arange(n_blocks, dtype=jnp.int32) * ROUTE_BLOCK
    block_expert = jnp.minimum(jnp.sum((gend[None, :] <= block_first[:, None]).astype(jnp.int32), axis=1),
                               N_EXPERTS - 1)
    n_used = (gend[-1:] // ROUTE_BLOCK).astype(jnp.int32)
    loff_f = loff.astype(F32)
    loff_col = jnp.broadcast_to(loff_f[:, :, None], (nt, N_EXPERTS, LANES))
    loff_row = jnp.broadcast_to(jnp.pad(loff_f, ((0, 0), (0, LANES - N_EXPERTS)))[:, None, :],
                                (nt, SUBLANES, LANES))
    return dst, src, block_expert, n_used, loff_col, loff_row


def _dispatch_kernel(dst_ref, h_ref, rl_ref, loff_ref, xb_in_ref, xb_ref, sbuf, sem):
    del xb_in_ref
    i = pl.program_id(0)
    n = SEQ_TILE
    rl = rl_ref[...]
    xiota = lax.broadcasted_iota(jnp.int32, (N_EXPERTS, n), 0)
    loff = loff_ref[0][:, 0:1]
    slots = []
    for k in range(TOP_K):
        e = rl[k:k + 1].astype(jnp.int32)
        base = jnp.sum(jnp.where(xiota == e, loff, 0.0), axis=0, keepdims=True)
        slots.append((base + rl[4 + k:5 + k]).astype(jnp.int32))
    riota = lax.broadcasted_iota(jnp.int32, (LOCAL_ROWS, n), 0)
    perm = jnp.where(riota == slots[0], 1.0, jnp.where(riota == slots[1], 1.0, 0.0)).astype(MXU_DTYPE)
    sbuf[...] = _pack_rows(_dot(perm, h_ref[...].astype(MXU_DTYPE)))

    def body(k, carry):
        d = pl.multiple_of(dst_ref[i * N_CHUNKS + k], CHUNK_ROWS)
        src = sbuf.at[pl.ds(pl.multiple_of(k * CHUNK_ROWS, CHUNK_ROWS), CHUNK_ROWS)]
        pltpu.make_async_copy(src, xb_ref.at[pl.ds(d, CHUNK_ROWS)], sem).start()
        return carry

    lax.fori_loop(0, N_CHUNKS, body, 0, unroll=8)
    pltpu.make_async_copy(sbuf, xb_ref.at[pl.ds(0, LOCAL_ROWS)], sem).wait()


def _dispatch(dst, h2, rl, loff_col, n_blocks):
    T = h2.shape[0]
    n = SEQ_TILE
    rows = n_blocks * ROUTE_BLOCK + LOCAL_ROWS
    xb0 = jnp.zeros((rows, PACK_COLS), jnp.uint32)
    grid_spec = pltpu.PrefetchScalarGridSpec(
        num_scalar_prefetch=1,
        grid=(T // n,),
        in_specs=[
            pl.BlockSpec((n, D_MODEL), lambda i, d: (i, 0)),
            pl.BlockSpec((ROUTE_ROWS, n), lambda i, d: (0, i)),
            pl.BlockSpec((1, N_EXPERTS, LANES), lambda i, d: (i, 0, 0)),
            pl.BlockSpec(memory_space=pl.ANY),
        ],
        out_specs=pl.BlockSpec(memory_space=pl.ANY),
        scratch_shapes=[pltpu.VMEM((LOCAL_ROWS, PACK_COLS), jnp.uint32), pltpu.SemaphoreType.DMA],
    )
    return pl.pallas_call(
        _dispatch_kernel,
        grid_spec=grid_spec,
        out_shape=jax.ShapeDtypeStruct((rows, PACK_COLS), jnp.uint32),
        input_output_aliases={4: 0},
        compiler_params=_params("arbitrary"),
        name="dispatch",
    )(dst, h2, rl, loff_col, xb0)


def _expert_kernel(be_ref, nused_ref, x_ref, wg_ref, wu_ref, wd_ref, y_ref, wg_b, wu_b, wd_b):
    p = pl.program_id(0)
    new_expert = jnp.logical_or(p == 0, be_ref[p] != be_ref[jnp.maximum(p - 1, 0)])

    @pl.when(new_expert)
    def _():
        wg_b[...] = wg_ref[...].astype(MXU_DTYPE)
        wu_b[...] = wu_ref[...].astype(MXU_DTYPE)
        wd_b[...] = wd_ref[...].astype(MXU_DTYPE)

    @pl.when(p < nused_ref[0])
    def _():
        lo, hi = _unpack_rows(x_ref[...])
        x = jnp.concatenate([lo.astype(MXU_DTYPE), hi.astype(MXU_DTYPE)], axis=1)
        gate = _dot(x, wg_b[...])
        up = _dot(x, wu_b[...])
        hid = (gate * _sigmoid(gate) * up).astype(MXU_DTYPE)
        y_ref[...] = _pack_rows(_dot(hid, wd_b[...]))

    @pl.when(p >= nused_ref[0])
    def _():
        y_ref[...] = jnp.zeros_like(y_ref)


def _experts(block_expert, n_used, xb, wg, wu, wd):
    nb = block_expert.shape[0]
    n_slots = nb * ROUTE_BLOCK
    grid_spec = pltpu.PrefetchScalarGridSpec(
        num_scalar_prefetch=2,
        grid=(nb,),
        in_specs=[
            pl.BlockSpec((ROUTE_BLOCK, PACK_COLS), lambda p, be, nu: (p, 0)),
            pl.BlockSpec((None, D_MODEL, D_EXPERT), lambda p, be, nu: (be[p], 0, 0)),
            pl.BlockSpec((None, D_MODEL, D_EXPERT), lambda p, be, nu: (be[p], 0, 0)),
            pl.BlockSpec((None, D_EXPERT, D_MODEL), lambda p, be, nu: (be[p], 0, 0)),
        ],
        out_specs=pl.BlockSpec((ROUTE_BLOCK, PACK_COLS), lambda p, be, nu: (p, 0)),
        scratch_shapes=[pltpu.VMEM((D_MODEL, D_EXPERT), MXU_DTYPE), pltpu.VMEM((D_MODEL, D_EXPERT), MXU_DTYPE),
                        pltpu.VMEM((D_EXPERT, D_MODEL), MXU_DTYPE)],
    )
    return pl.pallas_call(
        _expert_kernel,
        grid_spec=grid_spec,
        out_shape=jax.ShapeDtypeStruct((n_slots, PACK_COLS), jnp.uint32),
        compiler_params=_params("arbitrary"),
        name="experts",
    )(block_expert, n_used, xb, wg, wu, wd)


def _combine_kernel(src_ref, yb_ref, rr_ref, loff_ref, h_ref, g_ref, b_ref, o_ref, ybuf, sem):
    i = pl.program_id(0)
    n = SEQ_TILE

    def body(k, carry):
        s = pl.multiple_of(src_ref[i * N_CHUNKS + k], CHUNK_ROWS)
        dst = ybuf.at[pl.ds(pl.multiple_of(k * CHUNK_ROWS, CHUNK_ROWS), CHUNK_ROWS)]
        pltpu.make_async_copy(yb_ref.at[pl.ds(s, CHUNK_ROWS)], dst, sem).start()
        return carry

    lax.fori_loop(0, N_CHUNKS, body, 0, unroll=8)

    rr = rr_ref[...]
    lane = lax.broadcasted_iota(jnp.int32, (n, LANES), 1)
    loff = loff_ref[0][0:1, :]
    ciota = lax.broadcasted_iota(jnp.int32, (n, LOCAL_ROWS), 1)
    mix = jnp.zeros((n, LOCAL_ROWS), F32)
    for k in range(TOP_K):
        e = rr[:, k:k + 1].astype(jnp.int32)
        base = jnp.sum(jnp.where(lane == e, loff, 0.0), axis=1, keepdims=True)
        slot = (base + rr[:, 4 + k:5 + k]).astype(jnp.int32)
        mix = jnp.where(ciota == slot, rr[:, 2 + k:3 + k], mix)

    pltpu.make_async_copy(yb_ref.at[pl.ds(0, LOCAL_ROWS)], ybuf, sem).wait()
    lo, hi = _unpack_rows(ybuf[...])
    mixb = mix.astype(MXU_DTYPE)
    y = jnp.concatenate([_dot(mixb, lo.astype(MXU_DTYPE)), _dot(mixb, hi.astype(MXU_DTYPE))], axis=1)
    o_ref[...] = _layer_norm(DEEPNORM_ALPHA * h_ref[...] + y, g_ref[...], b_ref[...])


def _combine(src, yb, rr, loff_row, h2, ln_g, ln_b):
    T = h2.shape[0]
    n = SEQ_TILE
    row = lambda i, s: (i, 0)
    grid_spec = pltpu.PrefetchScalarGridSpec(
        num_scalar_prefetch=1,
        grid=(T // n,),
        in_specs=[
            pl.BlockSpec(memory_space=pl.ANY),
            pl.BlockSpec((n, LANES), row),
            pl.BlockSpec((1, SUBLANES, LANES), lambda i, s: (i, 0, 0)),
            pl.BlockSpec((n, D_MODEL), row),
            pl.BlockSpec((1, D_MODEL), lambda i, s: (0, 0)),
            pl.BlockSpec((1, D_MODEL), lambda i, s: (0, 0)),
        ],
        out_specs=pl.BlockSpec((n, D_MODEL), row),
        scratch_shapes=[pltpu.VMEM((LOCAL_ROWS, PACK_COLS), jnp.uint32), pltpu.SemaphoreType.DMA],
    )
    return pl.pallas_call(
        _combine_kernel,
        grid_spec=grid_spec,
        out_shape=jax.ShapeDtypeStruct((T, D_MODEL), F32),
        compiler_params=_params("arbitrary"),
        name="combine",
    )(src, yb, rr, loff_row, h2, ln_g, ln_b)


def _router_weights(w_group, b_group, w_route, b_route):
    wr = jnp.zeros((ROUTER_ROWS, D_MODEL), F32)
    wr = wr.at[GROUP_ROW0:GROUP_ROW0 + N_GROUPS].set(w_group.T)
    wr = wr.at[EXPERT_ROW0:EXPERT_ROW0 + N_EXPERTS].set(w_route.T)
    br = jnp.zeros((ROUTER_ROWS,), F32)
    br = br.at[GROUP_ROW0:GROUP_ROW0 + N_GROUPS].set(b_group)
    br = br.at[EXPERT_ROW0:EXPERT_ROW0 + N_EXPERTS].set(b_route.reshape(-1))
    return wr.astype(MXU_DTYPE), br.reshape(ROUTER_ROWS, 1)


def kernel(x, mem, positions, ln_in_g, ln_in_b, w_in, rel_bias, w_proj_ret, w_proj_att, w_out, ln1_g, ln1_b, w_q_mem, w_kv_mem, w_o_mem, ln2_g, ln2_b, w_group, b_group, w_route, b_route, w_gate, w_up, w_down, ln3_g, ln3_b):
    B, S, D = x.shape
    assert D == D_MODEL and S % PROJ_TILE == 0 and S % SEQ_TILE == 0
    T = B * S
    A = T * TOP_K
    n_blocks = -(-(A + (T // SEQ_TILE) * N_EXPERTS * (CHUNK_ROWS - 1) + N_EXPERTS * (ROUTE_BLOCK - 1)) // ROUTE_BLOCK)
    bf = MXU_DTYPE

    cos_t, sin_t = _rope_tables(positions)
    tables = _retention_tables()
    tri = jnp.asarray(np.triu(np.ones((SEQ_TILE, SEQ_TILE), np.float32), 1), bf)
    mem2d = mem.reshape(-1, D)
    row2 = lambda v: v.reshape(1, D)

    h = x.reshape(T, D)
    for l in range(DEPTH):
        w_in_b = _to_mxu(w_in, l)
        if l == 0:
            z, h = _inproj(h, row2(ln_in_g), row2(ln_in_b), w_in_b, cos_t, sin_t, True)
        else:
            (z,) = _inproj(h, row2(ln_in_g), row2(ln_in_b), w_in_b, cos_t, sin_t, False)
        bias = _attention_bias(rel_bias[l])
        h = _mixer(z, h, B, S, tables, bias, _to_mxu(w_proj_ret, l), _to_mxu(w_proj_att, l),
                   _to_mxu(w_out, l), row2(ln1_g[l]), row2(ln1_b[l]))

        kv = _mem_kv(mem2d, _to_mxu(w_kv_mem, l))
        wr_t, br_col = _router_weights(w_group[l], b_group[l], w_route[l], b_route[l])
        h2, rl, rr, counts = _memattn(h, kv, B, S, _to_mxu(w_q_mem, l), _to_mxu(w_o_mem, l),
                                      row2(ln2_g[l]), row2(ln2_b[l]), wr_t, br_col, tri)

        dst, src, block_expert, n_used, loff_col, loff_row = _chunk_plan(counts[:, :, 0], n_blocks)
        xb = _dispatch(dst, h2, rl, loff_col, n_blocks)
        yb = _experts(block_expert, n_used, xb, w_gate[l], w_up[l], w_down[l])
        h = _combine(src, yb, rr, loff_row, h2, row2(ln3_g[l]), row2(ln3_b[l]))
    return h.reshape(B, S, D)
```

```python
import functools

import jax
import jax.numpy as jnp
import numpy as np
from jax import lax
from jax.experimental import pallas as pl
from jax.experimental.pallas import tpu as pltpu

D_MODEL = 1024
DEPTH = 2
CHUNK = 64
RET_HEADS = 8
RET_DK = 64
RET_DV = 128
RET_QK = RET_HEADS * RET_DK
RET_V = RET_HEADS * RET_DV
ROPE_BASE = 10000.0
ATT_HEADS = 8
ATT_DH = 64
ATT_W = ATT_HEADS * ATT_DH
LEFT_CHUNKS = 8
MAX_REL = 256
MEM_HEADS = 4
MEM_DH = D_MODEL // MEM_HEADS
N_GROUPS = 4
EXPERTS_PER_GROUP = 8
N_EXPERTS = N_GROUPS * EXPERTS_PER_GROUP
TOP_K = 2
D_EXPERT = 512
ROUTE_BLOCK = 256
LN_EPS = 1e-5
DEEPNORM_ALPHA = (2.0 * DEPTH) ** 0.25
IN_TOTAL = 2 * RET_QK + 2 * RET_V + 3 * ATT_W + 2 * D_MODEL

LANES = 128
SUBLANES = 8
VMEM_LIMIT_BYTES = 56 * 1024 * 1024

SEQ_TILE = 256
PROJ_TILE = 512
PROJ_COLS = 512
HEAD_PAIR = 2 * RET_DK
NEG_BIG = -1e30
LOG2E = 1.4426950408889634

F32 = jnp.float32
MXU_DTYPE = jnp.bfloat16
PACK_COLS = D_MODEL // 2

Z_QK = 0
Z_VR = Z_QK + 2 * RET_QK
Z_GR = Z_VR + RET_V
Z_QA = Z_GR + RET_V
Z_KA = Z_QA + ATT_W
Z_VA = Z_KA + ATT_W
Z_GATE_R = Z_VA + ATT_W
Z_GATE_A = Z_GATE_R + D_MODEL
GATE_HALF = D_MODEL // 2


def _layer_norm(x, g, b):
    mu = jnp.mean(x, axis=-1, keepdims=True)
    xc = x - mu
    var = jnp.mean(xc * xc, axis=-1, keepdims=True)
    return xc * lax.rsqrt(var + LN_EPS) * g + b


def _sigmoid(x):
    return 1.0 / (1.0 + jnp.exp(-x))


def _dot(a, b):
    return jnp.dot(a, b, preferred_element_type=F32)


def _dot_nt(a, b):
    return lax.dot_general(a, b, (((1,), (1,)), ((), ())), preferred_element_type=F32)


def _dot_tn(a, b):
    return lax.dot_general(a, b, (((0,), (0,)), ((), ())), preferred_element_type=F32)


def _params(*semantics):
    return pltpu.CompilerParams(dimension_semantics=semantics, vmem_limit_bytes=VMEM_LIMIT_BYTES)


def _cast_kernel(x_ref, o_ref):
    o_ref[...] = x_ref[...].astype(o_ref.dtype)


def _to_mxu(w_stack, layer):
    _, r, c = w_stack.shape
    tr = 256
    return pl.pallas_call(
        _cast_kernel,
        grid=(r // tr,),
        in_specs=[pl.BlockSpec((None, tr, c), lambda i: (layer, i, 0))],
        out_specs=pl.BlockSpec((tr, c), lambda i: (i, 0)),
        out_shape=jax.ShapeDtypeStruct((r, c), MXU_DTYPE),
        compiler_params=_params("parallel"),
        name="cast_weight",
    )(w_stack)


def _rope_kernel(pos_ref, invf_ref, cos_ref, sin_ref):
    ang = pos_ref[...] * invf_ref[...]
    lane = lax.broadcasted_iota(jnp.int32, ang.shape, 1)
    s = jnp.sin(ang)
    cos_ref[...] = jnp.cos(ang)
    sin_ref[...] = jnp.where((lane % RET_DK) < RET_DK // 2, -s, s)


def _rope_tables(positions):
    T = positions.size
    half = RET_DK // 2
    inv_freq = 1.0 / (ROPE_BASE ** jnp.linspace(0.0, 1.0, half, dtype=F32))
    invf = jnp.tile(inv_freq, LANES // half).reshape(1, LANES)
    pos = jnp.broadcast_to(positions.reshape(T, 1).astype(F32), (T, LANES))
    tm = 1024
    spec = pl.BlockSpec((tm, LANES), lambda i: (i, 0))
    return pl.pallas_call(
        _rope_kernel,
        grid=(T // tm,),
        in_specs=[spec, pl.BlockSpec((1, LANES), lambda i: (0, 0))],
        out_specs=[spec, spec],
        out_shape=[jax.ShapeDtypeStruct((T, LANES), F32)] * 2,
        compiler_params=_params("parallel"),
        name="rope_tables",
    )(pos, invf)


def _rotary(a, cos, sin_signed):
    outs = []
    lane = lax.broadcasted_iota(jnp.int32, cos.shape, 1)
    first_half = (lane % RET_DK) < RET_DK // 2
    for g in range(a.shape[1] // LANES):
        x = a[:, g * LANES:(g + 1) * LANES]
        rot = jnp.where(first_half, pltpu.roll(x, LANES - RET_DK // 2, 1), pltpu.roll(x, RET_DK // 2, 1))
        outs.append(x * cos + rot * sin_signed)
    return jnp.concatenate(outs, axis=1)


def _inproj_kernel(apply_ln, h_ref, g_ref, b_ref, w_ref, cos_ref, sin_ref, z_ref, *h_out):
    x = h_ref[...]
    if apply_ln:
        x = _layer_norm(x, g_ref[...], b_ref[...])
        h_out[0][...] = x
    xb = x.astype(MXU_DTYPE)
    cos = cos_ref[...]
    sin = sin_ref[...]
    for c in range(IN_TOTAL // PROJ_COLS):
        lo = c * PROJ_COLS
        acc = _dot(xb, w_ref[:, lo:lo + PROJ_COLS])
        if lo < 2 * RET_QK:
            acc = _rotary(acc, cos, sin)
            if lo >= RET_QK:
                acc = acc * (RET_DK ** -0.5)
        z_ref[:, lo:lo + PROJ_COLS] = acc.astype(z_ref.dtype)


def _inproj(h, ln_g, ln_b, w_in_b, cos_t, sin_t, apply_ln):
    T = h.shape[0]
    tm = PROJ_TILE
    row = lambda i: (i, 0)
    const = lambda i: (0, 0)
    out_shape = [jax.ShapeDtypeStruct((T, IN_TOTAL), MXU_DTYPE)]
    out_specs = [pl.BlockSpec((tm, IN_TOTAL), row)]
    if apply_ln:
        out_shape.append(jax.ShapeDtypeStruct((T, D_MODEL), F32))
        out_specs.append(pl.BlockSpec((tm, D_MODEL), row))
    res = pl.pallas_call(
        functools.partial(_inproj_kernel, apply_ln),
        grid=(T // tm,),
        in_specs=[
            pl.BlockSpec((tm, D_MODEL), row),
            pl.BlockSpec((1, D_MODEL), const),
            pl.BlockSpec((1, D_MODEL), const),
            pl.BlockSpec((D_MODEL, IN_TOTAL), const, pipeline_mode=pl.Buffered(1)),
            pl.BlockSpec((tm, LANES), row),
            pl.BlockSpec((tm, LANES), row),
        ],
        out_specs=out_specs,
        out_shape=out_shape,
        compiler_params=_params("parallel"),
        name="inproj",
    )(h, ln_g, ln_b, w_in_b, cos_t, sin_t)
    return res


def _retention_tables():
    n = SEQ_TILE
    log_g = np.log1p(-np.exp2(-5.0 - np.arange(RET_HEADS, dtype=np.float64)))
    i = np.arange(n)
    same = (i[:, None] // CHUNK) == (i[None, :] // CHUNK)
    earlier = (i[None, :] // CHUNK) < (i[:, None] // CHUNK)
    diff = i[:, None] - i[None, :]
    expo = np.where(same, np.abs(diff), np.where(earlier, diff, 0)).astype(np.float64)
    decay = np.exp(log_g[:, None, None] * expo) * (same | earlier)
    q_decay = np.exp(log_g[:, None] * (i + 1.0))
    k_decay = np.exp(log_g[:, None] * (n - 1.0 - i))
    tile_decay = np.exp(log_g * n)
    wide = lambda t: np.broadcast_to(t[:, :, None], t.shape + (HEAD_PAIR,))
    owner = (np.arange(HEAD_PAIR)[None, :] // RET_DK) == (np.arange(RET_HEADS)[:, None] % 2)
    k_table = wide(k_decay) * owner[:, None, :]
    return (jnp.asarray(decay, F32), jnp.asarray(wide(q_decay), F32), jnp.asarray(k_table, F32),
            tuple(float(v) for v in tile_decay))


def _attention_bias(rel_bias):
    n = SEQ_TILE
    width = 3 * n
    pad = LEFT_CHUNKS * CHUNK
    assert pad == 2 * n and n <= MAX_REL
    heads = rel_bias.shape[0]
    length = width + n
    near = rel_bias[:, MAX_REL - n:].astype(F32)
    far = jnp.broadcast_to(rel_bias[:, -1:].astype(F32), (heads, length - near.shape[1]))
    g_rev = jnp.concatenate([near, far], axis=1)[:, ::-1]
    rows = jnp.tile(g_rev, (1, n + 1))[:, :n * (length - 1)].reshape(heads, n, length - 1)
    bias = rows[:, :, n - 1:n - 1 + width]
    qi = np.arange(n)
    kj = np.arange(width)
    q_chunk = (pad + qi) // CHUNK
    k_chunk = kj // CHUNK
    in_band = (k_chunk[None, :] <= q_chunk[:, None]) & (k_chunk[None, :] >= q_chunk[:, None] - LEFT_CHUNKS)
    return jnp.where(jnp.asarray(in_band)[None], bias * LOG2E, NEG_BIG)


def _mixer_kernel(tile_decay,
                  qk_ref, vr_ref, gr_ref, gate_r0_ref, gate_r1_ref, gate_a0_ref, gate_a1_ref, qa_ref,
                  k2_ref, k1_ref, k0_ref, v2_ref, v1_ref, v0_ref,
                  h_ref, decay_ref, qdec_ref, kdec_ref, bias_ref,
                  wpr_ref, wpa_ref, wout_ref, g_ref, b_ref,
                  o_ref, state_ref, yr_ref, ya_ref):
    i = pl.program_id(1)

    @pl.when(i == 0)
    def _():
        state_ref[...] = jnp.zeros_like(state_ref)

    n = SEQ_TILE
    lane = lax.broadcasted_iota(jnp.int32, (n, HEAD_PAIR), 1)
    low_half = lane < RET_DK

    def head_lanes(odd):
        return low_half if odd == 0 else jnp.logical_not(low_half)

    def retention_first(hd):
        p, odd = divmod(hd, 2)
        q2 = qk_ref[:, p * HEAD_PAIR:(p + 1) * HEAD_PAIR]
        k2 = qk_ref[:, RET_QK + p * HEAD_PAIR:RET_QK + (p + 1) * HEAD_PAIR]
        qm = jnp.where(head_lanes(odd), q2, jnp.zeros_like(q2))
        v = vr_ref[:, hd * RET_DV:(hd + 1) * RET_DV]
        raw = _dot_nt(qm, k2)
        read = _dot(qm, state_ref[hd].astype(MXU_DTYPE))
        return raw, read, v

    def retention_state_update(hd):
        p = hd // 2
        k2 = qk_ref[:, RET_QK + p * HEAD_PAIR:RET_QK + (p + 1) * HEAD_PAIR]
        v = vr_ref[:, hd * RET_DV:(hd + 1) * RET_DV]
        k_dec = (k2.astype(F32) * kdec_ref[hd]).astype(MXU_DTYPE)
        state_ref[hd] = state_ref[hd] * tile_decay[hd] + _dot_tn(k_dec, v)

    def retention_second(hd, raw, read, v):
        intra = _dot((raw * decay_ref[hd]).astype(MXU_DTYPE), v)
        ret = intra + qdec_ref[hd] * read
        mu = jnp.mean(ret, axis=-1, keepdims=True)
        rc = ret - mu
        var = jnp.mean(rc * rc, axis=-1, keepdims=True)
        gn = rc * lax.rsqrt(var + LN_EPS)
        gate = gr_ref[:, hd * RET_DV:(hd + 1) * RET_DV].astype(F32)
        yr_ref[:, hd * RET_DV:(hd + 1) * RET_DV] = (gate * _sigmoid(gate) * gn).astype(MXU_DTYPE)

    pen2 = jnp.where(i >= 2, 0.0, NEG_BIG).astype(F32)
    pen1 = jnp.where(i >= 1, 0.0, NEG_BIG).astype(F32)
    k_refs = (k2_ref, k1_ref, k0_ref)
    v_refs = (v2_ref, v1_ref, v0_ref)
    pens = (pen2, pen1, None)

    def attention_first(hd):
        p, odd = divmod(hd, 2)
        cols = slice(p * HEAD_PAIR, (p + 1) * HEAD_PAIR)
        qs = (qa_ref[:, cols].astype(F32) * (ATT_DH ** -0.5 * LOG2E)).astype(MXU_DTYPE)
        qm = jnp.where(head_lanes(odd), qs, jnp.zeros_like(qs))
        s = []
        for kb in range(3):
            sk = _dot_nt(qm, k_refs[kb][:, cols]) + bias_ref[hd, :, kb * n:(kb + 1) * n]
            if pens[kb] is not None:
                sk = sk + pens[kb]
            s.append(sk)
        return (s,)

    def attention_second(hd, s):
        p, odd = divmod(hd, 2)
        cols = slice(p * HEAD_PAIR, (p + 1) * HEAD_PAIR)
        m = jnp.maximum(jnp.maximum(jnp.max(s[0], axis=-1, keepdims=True),
                                    jnp.max(s[1], axis=-1, keepdims=True)),
                        jnp.max(s[2], axis=-1, keepdims=True))
        e = [jnp.exp2(sk - m) for sk in s]
        denom = (jnp.sum(e[0], axis=-1, keepdims=True) + jnp.sum(e[1], axis=-1, keepdims=True)
                 + jnp.sum(e[2], axis=-1, keepdims=True))
        pv = (_dot(e[0].astype(MXU_DTYPE), v_refs[0][:, cols])
              + _dot(e[1].astype(MXU_DTYPE), v_refs[1][:, cols])
              + _dot(e[2].astype(MXU_DTYPE), v_refs[2][:, cols]))
        out = (pv / denom).astype(MXU_DTYPE)
        lo = odd * ATT_DH
        ya_ref[:, p * HEAD_PAIR + lo:p * HEAD_PAIR + lo + ATT_DH] = out[:, lo:lo + ATT_DH]

    work = []
    for hd in range(max(RET_HEADS, ATT_HEADS)):
        if hd < RET_HEADS:
            work.append((retention_first, retention_second, hd))
        if hd < ATT_HEADS:
            work.append((attention_first, attention_second, hd))
    ahead = 2
    pending = [first(hd) for first, _, hd in work[:ahead]]
    for j, (_, second, hd) in enumerate(work):
        if j + ahead < len(work):
            first, _, hd_next = work[j + ahead]
            pending.append(first(hd_next))
        second(hd, *pending.pop(0))

    pr = _dot(yr_ref[...], wpr_ref[...])
    pa = _dot(ya_ref[...], wpa_ref[...])
    gate_r = jnp.concatenate([gate_r0_ref[...], gate_r1_ref[...]], axis=1).astype(F32)
    gate_a = jnp.concatenate([gate_a0_ref[...], gate_a1_ref[...]], axis=1).astype(F32)
    merged = _sigmoid(gate_r) * pr + _sigmoid(gate_a) * pa
    mix = _dot(merged.astype(MXU_DTYPE), wout_ref[...])
    for hd in range(RET_HEADS):
        retention_state_update(hd)
    o_ref[...] = _layer_norm(DEEPNORM_ALPHA * h_ref[...] + mix, g_ref[...], b_ref[...])


def _mixer(z, h, B, S, tables, bias, wpr, wpa, wout, ln_g, ln_b):
    decay, q_decay, k_decay, tile_decay = tables
    T = B * S
    n = SEQ_TILE
    nt = S // n

    def zspec(width, col_off, back=0):
        cb = col_off // width
        return pl.BlockSpec((n, width), lambda b, i: (b * nt + jnp.maximum(i - back, 0), cb))

    row = lambda b, i: (b * nt + i, 0)
    c2 = lambda b, i: (0, 0)
    c3 = lambda b, i: (0, 0, 0)
    one = pl.Buffered(1)
    in_specs = [
        zspec(2 * RET_QK, Z_QK), zspec(RET_V, Z_VR), zspec(RET_V, Z_GR),
        zspec(GATE_HALF, Z_GATE_R), zspec(GATE_HALF, Z_GATE_R + GATE_HALF),
        zspec(GATE_HALF, Z_GATE_A), zspec(GATE_HALF, Z_GATE_A + GATE_HALF), zspec(ATT_W, Z_QA),
        zspec(ATT_W, Z_KA, 2), zspec(ATT_W, Z_KA, 1), zspec(ATT_W, Z_KA, 0),
        zspec(ATT_W, Z_VA, 2), zspec(ATT_W, Z_VA, 1), zspec(ATT_W, Z_VA, 0),
        pl.BlockSpec((n, D_MODEL), row),
        pl.BlockSpec((RET_HEADS, n, n), c3, pipeline_mode=one),
        pl.BlockSpec((RET_HEADS, n, HEAD_PAIR), c3, pipeline_mode=one),
        pl.BlockSpec((RET_HEADS, n, HEAD_PAIR), c3, pipeline_mode=one),
        pl.BlockSpec((ATT_HEADS, n, 3 * n), c3, pipeline_mode=one),
        pl.BlockSpec((RET_V, D_MODEL), c2, pipeline_mode=one),
        pl.BlockSpec((ATT_W, D_MODEL), c2, pipeline_mode=one),
        pl.BlockSpec((D_MODEL, D_MODEL), c2, pipeline_mode=one),
        pl.BlockSpec((1, D_MODEL), c2),
        pl.BlockSpec((1, D_MODEL), c2),
    ]
    return pl.pallas_call(
        functools.partial(_mixer_kernel, tile_decay),
        grid=(B, nt),
        in_specs=in_specs,
        out_specs=pl.BlockSpec((n, D_MODEL), row),
        out_shape=jax.ShapeDtypeStruct((T, D_MODEL), F32),
        scratch_shapes=[
            pltpu.VMEM((RET_HEADS, HEAD_PAIR, RET_DV), F32),
            pltpu.VMEM((n, RET_V), MXU_DTYPE),
            pltpu.VMEM((n, ATT_W), MXU_DTYPE),
        ],
        compiler_params=_params("arbitrary", "arbitrary"),
        name="mixer",
    )(*([z] * 14), h, decay, q_decay, k_decay, bias, wpr, wpa, wout, ln_g, ln_b)


def _kv_kernel(m_ref, w_ref, o_ref):
    o_ref[...] = _dot(m_ref[...].astype(MXU_DTYPE), w_ref[...]).astype(o_ref.dtype)


def _mem_kv(mem2d, w_kv):
    M = mem2d.shape[0]
    tm = 256
    return pl.pallas_call(
        _kv_kernel,
        grid=(M // tm,),
        in_specs=[pl.BlockSpec((tm, D_MODEL), lambda i: (i, 0)),
                  pl.BlockSpec((D_MODEL, 2 * D_MODEL), lambda i: (0, 0), pipeline_mode=pl.Buffered(1))],
        out_specs=pl.BlockSpec((tm, 2 * D_MODEL), lambda i: (i, 0)),
        out_shape=jax.ShapeDtypeStruct((M, 2 * D_MODEL), MXU_DTYPE),
        compiler_params=_params("parallel"),
        name="mem_kv",
    )(mem2d, w_kv)


ROUTE_ROWS = 8
GROUP_ROW0 = 0
EXPERT_ROW0 = 8
ROUTER_ROWS = 128


def _memattn_kernel(h_ref, kv_ref, wq_ref, wo_ref, g_ref, b_ref, wr_ref, br_ref, tri_ref,
                    h2_ref, rl_ref, rr_ref, cnt_ref, o_scr):
    n = SEQ_TILE
    h1 = h_ref[...]
    q = (_dot(h1.astype(MXU_DTYPE), wq_ref[...]) * (MEM_DH ** -0.5)).astype(MXU_DTYPE)
    for hd in range(MEM_HEADS):
        cols = slice(hd * MEM_DH, (hd + 1) * MEM_DH)
        s = _dot_nt(q[:, cols], kv_ref[:, cols])
        m = jnp.max(s, axis=-1, keepdims=True)
        e = jnp.exp(s - m)
        denom = jnp.sum(e, axis=-1, keepdims=True)
        v = kv_ref[:, D_MODEL + hd * MEM_DH:D_MODEL + (hd + 1) * MEM_DH]
        o_scr[:, cols] = (_dot(e.astype(MXU_DTYPE), v) / denom).astype(MXU_DTYPE)
    cross = _dot(o_scr[...], wo_ref[...])
    h2 = _layer_norm(DEEPNORM_ALPHA * h1 + cross, g_ref[...], b_ref[...])
    h2_ref[...] = h2

    logits = _dot_nt(wr_ref[...], h2.astype(MXU_DTYPE)) + br_ref[...]
    glog = logits[GROUP_ROW0:GROUP_ROW0 + N_GROUPS]
    gmax = jnp.max(glog, axis=0, keepdims=True)
    g_w = 1.0 / jnp.sum(jnp.exp(glog - gmax), axis=0, keepdims=True)
    giota = lax.broadcasted_iota(jnp.int32, glog.shape, 0)
    g_idx = jnp.min(jnp.where(glog == gmax, giota, N_GROUPS), axis=0, keepdims=True)
    el = jnp.zeros((EXPERTS_PER_GROUP, n), F32)
    for g in range(N_GROUPS):
        r0 = EXPERT_ROW0 + g * EXPERTS_PER_GROUP
        el = jnp.where(g_idx == g, logits[r0:r0 + EXPERTS_PER_GROUP], el)
    eiota = lax.broadcasted_iota(jnp.int32, el.shape, 0)
    m1 = jnp.max(el, axis=0, keepdims=True)
    i1 = jnp.min(jnp.where(el == m1, eiota, EXPERTS_PER_GROUP), axis=0, keepdims=True)
    el2 = jnp.where(eiota == i1, -jnp.inf, el)
    m2 = jnp.max(el2, axis=0, keepdims=True)
    i2 = jnp.min(jnp.where(el2 == m2, eiota, EXPERTS_PER_GROUP), axis=0, keepdims=True)
    r = jnp.exp(m2 - m1)
    w1 = g_w / (1.0 + r)
    w2 = g_w * r / (1.0 + r)
    e1 = g_idx * EXPERTS_PER_GROUP + i1
    e2 = g_idx * EXPERTS_PER_GROUP + i2

    xiota = lax.broadcasted_iota(jnp.int32, (N_EXPERTS, n), 0)
    oh1 = xiota == e1
    oh2 = xiota == e2
    cnt = jnp.where(oh1, 1.0, 0.0) + jnp.where(oh2, 1.0, 0.0)
    before = _dot(cnt.astype(MXU_DTYPE), tri_ref[...])
    rank1 = jnp.sum(jnp.where(oh1, before, 0.0), axis=0, keepdims=True)
    rank2 = jnp.sum(jnp.where(oh2, before, 0.0), axis=0, keepdims=True)
    cnt_ref[0] = jnp.broadcast_to(jnp.sum(cnt, axis=1, keepdims=True), (N_EXPERTS, LANES))

    zero = jnp.zeros((1, n), F32)
    rec = jnp.concatenate([e1.astype(F32), e2.astype(F32), w1, w2, rank1, rank2, zero, zero], axis=0)
    rl_ref[...] = rec
    rec_full = jnp.concatenate([rec, jnp.zeros((LANES - ROUTE_ROWS, n), F32)], axis=0)
    rr_ref[...] = rec_full.T


def _memattn(h1, kv, B, S, wq, wo, ln_g, ln_b, wr_t, br_col, tri):
    T = B * S
    n = SEQ_TILE
    nt = S // n
    row = lambda b, i: (b * nt + i, 0)
    c2 = lambda b, i: (0, 0)
    one = pl.Buffered(1)
    return pl.pallas_call(
        _memattn_kernel,
        grid=(B, nt),
        in_specs=[
            pl.BlockSpec((n, D_MODEL), row),
            pl.BlockSpec((kv.shape[0] // B, 2 * D_MODEL), lambda b, i: (b, 0)),
            pl.BlockSpec((D_MODEL, D_MODEL), c2, pipeline_mode=one),
            pl.BlockSpec((D_MODEL, D_MODEL), c2, pipeline_mode=one),
            pl.BlockSpec((1, D_MODEL), c2),
            pl.BlockSpec((1, D_MODEL), c2),
            pl.BlockSpec((ROUTER_ROWS, D_MODEL), c2, pipeline_mode=one),
            pl.BlockSpec((ROUTER_ROWS, 1), c2),
            pl.BlockSpec((n, n), c2, pipeline_mode=one),
        ],
        out_specs=[
            pl.BlockSpec((n, D_MODEL), row),
            pl.BlockSpec((ROUTE_ROWS, n), lambda b, i: (0, b * nt + i)),
            pl.BlockSpec((n, LANES), row),
            pl.BlockSpec((1, N_EXPERTS, LANES), lambda b, i: (b * nt + i, 0, 0)),
        ],
        out_shape=[
            jax.ShapeDtypeStruct((T, D_MODEL), F32),
            jax.ShapeDtypeStruct((ROUTE_ROWS, T), F32),
            jax.ShapeDtypeStruct((T, LANES), F32),
            jax.ShapeDtypeStruct((T // n, N_EXPERTS, LANES), F32),
        ],
        scratch_shapes=[pltpu.VMEM((n, D_MODEL), MXU_DTYPE)],
        compiler_params=_params("parallel", "parallel"),
        name="memattn_router",
    )(h1, kv, wq, wo, ln_g, ln_b, wr_t, br_col, tri)


CHUNK_ROWS = SUBLANES
LOCAL_ROWS = 768
N_CHUNKS = LOCAL_ROWS // CHUNK_ROWS
assert TOP_K * SEQ_TILE + N_EXPERTS * (CHUNK_ROWS - 1) <= LOCAL_ROWS


def _unpack_rows(packed):
    lo = lax.bitcast_convert_type(packed << 16, F32)
    hi = lax.bitcast_convert_type(packed & jnp.uint32(0xFFFF0000), F32)
    return lo, hi


def _pack_rows(y):
    bits = lax.bitcast_convert_type(y.astype(jnp.bfloat16).astype(F32), jnp.uint32)
    return (bits[:, :PACK_COLS] >> 16) | bits[:, PACK_COLS:]


def _chunk_plan(counts, n_blocks):
    nt = counts.shape[0]
    c = counts.astype(jnp.int32)
    incl = jnp.asarray(np.tril(np.ones((N_EXPERTS, N_EXPERTS), bool)))
    padc = (c + CHUNK_ROWS - 1) // CHUNK_ROWS * CHUNK_ROWS
    lend = jnp.sum(jnp.where(incl[None], padc[:, None, :], 0), axis=2)
    loff = lend - padc
    earlier = jnp.asarray(np.tril(np.ones((nt, nt), bool), -1))
    cbefore = jnp.sum(jnp.where(earlier[:, :, None], padc[None, :, :], 0), axis=1)
    total = jnp.sum(padc, axis=0)
    region = (total + ROUTE_BLOCK - 1) // ROUTE_BLOCK * ROUTE_BLOCK
    gend = jnp.sum(jnp.where(incl, region[None, :], 0), axis=1)
    gstart = gend - region
    delta = gstart[None, :] + cbefore - loff
    k_row = jnp.arange(N_CHUNKS, dtype=jnp.int32) * CHUNK_ROWS
    e_of_chunk = jnp.sum((lend[:, None, :] <= k_row[None, :, None]).astype(jnp.int32), axis=2)
    used = e_of_chunk < N_EXPERTS
    onehot = e_of_chunk[:, :, None] == jnp.arange(N_EXPERTS, dtype=jnp.int32)[None, None, :]
    grow = jnp.sum(jnp.where(onehot, delta[:, None, :], 0), axis=2) + k_row[None, :]
    parity = (jnp.arange(nt, dtype=jnp.int32) % 2)[:, None]
    dump = n_blocks * ROUTE_BLOCK + parity * LOCAL_ROWS + k_row[None, :]
    dst = jnp.where(used, grow, dump).reshape(-1)
    src = jnp.where(used, grow, 0).reshape(-1)
    block_first = jnp.arange(n_blocks, dtype=jnp.int32) * ROUTE_BLOCK
    block_expert = jnp.minimum(jnp.sum((gend[None, :] <= block_first[:, None]).astype(jnp.int32), axis=1),
                               N_EXPERTS - 1)
    n_used = (gend[-1:] // ROUTE_BLOCK).astype(jnp.int32)
    loff_f = loff.astype(F32)
    loff_col = jnp.broadcast_to(loff_f[:, :, None], (nt, N_EXPERTS, LANES))
    loff_row = jnp.broadcast_to(jnp.pad(loff_f, ((0, 0), (0, LANES - N_EXPERTS)))[:, None, :],
                                (nt, SUBLANES, LANES))
    return dst, src, block_expert, n_used, loff_col, loff_row


def _dispatch_kernel(dst_ref, h_ref, rl_ref, loff_ref, xb_in_ref, xb_ref, sbuf, sem):
    del xb_in_ref
    i = pl.program_id(0)
    n = SEQ_TILE
    rl = rl_ref[...]
    xiota = lax.broadcasted_iota(jnp.int32, (N_EXPERTS, n), 0)
    loff = loff_ref[0][:, 0:1]
    slots = []
    for k in range(TOP_K):
        e = rl[k:k + 1].astype(jnp.int32)
        base = jnp.sum(jnp.where(xiota == e, loff, 0.0), axis=0, keepdims=True)
        slots.append((base + rl[4 + k:5 + k]).astype(jnp.int32))
    riota = lax.broadcasted_iota(jnp.int32, (LOCAL_ROWS, n), 0)
    perm = jnp.where(riota == slots[0], 1.0, jnp.where(riota == slots[1], 1.0, 0.0)).astype(MXU_DTYPE)
    packed = _pack_rows(_dot(perm, h_ref[...].astype(MXU_DTYPE)))

    def wait_all(slot):
        pltpu.make_async_copy(sbuf.at[slot], xb_ref.at[pl.ds(0, LOCAL_ROWS)], sem.at[slot]).wait()

    slot = i % 2

    @pl.when(i >= 2)
    def _():
        wait_all(slot)

    sbuf.at[slot][...] = packed

    def body(k, carry):
        d = pl.multiple_of(dst_ref[i * N_CHUNKS + k], CHUNK_ROWS)
        src = sbuf.at[slot, pl.ds(pl.multiple_of(k * CHUNK_ROWS, CHUNK_ROWS), CHUNK_ROWS)]
        pltpu.make_async_copy(src, xb_ref.at[pl.ds(d, CHUNK_ROWS)], sem.at[slot]).start()
        return carry

    lax.fori_loop(0, N_CHUNKS, body, 0, unroll=8)

    @pl.when(i == pl.num_programs(0) - 1)
    def _():
        wait_all(1 - slot)
        wait_all(slot)


def _dispatch(dst, h2, rl, loff_col, n_blocks):
    T = h2.shape[0]
    n = SEQ_TILE
    rows = n_blocks * ROUTE_BLOCK + 2 * LOCAL_ROWS
    xb0 = jnp.zeros((rows, PACK_COLS), jnp.uint32)
    grid_spec = pltpu.PrefetchScalarGridSpec(
        num_scalar_prefetch=1,
        grid=(T // n,),
        in_specs=[
            pl.BlockSpec((n, D_MODEL), lambda i, d: (i, 0)),
            pl.BlockSpec((ROUTE_ROWS, n), lambda i, d: (0, i)),
            pl.BlockSpec((1, N_EXPERTS, LANES), lambda i, d: (i, 0, 0)),
            pl.BlockSpec(memory_space=pl.ANY),
        ],
        out_specs=pl.BlockSpec(memory_space=pl.ANY),
        scratch_shapes=[pltpu.VMEM((2, LOCAL_ROWS, PACK_COLS), jnp.uint32), pltpu.SemaphoreType.DMA((2,))],
    )
    assert T // n >= 2
    return pl.pallas_call(
        _dispatch_kernel,
        grid_spec=grid_spec,
        out_shape=jax.ShapeDtypeStruct((rows, PACK_COLS), jnp.uint32),
        input_output_aliases={4: 0},
        compiler_params=_params("arbitrary"),
        name="dispatch",
    )(dst, h2, rl, loff_col, xb0)


def _expert_kernel(be_ref, nused_ref, x_ref, wg_ref, wu_ref, wd_ref, y_ref, wg_b, wu_b, wd_b):
    p = pl.program_id(0)
    new_expert = jnp.logical_or(p == 0, be_ref[p] != be_ref[jnp.maximum(p - 1, 0)])

    @pl.when(new_expert)
    def _():
        wg_b[...] = wg_ref[...].astype(MXU_DTYPE)
        wu_b[...] = wu_ref[...].astype(MXU_DTYPE)
        wd_b[...] = wd_ref[...].astype(MXU_DTYPE)

    @pl.when(p < nused_ref[0])
    def _():
        lo, hi = _unpack_rows(x_ref[...])
        x = jnp.concatenate([lo.astype(MXU_DTYPE), hi.astype(MXU_DTYPE)], axis=1)
        gate = _dot(x, wg_b[...])
        up = _dot(x, wu_b[...])
        hid = (gate * _sigmoid(gate) * up).astype(MXU_DTYPE)
        y_ref[...] = _pack_rows(_dot(hid, wd_b[...]))

    @pl.when(p >= nused_ref[0])
    def _():
        y_ref[...] = jnp.zeros_like(y_ref)


def _experts(block_expert, n_used, xb, wg, wu, wd, layer):
    nb = block_expert.shape[0]
    n_slots = nb * ROUTE_BLOCK
    grid_spec = pltpu.PrefetchScalarGridSpec(
        num_scalar_prefetch=2,
        grid=(nb,),
        in_specs=[
            pl.BlockSpec((ROUTE_BLOCK, PACK_COLS), lambda p, be, nu: (p, 0)),
            pl.BlockSpec((None, None, D_MODEL, D_EXPERT), lambda p, be, nu: (layer, be[p], 0, 0)),
            pl.BlockSpec((None, None, D_MODEL, D_EXPERT), lambda p, be, nu: (layer, be[p], 0, 0)),
            pl.BlockSpec((None, None, D_EXPERT, D_MODEL), lambda p, be, nu: (layer, be[p], 0, 0)),
        ],
        out_specs=pl.BlockSpec((ROUTE_BLOCK, PACK_COLS), lambda p, be, nu: (p, 0)),
        scratch_shapes=[pltpu.VMEM((D_MODEL, D_EXPERT), MXU_DTYPE), pltpu.VMEM((D_MODEL, D_EXPERT), MXU_DTYPE),
                        pltpu.VMEM((D_EXPERT, D_MODEL), MXU_DTYPE)],
    )
    return pl.pallas_call(
        _expert_kernel,
        grid_spec=grid_spec,
        out_shape=jax.ShapeDtypeStruct((n_slots, PACK_COLS), jnp.uint32),
        compiler_params=_params("arbitrary"),
        name="experts",
    )(block_expert, n_used, xb, wg, wu, wd)


def _combine_kernel(src_ref, yb_ref, rr_ref, loff_ref, h_ref, g_ref, b_ref, o_ref, ybuf, sem):
    i = pl.program_id(0)
    n = SEQ_TILE
    slot = i % 2

    def fetch(tile, into):
        def body(k, carry):
            s = pl.multiple_of(src_ref[tile * N_CHUNKS + k], CHUNK_ROWS)
            dst = ybuf.at[into, pl.ds(pl.multiple_of(k * CHUNK_ROWS, CHUNK_ROWS), CHUNK_ROWS)]
            pltpu.make_async_copy(yb_ref.at[pl.ds(s, CHUNK_ROWS)], dst, sem.at[into]).start()
            return carry

        lax.fori_loop(0, N_CHUNKS, body, 0, unroll=8)

    @pl.when(i == 0)
    def _():
        fetch(0, 0)

    @pl.when(i + 1 < pl.num_programs(0))
    def _():
        fetch(i + 1, 1 - slot)

    rr = rr_ref[...]
    lane = lax.broadcasted_iota(jnp.int32, (n, LANES), 1)
    loff = loff_ref[0][0:1, :]
    ciota = lax.broadcasted_iota(jnp.int32, (n, LOCAL_ROWS), 1)
    mix = jnp.zeros((n, LOCAL_ROWS), F32)
    for k in range(TOP_K):
        e = rr[:, k:k + 1].astype(jnp.int32)
        base = jnp.sum(jnp.where(lane == e, loff, 0.0), axis=1, keepdims=True)
        sorted_row = (base + rr[:, 4 + k:5 + k]).astype(jnp.int32)
        mix = jnp.where(ciota == sorted_row, rr[:, 2 + k:3 + k], mix)

    pltpu.make_async_copy(yb_ref.at[pl.ds(0, LOCAL_ROWS)], ybuf.at[slot], sem.at[slot]).wait()
    lo, hi = _unpack_rows(ybuf.at[slot][...])
    mixb = mix.astype(MXU_DTYPE)
    y = jnp.concatenate([_dot(mixb, lo.astype(MXU_DTYPE)), _dot(mixb, hi.astype(MXU_DTYPE))], axis=1)
    o_ref[...] = _layer_norm(DEEPNORM_ALPHA * h_ref[...] + y, g_ref[...], b_ref[...])


def _combine(src, yb, rr, loff_row, h2, ln_g, ln_b):
    T = h2.shape[0]
    n = SEQ_TILE
    row = lambda i, s: (i, 0)
    grid_spec = pltpu.PrefetchScalarGridSpec(
        num_scalar_prefetch=1,
        grid=(T // n,),
        in_specs=[
            pl.BlockSpec(memory_space=pl.ANY),
            pl.BlockSpec((n, LANES), row),
            pl.BlockSpec((1, SUBLANES, LANES), lambda i, s: (i, 0, 0)),
            pl.BlockSpec((n, D_MODEL), row),
            pl.BlockSpec((1, D_MODEL), lambda i, s: (0, 0)),
            pl.BlockSpec((1, D_MODEL), lambda i, s: (0, 0)),
        ],
        out_specs=pl.BlockSpec((n, D_MODEL), row),
        scratch_shapes=[pltpu.VMEM((2, LOCAL_ROWS, PACK_COLS), jnp.uint32), pltpu.SemaphoreType.DMA((2,))],
    )
    return pl.pallas_call(
        _combine_kernel,
        grid_spec=grid_spec,
        out_shape=jax.ShapeDtypeStruct((T, D_MODEL), F32),
        compiler_params=_params("arbitrary"),
        name="combine",
    )(src, yb, rr, loff_row, h2, ln_g, ln_b)


def _router_weights(w_group, b_group, w_route, b_route):
    wr = jnp.zeros((ROUTER_ROWS, D_MODEL), F32)
    wr = wr.at[GROUP_ROW0:GROUP_ROW0 + N_GROUPS].set(w_group.T)
    wr = wr.at[EXPERT_ROW0:EXPERT_ROW0 + N_EXPERTS].set(w_route.T)
    br = jnp.zeros((ROUTER_ROWS,), F32)
    br = br.at[GROUP_ROW0:GROUP_ROW0 + N_GROUPS].set(b_group)
    br = br.at[EXPERT_ROW0:EXPERT_ROW0 + N_EXPERTS].set(b_route.reshape(-1))
    return wr.astype(MXU_DTYPE), br.reshape(ROUTER_ROWS, 1)


def kernel(x, mem, positions, ln_in_g, ln_in_b, w_in, rel_bias, w_proj_ret, w_proj_att, w_out, ln1_g, ln1_b, w_q_mem, w_kv_mem, w_o_mem, ln2_g, ln2_b, w_group, b_group, w_route, b_route, w_gate, w_up, w_down, ln3_g, ln3_b):
    B, S, D = x.shape
    assert D == D_MODEL and S % PROJ_TILE == 0 and S % SEQ_TILE == 0
    T = B * S
    A = T * TOP_K
    n_blocks = -(-(A + (T // SEQ_TILE) * N_EXPERTS * (CHUNK_ROWS - 1) + N_EXPERTS * (ROUTE_BLOCK - 1)) // ROUTE_BLOCK)
    bf = MXU_DTYPE

    cos_t, sin_t = _rope_tables(positions)
    tables = _retention_tables()
    tri = jnp.asarray(np.triu(np.ones((SEQ_TILE, SEQ_TILE), np.float32), 1), bf)
    mem2d = mem.reshape(-1, D)
    row2 = lambda v: v.reshape(1, D)

    h = x.reshape(T, D)
    for l in range(DEPTH):
        w_in_b = _to_mxu(w_in, l)
        if l == 0:
            z, h = _inproj(h, row2(ln_in_g), row2(ln_in_b), w_in_b, cos_t, sin_t, True)
        else:
            (z,) = _inproj(h, row2(ln_in_g), row2(ln_in_b), w_in_b, cos_t, sin_t, False)
        bias = _attention_bias(rel_bias[l])
        h = _mixer(z, h, B, S, tables, bias, _to_mxu(w_proj_ret, l), _to_mxu(w_proj_att, l),
                   _to_mxu(w_out, l), row2(ln1_g[l]), row2(ln1_b[l]))

        kv = _mem_kv(mem2d, _to_mxu(w_kv_mem, l))
        wr_t, br_col = _router_weights(w_group[l], b_group[l], w_route[l], b_route[l])
        h2, rl, rr, counts = _memattn(h, kv, B, S, _to_mxu(w_q_mem, l), _to_mxu(w_o_mem, l),
                                      row2(ln2_g[l]), row2(ln2_b[l]), wr_t, br_col, tri)

        dst, src, block_expert, n_used, loff_col, loff_row = _chunk_plan(counts[:, :, 0], n_blocks)
        xb = _dispatch(dst, h2, rl, loff_col, n_blocks)
        yb = _experts(block_expert, n_used, xb, w_gate, w_up, w_down, l)
        h = _combine(src, yb, rr, loff_row, h2, row2(ln3_g[l]), row2(ln3_b[l]))
    return h.reshape(B, S, D)
```

```python
import functools

import jax
import jax.numpy as jnp
import numpy as np
from jax import lax
from jax.experimental import pallas as pl
from jax.experimental.pallas import tpu as pltpu

D_MODEL = 1024
DEPTH = 2
CHUNK = 64
RET_HEADS = 8
RET_DK = 64
RET_DV = 128
RET_QK = RET_HEADS * RET_DK
RET_V = RET_HEADS * RET_DV
ROPE_BASE = 10000.0
ATT_HEADS = 8
ATT_DH = 64
ATT_W = ATT_HEADS * ATT_DH
LEFT_CHUNKS = 8
MAX_REL = 256
MEM_HEADS = 4
MEM_DH = D_MODEL // MEM_HEADS
N_GROUPS = 4
EXPERTS_PER_GROUP = 8
N_EXPERTS = N_GROUPS * EXPERTS_PER_GROUP
TOP_K = 2
D_EXPERT = 512
LN_EPS = 1e-5
DEEPNORM_ALPHA = (2.0 * DEPTH) ** 0.25
IN_TOTAL = 2 * RET_QK + 2 * RET_V + 3 * ATT_W + 2 * D_MODEL

LANES = 128
SUBLANES = 8
VMEM_LIMIT_BYTES = 56 * 1024 * 1024

SEQ_TILE = 256
PROJ_TILE = 512
PROJ_COLS = 512
EXPERT_BLOCK = 512
HEAD_PAIR = 2 * RET_DK
NEG_BIG = -1e30
LOG2E = 1.4426950408889634

F32 = jnp.float32
MXU_DTYPE = jnp.bfloat16
PACK_COLS = D_MODEL // 2

Z_QK = 0
Z_VR = Z_QK + 2 * RET_QK
Z_GR = Z_VR + RET_V
Z_QA = Z_GR + RET_V
Z_KA = Z_QA + ATT_W
Z_VA = Z_KA + ATT_W
Z_GATE_R = Z_VA + ATT_W
Z_GATE_A = Z_GATE_R + D_MODEL
GATE_HALF = D_MODEL // 2


def _layer_norm(x, g, b):
    mu = jnp.mean(x, axis=-1, keepdims=True)
    xc = x - mu
    var = jnp.mean(xc * xc, axis=-1, keepdims=True)
    return xc * lax.rsqrt(var + LN_EPS) * g + b


def _sigmoid(x):
    return 1.0 / (1.0 + jnp.exp(-x))


def _dot(a, b):
    return jnp.dot(a, b, preferred_element_type=F32)


def _dot_nt(a, b):
    return lax.dot_general(a, b, (((1,), (1,)), ((), ())), preferred_element_type=F32)


def _dot_tn(a, b):
    return lax.dot_general(a, b, (((0,), (0,)), ((), ())), preferred_element_type=F32)


def _params(*semantics):
    return pltpu.CompilerParams(dimension_semantics=semantics, vmem_limit_bytes=VMEM_LIMIT_BYTES)


def _cast_kernel(x_ref, o_ref):
    o_ref[...] = x_ref[...].astype(o_ref.dtype)


def _to_mxu(w_stack, layer):
    _, r, c = w_stack.shape
    tr = 256
    return pl.pallas_call(
        _cast_kernel,
        grid=(r // tr,),
        in_specs=[pl.BlockSpec((None, tr, c), lambda i: (layer, i, 0))],
        out_specs=pl.BlockSpec((tr, c), lambda i: (i, 0)),
        out_shape=jax.ShapeDtypeStruct((r, c), MXU_DTYPE),
        compiler_params=_params("parallel"),
        name="cast_weight",
    )(w_stack)


def _rope_kernel(pos_ref, invf_ref, cos_ref, sin_ref):
    ang = pos_ref[...] * invf_ref[...]
    lane = lax.broadcasted_iota(jnp.int32, ang.shape, 1)
    s = jnp.sin(ang)
    cos_ref[...] = jnp.cos(ang)
    sin_ref[...] = jnp.where((lane % RET_DK) < RET_DK // 2, -s, s)


def _rope_tables(positions):
    T = positions.size
    half = RET_DK // 2
    inv_freq = 1.0 / (ROPE_BASE ** jnp.linspace(0.0, 1.0, half, dtype=F32))
    invf = jnp.tile(inv_freq, LANES // half).reshape(1, LANES)
    pos = jnp.broadcast_to(positions.reshape(T, 1).astype(F32), (T, LANES))
    tm = 1024
    spec = pl.BlockSpec((tm, LANES), lambda i: (i, 0))
    return pl.pallas_call(
        _rope_kernel,
        grid=(T // tm,),
        in_specs=[spec, pl.BlockSpec((1, LANES), lambda i: (0, 0))],
        out_specs=[spec, spec],
        out_shape=[jax.ShapeDtypeStruct((T, LANES), F32)] * 2,
        compiler_params=_params("parallel"),
        name="rope_tables",
    )(pos, invf)


def _rotary(a, cos, sin_signed):
    outs = []
    lane = lax.broadcasted_iota(jnp.int32, cos.shape, 1)
    first_half = (lane % RET_DK) < RET_DK // 2
    for g in range(a.shape[1] // LANES):
        x = a[:, g * LANES:(g + 1) * LANES]
        rot = jnp.where(first_half, pltpu.roll(x, LANES - RET_DK // 2, 1), pltpu.roll(x, RET_DK // 2, 1))
        outs.append(x * cos + rot * sin_signed)
    return jnp.concatenate(outs, axis=1)


def _inproj_kernel(apply_ln, h_ref, g_ref, b_ref, w_ref, cos_ref, sin_ref, z_ref, *h_out):
    x = h_ref[...]
    if apply_ln:
        x = _layer_norm(x, g_ref[...], b_ref[...])
        h_out[0][...] = x
    xb = x.astype(MXU_DTYPE)
    cos = cos_ref[...]
    sin = sin_ref[...]
    for c in range(IN_TOTAL // PROJ_COLS):
        lo = c * PROJ_COLS
        acc = _dot(xb, w_ref[:, lo:lo + PROJ_COLS])
        if lo < 2 * RET_QK:
            acc = _rotary(acc, cos, sin)
            if lo >= RET_QK:
                acc = acc * (RET_DK ** -0.5)
        z_ref[:, lo:lo + PROJ_COLS] = acc.astype(z_ref.dtype)


def _inproj(h, ln_g, ln_b, w_in_b, cos_t, sin_t, apply_ln):
    T = h.shape[0]
    tm = PROJ_TILE
    row = lambda i: (i, 0)
    const = lambda i: (0, 0)
    out_shape = [jax.ShapeDtypeStruct((T, IN_TOTAL), MXU_DTYPE)]
    out_specs = [pl.BlockSpec((tm, IN_TOTAL), row)]
    if apply_ln:
        out_shape.append(jax.ShapeDtypeStruct((T, D_MODEL), F32))
        out_specs.append(pl.BlockSpec((tm, D_MODEL), row))
    res = pl.pallas_call(
        functools.partial(_inproj_kernel, apply_ln),
        grid=(T // tm,),
        in_specs=[
            pl.BlockSpec((tm, D_MODEL), row),
            pl.BlockSpec((1, D_MODEL), const),
            pl.BlockSpec((1, D_MODEL), const),
            pl.BlockSpec((D_MODEL, IN_TOTAL), const, pipeline_mode=pl.Buffered(1)),
            pl.BlockSpec((tm, LANES), row),
            pl.BlockSpec((tm, LANES), row),
        ],
        out_specs=out_specs,
        out_shape=out_shape,
        compiler_params=_params("parallel"),
        name="inproj",
    )(h, ln_g, ln_b, w_in_b, cos_t, sin_t)
    return res


def _retention_tables():
    n = SEQ_TILE
    log_g = np.log1p(-np.exp2(-5.0 - np.arange(RET_HEADS, dtype=np.float64)))
    i = np.arange(n)
    same = (i[:, None] // CHUNK) == (i[None, :] // CHUNK)
    earlier = (i[None, :] // CHUNK) < (i[:, None] // CHUNK)
    diff = i[:, None] - i[None, :]
    expo = np.where(same, np.abs(diff), np.where(earlier, diff, 0)).astype(np.float64)
    decay = np.exp(log_g[:, None, None] * expo) * (same | earlier)
    q_decay = np.exp(log_g[:, None] * (i + 1.0))
    k_decay = np.exp(log_g[:, None] * (n - 1.0 - i))
    tile_decay = np.exp(log_g * n)
    wide = lambda t: np.broadcast_to(t[:, :, None], t.shape + (HEAD_PAIR,))
    owner = (np.arange(HEAD_PAIR)[None, :] // RET_DK) == (np.arange(RET_HEADS)[:, None] % 2)
    k_table = wide(k_decay) * owner[:, None, :]
    return (jnp.asarray(decay, F32), jnp.asarray(wide(q_decay), F32), jnp.asarray(k_table, F32),
            tuple(float(v) for v in tile_decay))


def _attention_bias(rel_bias):
    n = SEQ_TILE
    width = 3 * n
    pad = LEFT_CHUNKS * CHUNK
    assert pad == 2 * n and n <= MAX_REL
    heads = rel_bias.shape[0]
    length = width + n
    near = rel_bias[:, MAX_REL - n:].astype(F32)
    far = jnp.broadcast_to(rel_bias[:, -1:].astype(F32), (heads, length - near.shape[1]))
    g_rev = jnp.concatenate([near, far], axis=1)[:, ::-1]
    rows = jnp.tile(g_rev, (1, n + 1))[:, :n * (length - 1)].reshape(heads, n, length - 1)
    bias = rows[:, :, n - 1:n - 1 + width]
    qi = np.arange(n)
    kj = np.arange(width)
    q_chunk = (pad + qi) // CHUNK
    k_chunk = kj // CHUNK
    in_band = (k_chunk[None, :] <= q_chunk[:, None]) & (k_chunk[None, :] >= q_chunk[:, None] - LEFT_CHUNKS)
    return jnp.where(jnp.asarray(in_band)[None], bias * LOG2E, NEG_BIG)


def _mixer_kernel(tile_decay,
                  qk_ref, vr_ref, gr_ref, gate_r0_ref, gate_r1_ref, gate_a0_ref, gate_a1_ref, qa_ref,
                  k2_ref, k1_ref, k0_ref, v2_ref, v1_ref, v0_ref,
                  h_ref, decay_ref, qdec_ref, kdec_ref, bias_ref,
                  wpr_ref, wpa_ref, wout_ref, g_ref, b_ref,
                  o_ref, state_ref, yr_ref, ya_ref):
    i = pl.program_id(1)

    @pl.when(i == 0)
    def _():
        state_ref[...] = jnp.zeros_like(state_ref)

    n = SEQ_TILE
    lane = lax.broadcasted_iota(jnp.int32, (n, HEAD_PAIR), 1)
    low_half = lane < RET_DK

    def head_lanes(odd):
        return low_half if odd == 0 else jnp.logical_not(low_half)

    def retention_first(hd):
        p, odd = divmod(hd, 2)
        q2 = qk_ref[:, p * HEAD_PAIR:(p + 1) * HEAD_PAIR]
        k2 = qk_ref[:, RET_QK + p * HEAD_PAIR:RET_QK + (p + 1) * HEAD_PAIR]
        qm = jnp.where(head_lanes(odd), q2, jnp.zeros_like(q2))
        v = vr_ref[:, hd * RET_DV:(hd + 1) * RET_DV]
        raw = _dot_nt(qm, k2)
        read = _dot(qm, state_ref[hd].astype(MXU_DTYPE))
        return raw, read, v

    def retention_state_update(hd):
        p = hd // 2
        k2 = qk_ref[:, RET_QK + p * HEAD_PAIR:RET_QK + (p + 1) * HEAD_PAIR]
        v = vr_ref[:, hd * RET_DV:(hd + 1) * RET_DV]
        k_dec = (k2.astype(F32) * kdec_ref[hd]).astype(MXU_DTYPE)
        state_ref[hd] = state_ref[hd] * tile_decay[hd] + _dot_tn(k_dec, v)

    def retention_second(hd, raw, read, v):
        intra = _dot((raw * decay_ref[hd]).astype(MXU_DTYPE), v)
        ret = intra + qdec_ref[hd] * read
        mu = jnp.mean(ret, axis=-1, keepdims=True)
        rc = ret - mu
        var = jnp.mean(rc * rc, axis=-1, keepdims=True)
        gn = rc * lax.rsqrt(var + LN_EPS)
        gate = gr_ref[:, hd * RET_DV:(hd + 1) * RET_DV].astype(F32)
        yr_ref[:, hd * RET_DV:(hd + 1) * RET_DV] = (gate * _sigmoid(gate) * gn).astype(MXU_DTYPE)

    pen2 = jnp.where(i >= 2, 0.0, NEG_BIG).astype(F32)
    pen1 = jnp.where(i >= 1, 0.0, NEG_BIG).astype(F32)
    k_refs = (k2_ref, k1_ref, k0_ref)
    v_refs = (v2_ref, v1_ref, v0_ref)
    pens = (pen2, pen1, None)

    def attention_first(hd):
        p, odd = divmod(hd, 2)
        cols = slice(p * HEAD_PAIR, (p + 1) * HEAD_PAIR)
        qs = (qa_ref[:, cols].astype(F32) * (ATT_DH ** -0.5 * LOG2E)).astype(MXU_DTYPE)
        qm = jnp.where(head_lanes(odd), qs, jnp.zeros_like(qs))
        s = []
        for kb in range(3):
            sk = _dot_nt(qm, k_refs[kb][:, cols]) + bias_ref[hd, :, kb * n:(kb + 1) * n]
            if pens[kb] is not None:
                sk = sk + pens[kb]
            s.append(sk)
        return (s,)

    def attention_second(hd, s):
        p, odd = divmod(hd, 2)
        cols = slice(p * HEAD_PAIR, (p + 1) * HEAD_PAIR)
        m = jnp.maximum(jnp.maximum(jnp.max(s[0], axis=-1, keepdims=True),
                                    jnp.max(s[1], axis=-1, keepdims=True)),
                        jnp.max(s[2], axis=-1, keepdims=True))
        e = [jnp.exp2(sk - m) for sk in s]
        denom = (jnp.sum(e[0], axis=-1, keepdims=True) + jnp.sum(e[1], axis=-1, keepdims=True)
                 + jnp.sum(e[2], axis=-1, keepdims=True))
        pv = (_dot(e[0].astype(MXU_DTYPE), v_refs[0][:, cols])
              + _dot(e[1].astype(MXU_DTYPE), v_refs[1][:, cols])
              + _dot(e[2].astype(MXU_DTYPE), v_refs[2][:, cols]))
        out = (pv / denom).astype(MXU_DTYPE)
        lo = odd * ATT_DH
        ya_ref[:, p * HEAD_PAIR + lo:p * HEAD_PAIR + lo + ATT_DH] = out[:, lo:lo + ATT_DH]

    work = []
    for hd in range(max(RET_HEADS, ATT_HEADS)):
        if hd < RET_HEADS:
            work.append((retention_first, retention_second, hd))
        if hd < ATT_HEADS:
            work.append((attention_first, attention_second, hd))
    ahead = 2
    pending = [first(hd) for first, _, hd in work[:ahead]]
    for j, (_, second, hd) in enumerate(work):
        if j + ahead < len(work):
            first, _, hd_next = work[j + ahead]
            pending.append(first(hd_next))
        second(hd, *pending.pop(0))

    pr = _dot(yr_ref[...], wpr_ref[...])
    pa = _dot(ya_ref[...], wpa_ref[...])
    gate_r = jnp.concatenate([gate_r0_ref[...], gate_r1_ref[...]], axis=1).astype(F32)
    gate_a = jnp.concatenate([gate_a0_ref[...], gate_a1_ref[...]], axis=1).astype(F32)
    merged = _sigmoid(gate_r) * pr + _sigmoid(gate_a) * pa
    mix = _dot(merged.astype(MXU_DTYPE), wout_ref[...])
    for hd in range(RET_HEADS):
        retention_state_update(hd)
    o_ref[...] = _layer_norm(DEEPNORM_ALPHA * h_ref[...] + mix, g_ref[...], b_ref[...])


def _mixer(z, h, B, S, tables, bias, wpr, wpa, wout, ln_g, ln_b):
    decay, q_decay, k_decay, tile_decay = tables
    T = B * S
    n = SEQ_TILE
    nt = S // n

    def zspec(width, col_off, back=0):
        cb = col_off // width
        return pl.BlockSpec((n, width), lambda b, i: (b * nt + jnp.maximum(i - back, 0), cb))

    row = lambda b, i: (b * nt + i, 0)
    c2 = lambda b, i: (0, 0)
    c3 = lambda b, i: (0, 0, 0)
    one = pl.Buffered(1)
    in_specs = [
        zspec(2 * RET_QK, Z_QK), zspec(RET_V, Z_VR), zspec(RET_V, Z_GR),
        zspec(GATE_HALF, Z_GATE_R), zspec(GATE_HALF, Z_GATE_R + GATE_HALF),
        zspec(GATE_HALF, Z_GATE_A), zspec(GATE_HALF, Z_GATE_A + GATE_HALF), zspec(ATT_W, Z_QA),
        zspec(ATT_W, Z_KA, 2), zspec(ATT_W, Z_KA, 1), zspec(ATT_W, Z_KA, 0),
        zspec(ATT_W, Z_VA, 2), zspec(ATT_W, Z_VA, 1), zspec(ATT_W, Z_VA, 0),
        pl.BlockSpec((n, D_MODEL), row),
        pl.BlockSpec((RET_HEADS, n, n), c3, pipeline_mode=one),
        pl.BlockSpec((RET_HEADS, n, HEAD_PAIR), c3, pipeline_mode=one),
        pl.BlockSpec((RET_HEADS, n, HEAD_PAIR), c3, pipeline_mode=one),
        pl.BlockSpec((ATT_HEADS, n, 3 * n), c3, pipeline_mode=one),
        pl.BlockSpec((RET_V, D_MODEL), c2, pipeline_mode=one),
        pl.BlockSpec((ATT_W, D_MODEL), c2, pipeline_mode=one),
        pl.BlockSpec((D_MODEL, D_MODEL), c2, pipeline_mode=one),
        pl.BlockSpec((1, D_MODEL), c2),
        pl.BlockSpec((1, D_MODEL), c2),
    ]
    return pl.pallas_call(
        functools.partial(_mixer_kernel, tile_decay),
        grid=(B, nt),
        in_specs=in_specs,
        out_specs=pl.BlockSpec((n, D_MODEL), row),
        out_shape=jax.ShapeDtypeStruct((T, D_MODEL), F32),
        scratch_shapes=[
            pltpu.VMEM((RET_HEADS, HEAD_PAIR, RET_DV), F32),
            pltpu.VMEM((n, RET_V), MXU_DTYPE),
            pltpu.VMEM((n, ATT_W), MXU_DTYPE),
        ],
        compiler_params=_params("arbitrary", "arbitrary"),
        name="mixer",
    )(*([z] * 14), h, decay, q_decay, k_decay, bias, wpr, wpa, wout, ln_g, ln_b)


def _kv_kernel(m_ref, w_ref, o_ref):
    o_ref[...] = _dot(m_ref[...].astype(MXU_DTYPE), w_ref[...]).astype(o_ref.dtype)


def _mem_kv(mem2d, w_kv):
    M = mem2d.shape[0]
    tm = 256
    return pl.pallas_call(
        _kv_kernel,
        grid=(M // tm,),
        in_specs=[pl.BlockSpec((tm, D_MODEL), lambda i: (i, 0)),
                  pl.BlockSpec((D_MODEL, 2 * D_MODEL), lambda i: (0, 0), pipeline_mode=pl.Buffered(1))],
        out_specs=pl.BlockSpec((tm, 2 * D_MODEL), lambda i: (i, 0)),
        out_shape=jax.ShapeDtypeStruct((M, 2 * D_MODEL), MXU_DTYPE),
        compiler_params=_params("parallel"),
        name="mem_kv",
    )(mem2d, w_kv)


ROUTE_ROWS = 8
GROUP_ROW0 = 0
EXPERT_ROW0 = 8
ROUTER_ROWS = 128


def _memattn_kernel(h_ref, kv_ref, wq_ref, wo_ref, g_ref, b_ref, wr_ref, br_ref, tri_ref,
                    h2_ref, rl_ref, rr_ref, cnt_ref, o_scr):
    n = SEQ_TILE
    h1 = h_ref[...]
    q = (_dot(h1.astype(MXU_DTYPE), wq_ref[...]) * (MEM_DH ** -0.5 * LOG2E)).astype(MXU_DTYPE)
    scores = [_dot_nt(q[:, hd * MEM_DH:(hd + 1) * MEM_DH], kv_ref[:, hd * MEM_DH:(hd + 1) * MEM_DH])
              for hd in range(MEM_HEADS)]
    for hd in range(MEM_HEADS):
        cols = slice(hd * MEM_DH, (hd + 1) * MEM_DH)
        s = scores[hd]
        m = jnp.max(s, axis=-1, keepdims=True)
        e = jnp.exp2(s - m)
        denom = jnp.sum(e, axis=-1, keepdims=True)
        v = kv_ref[:, D_MODEL + hd * MEM_DH:D_MODEL + (hd + 1) * MEM_DH]
        o_scr[:, cols] = (_dot(e.astype(MXU_DTYPE), v) / denom).astype(MXU_DTYPE)
    cross = _dot(o_scr[...], wo_ref[...])
    h2 = _layer_norm(DEEPNORM_ALPHA * h1 + cross, g_ref[...], b_ref[...])
    h2_ref[...] = h2

    logits = _dot_nt(wr_ref[...], h2.astype(MXU_DTYPE)) + br_ref[...]
    glog = logits[GROUP_ROW0:GROUP_ROW0 + N_GROUPS]
    gmax = jnp.max(glog, axis=0, keepdims=True)
    g_w = 1.0 / jnp.sum(jnp.exp(glog - gmax), axis=0, keepdims=True)
    giota = lax.broadcasted_iota(jnp.int32, glog.shape, 0)
    g_idx = jnp.min(jnp.where(glog == gmax, giota, N_GROUPS), axis=0, keepdims=True)
    el = jnp.zeros((EXPERTS_PER_GROUP, n), F32)
    for g in range(N_GROUPS):
        r0 = EXPERT_ROW0 + g * EXPERTS_PER_GROUP
        el = jnp.where(g_idx == g, logits[r0:r0 + EXPERTS_PER_GROUP], el)
    eiota = lax.broadcasted_iota(jnp.int32, el.shape, 0)
    m1 = jnp.max(el, axis=0, keepdims=True)
    i1 = jnp.min(jnp.where(el == m1, eiota, EXPERTS_PER_GROUP), axis=0, keepdims=True)
    el2 = jnp.where(eiota == i1, -jnp.inf, el)
    m2 = jnp.max(el2, axis=0, keepdims=True)
    i2 = jnp.min(jnp.where(el2 == m2, eiota, EXPERTS_PER_GROUP), axis=0, keepdims=True)
    r = jnp.exp(m2 - m1)
    w1 = g_w / (1.0 + r)
    w2 = g_w * r / (1.0 + r)
    e1 = g_idx * EXPERTS_PER_GROUP + i1
    e2 = g_idx * EXPERTS_PER_GROUP + i2

    xiota = lax.broadcasted_iota(jnp.int32, (N_EXPERTS, n), 0)
    oh1 = xiota == e1
    oh2 = xiota == e2
    cnt = jnp.where(oh1, 1.0, 0.0) + jnp.where(oh2, 1.0, 0.0)
    before = _dot(cnt.astype(MXU_DTYPE), tri_ref[...])
    rank1 = jnp.sum(jnp.where(oh1, before, 0.0), axis=0, keepdims=True)
    rank2 = jnp.sum(jnp.where(oh2, before, 0.0), axis=0, keepdims=True)
    cnt_ref[0] = jnp.broadcast_to(jnp.sum(cnt, axis=1, keepdims=True), (N_EXPERTS, LANES))

    zero = jnp.zeros((1, n), F32)
    rec = jnp.concatenate([e1.astype(F32), e2.astype(F32), w1, w2, rank1, rank2, zero, zero], axis=0)
    rl_ref[...] = rec
    rec_full = jnp.concatenate([rec, jnp.zeros((LANES - ROUTE_ROWS, n), F32)], axis=0)
    rr_ref[...] = rec_full.T


def _memattn(h1, kv, B, S, wq, wo, ln_g, ln_b, wr_t, br_col, tri):
    T = B * S
    n = SEQ_TILE
    nt = S // n
    row = lambda b, i: (b * nt + i, 0)
    c2 = lambda b, i: (0, 0)
    one = pl.Buffered(1)
    return pl.pallas_call(
        _memattn_kernel,
        grid=(B, nt),
        in_specs=[
            pl.BlockSpec((n, D_MODEL), row),
            pl.BlockSpec((kv.shape[0] // B, 2 * D_MODEL), lambda b, i: (b, 0)),
            pl.BlockSpec((D_MODEL, D_MODEL), c2, pipeline_mode=one),
            pl.BlockSpec((D_MODEL, D_MODEL), c2, pipeline_mode=one),
            pl.BlockSpec((1, D_MODEL), c2),
            pl.BlockSpec((1, D_MODEL), c2),
            pl.BlockSpec((ROUTER_ROWS, D_MODEL), c2, pipeline_mode=one),
            pl.BlockSpec((ROUTER_ROWS, 1), c2),
            pl.BlockSpec((n, n), c2, pipeline_mode=one),
        ],
        out_specs=[
            pl.BlockSpec((n, D_MODEL), row),
            pl.BlockSpec((ROUTE_ROWS, n), lambda b, i: (0, b * nt + i)),
            pl.BlockSpec((n, LANES), row),
            pl.BlockSpec((1, N_EXPERTS, LANES), lambda b, i: (b * nt + i, 0, 0)),
        ],
        out_shape=[
            jax.ShapeDtypeStruct((T, D_MODEL), F32),
            jax.ShapeDtypeStruct((ROUTE_ROWS, T), F32),
            jax.ShapeDtypeStruct((T, LANES), F32),
            jax.ShapeDtypeStruct((T // n, N_EXPERTS, LANES), F32),
        ],
        scratch_shapes=[pltpu.VMEM((n, D_MODEL), MXU_DTYPE)],
        compiler_params=_params("parallel", "parallel"),
        name="memattn_router",
    )(h1, kv, wq, wo, ln_g, ln_b, wr_t, br_col, tri)


CHUNK_ROWS = SUBLANES
LOCAL_ROWS = 768
N_CHUNKS = LOCAL_ROWS // CHUNK_ROWS
assert TOP_K * SEQ_TILE + N_EXPERTS * (CHUNK_ROWS - 1) <= LOCAL_ROWS


def _unpack_rows(packed):
    lo = lax.bitcast_convert_type(packed << 16, F32)
    hi = lax.bitcast_convert_type(packed & jnp.uint32(0xFFFF0000), F32)
    return lo, hi


def _pack_rows(y):
    bits = lax.bitcast_convert_type(y.astype(jnp.bfloat16).astype(F32), jnp.uint32)
    return (bits[:, :PACK_COLS] >> 16) | bits[:, PACK_COLS:]


def _chunk_plan(counts, n_blocks):
    nt = counts.shape[0]
    c = counts.astype(jnp.int32)
    incl = jnp.asarray(np.tril(np.ones((N_EXPERTS, N_EXPERTS), bool)))
    padc = (c + CHUNK_ROWS - 1) // CHUNK_ROWS * CHUNK_ROWS
    lend = jnp.sum(jnp.where(incl[None], padc[:, None, :], 0), axis=2)
    loff = lend - padc
    earlier = jnp.asarray(np.tril(np.ones((nt, nt), bool), -1))
    cbefore = jnp.sum(jnp.where(earlier[:, :, None], padc[None, :, :], 0), axis=1)
    total = jnp.sum(padc, axis=0)
    region = (total + EXPERT_BLOCK - 1) // EXPERT_BLOCK * EXPERT_BLOCK
    gend = jnp.sum(jnp.where(incl, region[None, :], 0), axis=1)
    gstart = gend - region
    delta = gstart[None, :] + cbefore - loff
    k_row = jnp.arange(N_CHUNKS, dtype=jnp.int32) * CHUNK_ROWS
    e_of_chunk = jnp.sum((lend[:, None, :] <= k_row[None, :, None]).astype(jnp.int32), axis=2)
    used = e_of_chunk < N_EXPERTS
    onehot = e_of_chunk[:, :, None] == jnp.arange(N_EXPERTS, dtype=jnp.int32)[None, None, :]
    grow = jnp.sum(jnp.where(onehot, delta[:, None, :], 0), axis=2) + k_row[None, :]
    parity = (jnp.arange(nt, dtype=jnp.int32) % 2)[:, None]
    dump = n_blocks * EXPERT_BLOCK + parity * LOCAL_ROWS + k_row[None, :]
    dst = jnp.where(used, grow, dump).reshape(-1)
    src = jnp.where(used, grow, 0).reshape(-1)
    block_first = jnp.arange(n_blocks, dtype=jnp.int32) * EXPERT_BLOCK
    block_expert = jnp.minimum(jnp.sum((gend[None, :] <= block_first[:, None]).astype(jnp.int32), axis=1),
                               N_EXPERTS - 1)
    n_used = (gend[-1:] // EXPERT_BLOCK).astype(jnp.int32)
    tail = jnp.concatenate([gstart + total, (region - total) // CHUNK_ROWS, n_used]).astype(jnp.int32)
    loff_f = loff.astype(F32)
    loff_col = jnp.broadcast_to(loff_f[:, :, None], (nt, N_EXPERTS, LANES))
    loff_row = jnp.broadcast_to(jnp.pad(loff_f, ((0, 0), (0, LANES - N_EXPERTS)))[:, None, :],
                                (nt, SUBLANES, LANES))
    return dst, src, tail, block_expert, n_used, loff_col, loff_row


def _dispatch_kernel(dst_ref, tail_ref, h_ref, rl_ref, loff_ref, xb_ref, sbuf, zbuf, sem, zsem):
    i = pl.program_id(0)
    n = SEQ_TILE
    rl = rl_ref[...]
    xiota = lax.broadcasted_iota(jnp.int32, (N_EXPERTS, n), 0)
    loff = loff_ref[0][:, 0:1]
    slots = []
    for k in range(TOP_K):
        e = rl[k:k + 1].astype(jnp.int32)
        base = jnp.sum(jnp.where(xiota == e, loff, 0.0), axis=0, keepdims=True)
        slots.append((base + rl[4 + k:5 + k]).astype(jnp.int32))
    riota = lax.broadcasted_iota(jnp.int32, (LOCAL_ROWS, n), 0)
    perm = jnp.where(riota == slots[0], 1.0, jnp.where(riota == slots[1], 1.0, 0.0)).astype(MXU_DTYPE)
    packed = _pack_rows(_dot(perm, h_ref[...].astype(MXU_DTYPE)))

    def wait_all(slot):
        pltpu.make_async_copy(sbuf.at[slot], xb_ref.at[pl.ds(0, LOCAL_ROWS)], sem.at[slot]).wait()

    slot = i % 2

    @pl.when(i >= 2)
    def _():
        wait_all(slot)

    sbuf.at[slot][...] = packed

    def body(k, carry):
        d = pl.multiple_of(dst_ref[i * N_CHUNKS + k], CHUNK_ROWS)
        src = sbuf.at[slot, pl.ds(pl.multiple_of(k * CHUNK_ROWS, CHUNK_ROWS), CHUNK_ROWS)]
        pltpu.make_async_copy(src, xb_ref.at[pl.ds(d, CHUNK_ROWS)], sem.at[slot]).start()
        return carry

    lax.fori_loop(0, N_CHUNKS, body, 0, unroll=8)

    @pl.when(i == pl.num_programs(0) - 1)
    def _():
        zbuf[...] = jnp.zeros_like(zbuf)

        def fill(start_copy):
            def per_expert(e, carry):
                first = pl.multiple_of(tail_ref[e], CHUNK_ROWS)

                def per_chunk(c, inner):
                    row = pl.multiple_of(first + c * CHUNK_ROWS, CHUNK_ROWS)
                    copy = pltpu.make_async_copy(zbuf, xb_ref.at[pl.ds(row, CHUNK_ROWS)], zsem)
                    if start_copy:
                        copy.start()
                    else:
                        copy.wait()
                    return inner

                lax.fori_loop(0, tail_ref[N_EXPERTS + e], per_chunk, 0)
                return carry

            lax.fori_loop(0, N_EXPERTS, per_expert, 0)

        fill(True)
        wait_all(1 - slot)
        wait_all(slot)
        fill(False)

        zblock = sbuf.at[0, pl.ds(0, EXPERT_BLOCK)]
        zblock[...] = jnp.zeros_like(zblock)
        n_all = xb_ref.shape[0] // EXPERT_BLOCK

        def fill_blocks(start_copy):
            def per_block(p, carry):
                row = pl.multiple_of(p * EXPERT_BLOCK, EXPERT_BLOCK)
                copy = pltpu.make_async_copy(zblock, xb_ref.at[pl.ds(row, EXPERT_BLOCK)], zsem)
                if start_copy:
                    copy.start()
                else:
                    copy.wait()
                return carry

            lax.fori_loop(tail_ref[2 * N_EXPERTS], n_all, per_block, 0)

        fill_blocks(True)
        fill_blocks(False)


def _dispatch(dst, tail, h2, rl, loff_col, n_blocks):
    T = h2.shape[0]
    n = SEQ_TILE
    rows = n_blocks * EXPERT_BLOCK + 2 * LOCAL_ROWS
    grid_spec = pltpu.PrefetchScalarGridSpec(
        num_scalar_prefetch=2,
        grid=(T // n,),
        in_specs=[
            pl.BlockSpec((n, D_MODEL), lambda i, d, t: (i, 0)),
            pl.BlockSpec((ROUTE_ROWS, n), lambda i, d, t: (0, i)),
            pl.BlockSpec((1, N_EXPERTS, LANES), lambda i, d, t: (i, 0, 0)),
        ],
        out_specs=pl.BlockSpec(memory_space=pl.ANY),
        scratch_shapes=[pltpu.VMEM((2, LOCAL_ROWS, PACK_COLS), jnp.uint32),
                        pltpu.VMEM((CHUNK_ROWS, PACK_COLS), jnp.uint32),
                        pltpu.SemaphoreType.DMA((2,)), pltpu.SemaphoreType.DMA],
    )
    assert T // n >= 2
    return pl.pallas_call(
        _dispatch_kernel,
        grid_spec=grid_spec,
        out_shape=jax.ShapeDtypeStruct((rows, PACK_COLS), jnp.uint32),
        compiler_params=_params("arbitrary"),
        name="dispatch",
    )(dst, tail, h2, rl, loff_col)


def _expert_kernel(be_ref, nused_ref, x_ref, wg_ref, wu_ref, wd_ref, y_ref, wg_b, wu_b, wd_b):
    p = pl.program_id(0)
    new_expert = jnp.logical_or(p == 0, be_ref[p] != be_ref[jnp.maximum(p - 1, 0)])

    @pl.when(new_expert)
    def _():
        wg_b[...] = wg_ref[...].astype(MXU_DTYPE)
        wu_b[...] = wu_ref[...].astype(MXU_DTYPE)
        wd_b[...] = wd_ref[...].astype(MXU_DTYPE)

    @pl.when(p < nused_ref[0])
    def _():
        lo, hi = _unpack_rows(x_ref[...])
        x = jnp.concatenate([lo.astype(MXU_DTYPE), hi.astype(MXU_DTYPE)], axis=1)
        gate = _dot(x, wg_b[...])
        up = _dot(x, wu_b[...])
        hid = (gate * _sigmoid(gate) * up).astype(MXU_DTYPE)
        y_ref[...] = _pack_rows(_dot(hid, wd_b[...]))

    @pl.when(p >= nused_ref[0])
    def _():
        y_ref[...] = jnp.zeros_like(y_ref)


def _experts(block_expert, n_used, xb, wg, wu, wd, layer):
    nb = block_expert.shape[0]
    n_slots = nb * EXPERT_BLOCK
    grid_spec = pltpu.PrefetchScalarGridSpec(
        num_scalar_prefetch=2,
        grid=(nb,),
        in_specs=[
            pl.BlockSpec((EXPERT_BLOCK, PACK_COLS), lambda p, be, nu: (jnp.minimum(p, jnp.maximum(nu[0] - 1, 0)), 0)),
            pl.BlockSpec((None, None, D_MODEL, D_EXPERT), lambda p, be, nu: (layer, be[p], 0, 0)),
            pl.BlockSpec((None, None, D_MODEL, D_EXPERT), lambda p, be, nu: (layer, be[p], 0, 0)),
            pl.BlockSpec((None, None, D_EXPERT, D_MODEL), lambda p, be, nu: (layer, be[p], 0, 0)),
        ],
        out_specs=pl.BlockSpec((EXPERT_BLOCK, PACK_COLS), lambda p, be, nu: (p, 0)),
        scratch_shapes=[pltpu.VMEM((D_MODEL, D_EXPERT), MXU_DTYPE), pltpu.VMEM((D_MODEL, D_EXPERT), MXU_DTYPE),
                        pltpu.VMEM((D_EXPERT, D_MODEL), MXU_DTYPE)],
    )
    return pl.pallas_call(
        _expert_kernel,
        grid_spec=grid_spec,
        out_shape=jax.ShapeDtypeStruct((n_slots, PACK_COLS), jnp.uint32),
        compiler_params=_params("arbitrary"),
        name="experts",
    )(block_expert, n_used, xb, wg, wu, wd)


def _combine_kernel(src_ref, yb_ref, rr_ref, loff_ref, h_ref, g_ref, b_ref, o_ref, ybuf, sem):
    i = pl.program_id(0)
    n = SEQ_TILE
    slot = i % 2

    def fetch(tile, into):
        def body(k, carry):
            s = pl.multiple_of(src_ref[tile * N_CHUNKS + k], CHUNK_ROWS)
            dst = ybuf.at[into, pl.ds(pl.multiple_of(k * CHUNK_ROWS, CHUNK_ROWS), CHUNK_ROWS)]
            pltpu.make_async_copy(yb_ref.at[pl.ds(s, CHUNK_ROWS)], dst, sem.at[into]).start()
            return carry

        lax.fori_loop(0, N_CHUNKS, body, 0, unroll=8)

    @pl.when(i == 0)
    def _():
        fetch(0, 0)

    @pl.when(i + 1 < pl.num_programs(0))
    def _():
        fetch(i + 1, 1 - slot)

    rr = rr_ref[...]
    lane = lax.broadcasted_iota(jnp.int32, (n, LANES), 1)
    loff = loff_ref[0][0:1, :]
    ciota = lax.broadcasted_iota(jnp.int32, (n, LOCAL_ROWS), 1)
    mix = jnp.zeros((n, LOCAL_ROWS), F32)
    for k in range(TOP_K):
        e = rr[:, k:k + 1].astype(jnp.int32)
        base = jnp.sum(jnp.where(lane == e, loff, 0.0), axis=1, keepdims=True)
        sorted_row = (base + rr[:, 4 + k:5 + k]).astype(jnp.int32)
        mix = jnp.where(ciota == sorted_row, rr[:, 2 + k:3 + k], mix)

    pltpu.make_async_copy(yb_ref.at[pl.ds(0, LOCAL_ROWS)], ybuf.at[slot], sem.at[slot]).wait()
    lo, hi = _unpack_rows(ybuf.at[slot][...])
    mixb = mix.astype(MXU_DTYPE)
    y = jnp.concatenate([_dot(mixb, lo.astype(MXU_DTYPE)), _dot(mixb, hi.astype(MXU_DTYPE))], axis=1)
    o_ref[...] = _layer_norm(DEEPNORM_ALPHA * h_ref[...] + y, g_ref[...], b_ref[...])


def _combine(src, yb, rr, loff_row, h2, ln_g, ln_b):
    T = h2.shape[0]
    n = SEQ_TILE
    row = lambda i, s: (i, 0)
    grid_spec = pltpu.PrefetchScalarGridSpec(
        num_scalar_prefetch=1,
        grid=(T // n,),
        in_specs=[
            pl.BlockSpec(memory_space=pl.ANY),
            pl.BlockSpec((n, LANES), row),
            pl.BlockSpec((1, SUBLANES, LANES), lambda i, s: (i, 0, 0)),
            pl.BlockSpec((n, D_MODEL), row),
            pl.BlockSpec((1, D_MODEL), lambda i, s: (0, 0)),
            pl.BlockSpec((1, D_MODEL), lambda i, s: (0, 0)),
        ],
        out_specs=pl.BlockSpec((n, D_MODEL), row),
        scratch_shapes=[pltpu.VMEM((2, LOCAL_ROWS, PACK_COLS), jnp.uint32), pltpu.SemaphoreType.DMA((2,))],
    )
    return pl.pallas_call(
        _combine_kernel,
        grid_spec=grid_spec,
        out_shape=jax.ShapeDtypeStruct((T, D_MODEL), F32),
        compiler_params=_params("arbitrary"),
        name="combine",
    )(src, yb, rr, loff_row, h2, ln_g, ln_b)


def _router_weights(w_group, b_group, w_route, b_route):
    wr = jnp.zeros((ROUTER_ROWS, D_MODEL), F32)
    wr = wr.at[GROUP_ROW0:GROUP_ROW0 + N_GROUPS].set(w_group.T)
    wr = wr.at[EXPERT_ROW0:EXPERT_ROW0 + N_EXPERTS].set(w_route.T)
    br = jnp.zeros((ROUTER_ROWS,), F32)
    br = br.at[GROUP_ROW0:GROUP_ROW0 + N_GROUPS].set(b_group)
    br = br.at[EXPERT_ROW0:EXPERT_ROW0 + N_EXPERTS].set(b_route.reshape(-1))
    return wr.astype(MXU_DTYPE), br.reshape(ROUTER_ROWS, 1)


def kernel(x, mem, positions, ln_in_g, ln_in_b, w_in, rel_bias, w_proj_ret, w_proj_att, w_out, ln1_g, ln1_b, w_q_mem, w_kv_mem, w_o_mem, ln2_g, ln2_b, w_group, b_group, w_route, b_route, w_gate, w_up, w_down, ln3_g, ln3_b):
    B, S, D = x.shape
    assert D == D_MODEL and S % PROJ_TILE == 0 and S % SEQ_TILE == 0
    T = B * S
    A = T * TOP_K
    n_blocks = -(-(A + (T // SEQ_TILE) * N_EXPERTS * (CHUNK_ROWS - 1) + N_EXPERTS * (EXPERT_BLOCK - 1)) // EXPERT_BLOCK)
    bf = MXU_DTYPE

    cos_t, sin_t = _rope_tables(positions)
    tables = _retention_tables()
    tri = jnp.asarray(np.triu(np.ones((SEQ_TILE, SEQ_TILE), np.float32), 1), bf)
    mem2d = mem.reshape(-1, D)
    row2 = lambda v: v.reshape(1, D)

    h = x.reshape(T, D)
    for l in range(DEPTH):
        w_in_b = _to_mxu(w_in, l)
        if l == 0:
            z, h = _inproj(h, row2(ln_in_g), row2(ln_in_b), w_in_b, cos_t, sin_t, True)
        else:
            (z,) = _inproj(h, row2(ln_in_g), row2(ln_in_b), w_in_b, cos_t, sin_t, False)
        bias = _attention_bias(rel_bias[l])
        h = _mixer(z, h, B, S, tables, bias, _to_mxu(w_proj_ret, l), _to_mxu(w_proj_att, l),
                   _to_mxu(w_out, l), row2(ln1_g[l]), row2(ln1_b[l]))

        kv = _mem_kv(mem2d, _to_mxu(w_kv_mem, l))
        wr_t, br_col = _router_weights(w_group[l], b_group[l], w_route[l], b_route[l])
        h2, rl, rr, counts = _memattn(h, kv, B, S, _to_mxu(w_q_mem, l), _to_mxu(w_o_mem, l),
                                      row2(ln2_g[l]), row2(ln2_b[l]), wr_t, br_col, tri)

        dst, src, tail, block_expert, n_used, loff_col, loff_row = _chunk_plan(counts[:, :, 0], n_blocks)
        xb = _dispatch(dst, tail, h2, rl, loff_col, n_blocks)
        yb = _experts(block_expert, n_used, xb, w_gate, w_up, w_down, l)
        h = _combine(src, yb, rr, loff_row, h2, row2(ln3_g[l]), row2(ln3_b[l]))
    return h.reshape(B, S, D)
```

```python
import functools

import jax
import jax.numpy as jnp
import numpy as np
from jax import lax
from jax.experimental import pallas as pl
from jax.experimental.pallas import tpu as pltpu

D_MODEL = 1024
DEPTH = 2
CHUNK = 64
RET_HEADS = 8
RET_DK = 64
RET_DV = 128
RET_QK = RET_HEADS * RET_DK
RET_V = RET_HEADS * RET_DV
ROPE_BASE = 10000.0
ATT_HEADS = 8
ATT_DH = 64
ATT_W = ATT_HEADS * ATT_DH
LEFT_CHUNKS = 8
MAX_REL = 256
MEM_HEADS = 4
MEM_DH = D_MODEL // MEM_HEADS
N_GROUPS = 4
EXPERTS_PER_GROUP = 8
N_EXPERTS = N_GROUPS * EXPERTS_PER_GROUP
TOP_K = 2
D_EXPERT = 512
LN_EPS = 1e-5
DEEPNORM_ALPHA = (2.0 * DEPTH) ** 0.25
IN_TOTAL = 2 * RET_QK + 2 * RET_V + 3 * ATT_W + 2 * D_MODEL

LANES = 128
SUBLANES = 8
VMEM_LIMIT_BYTES = 56 * 1024 * 1024

SEQ_TILE = 256
MEM_TILE = 2 * SEQ_TILE
MIX_SUB = 2
PROJ_TILE = 512
PROJ_COLS = 512
EXPERT_BLOCK = 512
HEAD_PAIR = 2 * RET_DK
NEG_BIG = -1e30
LOG2E = 1.4426950408889634

F32 = jnp.float32
MXU_DTYPE = jnp.bfloat16
PACK_COLS = D_MODEL // 2

Z_QK = 0
Z_VR = Z_QK + 2 * RET_QK
Z_GR = Z_VR + RET_V
Z_QA = Z_GR + RET_V
Z_KA = Z_QA + ATT_W
Z_VA = Z_KA + ATT_W
Z_GATE_R = Z_VA + ATT_W
Z_GATE_A = Z_GATE_R + D_MODEL
GATE_HALF = D_MODEL // 2


def _layer_norm(x, g, b):
    mu = jnp.mean(x, axis=-1, keepdims=True)
    xc = x - mu
    var = jnp.mean(xc * xc, axis=-1, keepdims=True)
    return xc * lax.rsqrt(var + LN_EPS) * g + b


def _sigmoid(x):
    return 1.0 / (1.0 + jnp.exp(-x))


def _dot(a, b):
    return jnp.dot(a, b, preferred_element_type=F32)


def _dot_nt(a, b):
    return lax.dot_general(a, b, (((1,), (1,)), ((), ())), preferred_element_type=F32)


def _dot_tn(a, b):
    return lax.dot_general(a, b, (((0,), (0,)), ((), ())), preferred_element_type=F32)


def _params(*semantics):
    return pltpu.CompilerParams(dimension_semantics=semantics, vmem_limit_bytes=VMEM_LIMIT_BYTES)


def _cast_kernel(x_ref, o_ref):
    o_ref[...] = x_ref[...].astype(o_ref.dtype)


def _to_mxu(w_stack, layer):
    _, r, c = w_stack.shape
    tr = 256
    return pl.pallas_call(
        _cast_kernel,
        grid=(r // tr,),
        in_specs=[pl.BlockSpec((None, tr, c), lambda i: (layer, i, 0))],
        out_specs=pl.BlockSpec((tr, c), lambda i: (i, 0)),
        out_shape=jax.ShapeDtypeStruct((r, c), MXU_DTYPE),
        compiler_params=_params("parallel"),
        name="cast_weight",
    )(w_stack)


def _rope_kernel(pos_ref, invf_ref, cos_ref, sin_ref):
    ang = pos_ref[...] * invf_ref[...]
    lane = lax.broadcasted_iota(jnp.int32, ang.shape, 1)
    s = jnp.sin(ang)
    cos_ref[...] = jnp.cos(ang)
    sin_ref[...] = jnp.where((lane % RET_DK) < RET_DK // 2, -s, s)


def _rope_tables(positions):
    T = positions.size
    half = RET_DK // 2
    inv_freq = 1.0 / (ROPE_BASE ** jnp.linspace(0.0, 1.0, half, dtype=F32))
    invf = jnp.tile(inv_freq, LANES // half).reshape(1, LANES)
    pos = jnp.broadcast_to(positions.reshape(T, 1).astype(F32), (T, LANES))
    tm = 1024
    spec = pl.BlockSpec((tm, LANES), lambda i: (i, 0))
    return pl.pallas_call(
        _rope_kernel,
        grid=(T // tm,),
        in_specs=[spec, pl.BlockSpec((1, LANES), lambda i: (0, 0))],
        out_specs=[spec, spec],
        out_shape=[jax.ShapeDtypeStruct((T, LANES), F32)] * 2,
        compiler_params=_params("parallel"),
        name="rope_tables",
    )(pos, invf)


def _rotary(a, cos, sin_signed):
    outs = []
    lane = lax.broadcasted_iota(jnp.int32, cos.shape, 1)
    first_half = (lane % RET_DK) < RET_DK // 2
    for g in range(a.shape[1] // LANES):
        x = a[:, g * LANES:(g + 1) * LANES]
        rot = jnp.where(first_half, pltpu.roll(x, LANES - RET_DK // 2, 1), pltpu.roll(x, RET_DK // 2, 1))
        outs.append(x * cos + rot * sin_signed)
    return jnp.concatenate(outs, axis=1)


def _inproj_kernel(apply_ln, h_ref, g_ref, b_ref, w_ref, cos_ref, sin_ref, z_ref, *h_out):
    x = h_ref[...]
    if apply_ln:
        x = _layer_norm(x, g_ref[...], b_ref[...])
        h_out[0][...] = x
    xb = x.astype(MXU_DTYPE)
    cos = cos_ref[...]
    sin = sin_ref[...]
    for c in range(IN_TOTAL // PROJ_COLS):
        lo = c * PROJ_COLS
        acc = _dot(xb, w_ref[:, lo:lo + PROJ_COLS])
        if lo < 2 * RET_QK:
            acc = _rotary(acc, cos, sin)
            if lo >= RET_QK:
                acc = acc * (RET_DK ** -0.5)
        z_ref[:, lo:lo + PROJ_COLS] = acc.astype(z_ref.dtype)


def _inproj(h, ln_g, ln_b, w_in_b, cos_t, sin_t, apply_ln):
    T = h.shape[0]
    tm = PROJ_TILE
    row = lambda i: (i, 0)
    const = lambda i: (0, 0)
    out_shape = [jax.ShapeDtypeStruct((T, IN_TOTAL), MXU_DTYPE)]
    out_specs = [pl.BlockSpec((tm, IN_TOTAL), row)]
    if apply_ln:
        out_shape.append(jax.ShapeDtypeStruct((T, D_MODEL), F32))
        out_specs.append(pl.BlockSpec((tm, D_MODEL), row))
    res = pl.pallas_call(
        functools.partial(_inproj_kernel, apply_ln),
        grid=(T // tm,),
        in_specs=[
            pl.BlockSpec((tm, D_MODEL), row),
            pl.BlockSpec((1, D_MODEL), const),
            pl.BlockSpec((1, D_MODEL), const),
            pl.BlockSpec((D_MODEL, IN_TOTAL), const, pipeline_mode=pl.Buffered(1)),
            pl.BlockSpec((tm, LANES), row),
            pl.BlockSpec((tm, LANES), row),
        ],
        out_specs=out_specs,
        out_shape=out_shape,
        compiler_params=_params("parallel"),
        name="inproj",
    )(h, ln_g, ln_b, w_in_b, cos_t, sin_t)
    return res


def _retention_tables():
    n = SEQ_TILE
    log_g = np.log1p(-np.exp2(-5.0 - np.arange(RET_HEADS, dtype=np.float64)))
    i = np.arange(n)
    same = (i[:, None] // CHUNK) == (i[None, :] // CHUNK)
    earlier = (i[None, :] // CHUNK) < (i[:, None] // CHUNK)
    diff = i[:, None] - i[None, :]
    expo = np.where(same, np.abs(diff), np.where(earlier, diff, 0)).astype(np.float64)
    decay = np.exp(log_g[:, None, None] * expo) * (same | earlier)
    q_decay = np.exp(log_g[:, None] * (i + 1.0))
    k_decay = np.exp(log_g[:, None] * (n - 1.0 - i))
    tile_decay = np.exp(log_g * n)
    wide = lambda t: np.broadcast_to(t[:, :, None], t.shape + (HEAD_PAIR,))
    owner = (np.arange(HEAD_PAIR)[None, :] // RET_DK) == (np.arange(RET_HEADS)[:, None] % 2)
    k_table = wide(k_decay) * owner[:, None, :]
    return (jnp.asarray(decay, F32), jnp.asarray(wide(q_decay), F32), jnp.asarray(k_table, F32),
            tuple(float(v) for v in tile_decay))


def _attention_bias(rel_bias):
    n = SEQ_TILE
    width = 3 * n
    pad = LEFT_CHUNKS * CHUNK
    assert pad == 2 * n and n <= MAX_REL
    heads = rel_bias.shape[0]
    length = width + n
    near = rel_bias[:, MAX_REL - n:].astype(F32)
    far = jnp.broadcast_to(rel_bias[:, -1:].astype(F32), (heads, length - near.shape[1]))
    g_rev = jnp.concatenate([near, far], axis=1)[:, ::-1]
    rows = jnp.tile(g_rev, (1, n + 1))[:, :n * (length - 1)].reshape(heads, n, length - 1)
    bias = rows[:, :, n - 1:n - 1 + width]
    qi = np.arange(n)
    kj = np.arange(width)
    q_chunk = (pad + qi) // CHUNK
    k_chunk = kj // CHUNK
    in_band = (k_chunk[None, :] <= q_chunk[:, None]) & (k_chunk[None, :] >= q_chunk[:, None] - LEFT_CHUNKS)
    return jnp.where(jnp.asarray(in_band)[None], bias * LOG2E, NEG_BIG)


def _mixer_kernel(tile_decay,
                  qk_ref, vr_ref, gr_ref, gate_r0_ref, gate_r1_ref, gate_a0_ref, gate_a1_ref, qa_ref,
                  kp_ref, kc_ref, vp_ref, vc_ref,
                  h_ref, decay_ref, qdec_ref, kdec_ref, bias_ref,
                  wpr_ref, wpa_ref, wout_ref, g_ref, b_ref,
                  o_ref, state_ref, yr_ref, ya_ref):
    i = pl.program_id(1)

    @pl.when(i == 0)
    def _():
        state_ref[...] = jnp.zeros_like(state_ref)

    n = SEQ_TILE
    lane = lax.broadcasted_iota(jnp.int32, (n, HEAD_PAIR), 1)
    low_half = lane < RET_DK

    def head_lanes(odd):
        return low_half if odd == 0 else jnp.logical_not(low_half)

    def rows(t):
        return slice(t * n, (t + 1) * n)

    def retention_state_update(t, hd):
        p = hd // 2
        k2 = qk_ref[rows(t), RET_QK + p * HEAD_PAIR:RET_QK + (p + 1) * HEAD_PAIR]
        v = vr_ref[rows(t), hd * RET_DV:(hd + 1) * RET_DV]
        k_dec = (k2.astype(F32) * kdec_ref[hd]).astype(MXU_DTYPE)
        state_ref[hd] = state_ref[hd] * tile_decay[hd] + _dot_tn(k_dec, v)

    def retention_first(t, hd):
        p, odd = divmod(hd, 2)
        q2 = qk_ref[rows(t), p * HEAD_PAIR:(p + 1) * HEAD_PAIR]
        k2 = qk_ref[rows(t), RET_QK + p * HEAD_PAIR:RET_QK + (p + 1) * HEAD_PAIR]
        qm = jnp.where(head_lanes(odd), q2, jnp.zeros_like(q2))
        v = vr_ref[rows(t), hd * RET_DV:(hd + 1) * RET_DV]
        raw = _dot_nt(qm, k2)
        read = _dot(qm, state_ref[hd].astype(MXU_DTYPE))
        if t + 1 < MIX_SUB:
            retention_state_update(t, hd)
        return raw, read, v

    def retention_second(t, hd, raw, read, v):
        intra = _dot((raw * decay_ref[hd]).astype(MXU_DTYPE), v)
        ret = intra + qdec_ref[hd] * read
        mu = jnp.mean(ret, axis=-1, keepdims=True)
        rc = ret - mu
        var = jnp.mean(rc * rc, axis=-1, keepdims=True)
        gn = rc * lax.rsqrt(var + LN_EPS)
        gate = gr_ref[rows(t), hd * RET_DV:(hd + 1) * RET_DV].astype(F32)
        yr_ref[rows(t), hd * RET_DV:(hd + 1) * RET_DV] = (gate * _sigmoid(gate) * gn).astype(MXU_DTYPE)

    pen_prev = jnp.where(i >= 1, 0.0, NEG_BIG).astype(F32)

    def key_block(t, kb):
        j = t + kb + (MIX_SUB - 2)
        if j < MIX_SUB:
            return kp_ref, vp_ref, rows(j), pen_prev
        return kc_ref, vc_ref, rows(j - MIX_SUB), None

    def attention_first(t, hd):
        p, odd = divmod(hd, 2)
        cols = slice(p * HEAD_PAIR, (p + 1) * HEAD_PAIR)
        qs = (qa_ref[rows(t), cols].astype(F32) * (ATT_DH ** -0.5 * LOG2E)).astype(MXU_DTYPE)
        qm = jnp.where(head_lanes(odd), qs, jnp.zeros_like(qs))
        s = []
        for kb in range(3):
            k_ref, _, krows, pen = key_block(t, kb)
            sk = _dot_nt(qm, k_ref[krows, cols]) + bias_ref[hd, :, kb * n:(kb + 1) * n]
            if pen is not None:
                sk = sk + pen
            s.append(sk)
        return (s,)

    def attention_second(t, hd, s):
        p, odd = divmod(hd, 2)
        cols = slice(p * HEAD_PAIR, (p + 1) * HEAD_PAIR)
        m = jnp.maximum(jnp.maximum(jnp.max(s[0], axis=-1, keepdims=True),
                                    jnp.max(s[1], axis=-1, keepdims=True)),
                        jnp.max(s[2], axis=-1, keepdims=True))
        e = [jnp.exp2(sk - m) for sk in s]
        denom = (jnp.sum(e[0], axis=-1, keepdims=True) + jnp.sum(e[1], axis=-1, keepdims=True)
                 + jnp.sum(e[2], axis=-1, keepdims=True))
        pv = None
        for kb in range(3):
            _, v_ref, krows, _ = key_block(t, kb)
            term = _dot(e[kb].astype(MXU_DTYPE), v_ref[krows, cols])
            pv = term if pv is None else pv + term
        out = (pv / denom).astype(MXU_DTYPE)
        lo = odd * ATT_DH
        ya_ref[rows(t), p * HEAD_PAIR + lo:p * HEAD_PAIR + lo + ATT_DH] = out[:, lo:lo + ATT_DH]

    work = []
    for t in range(MIX_SUB):
        for hd in range(max(RET_HEADS, ATT_HEADS)):
            if hd < RET_HEADS:
                work.append((retention_first, retention_second, t, hd))
            if hd < ATT_HEADS:
                work.append((attention_first, attention_second, t, hd))
    ahead = 2
    pending = [first(t, hd) for first, _, t, hd in work[:ahead]]
    for j, (_, second, t, hd) in enumerate(work):
        if j + ahead < len(work):
            first, _, t_next, hd_next = work[j + ahead]
            pending.append(first(t_next, hd_next))
        second(t, hd, *pending.pop(0))

    pr = _dot(yr_ref[...], wpr_ref[...])
    pa = _dot(ya_ref[...], wpa_ref[...])
    gate_r = jnp.concatenate([gate_r0_ref[...], gate_r1_ref[...]], axis=1).astype(F32)
    gate_a = jnp.concatenate([gate_a0_ref[...], gate_a1_ref[...]], axis=1).astype(F32)
    merged = _sigmoid(gate_r) * pr + _sigmoid(gate_a) * pa
    mix = _dot(merged.astype(MXU_DTYPE), wout_ref[...])
    for hd in range(RET_HEADS):
        retention_state_update(MIX_SUB - 1, hd)
    o_ref[...] = _layer_norm(DEEPNORM_ALPHA * h_ref[...] + mix, g_ref[...], b_ref[...])


def _mixer(z, h, B, S, tables, bias, wpr, wpa, wout, ln_g, ln_b):
    decay, q_decay, k_decay, tile_decay = tables
    T = B * S
    n = SEQ_TILE
    step_rows = MIX_SUB * n
    nt = S // step_rows
    assert MIX_SUB >= 2 and S % step_rows == 0

    def zspec(width, col_off, back=0):
        cb = col_off // width
        return pl.BlockSpec((step_rows, width), lambda b, i: (b * nt + jnp.maximum(i - back, 0), cb))

    row = lambda b, i: (b * nt + i, 0)
    c2 = lambda b, i: (0, 0)
    c3 = lambda b, i: (0, 0, 0)
    one = pl.Buffered(1)
    in_specs = [
        zspec(2 * RET_QK, Z_QK), zspec(RET_V, Z_VR), zspec(RET_V, Z_GR),
        zspec(GATE_HALF, Z_GATE_R), zspec(GATE_HALF, Z_GATE_R + GATE_HALF),
        zspec(GATE_HALF, Z_GATE_A), zspec(GATE_HALF, Z_GATE_A + GATE_HALF), zspec(ATT_W, Z_QA),
        zspec(ATT_W, Z_KA, 1), zspec(ATT_W, Z_KA, 0),
        zspec(ATT_W, Z_VA, 1), zspec(ATT_W, Z_VA, 0),
        pl.BlockSpec((step_rows, D_MODEL), row),
        pl.BlockSpec((RET_HEADS, n, n), c3, pipeline_mode=one),
        pl.BlockSpec((RET_HEADS, n, HEAD_PAIR), c3, pipeline_mode=one),
        pl.BlockSpec((RET_HEADS, n, HEAD_PAIR), c3, pipeline_mode=one),
        pl.BlockSpec((ATT_HEADS, n, 3 * n), c3, pipeline_mode=one),
        pl.BlockSpec((RET_V, D_MODEL), c2, pipeline_mode=one),
        pl.BlockSpec((ATT_W, D_MODEL), c2, pipeline_mode=one),
        pl.BlockSpec((D_MODEL, D_MODEL), c2, pipeline_mode=one),
        pl.BlockSpec((1, D_MODEL), c2),
        pl.BlockSpec((1, D_MODEL), c2),
    ]
    return pl.pallas_call(
        functools.partial(_mixer_kernel, tile_decay),
        grid=(B, nt),
        in_specs=in_specs,
        out_specs=pl.BlockSpec((step_rows, D_MODEL), row),
        out_shape=jax.ShapeDtypeStruct((T, D_MODEL), F32),
        scratch_shapes=[
            pltpu.VMEM((RET_HEADS, HEAD_PAIR, RET_DV), F32),
            pltpu.VMEM((step_rows, RET_V), MXU_DTYPE),
            pltpu.VMEM((step_rows, ATT_W), MXU_DTYPE),
        ],
        compiler_params=_params("arbitrary", "arbitrary"),
        name="mixer",
    )(*([z] * 12), h, decay, q_decay, k_decay, bias, wpr, wpa, wout, ln_g, ln_b)


def _kv_kernel(m_ref, w_ref, o_ref):
    o_ref[...] = _dot(m_ref[...].astype(MXU_DTYPE), w_ref[...]).astype(o_ref.dtype)


def _mem_kv(mem2d, w_kv):
    M = mem2d.shape[0]
    tm = 256
    return pl.pallas_call(
        _kv_kernel,
        grid=(M // tm,),
        in_specs=[pl.BlockSpec((tm, D_MODEL), lambda i: (i, 0)),
                  pl.BlockSpec((D_MODEL, 2 * D_MODEL), lambda i: (0, 0), pipeline_mode=pl.Buffered(1))],
        out_specs=pl.BlockSpec((tm, 2 * D_MODEL), lambda i: (i, 0)),
        out_shape=jax.ShapeDtypeStruct((M, 2 * D_MODEL), MXU_DTYPE),
        compiler_params=_params("parallel"),
        name="mem_kv",
    )(mem2d, w_kv)


ROUTE_ROWS = 8
GROUP_ROW0 = 0
EXPERT_ROW0 = 8
ROUTER_ROWS = 128


def _memattn_kernel(h_ref, kv_ref, wq_ref, wo_ref, g_ref, b_ref, wr_ref, br_ref, tri_ref,
                    h2_ref, rl_ref, rr_ref, cnt_ref, o_scr):
    n = MEM_TILE
    h1 = h_ref[...]
    q = (_dot(h1.astype(MXU_DTYPE), wq_ref[...]) * (MEM_DH ** -0.5 * LOG2E)).astype(MXU_DTYPE)
    scores = [_dot_nt(q[:, hd * MEM_DH:(hd + 1) * MEM_DH], kv_ref[:, hd * MEM_DH:(hd + 1) * MEM_DH])
              for hd in range(MEM_HEADS)]
    for hd in range(MEM_HEADS):
        cols = slice(hd * MEM_DH, (hd + 1) * MEM_DH)
        s = scores[hd]
        m = jnp.max(s, axis=-1, keepdims=True)
        e = jnp.exp2(s - m)
        denom = jnp.sum(e, axis=-1, keepdims=True)
        v = kv_ref[:, D_MODEL + hd * MEM_DH:D_MODEL + (hd + 1) * MEM_DH]
        o_scr[:, cols] = (_dot(e.astype(MXU_DTYPE), v) / denom).astype(MXU_DTYPE)
    cross = _dot(o_scr[...], wo_ref[...])
    h2 = _layer_norm(DEEPNORM_ALPHA * h1 + cross, g_ref[...], b_ref[...])
    h2_ref[...] = h2

    logits = _dot_nt(wr_ref[...], h2.astype(MXU_DTYPE)) + br_ref[...]
    glog = logits[GROUP_ROW0:GROUP_ROW0 + N_GROUPS]
    gmax = jnp.max(glog, axis=0, keepdims=True)
    g_w = 1.0 / jnp.sum(jnp.exp(glog - gmax), axis=0, keepdims=True)
    giota = lax.broadcasted_iota(jnp.int32, glog.shape, 0)
    g_idx = jnp.min(jnp.where(glog == gmax, giota, N_GROUPS), axis=0, keepdims=True)
    el = jnp.zeros((EXPERTS_PER_GROUP, n), F32)
    for g in range(N_GROUPS):
        r0 = EXPERT_ROW0 + g * EXPERTS_PER_GROUP
        el = jnp.where(g_idx == g, logits[r0:r0 + EXPERTS_PER_GROUP], el)
    eiota = lax.broadcasted_iota(jnp.int32, el.shape, 0)
    m1 = jnp.max(el, axis=0, keepdims=True)
    i1 = jnp.min(jnp.where(el == m1, eiota, EXPERTS_PER_GROUP), axis=0, keepdims=True)
    el2 = jnp.where(eiota == i1, -jnp.inf, el)
    m2 = jnp.max(el2, axis=0, keepdims=True)
    i2 = jnp.min(jnp.where(el2 == m2, eiota, EXPERTS_PER_GROUP), axis=0, keepdims=True)
    r = jnp.exp(m2 - m1)
    w1 = g_w / (1.0 + r)
    w2 = g_w * r / (1.0 + r)
    e1 = g_idx * EXPERTS_PER_GROUP + i1
    e2 = g_idx * EXPERTS_PER_GROUP + i2

    xiota = lax.broadcasted_iota(jnp.int32, (N_EXPERTS, n), 0)
    oh1 = xiota == e1
    oh2 = xiota == e2
    cnt = jnp.where(oh1, 1.0, 0.0) + jnp.where(oh2, 1.0, 0.0)
    before = _dot(cnt.astype(MXU_DTYPE), tri_ref[...])
    rank1 = jnp.sum(jnp.where(oh1, before, 0.0), axis=0, keepdims=True)
    rank2 = jnp.sum(jnp.where(oh2, before, 0.0), axis=0, keepdims=True)
    for j in range(n // SEQ_TILE):
        tile_cnt = jnp.sum(cnt[:, j * SEQ_TILE:(j + 1) * SEQ_TILE], axis=1, keepdims=True)
        cnt_ref[j] = jnp.broadcast_to(tile_cnt, (N_EXPERTS, LANES))

    zero = jnp.zeros((1, n), F32)
    rec = jnp.concatenate([e1.astype(F32), e2.astype(F32), w1, w2, rank1, rank2, zero, zero], axis=0)
    rl_ref[...] = rec
    rec_full = jnp.concatenate([rec, jnp.zeros((LANES - ROUTE_ROWS, n), F32)], axis=0)
    rr_ref[...] = rec_full.T


def _memattn(h1, kv, B, S, wq, wo, ln_g, ln_b, wr_t, br_col, tri):
    T = B * S
    n = MEM_TILE
    nt = S // n
    sub = n // SEQ_TILE
    row = lambda b, i: (b * nt + i, 0)
    c2 = lambda b, i: (0, 0)
    one = pl.Buffered(1)
    return pl.pallas_call(
        _memattn_kernel,
        grid=(B, nt),
        in_specs=[
            pl.BlockSpec((n, D_MODEL), row),
            pl.BlockSpec((kv.shape[0] // B, 2 * D_MODEL), lambda b, i: (b, 0)),
            pl.BlockSpec((D_MODEL, D_MODEL), c2, pipeline_mode=one),
            pl.BlockSpec((D_MODEL, D_MODEL), c2, pipeline_mode=one),
            pl.BlockSpec((1, D_MODEL), c2),
            pl.BlockSpec((1, D_MODEL), c2),
            pl.BlockSpec((ROUTER_ROWS, D_MODEL), c2, pipeline_mode=one),
            pl.BlockSpec((ROUTER_ROWS, 1), c2),
            pl.BlockSpec((n, n), c2, pipeline_mode=one),
        ],
        out_specs=[
            pl.BlockSpec((n, D_MODEL), row),
            pl.BlockSpec((ROUTE_ROWS, n), lambda b, i: (0, b * nt + i)),
            pl.BlockSpec((n, LANES), row),
            pl.BlockSpec((sub, N_EXPERTS, LANES), lambda b, i: (b * nt + i, 0, 0)),
        ],
        out_shape=[
            jax.ShapeDtypeStruct((T, D_MODEL), F32),
            jax.ShapeDtypeStruct((ROUTE_ROWS, T), F32),
            jax.ShapeDtypeStruct((T, LANES), F32),
            jax.ShapeDtypeStruct((T // SEQ_TILE, N_EXPERTS, LANES), F32),
        ],
        scratch_shapes=[pltpu.VMEM((n, D_MODEL), MXU_DTYPE)],
        compiler_params=_params("parallel", "parallel"),
        name="memattn_router",
    )(h1, kv, wq, wo, ln_g, ln_b, wr_t, br_col, tri)


CHUNK_ROWS = SUBLANES
LOCAL_ROWS = 768
N_CHUNKS = LOCAL_ROWS // CHUNK_ROWS
assert TOP_K * SEQ_TILE + N_EXPERTS * (CHUNK_ROWS - 1) <= LOCAL_ROWS


def _unpack_rows(packed):
    lo = lax.bitcast_convert_type(packed << 16, F32)
    hi = lax.bitcast_convert_type(packed & jnp.uint32(0xFFFF0000), F32)
    return lo, hi


def _pack_rows(y):
    bits = lax.bitcast_convert_type(y.astype(jnp.bfloat16).astype(F32), jnp.uint32)
    return (bits[:, :PACK_COLS] >> 16) | bits[:, PACK_COLS:]


def _chunk_plan(counts, n_blocks):
    nt = counts.shape[0]
    c = counts.astype(jnp.int32)
    incl = jnp.asarray(np.tril(np.ones((N_EXPERTS, N_EXPERTS), bool)))
    padc = (c + CHUNK_ROWS - 1) // CHUNK_ROWS * CHUNK_ROWS
    lend = jnp.sum(jnp.where(incl[None], padc[:, None, :], 0), axis=2)
    loff = lend - padc
    earlier = jnp.asarray(np.tril(np.ones((nt, nt), bool), -1))
    cbefore = jnp.sum(jnp.where(earlier[:, :, None], padc[None, :, :], 0), axis=1)
    total = jnp.sum(padc, axis=0)
    region = (total + EXPERT_BLOCK - 1) // EXPERT_BLOCK * EXPERT_BLOCK
    gend = jnp.sum(jnp.where(incl, region[None, :], 0), axis=1)
    gstart = gend - region
    delta = gstart[None, :] + cbefore - loff
    k_row = jnp.arange(N_CHUNKS, dtype=jnp.int32) * CHUNK_ROWS
    e_of_chunk = jnp.sum((lend[:, None, :] <= k_row[None, :, None]).astype(jnp.int32), axis=2)
    used = e_of_chunk < N_EXPERTS
    onehot = e_of_chunk[:, :, None] == jnp.arange(N_EXPERTS, dtype=jnp.int32)[None, None, :]
    grow = jnp.sum(jnp.where(onehot, delta[:, None, :], 0), axis=2) + k_row[None, :]
    parity = (jnp.arange(nt, dtype=jnp.int32) % 2)[:, None]
    dump = n_blocks * EXPERT_BLOCK + parity * LOCAL_ROWS + k_row[None, :]
    dst = jnp.where(used, grow, dump).reshape(-1)
    src = jnp.where(used, grow, 0).reshape(-1)
    block_first = jnp.arange(n_blocks, dtype=jnp.int32) * EXPERT_BLOCK
    block_expert = jnp.minimum(jnp.sum((gend[None, :] <= block_first[:, None]).astype(jnp.int32), axis=1),
                               N_EXPERTS - 1)
    n_used = (gend[-1:] // EXPERT_BLOCK).astype(jnp.int32)
    tail = jnp.concatenate([gstart + total, (region - total) // CHUNK_ROWS, n_used]).astype(jnp.int32)
    loff_f = loff.astype(F32)
    loff_col = jnp.broadcast_to(loff_f[:, :, None], (nt, N_EXPERTS, LANES))
    loff_row = jnp.broadcast_to(jnp.pad(loff_f, ((0, 0), (0, LANES - N_EXPERTS)))[:, None, :],
                                (nt, SUBLANES, LANES))
    return dst, src, tail, block_expert, n_used, loff_col, loff_row


def _dispatch_kernel(dst_ref, tail_ref, h_ref, rl_ref, loff_ref, xb_ref, sbuf, zbuf, sem, zsem):
    i = pl.program_id(0)
    n = SEQ_TILE
    rl = rl_ref[...]
    xiota = lax.broadcasted_iota(jnp.int32, (N_EXPERTS, n), 0)
    loff = loff_ref[0][:, 0:1]
    slots = []
    for k in range(TOP_K):
        e = rl[k:k + 1].astype(jnp.int32)
        base = jnp.sum(jnp.where(xiota == e, loff, 0.0), axis=0, keepdims=True)
        slots.append((base + rl[4 + k:5 + k]).astype(jnp.int32))
    riota = lax.broadcasted_iota(jnp.int32, (LOCAL_ROWS, n), 0)
    perm = jnp.where(riota == slots[0], 1.0, jnp.where(riota == slots[1], 1.0, 0.0)).astype(MXU_DTYPE)
    packed = _pack_rows(_dot(perm, h_ref[...].astype(MXU_DTYPE)))

    def wait_all(slot):
        pltpu.make_async_copy(sbuf.at[slot], xb_ref.at[pl.ds(0, LOCAL_ROWS)], sem.at[slot]).wait()

    slot = i % 2

    @pl.when(i >= 2)
    def _():
        wait_all(slot)

    sbuf.at[slot][...] = packed

    def body(k, carry):
        d = pl.multiple_of(dst_ref[i * N_CHUNKS + k], CHUNK_ROWS)
        src = sbuf.at[slot, pl.ds(pl.multiple_of(k * CHUNK_ROWS, CHUNK_ROWS), CHUNK_ROWS)]
        pltpu.make_async_copy(src, xb_ref.at[pl.ds(d, CHUNK_ROWS)], sem.at[slot]).start()
        return carry

    lax.fori_loop(0, N_CHUNKS, body, 0, unroll=8)

    @pl.when(i == pl.num_programs(0) - 1)
    def _():
        zbuf[...] = jnp.zeros_like(zbuf)

        def fill(start_copy):
            def per_expert(e, carry):
                first = pl.multiple_of(tail_ref[e], CHUNK_ROWS)

                def per_chunk(c, inner):
                    row = pl.multiple_of(first + c * CHUNK_ROWS, CHUNK_ROWS)
                    copy = pltpu.make_async_copy(zbuf, xb_ref.at[pl.ds(row, CHUNK_ROWS)], zsem)
                    if start_copy:
                        copy.start()
                    else:
                        copy.wait()
                    return inner

                lax.fori_loop(0, tail_ref[N_EXPERTS + e], per_chunk, 0)
                return carry

            lax.fori_loop(0, N_EXPERTS, per_expert, 0)

        fill(True)
        wait_all(1 - slot)
        wait_all(slot)
        fill(False)

        zblock = sbuf.at[0, pl.ds(0, EXPERT_BLOCK)]
        zblock[...] = jnp.zeros_like(zblock)
        n_all = xb_ref.shape[0] // EXPERT_BLOCK

        def fill_blocks(start_copy):
            def per_block(p, carry):
                row = pl.multiple_of(p * EXPERT_BLOCK, EXPERT_BLOCK)
                copy = pltpu.make_async_copy(zblock, xb_ref.at[pl.ds(row, EXPERT_BLOCK)], zsem)
                if start_copy:
                    copy.start()
                else:
                    copy.wait()
                return carry

            lax.fori_loop(tail_ref[2 * N_EXPERTS], n_all, per_block, 0)

        fill_blocks(True)
        fill_blocks(False)


def _dispatch(dst, tail, h2, rl, loff_col, n_blocks):
    T = h2.shape[0]
    n = SEQ_TILE
    rows = n_blocks * EXPERT_BLOCK + 2 * LOCAL_ROWS
    grid_spec = pltpu.PrefetchScalarGridSpec(
        num_scalar_prefetch=2,
        grid=(T // n,),
        in_specs=[
            pl.BlockSpec((n, D_MODEL), lambda i, d, t: (i, 0)),
            pl.BlockSpec((ROUTE_ROWS, n), lambda i, d, t: (0, i)),
            pl.BlockSpec((1, N_EXPERTS, LANES), lambda i, d, t: (i, 0, 0)),
        ],
        out_specs=pl.BlockSpec(memory_space=pl.ANY),
        scratch_shapes=[pltpu.VMEM((2, LOCAL_ROWS, PACK_COLS), jnp.uint32),
                        pltpu.VMEM((CHUNK_ROWS, PACK_COLS), jnp.uint32),
                        pltpu.SemaphoreType.DMA((2,)), pltpu.SemaphoreType.DMA],
    )
    assert T // n >= 2
    return pl.pallas_call(
        _dispatch_kernel,
        grid_spec=grid_spec,
        out_shape=jax.ShapeDtypeStruct((rows, PACK_COLS), jnp.uint32),
        compiler_params=_params("arbitrary"),
        name="dispatch",
    )(dst, tail, h2, rl, loff_col)


def _expert_kernel(be_ref, nused_ref, x_ref, wg_ref, wu_ref, wd_ref, y_ref, wg_b, wu_b, wd_b):
    p = pl.program_id(0)
    new_expert = jnp.logical_or(p == 0, be_ref[p] != be_ref[jnp.maximum(p - 1, 0)])

    @pl.when(new_expert)
    def _():
        wg_b[...] = wg_ref[...].astype(MXU_DTYPE)
        wu_b[...] = wu_ref[...].astype(MXU_DTYPE)
        wd_b[...] = wd_ref[...].astype(MXU_DTYPE)

    @pl.when(p < nused_ref[0])
    def _():
        lo, hi = _unpack_rows(x_ref[...])
        x = jnp.concatenate([lo.astype(MXU_DTYPE), hi.astype(MXU_DTYPE)], axis=1)
        gate = _dot(x, wg_b[...])
        up = _dot(x, wu_b[...])
        hid = (gate * _sigmoid(gate) * up).astype(MXU_DTYPE)
        y_ref[...] = _pack_rows(_dot(hid, wd_b[...]))

    @pl.when(p >= nused_ref[0])
    def _():
        y_ref[...] = jnp.zeros_like(y_ref)


def _experts(block_expert, n_used, xb, wg, wu, wd, layer):
    nb = block_expert.shape[0]
    n_slots = nb * EXPERT_BLOCK
    grid_spec = pltpu.PrefetchScalarGridSpec(
        num_scalar_prefetch=2,
        grid=(nb,),
        in_specs=[
            pl.BlockSpec((EXPERT_BLOCK, PACK_COLS), lambda p, be, nu: (jnp.minimum(p, jnp.maximum(nu[0] - 1, 0)), 0)),
            pl.BlockSpec((None, None, D_MODEL, D_EXPERT), lambda p, be, nu: (layer, be[p], 0, 0)),
            pl.BlockSpec((None, None, D_MODEL, D_EXPERT), lambda p, be, nu: (layer, be[p], 0, 0)),
            pl.BlockSpec((None, None, D_EXPERT, D_MODEL), lambda p, be, nu: (layer, be[p], 0, 0)),
        ],
        out_specs=pl.BlockSpec((EXPERT_BLOCK, PACK_COLS), lambda p, be, nu: (p, 0)),
        scratch_shapes=[pltpu.VMEM((D_MODEL, D_EXPERT), MXU_DTYPE), pltpu.VMEM((D_MODEL, D_EXPERT), MXU_DTYPE),
                        pltpu.VMEM((D_EXPERT, D_MODEL), MXU_DTYPE)],
    )
    return pl.pallas_call(
        _expert_kernel,
        grid_spec=grid_spec,
        out_shape=jax.ShapeDtypeStruct((n_slots, PACK_COLS), jnp.uint32),
        compiler_params=_params("arbitrary"),
        name="experts",
    )(block_expert, n_used, xb, wg, wu, wd)


def _combine_kernel(src_ref, yb_ref, rr_ref, loff_ref, h_ref, g_ref, b_ref, o_ref, ybuf, sem):
    i = pl.program_id(0)
    n = SEQ_TILE
    slot = i % 2

    def fetch(tile, into):
        def body(k, carry):
            s = pl.multiple_of(src_ref[tile * N_CHUNKS + k], CHUNK_ROWS)
            dst = ybuf.at[into, pl.ds(pl.multiple_of(k * CHUNK_ROWS, CHUNK_ROWS), CHUNK_ROWS)]
            pltpu.make_async_copy(yb_ref.at[pl.ds(s, CHUNK_ROWS)], dst, sem.at[into]).start()
            return carry

        lax.fori_loop(0, N_CHUNKS, body, 0, unroll=8)

    @pl.when(i == 0)
    def _():
        fetch(0, 0)

    @pl.when(i + 1 < pl.num_programs(0))
    def _():
        fetch(i + 1, 1 - slot)

    rr = rr_ref[...]
    lane = lax.broadcasted_iota(jnp.int32, (n, LANES), 1)
    loff = loff_ref[0][0:1, :]
    ciota = lax.broadcasted_iota(jnp.int32, (n, LOCAL_ROWS), 1)
    mix = jnp.zeros((n, LOCAL_ROWS), F32)
    for k in range(TOP_K):
        e = rr[:, k:k + 1].astype(jnp.int32)
        base = jnp.sum(jnp.where(lane == e, loff, 0.0), axis=1, keepdims=True)
        sorted_row = (base + rr[:, 4 + k:5 + k]).astype(jnp.int32)
        mix = jnp.where(ciota == sorted_row, rr[:, 2 + k:3 + k], mix)

    pltpu.make_async_copy(yb_ref.at[pl.ds(0, LOCAL_ROWS)], ybuf.at[slot], sem.at[slot]).wait()
    lo, hi = _unpack_rows(ybuf.at[slot][...])
    mixb = mix.astype(MXU_DTYPE)
    y = jnp.concatenate([_dot(mixb, lo.astype(MXU_DTYPE)), _dot(mixb, hi.astype(MXU_DTYPE))], axis=1)
    o_ref[...] = _layer_norm(DEEPNORM_ALPHA * h_ref[...] + y, g_ref[...], b_ref[...])


def _combine(src, yb, rr, loff_row, h2, ln_g, ln_b):
    T = h2.shape[0]
    n = SEQ_TILE
    row = lambda i, s: (i, 0)
    grid_spec = pltpu.PrefetchScalarGridSpec(
        num_scalar_prefetch=1,
        grid=(T // n,),
        in_specs=[
            pl.BlockSpec(memory_space=pl.ANY),
            pl.BlockSpec((n, LANES), row),
            pl.BlockSpec((1, SUBLANES, LANES), lambda i, s: (i, 0, 0)),
            pl.BlockSpec((n, D_MODEL), row),
            pl.BlockSpec((1, D_MODEL), lambda i, s: (0, 0)),
            pl.BlockSpec((1, D_MODEL), lambda i, s: (0, 0)),
        ],
        out_specs=pl.BlockSpec((n, D_MODEL), row),
        scratch_shapes=[pltpu.VMEM((2, LOCAL_ROWS, PACK_COLS), jnp.uint32), pltpu.SemaphoreType.DMA((2,))],
    )
    return pl.pallas_call(
        _combine_kernel,
        grid_spec=grid_spec,
        out_shape=jax.ShapeDtypeStruct((T, D_MODEL), F32),
        compiler_params=_params("arbitrary"),
        name="combine",
    )(src, yb, rr, loff_row, h2, ln_g, ln_b)


def _router_weights(w_group, b_group, w_route, b_route):
    wr = jnp.zeros((ROUTER_ROWS, D_MODEL), F32)
    wr = wr.at[GROUP_ROW0:GROUP_ROW0 + N_GROUPS].set(w_group.T)
    wr = wr.at[EXPERT_ROW0:EXPERT_ROW0 + N_EXPERTS].set(w_route.T)
    br = jnp.zeros((ROUTER_ROWS,), F32)
    br = br.at[GROUP_ROW0:GROUP_ROW0 + N_GROUPS].set(b_group)
    br = br.at[EXPERT_ROW0:EXPERT_ROW0 + N_EXPERTS].set(b_route.reshape(-1))
    return wr.astype(MXU_DTYPE), br.reshape(ROUTER_ROWS, 1)


def kernel(x, mem, positions, ln_in_g, ln_in_b, w_in, rel_bias, w_proj_ret, w_proj_att, w_out, ln1_g, ln1_b, w_q_mem, w_kv_mem, w_o_mem, ln2_g, ln2_b, w_group, b_group, w_route, b_route, w_gate, w_up, w_down, ln3_g, ln3_b):
    B, S, D = x.shape
    assert D == D_MODEL and S % PROJ_TILE == 0 and S % SEQ_TILE == 0 and S % MEM_TILE == 0
    T = B * S
    A = T * TOP_K
    n_blocks = -(-(A + (T // SEQ_TILE) * N_EXPERTS * (CHUNK_ROWS - 1) + N_EXPERTS * (EXPERT_BLOCK - 1)) // EXPERT_BLOCK)
    bf = MXU_DTYPE

    cos_t, sin_t = _rope_tables(positions)
    tables = _retention_tables()
    pos = np.arange(MEM_TILE)
    tri = jnp.asarray((pos[:, None] < pos[None, :]) & (pos[:, None] // SEQ_TILE == pos[None, :] // SEQ_TILE), bf)
    mem2d = mem.reshape(-1, D)
    row2 = lambda v: v.reshape(1, D)

    h = x.reshape(T, D)
    for l in range(DEPTH):
        w_in_b = _to_mxu(w_in, l)
        if l == 0:
            z, h = _inproj(h, row2(ln_in_g), row2(ln_in_b), w_in_b, cos_t, sin_t, True)
        else:
            (z,) = _inproj(h, row2(ln_in_g), row2(ln_in_b), w_in_b, cos_t, sin_t, False)
        bias = _attention_bias(rel_bias[l])
        h = _mixer(z, h, B, S, tables, bias, _to_mxu(w_proj_ret, l), _to_mxu(w_proj_att, l),
                   _to_mxu(w_out, l), row2(ln1_g[l]), row2(ln1_b[l]))

        kv = _mem_kv(mem2d, _to_mxu(w_kv_mem, l))
        wr_t, br_col = _router_weights(w_group[l], b_group[l], w_route[l], b_route[l])
        h2, rl, rr, counts = _memattn(h, kv, B, S, _to_mxu(w_q_mem, l), _to_mxu(w_o_mem, l),
                                      row2(ln2_g[l]), row2(ln2_b[l]), wr_t, br_col, tri)

        dst, src, tail, block_expert, n_used, loff_col, loff_row = _chunk_plan(counts[:, :, 0], n_blocks)
        xb = _dispatch(dst, tail, h2, rl, loff_col, n_blocks)
        yb = _experts(block_expert, n_used, xb, w_gate, w_up, w_down, l)
        h = _combine(src, yb, rr, loff_row, h2, row2(ln3_g[l]), row2(ln3_b[l]))
    return h.reshape(B, S, D)
```

```python
import functools

import jax
import jax.numpy as jnp
import numpy as np
from jax import lax
from jax.experimental import pallas as pl
from jax.experimental.pallas import tpu as pltpu

D_MODEL = 1024
DEPTH = 2
CHUNK = 64
RET_HEADS = 8
RET_DK = 64
RET_DV = 128
RET_QK = RET_HEADS * RET_DK
RET_V = RET_HEADS * RET_DV
ROPE_BASE = 10000.0
ATT_HEADS = 8
ATT_DH = 64
ATT_W = ATT_HEADS * ATT_DH
LEFT_CHUNKS = 8
MAX_REL = 256
MEM_HEADS = 4
MEM_DH = D_MODEL // MEM_HEADS
N_GROUPS = 4
EXPERTS_PER_GROUP = 8
N_EXPERTS = N_GROUPS * EXPERTS_PER_GROUP
TOP_K = 2
D_EXPERT = 512
LN_EPS = 1e-5
DEEPNORM_ALPHA = (2.0 * DEPTH) ** 0.25
IN_TOTAL = 2 * RET_QK + 2 * RET_V + 3 * ATT_W + 2 * D_MODEL

LANES = 128
SUBLANES = 8
VMEM_LIMIT_BYTES = 56 * 1024 * 1024

SEQ_TILE = 256
MEM_TILE = 2 * SEQ_TILE
MIX_SUB = 2
DC_SUB = 2
PROJ_TILE = 512
PROJ_COLS = 512
EXPERT_BLOCK = 512
HEAD_PAIR = 2 * RET_DK
NEG_BIG = -1e30
LOG2E = 1.4426950408889634

F32 = jnp.float32
MXU_DTYPE = jnp.bfloat16
PACK_COLS = D_MODEL // 2

Z_QK = 0
Z_VR = Z_QK + 2 * RET_QK
Z_GR = Z_VR + RET_V
Z_QA = Z_GR + RET_V
Z_KA = Z_QA + ATT_W
Z_VA = Z_KA + ATT_W
Z_GATE_R = Z_VA + ATT_W
Z_GATE_A = Z_GATE_R + D_MODEL
GATE_HALF = D_MODEL // 2


def _layer_norm(x, g, b):
    mu = jnp.mean(x, axis=-1, keepdims=True)
    xc = x - mu
    var = jnp.mean(xc * xc, axis=-1, keepdims=True)
    return xc * lax.rsqrt(var + LN_EPS) * g + b


def _sigmoid(x):
    return 1.0 / (1.0 + jnp.exp(-x))


def _dot(a, b):
    return jnp.dot(a, b, preferred_element_type=F32)


def _dot_nt(a, b):
    return lax.dot_general(a, b, (((1,), (1,)), ((), ())), preferred_element_type=F32)


def _dot_tn(a, b):
    return lax.dot_general(a, b, (((0,), (0,)), ((), ())), preferred_element_type=F32)


def _params(*semantics):
    return pltpu.CompilerParams(dimension_semantics=semantics, vmem_limit_bytes=VMEM_LIMIT_BYTES)


def _cast_kernel(x_ref, o_ref):
    o_ref[...] = x_ref[...].astype(o_ref.dtype)


def _to_mxu(w_stack, layer):
    _, r, c = w_stack.shape
    tr = 256
    return pl.pallas_call(
        _cast_kernel,
        grid=(r // tr,),
        in_specs=[pl.BlockSpec((None, tr, c), lambda i: (layer, i, 0))],
        out_specs=pl.BlockSpec((tr, c), lambda i: (i, 0)),
        out_shape=jax.ShapeDtypeStruct((r, c), MXU_DTYPE),
        compiler_params=_params("parallel"),
        name="cast_weight",
    )(w_stack)


def _rope_kernel(pos_ref, invf_ref, cos_ref, sin_ref):
    ang = pos_ref[...] * invf_ref[...]
    lane = lax.broadcasted_iota(jnp.int32, ang.shape, 1)
    s = jnp.sin(ang)
    cos_ref[...] = jnp.cos(ang)
    sin_ref[...] = jnp.where((lane % RET_DK) < RET_DK // 2, -s, s)


def _rope_tables(positions):
    T = positions.size
    half = RET_DK // 2
    inv_freq = 1.0 / (ROPE_BASE ** jnp.linspace(0.0, 1.0, half, dtype=F32))
    invf = jnp.tile(inv_freq, LANES // half).reshape(1, LANES)
    pos = jnp.broadcast_to(positions.reshape(T, 1).astype(F32), (T, LANES))
    tm = 1024
    spec = pl.BlockSpec((tm, LANES), lambda i: (i, 0))
    return pl.pallas_call(
        _rope_kernel,
        grid=(T // tm,),
        in_specs=[spec, pl.BlockSpec((1, LANES), lambda i: (0, 0))],
        out_specs=[spec, spec],
        out_shape=[jax.ShapeDtypeStruct((T, LANES), F32)] * 2,
        compiler_params=_params("parallel"),
        name="rope_tables",
    )(pos, invf)


def _rotary(a, cos, sin_signed):
    outs = []
    lane = lax.broadcasted_iota(jnp.int32, cos.shape, 1)
    first_half = (lane % RET_DK) < RET_DK // 2
    for g in range(a.shape[1] // LANES):
        x = a[:, g * LANES:(g + 1) * LANES]
        rot = jnp.where(first_half, pltpu.roll(x, LANES - RET_DK // 2, 1), pltpu.roll(x, RET_DK // 2, 1))
        outs.append(x * cos + rot * sin_signed)
    return jnp.concatenate(outs, axis=1)


def _inproj_kernel(apply_ln, h_ref, g_ref, b_ref, w_ref, cos_ref, sin_ref, z_ref, *h_out):
    x = h_ref[...]
    if apply_ln:
        x = _layer_norm(x, g_ref[...], b_ref[...])
        h_out[0][...] = x
    xb = x.astype(MXU_DTYPE)
    cos = cos_ref[...]
    sin = sin_ref[...]
    for c in range(IN_TOTAL // PROJ_COLS):
        lo = c * PROJ_COLS
        acc = _dot(xb, w_ref[:, lo:lo + PROJ_COLS])
        if lo < 2 * RET_QK:
            acc = _rotary(acc, cos, sin)
            if lo >= RET_QK:
                acc = acc * (RET_DK ** -0.5)
        z_ref[:, lo:lo + PROJ_COLS] = acc.astype(z_ref.dtype)


def _inproj(h, ln_g, ln_b, w_in_b, cos_t, sin_t, apply_ln):
    T = h.shape[0]
    tm = PROJ_TILE
    row = lambda i: (i, 0)
    const = lambda i: (0, 0)
    out_shape = [jax.ShapeDtypeStruct((T, IN_TOTAL), MXU_DTYPE)]
    out_specs = [pl.BlockSpec((tm, IN_TOTAL), row)]
    if apply_ln:
        out_shape.append(jax.ShapeDtypeStruct((T, D_MODEL), F32))
        out_specs.append(pl.BlockSpec((tm, D_MODEL), row))
    res = pl.pallas_call(
        functools.partial(_inproj_kernel, apply_ln),
        grid=(T // tm,),
        in_specs=[
            pl.BlockSpec((tm, D_MODEL), row),
            pl.BlockSpec((1, D_MODEL), const),
            pl.BlockSpec((1, D_MODEL), const),
            pl.BlockSpec((D_MODEL, IN_TOTAL), const, pipeline_mode=pl.Buffered(1)),
            pl.BlockSpec((tm, LANES), row),
            pl.BlockSpec((tm, LANES), row),
        ],
        out_specs=out_specs,
        out_shape=out_shape,
        compiler_params=_params("parallel"),
        name="inproj",
    )(h, ln_g, ln_b, w_in_b, cos_t, sin_t)
    return res


def _retention_tables():
    n = SEQ_TILE
    log_g = np.log1p(-np.exp2(-5.0 - np.arange(RET_HEADS, dtype=np.float64)))
    i = np.arange(n)
    same = (i[:, None] // CHUNK) == (i[None, :] // CHUNK)
    earlier = (i[None, :] // CHUNK) < (i[:, None] // CHUNK)
    diff = i[:, None] - i[None, :]
    expo = np.where(same, np.abs(diff), np.where(earlier, diff, 0)).astype(np.float64)
    decay = np.exp(log_g[:, None, None] * expo) * (same | earlier)
    q_decay = np.exp(log_g[:, None] * (i + 1.0))
    k_decay = np.exp(log_g[:, None] * (n - 1.0 - i))
    tile_decay = np.exp(log_g * n)
    wide = lambda t: np.broadcast_to(t[:, :, None], t.shape + (HEAD_PAIR,))
    owner = (np.arange(HEAD_PAIR)[None, :] // RET_DK) == (np.arange(RET_HEADS)[:, None] % 2)
    k_table = wide(k_decay) * owner[:, None, :]
    return (jnp.asarray(decay, F32), jnp.asarray(wide(q_decay), F32), jnp.asarray(k_table, F32),
            tuple(float(v) for v in tile_decay))


def _attention_bias(rel_bias):
    n = SEQ_TILE
    width = 3 * n
    pad = LEFT_CHUNKS * CHUNK
    assert pad == 2 * n and n <= MAX_REL
    heads = rel_bias.shape[0]
    length = width + n
    near = rel_bias[:, MAX_REL - n:].astype(F32)
    far = jnp.broadcast_to(rel_bias[:, -1:].astype(F32), (heads, length - near.shape[1]))
    g_rev = jnp.concatenate([near, far], axis=1)[:, ::-1]
    rows = jnp.tile(g_rev, (1, n + 1))[:, :n * (length - 1)].reshape(heads, n, length - 1)
    bias = rows[:, :, n - 1:n - 1 + width]
    qi = np.arange(n)
    kj = np.arange(width)
    q_chunk = (pad + qi) // CHUNK
    k_chunk = kj // CHUNK
    in_band = (k_chunk[None, :] <= q_chunk[:, None]) & (k_chunk[None, :] >= q_chunk[:, None] - LEFT_CHUNKS)
    return jnp.where(jnp.asarray(in_band)[None], bias * LOG2E, NEG_BIG)


def _mixer_kernel(tile_decay,
                  qk_ref, vr_ref, gr_ref, gate_r0_ref, gate_r1_ref, gate_a0_ref, gate_a1_ref, qa_ref,
                  kp_ref, kc_ref, vp_ref, vc_ref,
                  h_ref, decay_ref, qdec_ref, kdec_ref, bias_ref,
                  wpr_ref, wpa_ref, wout_ref, g_ref, b_ref,
                  o_ref, state_ref, yr_ref, ya_ref):
    i = pl.program_id(1)

    @pl.when(i == 0)
    def _():
        state_ref[...] = jnp.zeros_like(state_ref)

    n = SEQ_TILE
    lane = lax.broadcasted_iota(jnp.int32, (n, HEAD_PAIR), 1)
    low_half = lane < RET_DK

    def head_lanes(odd):
        return low_half if odd == 0 else jnp.logical_not(low_half)

    def rows(t):
        return slice(t * n, (t + 1) * n)

    def retention_state_update(t, hd):
        p = hd // 2
        k2 = qk_ref[rows(t), RET_QK + p * HEAD_PAIR:RET_QK + (p + 1) * HEAD_PAIR]
        v = vr_ref[rows(t), hd * RET_DV:(hd + 1) * RET_DV]
        k_dec = (k2.astype(F32) * kdec_ref[hd]).astype(MXU_DTYPE)
        state_ref[hd] = state_ref[hd] * tile_decay[hd] + _dot_tn(k_dec, v)

    def retention_first(t, hd):
        p, odd = divmod(hd, 2)
        q2 = qk_ref[rows(t), p * HEAD_PAIR:(p + 1) * HEAD_PAIR]
        k2 = qk_ref[rows(t), RET_QK + p * HEAD_PAIR:RET_QK + (p + 1) * HEAD_PAIR]
        qm = jnp.where(head_lanes(odd), q2, jnp.zeros_like(q2))
        v = vr_ref[rows(t), hd * RET_DV:(hd + 1) * RET_DV]
        raw = _dot_nt(qm, k2)
        read = _dot(qm, state_ref[hd].astype(MXU_DTYPE))
        if t + 1 < MIX_SUB:
            retention_state_update(t, hd)
        return raw, read, v

    def retention_second(t, hd, raw, read, v):
        intra = _dot((raw * decay_ref[hd]).astype(MXU_DTYPE), v)
        ret = intra + qdec_ref[hd] * read
        mu = jnp.mean(ret, axis=-1, keepdims=True)
        rc = ret - mu
        var = jnp.mean(rc * rc, axis=-1, keepdims=True)
        gn = rc * lax.rsqrt(var + LN_EPS)
        gate = gr_ref[rows(t), hd * RET_DV:(hd + 1) * RET_DV].astype(F32)
        yr_ref[rows(t), hd * RET_DV:(hd + 1) * RET_DV] = (gate * _sigmoid(gate) * gn).astype(MXU_DTYPE)

    pen_prev = jnp.where(i >= 1, 0.0, NEG_BIG).astype(F32)

    def key_block(t, kb):
        j = t + kb + (MIX_SUB - 2)
        if j < MIX_SUB:
            return kp_ref, vp_ref, rows(j), pen_prev
        return kc_ref, vc_ref, rows(j - MIX_SUB), None

    def attention_first(t, hd):
        p, odd = divmod(hd, 2)
        cols = slice(p * HEAD_PAIR, (p + 1) * HEAD_PAIR)
        qs = (qa_ref[rows(t), cols].astype(F32) * (ATT_DH ** -0.5 * LOG2E)).astype(MXU_DTYPE)
        qm = jnp.where(head_lanes(odd), qs, jnp.zeros_like(qs))
        s = []
        for kb in range(3):
            k_ref, _, krows, pen = key_block(t, kb)
            sk = _dot_nt(qm, k_ref[krows, cols]) + bias_ref[hd, :, kb * n:(kb + 1) * n]
            if pen is not None:
                sk = sk + pen
            s.append(sk)
        return (s,)

    def attention_second(t, hd, s):
        p, odd = divmod(hd, 2)
        cols = slice(p * HEAD_PAIR, (p + 1) * HEAD_PAIR)
        m = jnp.maximum(jnp.maximum(jnp.max(s[0], axis=-1, keepdims=True),
                                    jnp.max(s[1], axis=-1, keepdims=True)),
                        jnp.max(s[2], axis=-1, keepdims=True))
        e = [jnp.exp2(sk - m) for sk in s]
        denom = (jnp.sum(e[0], axis=-1, keepdims=True) + jnp.sum(e[1], axis=-1, keepdims=True)
                 + jnp.sum(e[2], axis=-1, keepdims=True))
        pv = None
        for kb in range(3):
            _, v_ref, krows, _ = key_block(t, kb)
            term = _dot(e[kb].astype(MXU_DTYPE), v_ref[krows, cols])
            pv = term if pv is None else pv + term
        out = (pv / denom).astype(MXU_DTYPE)
        lo = odd * ATT_DH
        ya_ref[rows(t), p * HEAD_PAIR + lo:p * HEAD_PAIR + lo + ATT_DH] = out[:, lo:lo + ATT_DH]

    work = []
    for t in range(MIX_SUB):
        for hd in range(max(RET_HEADS, ATT_HEADS)):
            if hd < RET_HEADS:
                work.append((retention_first, retention_second, t, hd))
            if hd < ATT_HEADS:
                work.append((attention_first, attention_second, t, hd))
    ahead = 2
    pending = [first(t, hd) for first, _, t, hd in work[:ahead]]
    for j, (_, second, t, hd) in enumerate(work):
        if j + ahead < len(work):
            first, _, t_next, hd_next = work[j + ahead]
            pending.append(first(t_next, hd_next))
        second(t, hd, *pending.pop(0))

    pr = _dot(yr_ref[...], wpr_ref[...])
    pa = _dot(ya_ref[...], wpa_ref[...])
    gate_r = jnp.concatenate([gate_r0_ref[...], gate_r1_ref[...]], axis=1).astype(F32)
    gate_a = jnp.concatenate([gate_a0_ref[...], gate_a1_ref[...]], axis=1).astype(F32)
    merged = _sigmoid(gate_r) * pr + _sigmoid(gate_a) * pa
    mix = _dot(merged.astype(MXU_DTYPE), wout_ref[...])
    for hd in range(RET_HEADS):
        retention_state_update(MIX_SUB - 1, hd)
    o_ref[...] = _layer_norm(DEEPNORM_ALPHA * h_ref[...] + mix, g_ref[...], b_ref[...])


def _mixer(z, h, B, S, tables, bias, wpr, wpa, wout, ln_g, ln_b):
    decay, q_decay, k_decay, tile_decay = tables
    T = B * S
    n = SEQ_TILE
    step_rows = MIX_SUB * n
    nt = S // step_rows
    assert MIX_SUB >= 2 and S % step_rows == 0

    def zspec(width, col_off, back=0):
        cb = col_off // width
        return pl.BlockSpec((step_rows, width), lambda b, i: (b * nt + jnp.maximum(i - back, 0), cb))

    row = lambda b, i: (b * nt + i, 0)
    c2 = lambda b, i: (0, 0)
    c3 = lambda b, i: (0, 0, 0)
    one = pl.Buffered(1)
    in_specs = [
        zspec(2 * RET_QK, Z_QK), zspec(RET_V, Z_VR), zspec(RET_V, Z_GR),
        zspec(GATE_HALF, Z_GATE_R), zspec(GATE_HALF, Z_GATE_R + GATE_HALF),
        zspec(GATE_HALF, Z_GATE_A), zspec(GATE_HALF, Z_GATE_A + GATE_HALF), zspec(ATT_W, Z_QA),
        zspec(ATT_W, Z_KA, 1), zspec(ATT_W, Z_KA, 0),
        zspec(ATT_W, Z_VA, 1), zspec(ATT_W, Z_VA, 0),
        pl.BlockSpec((step_rows, D_MODEL), row),
        pl.BlockSpec((RET_HEADS, n, n), c3, pipeline_mode=one),
        pl.BlockSpec((RET_HEADS, n, HEAD_PAIR), c3, pipeline_mode=one),
        pl.BlockSpec((RET_HEADS, n, HEAD_PAIR), c3, pipeline_mode=one),
        pl.BlockSpec((ATT_HEADS, n, 3 * n), c3, pipeline_mode=one),
        pl.BlockSpec((RET_V, D_MODEL), c2, pipeline_mode=one),
        pl.BlockSpec((ATT_W, D_MODEL), c2, pipeline_mode=one),
        pl.BlockSpec((D_MODEL, D_MODEL), c2, pipeline_mode=one),
        pl.BlockSpec((1, D_MODEL), c2),
        pl.BlockSpec((1, D_MODEL), c2),
    ]
    return pl.pallas_call(
        functools.partial(_mixer_kernel, tile_decay),
        grid=(B, nt),
        in_specs=in_specs,
        out_specs=pl.BlockSpec((step_rows, D_MODEL), row),
        out_shape=jax.ShapeDtypeStruct((T, D_MODEL), F32),
        scratch_shapes=[
            pltpu.VMEM((RET_HEADS, HEAD_PAIR, RET_DV), F32),
            pltpu.VMEM((step_rows, RET_V), MXU_DTYPE),
            pltpu.VMEM((step_rows, ATT_W), MXU_DTYPE),
        ],
        compiler_params=_params("arbitrary", "arbitrary"),
        name="mixer",
    )(*([z] * 12), h, decay, q_decay, k_decay, bias, wpr, wpa, wout, ln_g, ln_b)


def _kv_kernel(m_ref, w_ref, o_ref):
    o_ref[...] = _dot(m_ref[...].astype(MXU_DTYPE), w_ref[...]).astype(o_ref.dtype)


def _mem_kv(mem2d, w_kv):
    M = mem2d.shape[0]
    tm = 256
    return pl.pallas_call(
        _kv_kernel,
        grid=(M // tm,),
        in_specs=[pl.BlockSpec((tm, D_MODEL), lambda i: (i, 0)),
                  pl.BlockSpec((D_MODEL, 2 * D_MODEL), lambda i: (0, 0), pipeline_mode=pl.Buffered(1))],
        out_specs=pl.BlockSpec((tm, 2 * D_MODEL), lambda i: (i, 0)),
        out_shape=jax.ShapeDtypeStruct((M, 2 * D_MODEL), MXU_DTYPE),
        compiler_params=_params("parallel"),
        name="mem_kv",
    )(mem2d, w_kv)


ROUTE_ROWS = 8
GROUP_ROW0 = 0
EXPERT_ROW0 = 8
ROUTER_ROWS = 128


def _memattn_kernel(h_ref, kv_ref, wq_ref, wo_ref, g_ref, b_ref, wr_ref, br_ref, tri_ref,
                    h2_ref, rl_ref, rr_ref, cnt_ref, o_scr):
    n = MEM_TILE
    h1 = h_ref[...]
    q = (_dot(h1.astype(MXU_DTYPE), wq_ref[...]) * (MEM_DH ** -0.5 * LOG2E)).astype(MXU_DTYPE)
    scores = [_dot_nt(q[:, hd * MEM_DH:(hd + 1) * MEM_DH], kv_ref[:, hd * MEM_DH:(hd + 1) * MEM_DH])
              for hd in range(MEM_HEADS)]
    for hd in range(MEM_HEADS):
        cols = slice(hd * MEM_DH, (hd + 1) * MEM_DH)
        s = scores[hd]
        m = jnp.max(s, axis=-1, keepdims=True)
        e = jnp.exp2(s - m)
        denom = jnp.sum(e, axis=-1, keepdims=True)
        v = kv_ref[:, D_MODEL + hd * MEM_DH:D_MODEL + (hd + 1) * MEM_DH]
        o_scr[:, cols] = (_dot(e.astype(MXU_DTYPE), v) / denom).astype(MXU_DTYPE)
    cross = _dot(o_scr[...], wo_ref[...])
    h2 = _layer_norm(DEEPNORM_ALPHA * h1 + cross, g_ref[...], b_ref[...])
    h2_ref[...] = h2

    logits = _dot_nt(wr_ref[...], h2.astype(MXU_DTYPE)) + br_ref[...]
    glog = logits[GROUP_ROW0:GROUP_ROW0 + N_GROUPS]
    gmax = jnp.max(glog, axis=0, keepdims=True)
    g_w = 1.0 / jnp.sum(jnp.exp(glog - gmax), axis=0, keepdims=True)
    giota = lax.broadcasted_iota(jnp.int32, glog.shape, 0)
    g_idx = jnp.min(jnp.where(glog == gmax, giota, N_GROUPS), axis=0, keepdims=True)
    el = jnp.zeros((EXPERTS_PER_GROUP, n), F32)
    for g in range(N_GROUPS):
        r0 = EXPERT_ROW0 + g * EXPERTS_PER_GROUP
        el = jnp.where(g_idx == g, logits[r0:r0 + EXPERTS_PER_GROUP], el)
    eiota = lax.broadcasted_iota(jnp.int32, el.shape, 0)
    m1 = jnp.max(el, axis=0, keepdims=True)
    i1 = jnp.min(jnp.where(el == m1, eiota, EXPERTS_PER_GROUP), axis=0, keepdims=True)
    el2 = jnp.where(eiota == i1, -jnp.inf, el)
    m2 = jnp.max(el2, axis=0, keepdims=True)
    i2 = jnp.min(jnp.where(el2 == m2, eiota, EXPERTS_PER_GROUP), axis=0, keepdims=True)
    r = jnp.exp(m2 - m1)
    w1 = g_w / (1.0 + r)
    w2 = g_w * r / (1.0 + r)
    e1 = g_idx * EXPERTS_PER_GROUP + i1
    e2 = g_idx * EXPERTS_PER_GROUP + i2

    xiota = lax.broadcasted_iota(jnp.int32, (N_EXPERTS, n), 0)
    oh1 = xiota == e1
    oh2 = xiota == e2
    cnt = jnp.where(oh1, 1.0, 0.0) + jnp.where(oh2, 1.0, 0.0)
    before = _dot(cnt.astype(MXU_DTYPE), tri_ref[...])
    rank1 = jnp.sum(jnp.where(oh1, before, 0.0), axis=0, keepdims=True)
    rank2 = jnp.sum(jnp.where(oh2, before, 0.0), axis=0, keepdims=True)
    for j in range(n // SEQ_TILE):
        tile_cnt = jnp.sum(cnt[:, j * SEQ_TILE:(j + 1) * SEQ_TILE], axis=1, keepdims=True)
        cnt_ref[j] = jnp.broadcast_to(tile_cnt, (N_EXPERTS, LANES))

    zero = jnp.zeros((1, n), F32)
    rec = jnp.concatenate([e1.astype(F32), e2.astype(F32), w1, w2, rank1, rank2, zero, zero], axis=0)
    rl_ref[...] = rec
    rec_full = jnp.concatenate([rec, jnp.zeros((LANES - ROUTE_ROWS, n), F32)], axis=0)
    rr_ref[...] = rec_full.T


def _memattn(h1, kv, B, S, wq, wo, ln_g, ln_b, wr_t, br_col, tri):
    T = B * S
    n = MEM_TILE
    nt = S // n
    sub = n // SEQ_TILE
    row = lambda b, i: (b * nt + i, 0)
    c2 = lambda b, i: (0, 0)
    one = pl.Buffered(1)
    return pl.pallas_call(
        _memattn_kernel,
        grid=(B, nt),
        in_specs=[
            pl.BlockSpec((n, D_MODEL), row),
            pl.BlockSpec((kv.shape[0] // B, 2 * D_MODEL), lambda b, i: (b, 0)),
            pl.BlockSpec((D_MODEL, D_MODEL), c2, pipeline_mode=one),
            pl.BlockSpec((D_MODEL, D_MODEL), c2, pipeline_mode=one),
            pl.BlockSpec((1, D_MODEL), c2),
            pl.BlockSpec((1, D_MODEL), c2),
            pl.BlockSpec((ROUTER_ROWS, D_MODEL), c2, pipeline_mode=one),
            pl.BlockSpec((ROUTER_ROWS, 1), c2),
            pl.BlockSpec((n, n), c2, pipeline_mode=one),
        ],
        out_specs=[
            pl.BlockSpec((n, D_MODEL), row),
            pl.BlockSpec((ROUTE_ROWS, n), lambda b, i: (0, b * nt + i)),
            pl.BlockSpec((n, LANES), row),
            pl.BlockSpec((sub, N_EXPERTS, LANES), lambda b, i: (b * nt + i, 0, 0)),
        ],
        out_shape=[
            jax.ShapeDtypeStruct((T, D_MODEL), F32),
            jax.ShapeDtypeStruct((ROUTE_ROWS, T), F32),
            jax.ShapeDtypeStruct((T, LANES), F32),
            jax.ShapeDtypeStruct((T // SEQ_TILE, N_EXPERTS, LANES), F32),
        ],
        scratch_shapes=[pltpu.VMEM((n, D_MODEL), MXU_DTYPE)],
        compiler_params=_params("parallel", "parallel"),
        name="memattn_router",
    )(h1, kv, wq, wo, ln_g, ln_b, wr_t, br_col, tri)


CHUNK_ROWS = SUBLANES
LOCAL_ROWS = 768
N_CHUNKS = LOCAL_ROWS // CHUNK_ROWS
assert TOP_K * SEQ_TILE + N_EXPERTS * (CHUNK_ROWS - 1) <= LOCAL_ROWS


def _unpack_rows(packed):
    lo = lax.bitcast_convert_type(packed << 16, F32)
    hi = lax.bitcast_convert_type(packed & jnp.uint32(0xFFFF0000), F32)
    return lo, hi


def _pack_rows(y):
    bits = lax.bitcast_convert_type(y.astype(jnp.bfloat16).astype(F32), jnp.uint32)
    return (bits[:, :PACK_COLS] >> 16) | bits[:, PACK_COLS:]


def _chunk_plan(counts, n_blocks):
    nt = counts.shape[0]
    c = counts.astype(jnp.int32)
    incl = jnp.asarray(np.tril(np.ones((N_EXPERTS, N_EXPERTS), bool)))
    padc = (c + CHUNK_ROWS - 1) // CHUNK_ROWS * CHUNK_ROWS
    lend = jnp.sum(jnp.where(incl[None], padc[:, None, :], 0), axis=2)
    loff = lend - padc
    earlier = jnp.asarray(np.tril(np.ones((nt, nt), bool), -1))
    cbefore = jnp.sum(jnp.where(earlier[:, :, None], padc[None, :, :], 0), axis=1)
    total = jnp.sum(padc, axis=0)
    region = (total + EXPERT_BLOCK - 1) // EXPERT_BLOCK * EXPERT_BLOCK
    gend = jnp.sum(jnp.where(incl, region[None, :], 0), axis=1)
    gstart = gend - region
    delta = gstart[None, :] + cbefore - loff
    k_row = jnp.arange(N_CHUNKS, dtype=jnp.int32) * CHUNK_ROWS
    e_of_chunk = jnp.sum((lend[:, None, :] <= k_row[None, :, None]).astype(jnp.int32), axis=2)
    used = e_of_chunk < N_EXPERTS
    onehot = e_of_chunk[:, :, None] == jnp.arange(N_EXPERTS, dtype=jnp.int32)[None, None, :]
    grow = jnp.sum(jnp.where(onehot, delta[:, None, :], 0), axis=2) + k_row[None, :]
    parity = (jnp.arange(nt, dtype=jnp.int32) % 2)[:, None]
    dump = n_blocks * EXPERT_BLOCK + parity * LOCAL_ROWS + k_row[None, :]
    dst = jnp.where(used, grow, dump).reshape(-1)
    src = jnp.where(used, grow, 0).reshape(-1)
    block_first = jnp.arange(n_blocks, dtype=jnp.int32) * EXPERT_BLOCK
    block_expert = jnp.minimum(jnp.sum((gend[None, :] <= block_first[:, None]).astype(jnp.int32), axis=1),
                               N_EXPERTS - 1)
    n_used = (gend[-1:] // EXPERT_BLOCK).astype(jnp.int32)
    tail = jnp.concatenate([gstart + total, (region - total) // CHUNK_ROWS, n_used]).astype(jnp.int32)
    loff_f = loff.astype(F32)
    loff_col = jnp.broadcast_to(loff_f[:, :, None], (nt, N_EXPERTS, LANES))
    loff_row = jnp.broadcast_to(jnp.pad(loff_f, ((0, 0), (0, LANES - N_EXPERTS)))[:, None, :],
                                (nt, SUBLANES, LANES))
    return dst, src, tail, block_expert, n_used, loff_col, loff_row


def _dispatch_kernel(dst_ref, tail_ref, h_ref, rl_ref, loff_ref, xb_ref, sbuf, zbuf, sem, zsem):
    i = pl.program_id(0)
    n = SEQ_TILE
    xiota = lax.broadcasted_iota(jnp.int32, (N_EXPERTS, n), 0)
    riota = lax.broadcasted_iota(jnp.int32, (LOCAL_ROWS, n), 0)

    def wait_all(slot):
        pltpu.make_async_copy(sbuf.at[slot], xb_ref.at[pl.ds(0, LOCAL_ROWS)], sem.at[slot]).wait()

    def sorted_rows(t):
        rl = rl_ref[:, t * n:(t + 1) * n]
        loff = loff_ref[t][:, 0:1]
        slots = []
        for k in range(TOP_K):
            e = rl[k:k + 1].astype(jnp.int32)
            base = jnp.sum(jnp.where(xiota == e, loff, 0.0), axis=0, keepdims=True)
            slots.append((base + rl[4 + k:5 + k]).astype(jnp.int32))
        perm = jnp.where(riota == slots[0], 1.0, jnp.where(riota == slots[1], 1.0, 0.0)).astype(MXU_DTYPE)
        return _pack_rows(_dot(perm, h_ref[t * n:(t + 1) * n, :].astype(MXU_DTYPE)))

    packed = [sorted_rows(t) for t in range(DC_SUB)]
    for t in range(DC_SUB):
        @pl.when(i >= 1)
        def _():
            wait_all(t)

        sbuf[t] = packed[t]
        tile = i * DC_SUB + t

        def body(k2, carry):
            for prio in range(2):
                k = 2 * k2 + prio
                d = pl.multiple_of(dst_ref[tile * N_CHUNKS + k], CHUNK_ROWS)
                src = sbuf.at[t, pl.ds(pl.multiple_of(k * CHUNK_ROWS, CHUNK_ROWS), CHUNK_ROWS)]
                pltpu.make_async_copy(src, xb_ref.at[pl.ds(d, CHUNK_ROWS)], sem.at[t]).start(priority=prio)
            return carry

        lax.fori_loop(0, N_CHUNKS // 2, body, 0, unroll=4)

    @pl.when(i == pl.num_programs(0) - 1)
    def _():
        zbuf[...] = jnp.zeros_like(zbuf)

        def fill(start_copy):
            def per_expert(e, carry):
                first = pl.multiple_of(tail_ref[e], CHUNK_ROWS)

                def per_chunk(c, inner):
                    row = pl.multiple_of(first + c * CHUNK_ROWS, CHUNK_ROWS)
                    copy = pltpu.make_async_copy(zbuf, xb_ref.at[pl.ds(row, CHUNK_ROWS)], zsem)
                    if start_copy:
                        copy.start()
                    else:
                        copy.wait()
                    return inner

                lax.fori_loop(0, tail_ref[N_EXPERTS + e], per_chunk, 0)
                return carry

            lax.fori_loop(0, N_EXPERTS, per_expert, 0)

        fill(True)
        for t in range(DC_SUB):
            wait_all(t)
        fill(False)

        zblock = sbuf.at[0, pl.ds(0, EXPERT_BLOCK)]
        zblock[...] = jnp.zeros_like(zblock)
        n_all = xb_ref.shape[0] // EXPERT_BLOCK

        def fill_blocks(start_copy):
            def per_block(p, carry):
                row = pl.multiple_of(p * EXPERT_BLOCK, EXPERT_BLOCK)
                copy = pltpu.make_async_copy(zblock, xb_ref.at[pl.ds(row, EXPERT_BLOCK)], zsem)
                if start_copy:
                    copy.start()
                else:
                    copy.wait()
                return carry

            lax.fori_loop(tail_ref[2 * N_EXPERTS], n_all, per_block, 0)

        fill_blocks(True)
        fill_blocks(False)


def _dispatch(dst, tail, h2, rl, loff_col, n_blocks):
    T = h2.shape[0]
    n = DC_SUB * SEQ_TILE
    rows = n_blocks * EXPERT_BLOCK + 2 * LOCAL_ROWS
    grid_spec = pltpu.PrefetchScalarGridSpec(
        num_scalar_prefetch=2,
        grid=(T // n,),
        in_specs=[
            pl.BlockSpec((n, D_MODEL), lambda i, d, t: (i, 0)),
            pl.BlockSpec((ROUTE_ROWS, n), lambda i, d, t: (0, i)),
            pl.BlockSpec((DC_SUB, N_EXPERTS, LANES), lambda i, d, t: (i, 0, 0)),
        ],
        out_specs=pl.BlockSpec(memory_space=pl.ANY),
        scratch_shapes=[pltpu.VMEM((DC_SUB, LOCAL_ROWS, PACK_COLS), jnp.uint32),
                        pltpu.VMEM((CHUNK_ROWS, PACK_COLS), jnp.uint32),
                        pltpu.SemaphoreType.DMA((DC_SUB,)), pltpu.SemaphoreType.DMA],
    )
    assert DC_SUB == 2 and T % n == 0
    return pl.pallas_call(
        _dispatch_kernel,
        grid_spec=grid_spec,
        out_shape=jax.ShapeDtypeStruct((rows, PACK_COLS), jnp.uint32),
        compiler_params=_params("arbitrary"),
        name="dispatch",
    )(dst, tail, h2, rl, loff_col)


def _expert_kernel(be_ref, nused_ref, x_ref, wg_ref, wu_ref, wd_ref, y_ref, wg_b, wu_b, wd_b):
    p = pl.program_id(0)
    new_expert = jnp.logical_or(p == 0, be_ref[p] != be_ref[jnp.maximum(p - 1, 0)])

    @pl.when(new_expert)
    def _():
        wg_b[...] = wg_ref[...].astype(MXU_DTYPE)
        wu_b[...] = wu_ref[...].astype(MXU_DTYPE)
        wd_b[...] = wd_ref[...].astype(MXU_DTYPE)

    @pl.when(p < nused_ref[0])
    def _():
        lo, hi = _unpack_rows(x_ref[...])
        x = jnp.concatenate([lo.astype(MXU_DTYPE), hi.astype(MXU_DTYPE)], axis=1)
        gate = _dot(x, wg_b[...])
        up = _dot(x, wu_b[...])
        hid = (gate * _sigmoid(gate) * up).astype(MXU_DTYPE)
        y_ref[...] = _pack_rows(_dot(hid, wd_b[...]))

    @pl.when(p >= nused_ref[0])
    def _():
        y_ref[...] = jnp.zeros_like(y_ref)


def _experts(block_expert, n_used, xb, wg, wu, wd, layer):
    nb = block_expert.shape[0]
    n_slots = nb * EXPERT_BLOCK
    grid_spec = pltpu.PrefetchScalarGridSpec(
        num_scalar_prefetch=2,
        grid=(nb,),
        in_specs=[
            pl.BlockSpec((EXPERT_BLOCK, PACK_COLS), lambda p, be, nu: (jnp.minimum(p, jnp.maximum(nu[0] - 1, 0)), 0)),
            pl.BlockSpec((None, None, D_MODEL, D_EXPERT), lambda p, be, nu: (layer, be[p], 0, 0)),
            pl.BlockSpec((None, None, D_MODEL, D_EXPERT), lambda p, be, nu: (layer, be[p], 0, 0)),
            pl.BlockSpec((None, None, D_EXPERT, D_MODEL), lambda p, be, nu: (layer, be[p], 0, 0)),
        ],
        out_specs=pl.BlockSpec((EXPERT_BLOCK, PACK_COLS), lambda p, be, nu: (p, 0)),
        scratch_shapes=[pltpu.VMEM((D_MODEL, D_EXPERT), MXU_DTYPE), pltpu.VMEM((D_MODEL, D_EXPERT), MXU_DTYPE),
                        pltpu.VMEM((D_EXPERT, D_MODEL), MXU_DTYPE)],
    )
    return pl.pallas_call(
        _expert_kernel,
        grid_spec=grid_spec,
        out_shape=jax.ShapeDtypeStruct((n_slots, PACK_COLS), jnp.uint32),
        compiler_params=_params("arbitrary"),
        name="experts",
    )(block_expert, n_used, xb, wg, wu, wd)


def _combine_kernel(src_ref, yb_ref, rr_ref, loff_ref, h_ref, g_ref, b_ref, o_ref, ybuf, sem):
    i = pl.program_id(0)
    n = SEQ_TILE

    def fetch(tile, into):
        def body(k2, carry):
            for prio in range(2):
                k = 2 * k2 + prio
                s = pl.multiple_of(src_ref[tile * N_CHUNKS + k], CHUNK_ROWS)
                dst = ybuf.at[into, pl.ds(pl.multiple_of(k * CHUNK_ROWS, CHUNK_ROWS), CHUNK_ROWS)]
                pltpu.make_async_copy(yb_ref.at[pl.ds(s, CHUNK_ROWS)], dst, sem.at[into]).start(priority=prio)
            return carry

        lax.fori_loop(0, N_CHUNKS // 2, body, 0, unroll=4)

    @pl.when(i == 0)
    def _():
        for t in range(DC_SUB):
            fetch(t, t)

    lane = lax.broadcasted_iota(jnp.int32, (n, LANES), 1)
    ciota = lax.broadcasted_iota(jnp.int32, (n, LOCAL_ROWS), 1)

    def weights(t):
        rr = rr_ref[t * n:(t + 1) * n, :]
        loff = loff_ref[t][0:1, :]
        mix = jnp.zeros((n, LOCAL_ROWS), F32)
        for k in range(TOP_K):
            e = rr[:, k:k + 1].astype(jnp.int32)
            base = jnp.sum(jnp.where(lane == e, loff, 0.0), axis=1, keepdims=True)
            sorted_row = (base + rr[:, 4 + k:5 + k]).astype(jnp.int32)
            mix = jnp.where(ciota == sorted_row, rr[:, 2 + k:3 + k], mix)
        return mix.astype(MXU_DTYPE)

    mixes = [weights(t) for t in range(DC_SUB)]
    for t in range(DC_SUB):
        pltpu.make_async_copy(yb_ref.at[pl.ds(0, LOCAL_ROWS)], ybuf.at[t], sem.at[t]).wait()
        lo, hi = _unpack_rows(ybuf[t])
        y = jnp.concatenate([_dot(mixes[t], lo.astype(MXU_DTYPE)), _dot(mixes[t], hi.astype(MXU_DTYPE))], axis=1)
        rows = slice(t * n, (t + 1) * n)
        o_ref[rows, :] = _layer_norm(DEEPNORM_ALPHA * h_ref[rows, :] + y, g_ref[...], b_ref[...])

        @pl.when(i + 1 < pl.num_programs(0))
        def _():
            fetch((i + 1) * DC_SUB + t, t)


def _combine(src, yb, rr, loff_row, h2, ln_g, ln_b):
    T = h2.shape[0]
    n = DC_SUB * SEQ_TILE
    row = lambda i, s: (i, 0)
    grid_spec = pltpu.PrefetchScalarGridSpec(
        num_scalar_prefetch=1,
        grid=(T // n,),
        in_specs=[
            pl.BlockSpec(memory_space=pl.ANY),
            pl.BlockSpec((n, LANES), row),
            pl.BlockSpec((DC_SUB, SUBLANES, LANES), lambda i, s: (i, 0, 0)),
            pl.BlockSpec((n, D_MODEL), row),
            pl.BlockSpec((1, D_MODEL), lambda i, s: (0, 0)),
            pl.BlockSpec((1, D_MODEL), lambda i, s: (0, 0)),
        ],
        out_specs=pl.BlockSpec((n, D_MODEL), row),
        scratch_shapes=[pltpu.VMEM((DC_SUB, LOCAL_ROWS, PACK_COLS), jnp.uint32),
                        pltpu.SemaphoreType.DMA((DC_SUB,))],
    )
    return pl.pallas_call(
        _combine_kernel,
        grid_spec=grid_spec,
        out_shape=jax.ShapeDtypeStruct((T, D_MODEL), F32),
        compiler_params=_params("arbitrary"),
        name="combine",
    )(src, yb, rr, loff_row, h2, ln_g, ln_b)


def _router_weights(w_group, b_group, w_route, b_route):
    wr = jnp.zeros((ROUTER_ROWS, D_MODEL), F32)
    wr = wr.at[GROUP_ROW0:GROUP_ROW0 + N_GROUPS].set(w_group.T)
    wr = wr.at[EXPERT_ROW0:EXPERT_ROW0 + N_EXPERTS].set(w_route.T)
    br = jnp.zeros((ROUTER_ROWS,), F32)
    br = br.at[GROUP_ROW0:GROUP_ROW0 + N_GROUPS].set(b_group)
    br = br.at[EXPERT_ROW0:EXPERT_ROW0 + N_EXPERTS].set(b_route.reshape(-1))
    return wr.astype(MXU_DTYPE), br.reshape(ROUTER_ROWS, 1)


def kernel(x, mem, positions, ln_in_g, ln_in_b, w_in, rel_bias, w_proj_ret, w_proj_att, w_out, ln1_g, ln1_b, w_q_mem, w_kv_mem, w_o_mem, ln2_g, ln2_b, w_group, b_group, w_route, b_route, w_gate, w_up, w_down, ln3_g, ln3_b):
    B, S, D = x.shape
    assert D == D_MODEL and S % PROJ_TILE == 0 and S % SEQ_TILE == 0 and S % MEM_TILE == 0
    T = B * S
    A = T * TOP_K
    n_blocks = -(-(A + (T // SEQ_TILE) * N_EXPERTS * (CHUNK_ROWS - 1) + N_EXPERTS * (EXPERT_BLOCK - 1)) // EXPERT_BLOCK)
    bf = MXU_DTYPE

    cos_t, sin_t = _rope_tables(positions)
    tables = _retention_tables()
    pos = np.arange(MEM_TILE)
    tri = jnp.asarray((pos[:, None] < pos[None, :]) & (pos[:, None] // SEQ_TILE == pos[None, :] // SEQ_TILE), bf)
    mem2d = mem.reshape(-1, D)
    row2 = lambda v: v.reshape(1, D)

    h = x.reshape(T, D)
    for l in range(DEPTH):
        w_in_b = _to_mxu(w_in, l)
        if l == 0:
            z, h = _inproj(h, row2(ln_in_g), row2(ln_in_b), w_in_b, cos_t, sin_t, True)
        else:
            (z,) = _inproj(h, row2(ln_in_g), row2(ln_in_b), w_in_b, cos_t, sin_t, False)
        bias = _attention_bias(rel_bias[l])
        h = _mixer(z, h, B, S, tables, bias, _to_mxu(w_proj_ret, l), _to_mxu(w_proj_att, l),
                   _to_mxu(w_out, l), row2(ln1_g[l]), row2(ln1_b[l]))

        kv = _mem_kv(mem2d, _to_mxu(w_kv_mem, l))
        wr_t, br_col = _router_weights(w_group[l], b_group[l], w_route[l], b_route[l])
        h2, rl, rr, counts = _memattn(h, kv, B, S, _to_mxu(w_q_mem, l), _to_mxu(w_o_mem, l),
                                      row2(ln2_g[l]), row2(ln2_b[l]), wr_t, br_col, tri)

        dst, src, tail, block_expert, n_used, loff_col, loff_row = _chunk_plan(counts[:, :, 0], n_blocks)
        xb = _dispatch(dst, tail, h2, rl, loff_col, n_blocks)
        yb = _experts(block_expert, n_used, xb, w_gate, w_up, w_down, l)
        h = _combine(src, yb, rr, loff_row, h2, row2(ln3_g[l]), row2(ln3_b[l]))
    return h.reshape(B, S, D)
```

```python
import functools

import jax
import jax.numpy as jnp
import numpy as np
from jax import lax
from jax.experimental import pallas as pl
from jax.experimental.pallas import tpu as pltpu

D_MODEL = 1024
DEPTH = 2
CHUNK = 64
RET_HEADS = 8
RET_DK = 64
RET_DV = 128
RET_QK = RET_HEADS * RET_DK
RET_V = RET_HEADS * RET_DV
ROPE_BASE = 10000.0
ATT_HEADS = 8
ATT_DH = 64
ATT_W = ATT_HEADS * ATT_DH
LEFT_CHUNKS = 8
MAX_REL = 256
MEM_HEADS = 4
MEM_DH = D_MODEL // MEM_HEADS
N_GROUPS = 4
EXPERTS_PER_GROUP = 8
N_EXPERTS = N_GROUPS * EXPERTS_PER_GROUP
TOP_K = 2
D_EXPERT = 512
LN_EPS = 1e-5
DEEPNORM_ALPHA = (2.0 * DEPTH) ** 0.25
IN_TOTAL = 2 * RET_QK + 2 * RET_V + 3 * ATT_W + 2 * D_MODEL

LANES = 128
SUBLANES = 8
VMEM_LIMIT_BYTES = 56 * 1024 * 1024

SEQ_TILE = 256
MEM_TILE = 2 * SEQ_TILE
MIX_SUB = 2
DC_SUB = 2
PROJ_TILE = 512
PROJ_COLS = 512
EXPERT_BLOCK = 512
HEAD_PAIR = 2 * RET_DK
NEG_BIG = -1e30
LOG2E = 1.4426950408889634

F32 = jnp.float32
MXU_DTYPE = jnp.bfloat16
PACK_COLS = D_MODEL // 2

Z_QK = 0
Z_VR = Z_QK + 2 * RET_QK
Z_GR = Z_VR + RET_V
Z_QA = Z_GR + RET_V
Z_KA = Z_QA + ATT_W
Z_VA = Z_KA + ATT_W
Z_GATE_R = Z_VA + ATT_W
Z_GATE_A = Z_GATE_R + D_MODEL
GATE_HALF = D_MODEL // 2


def _layer_norm(x, g, b):
    mu = jnp.mean(x, axis=-1, keepdims=True)
    xc = x - mu
    var = jnp.mean(xc * xc, axis=-1, keepdims=True)
    return xc * lax.rsqrt(var + LN_EPS) * g + b


def _sigmoid(x):
    return 1.0 / (1.0 + jnp.exp(-x))


def _dot(a, b):
    return jnp.dot(a, b, preferred_element_type=F32)


def _dot_nt(a, b):
    return lax.dot_general(a, b, (((1,), (1,)), ((), ())), preferred_element_type=F32)


def _dot_tn(a, b):
    return lax.dot_general(a, b, (((0,), (0,)), ((), ())), preferred_element_type=F32)


def _params(*semantics):
    return pltpu.CompilerParams(dimension_semantics=semantics, vmem_limit_bytes=VMEM_LIMIT_BYTES)


def _cast_kernel(x_ref, o_ref):
    o_ref[...] = x_ref[...].astype(o_ref.dtype)


def _to_mxu(w_stack, layer):
    _, r, c = w_stack.shape
    tr = 256
    return pl.pallas_call(
        _cast_kernel,
        grid=(r // tr,),
        in_specs=[pl.BlockSpec((None, tr, c), lambda i: (layer, i, 0))],
        out_specs=pl.BlockSpec((tr, c), lambda i: (i, 0)),
        out_shape=jax.ShapeDtypeStruct((r, c), MXU_DTYPE),
        compiler_params=_params("parallel"),
        name="cast_weight",
    )(w_stack)


def _rope_kernel(pos_ref, invf_ref, cos_ref, sin_ref):
    ang = pos_ref[...] * invf_ref[...]
    lane = lax.broadcasted_iota(jnp.int32, ang.shape, 1)
    s = jnp.sin(ang)
    cos_ref[...] = jnp.cos(ang)
    sin_ref[...] = jnp.where((lane % RET_DK) < RET_DK // 2, -s, s)


def _rope_tables(positions):
    T = positions.size
    half = RET_DK // 2
    inv_freq = 1.0 / (ROPE_BASE ** jnp.linspace(0.0, 1.0, half, dtype=F32))
    invf = jnp.tile(inv_freq, LANES // half).reshape(1, LANES)
    pos = jnp.broadcast_to(positions.reshape(T, 1).astype(F32), (T, LANES))
    tm = 1024
    spec = pl.BlockSpec((tm, LANES), lambda i: (i, 0))
    return pl.pallas_call(
        _rope_kernel,
        grid=(T // tm,),
        in_specs=[spec, pl.BlockSpec((1, LANES), lambda i: (0, 0))],
        out_specs=[spec, spec],
        out_shape=[jax.ShapeDtypeStruct((T, LANES), F32)] * 2,
        compiler_params=_params("parallel"),
        name="rope_tables",
    )(pos, invf)


def _rotary(a, cos, sin_signed):
    outs = []
    lane = lax.broadcasted_iota(jnp.int32, cos.shape, 1)
    first_half = (lane % RET_DK) < RET_DK // 2
    for g in range(a.shape[1] // LANES):
        x = a[:, g * LANES:(g + 1) * LANES]
        rot = jnp.where(first_half, pltpu.roll(x, LANES - RET_DK // 2, 1), pltpu.roll(x, RET_DK // 2, 1))
        outs.append(x * cos + rot * sin_signed)
    return jnp.concatenate(outs, axis=1)


def _inproj_kernel(apply_ln, h_ref, g_ref, b_ref, w_ref, cos_ref, sin_ref, z_ref, *h_out):
    x = h_ref[...]
    if apply_ln:
        x = _layer_norm(x, g_ref[...], b_ref[...])
        h_out[0][...] = x
    xb = x.astype(MXU_DTYPE)
    cos = cos_ref[...]
    sin = sin_ref[...]
    for c in range(IN_TOTAL // PROJ_COLS):
        lo = c * PROJ_COLS
        acc = _dot(xb, w_ref[:, lo:lo + PROJ_COLS])
        if lo < 2 * RET_QK:
            acc = _rotary(acc, cos, sin)
            if lo >= RET_QK:
                acc = acc * (RET_DK ** -0.5)
        z_ref[:, lo:lo + PROJ_COLS] = acc.astype(z_ref.dtype)


def _inproj(h, ln_g, ln_b, w_in_b, cos_t, sin_t, apply_ln):
    T = h.shape[0]
    tm = PROJ_TILE
    row = lambda i: (i, 0)
    const = lambda i: (0, 0)
    out_shape = [jax.ShapeDtypeStruct((T, IN_TOTAL), MXU_DTYPE)]
    out_specs = [pl.BlockSpec((tm, IN_TOTAL), row)]
    if apply_ln:
        out_shape.append(jax.ShapeDtypeStruct((T, D_MODEL), F32))
        out_specs.append(pl.BlockSpec((tm, D_MODEL), row))
    res = pl.pallas_call(
        functools.partial(_inproj_kernel, apply_ln),
        grid=(T // tm,),
        in_specs=[
            pl.BlockSpec((tm, D_MODEL), row),
            pl.BlockSpec((1, D_MODEL), const),
            pl.BlockSpec((1, D_MODEL), const),
            pl.BlockSpec((D_MODEL, IN_TOTAL), const, pipeline_mode=pl.Buffered(1)),
            pl.BlockSpec((tm, LANES), row),
            pl.BlockSpec((tm, LANES), row),
        ],
        out_specs=out_specs,
        out_shape=out_shape,
        compiler_params=_params("parallel"),
        name="inproj",
    )(h, ln_g, ln_b, w_in_b, cos_t, sin_t)
    return res


def _retention_tables():
    n = SEQ_TILE
    log_g = np.log1p(-np.exp2(-5.0 - np.arange(RET_HEADS, dtype=np.float64)))
    i = np.arange(n)
    same = (i[:, None] // CHUNK) == (i[None, :] // CHUNK)
    earlier = (i[None, :] // CHUNK) < (i[:, None] // CHUNK)
    diff = i[:, None] - i[None, :]
    expo = np.where(same, np.abs(diff), np.where(earlier, diff, 0)).astype(np.float64)
    decay = np.exp(log_g[:, None, None] * expo) * (same | earlier)
    q_decay = np.exp(log_g[:, None] * (i + 1.0))
    k_decay = np.exp(log_g[:, None] * (n - 1.0 - i))
    tile_decay = np.exp(log_g * n)
    wide = lambda t: np.broadcast_to(t[:, :, None], t.shape + (HEAD_PAIR,))
    owner = (np.arange(HEAD_PAIR)[None, :] // RET_DK) == (np.arange(RET_HEADS)[:, None] % 2)
    k_table = wide(k_decay) * owner[:, None, :]
    return (jnp.asarray(decay, F32), jnp.asarray(wide(q_decay), F32), jnp.asarray(k_table, F32),
            tuple(float(v) for v in tile_decay))


def _attention_bias(rel_bias):
    n = SEQ_TILE
    width = 3 * n
    pad = LEFT_CHUNKS * CHUNK
    assert pad == 2 * n and n <= MAX_REL
    heads = rel_bias.shape[0]
    length = width + n
    near = rel_bias[:, MAX_REL - n:].astype(F32)
    far = jnp.broadcast_to(rel_bias[:, -1:].astype(F32), (heads, length - near.shape[1]))
    g_rev = jnp.concatenate([near, far], axis=1)[:, ::-1]
    rows = jnp.tile(g_rev, (1, n + 1))[:, :n * (length - 1)].reshape(heads, n, length - 1)
    bias = rows[:, :, n - 1:n - 1 + width]
    qi = np.arange(n)
    kj = np.arange(width)
    q_chunk = (pad + qi) // CHUNK
    k_chunk = kj // CHUNK
    in_band = (k_chunk[None, :] <= q_chunk[:, None]) & (k_chunk[None, :] >= q_chunk[:, None] - LEFT_CHUNKS)
    return jnp.where(jnp.asarray(in_band)[None], bias * LOG2E, NEG_BIG)


def _mixer_kernel(tile_decay,
                  qk_ref, vr_ref, gr_ref, gate_r0_ref, gate_r1_ref, gate_a0_ref, gate_a1_ref, qa_ref,
                  kp_ref, kc_ref, vp_ref, vc_ref,
                  h_ref, decay_ref, qdec_ref, kdec_ref, bias_ref,
                  wpr_ref, wpa_ref, wout_ref, g_ref, b_ref,
                  o_ref, state_ref, yr_ref, ya_ref):
    i = pl.program_id(1)

    @pl.when(i == 0)
    def _():
        state_ref[...] = jnp.zeros_like(state_ref)

    n = SEQ_TILE
    lane = lax.broadcasted_iota(jnp.int32, (n, HEAD_PAIR), 1)
    low_half = lane < RET_DK

    def head_lanes(odd):
        return low_half if odd == 0 else jnp.logical_not(low_half)

    def rows(t):
        return slice(t * n, (t + 1) * n)

    def retention_state_update(t, hd):
        p = hd // 2
        k2 = qk_ref[rows(t), RET_QK + p * HEAD_PAIR:RET_QK + (p + 1) * HEAD_PAIR]
        v = vr_ref[rows(t), hd * RET_DV:(hd + 1) * RET_DV]
        k_dec = (k2.astype(F32) * kdec_ref[hd]).astype(MXU_DTYPE)
        state_ref[hd] = state_ref[hd] * tile_decay[hd] + _dot_tn(k_dec, v)

    def retention_first(t, hd):
        p, odd = divmod(hd, 2)
        q2 = qk_ref[rows(t), p * HEAD_PAIR:(p + 1) * HEAD_PAIR]
        k2 = qk_ref[rows(t), RET_QK + p * HEAD_PAIR:RET_QK + (p + 1) * HEAD_PAIR]
        qm = jnp.where(head_lanes(odd), q2, jnp.zeros_like(q2))
        v = vr_ref[rows(t), hd * RET_DV:(hd + 1) * RET_DV]
        raw = _dot_nt(qm, k2)
        read = _dot(qm, state_ref[hd].astype(MXU_DTYPE))
        if t + 1 < MIX_SUB:
            retention_state_update(t, hd)
        return raw, read, v

    def retention_second(t, hd, raw, read, v):
        intra = _dot((raw * decay_ref[hd]).astype(MXU_DTYPE), v)
        ret = intra + qdec_ref[hd] * read
        mu = jnp.mean(ret, axis=-1, keepdims=True)
        rc = ret - mu
        var = jnp.mean(rc * rc, axis=-1, keepdims=True)
        gn = rc * lax.rsqrt(var + LN_EPS)
        gate = gr_ref[rows(t), hd * RET_DV:(hd + 1) * RET_DV].astype(F32)
        yr_ref[rows(t), hd * RET_DV:(hd + 1) * RET_DV] = (gate * _sigmoid(gate) * gn).astype(MXU_DTYPE)

    pen_prev = jnp.where(i >= 1, 0.0, NEG_BIG).astype(F32)

    def key_block(t, kb):
        j = t + kb + (MIX_SUB - 2)
        if j < MIX_SUB:
            return kp_ref, vp_ref, rows(j), pen_prev
        return kc_ref, vc_ref, rows(j - MIX_SUB), None

    def attention_first(t, hd):
        p, odd = divmod(hd, 2)
        cols = slice(p * HEAD_PAIR, (p + 1) * HEAD_PAIR)
        qs = (qa_ref[rows(t), cols].astype(F32) * (ATT_DH ** -0.5 * LOG2E)).astype(MXU_DTYPE)
        qm = jnp.where(head_lanes(odd), qs, jnp.zeros_like(qs))
        s = []
        for kb in range(3):
            k_ref, _, krows, pen = key_block(t, kb)
            sk = _dot_nt(qm, k_ref[krows, cols]) + bias_ref[hd, :, kb * n:(kb + 1) * n]
            if pen is not None:
                sk = sk + pen
            s.append(sk)
        return (s,)

    def attention_second(t, hd, s):
        p, odd = divmod(hd, 2)
        cols = slice(p * HEAD_PAIR, (p + 1) * HEAD_PAIR)
        m = jnp.maximum(jnp.maximum(jnp.max(s[0], axis=-1, keepdims=True),
                                    jnp.max(s[1], axis=-1, keepdims=True)),
                        jnp.max(s[2], axis=-1, keepdims=True))
        e = [jnp.exp2(sk - m) for sk in s]
        denom = (jnp.sum(e[0], axis=-1, keepdims=True) + jnp.sum(e[1], axis=-1, keepdims=True)
                 + jnp.sum(e[2], axis=-1, keepdims=True))
        pv = None
        for kb in range(3):
            _, v_ref, krows, _ = key_block(t, kb)
            term = _dot(e[kb].astype(MXU_DTYPE), v_ref[krows, cols])
            pv = term if pv is None else pv + term
        out = (pv / denom).astype(MXU_DTYPE)
        lo = odd * ATT_DH
        ya_ref[rows(t), p * HEAD_PAIR + lo:p * HEAD_PAIR + lo + ATT_DH] = out[:, lo:lo + ATT_DH]

    work = []
    for t in range(MIX_SUB):
        for hd in range(max(RET_HEADS, ATT_HEADS)):
            if hd < RET_HEADS:
                work.append((retention_first, retention_second, t, hd))
            if hd < ATT_HEADS:
                work.append((attention_first, attention_second, t, hd))
    ahead = 2
    pending = [first(t, hd) for first, _, t, hd in work[:ahead]]
    for j, (_, second, t, hd) in enumerate(work):
        if j + ahead < len(work):
            first, _, t_next, hd_next = work[j + ahead]
            pending.append(first(t_next, hd_next))
        second(t, hd, *pending.pop(0))

    pr = _dot(yr_ref[...], wpr_ref[...])
    pa = _dot(ya_ref[...], wpa_ref[...])
    gate_r = jnp.concatenate([gate_r0_ref[...], gate_r1_ref[...]], axis=1).astype(F32)
    gate_a = jnp.concatenate([gate_a0_ref[...], gate_a1_ref[...]], axis=1).astype(F32)
    merged = _sigmoid(gate_r) * pr + _sigmoid(gate_a) * pa
    mix = _dot(merged.astype(MXU_DTYPE), wout_ref[...])
    for hd in range(RET_HEADS):
        retention_state_update(MIX_SUB - 1, hd)
    o_ref[...] = _layer_norm(DEEPNORM_ALPHA * h_ref[...] + mix, g_ref[...], b_ref[...])


def _mixer(z, h, B, S, tables, bias, wpr, wpa, wout, ln_g, ln_b):
    decay, q_decay, k_decay, tile_decay = tables
    T = B * S
    n = SEQ_TILE
    step_rows = MIX_SUB * n
    nt = S // step_rows
    assert MIX_SUB >= 2 and S % step_rows == 0

    def zspec(width, col_off, back=0):
        cb = col_off // width
        return pl.BlockSpec((step_rows, width), lambda b, i: (b * nt + jnp.maximum(i - back, 0), cb))

    row = lambda b, i: (b * nt + i, 0)
    c2 = lambda b, i: (0, 0)
    c3 = lambda b, i: (0, 0, 0)
    one = pl.Buffered(1)
    in_specs = [
        zspec(2 * RET_QK, Z_QK), zspec(RET_V, Z_VR), zspec(RET_V, Z_GR),
        zspec(GATE_HALF, Z_GATE_R), zspec(GATE_HALF, Z_GATE_R + GATE_HALF),
        zspec(GATE_HALF, Z_GATE_A), zspec(GATE_HALF, Z_GATE_A + GATE_HALF), zspec(ATT_W, Z_QA),
        zspec(ATT_W, Z_KA, 1), zspec(ATT_W, Z_KA, 0),
        zspec(ATT_W, Z_VA, 1), zspec(ATT_W, Z_VA, 0),
        pl.BlockSpec((step_rows, D_MODEL), row),
        pl.BlockSpec((RET_HEADS, n, n), c3, pipeline_mode=one),
        pl.BlockSpec((RET_HEADS, n, HEAD_PAIR), c3, pipeline_mode=one),
        pl.BlockSpec((RET_HEADS, n, HEAD_PAIR), c3, pipeline_mode=one),
        pl.BlockSpec((ATT_HEADS, n, 3 * n), c3, pipeline_mode=one),
        pl.BlockSpec((RET_V, D_MODEL), c2, pipeline_mode=one),
        pl.BlockSpec((ATT_W, D_MODEL), c2, pipeline_mode=one),
        pl.BlockSpec((D_MODEL, D_MODEL), c2, pipeline_mode=one),
        pl.BlockSpec((1, D_MODEL), c2),
        pl.BlockSpec((1, D_MODEL), c2),
    ]
    return pl.pallas_call(
        functools.partial(_mixer_kernel, tile_decay),
        grid=(B, nt),
        in_specs=in_specs,
        out_specs=pl.BlockSpec((step_rows, D_MODEL), row),
        out_shape=jax.ShapeDtypeStruct((T, D_MODEL), F32),
        scratch_shapes=[
            pltpu.VMEM((RET_HEADS, HEAD_PAIR, RET_DV), F32),
            pltpu.VMEM((step_rows, RET_V), MXU_DTYPE),
            pltpu.VMEM((step_rows, ATT_W), MXU_DTYPE),
        ],
        compiler_params=_params("arbitrary", "arbitrary"),
        name="mixer",
    )(*([z] * 12), h, decay, q_decay, k_decay, bias, wpr, wpa, wout, ln_g, ln_b)


def _kv_kernel(m_ref, w_ref, o_ref):
    o_ref[...] = _dot(m_ref[...].astype(MXU_DTYPE), w_ref[...]).astype(o_ref.dtype)


def _mem_kv(mem2d, w_kv):
    M = mem2d.shape[0]
    tm = 256
    return pl.pallas_call(
        _kv_kernel,
        grid=(M // tm,),
        in_specs=[pl.BlockSpec((tm, D_MODEL), lambda i: (i, 0)),
                  pl.BlockSpec((D_MODEL, 2 * D_MODEL), lambda i: (0, 0), pipeline_mode=pl.Buffered(1))],
        out_specs=pl.BlockSpec((tm, 2 * D_MODEL), lambda i: (i, 0)),
        out_shape=jax.ShapeDtypeStruct((M, 2 * D_MODEL), MXU_DTYPE),
        compiler_params=_params("parallel"),
        name="mem_kv",
    )(mem2d, w_kv)


ROUTE_ROWS = 8
GROUP_ROW0 = 0
EXPERT_ROW0 = 8
ROUTER_ROWS = 128


def _memattn_kernel(h_ref, kv_ref, wq_ref, wo_ref, g_ref, b_ref, wr_ref, br_ref, tri_ref,
                    h2_ref, hb_ref, rl_ref, rr_ref, cnt_ref, o_scr):
    n = MEM_TILE
    h1 = h_ref[...]
    q = (_dot(h1.astype(MXU_DTYPE), wq_ref[...]) * (MEM_DH ** -0.5 * LOG2E)).astype(MXU_DTYPE)
    scores = [_dot_nt(q[:, hd * MEM_DH:(hd + 1) * MEM_DH], kv_ref[:, hd * MEM_DH:(hd + 1) * MEM_DH])
              for hd in range(MEM_HEADS)]
    for hd in range(MEM_HEADS):
        cols = slice(hd * MEM_DH, (hd + 1) * MEM_DH)
        s = scores[hd]
        m = jnp.max(s, axis=-1, keepdims=True)
        e = jnp.exp2(s - m)
        denom = jnp.sum(e, axis=-1, keepdims=True)
        v = kv_ref[:, D_MODEL + hd * MEM_DH:D_MODEL + (hd + 1) * MEM_DH]
        o_scr[:, cols] = (_dot(e.astype(MXU_DTYPE), v) / denom).astype(MXU_DTYPE)
    cross = _dot(o_scr[...], wo_ref[...])
    h2 = _layer_norm(DEEPNORM_ALPHA * h1 + cross, g_ref[...], b_ref[...])
    h2_ref[...] = h2
    hb = h2.astype(MXU_DTYPE)
    hb_ref[...] = hb

    logits = _dot_nt(wr_ref[...], hb) + br_ref[...]
    glog = logits[GROUP_ROW0:GROUP_ROW0 + N_GROUPS]
    gmax = jnp.max(glog, axis=0, keepdims=True)
    g_w = 1.0 / jnp.sum(jnp.exp(glog - gmax), axis=0, keepdims=True)
    giota = lax.broadcasted_iota(jnp.int32, glog.shape, 0)
    g_idx = jnp.min(jnp.where(glog == gmax, giota, N_GROUPS), axis=0, keepdims=True)
    el = jnp.zeros((EXPERTS_PER_GROUP, n), F32)
    for g in range(N_GROUPS):
        r0 = EXPERT_ROW0 + g * EXPERTS_PER_GROUP
        el = jnp.where(g_idx == g, logits[r0:r0 + EXPERTS_PER_GROUP], el)
    eiota = lax.broadcasted_iota(jnp.int32, el.shape, 0)
    m1 = jnp.max(el, axis=0, keepdims=True)
    i1 = jnp.min(jnp.where(el == m1, eiota, EXPERTS_PER_GROUP), axis=0, keepdims=True)
    el2 = jnp.where(eiota == i1, -jnp.inf, el)
    m2 = jnp.max(el2, axis=0, keepdims=True)
    i2 = jnp.min(jnp.where(el2 == m2, eiota, EXPERTS_PER_GROUP), axis=0, keepdims=True)
    r = jnp.exp(m2 - m1)
    w1 = g_w / (1.0 + r)
    w2 = g_w * r / (1.0 + r)
    e1 = g_idx * EXPERTS_PER_GROUP + i1
    e2 = g_idx * EXPERTS_PER_GROUP + i2

    xiota = lax.broadcasted_iota(jnp.int32, (N_EXPERTS, n), 0)
    oh1 = xiota == e1
    oh2 = xiota == e2
    cnt = jnp.where(oh1, 1.0, 0.0) + jnp.where(oh2, 1.0, 0.0)
    before = _dot(cnt.astype(MXU_DTYPE), tri_ref[...])
    rank1 = jnp.sum(jnp.where(oh1, before, 0.0), axis=0, keepdims=True)
    rank2 = jnp.sum(jnp.where(oh2, before, 0.0), axis=0, keepdims=True)
    for j in range(n // SEQ_TILE):
        tile_cnt = jnp.sum(cnt[:, j * SEQ_TILE:(j + 1) * SEQ_TILE], axis=1, keepdims=True)
        cnt_ref[j] = jnp.broadcast_to(tile_cnt, (N_EXPERTS, LANES))

    zero = jnp.zeros((1, n), F32)
    rec = jnp.concatenate([e1.astype(F32), e2.astype(F32), w1, w2, rank1, rank2, zero, zero], axis=0)
    rl_ref[...] = rec
    rec_full = jnp.concatenate([rec, jnp.zeros((LANES - ROUTE_ROWS, n), F32)], axis=0)
    rr_ref[...] = rec_full.T


def _memattn(h1, kv, B, S, wq, wo, ln_g, ln_b, wr_t, br_col, tri):
    T = B * S
    n = MEM_TILE
    nt = S // n
    sub = n // SEQ_TILE
    row = lambda b, i: (b * nt + i, 0)
    c2 = lambda b, i: (0, 0)
    one = pl.Buffered(1)
    return pl.pallas_call(
        _memattn_kernel,
        grid=(B, nt),
        in_specs=[
            pl.BlockSpec((n, D_MODEL), row),
            pl.BlockSpec((kv.shape[0] // B, 2 * D_MODEL), lambda b, i: (b, 0)),
            pl.BlockSpec((D_MODEL, D_MODEL), c2, pipeline_mode=one),
            pl.BlockSpec((D_MODEL, D_MODEL), c2, pipeline_mode=one),
            pl.BlockSpec((1, D_MODEL), c2),
            pl.BlockSpec((1, D_MODEL), c2),
            pl.BlockSpec((ROUTER_ROWS, D_MODEL), c2, pipeline_mode=one),
            pl.BlockSpec((ROUTER_ROWS, 1), c2),
            pl.BlockSpec((n, n), c2, pipeline_mode=one),
        ],
        out_specs=[
            pl.BlockSpec((n, D_MODEL), row),
            pl.BlockSpec((n, D_MODEL), row),
            pl.BlockSpec((ROUTE_ROWS, n), lambda b, i: (0, b * nt + i)),
            pl.BlockSpec((n, LANES), row),
            pl.BlockSpec((sub, N_EXPERTS, LANES), lambda b, i: (b * nt + i, 0, 0)),
        ],
        out_shape=[
            jax.ShapeDtypeStruct((T, D_MODEL), F32),
            jax.ShapeDtypeStruct((T, D_MODEL), MXU_DTYPE),
            jax.ShapeDtypeStruct((ROUTE_ROWS, T), F32),
            jax.ShapeDtypeStruct((T, LANES), F32),
            jax.ShapeDtypeStruct((T // SEQ_TILE, N_EXPERTS, LANES), F32),
        ],
        scratch_shapes=[pltpu.VMEM((n, D_MODEL), MXU_DTYPE)],
        compiler_params=_params("parallel", "parallel"),
        name="memattn_router",
    )(h1, kv, wq, wo, ln_g, ln_b, wr_t, br_col, tri)


CHUNK_ROWS = SUBLANES
LOCAL_ROWS = 768
N_CHUNKS = LOCAL_ROWS // CHUNK_ROWS
MIN_CHUNKS = TOP_K * SEQ_TILE // CHUNK_ROWS
assert TOP_K * SEQ_TILE + N_EXPERTS * (CHUNK_ROWS - 1) <= LOCAL_ROWS


def _unpack_rows(packed):
    lo = lax.bitcast_convert_type(packed << 16, F32)
    hi = lax.bitcast_convert_type(packed & jnp.uint32(0xFFFF0000), F32)
    return lo, hi


def _pack_rows(y):
    bits = lax.bitcast_convert_type(y.astype(jnp.bfloat16).astype(F32), jnp.uint32)
    return (bits[:, :PACK_COLS] >> 16) | bits[:, PACK_COLS:]


def _chunk_plan(counts, n_blocks):
    nt = counts.shape[0]
    c = counts.astype(jnp.int32)
    incl = jnp.asarray(np.tril(np.ones((N_EXPERTS, N_EXPERTS), bool)))
    padc = (c + CHUNK_ROWS - 1) // CHUNK_ROWS * CHUNK_ROWS
    lend = jnp.sum(jnp.where(incl[None], padc[:, None, :], 0), axis=2)
    loff = lend - padc
    earlier = jnp.asarray(np.tril(np.ones((nt, nt), bool), -1))
    cbefore = jnp.sum(jnp.where(earlier[:, :, None], padc[None, :, :], 0), axis=1)
    total = jnp.sum(padc, axis=0)
    region = (total + EXPERT_BLOCK - 1) // EXPERT_BLOCK * EXPERT_BLOCK
    gend = jnp.sum(jnp.where(incl, region[None, :], 0), axis=1)
    gstart = gend - region
    delta = gstart[None, :] + cbefore - loff
    k_row = jnp.arange(N_CHUNKS, dtype=jnp.int32) * CHUNK_ROWS
    e_of_chunk = jnp.sum((lend[:, None, :] <= k_row[None, :, None]).astype(jnp.int32), axis=2)
    onehot = e_of_chunk[:, :, None] == jnp.arange(N_EXPERTS, dtype=jnp.int32)[None, None, :]
    chunk_row = (jnp.sum(jnp.where(onehot, delta[:, None, :], 0), axis=2) + k_row[None, :]).reshape(-1)
    n_chunks = lend[:, -1] // CHUNK_ROWS
    block_first = jnp.arange(n_blocks, dtype=jnp.int32) * EXPERT_BLOCK
    block_expert = jnp.minimum(jnp.sum((gend[None, :] <= block_first[:, None]).astype(jnp.int32), axis=1),
                               N_EXPERTS - 1)
    n_used = (gend[-1:] // EXPERT_BLOCK).astype(jnp.int32)
    tail = jnp.concatenate([gstart + total, (region - total) // CHUNK_ROWS, n_used]).astype(jnp.int32)
    loff_f = loff.astype(F32)
    loff_col = jnp.broadcast_to(loff_f[:, :, None], (nt, N_EXPERTS, LANES))
    loff_row = jnp.broadcast_to(jnp.pad(loff_f, ((0, 0), (0, LANES - N_EXPERTS)))[:, None, :],
                                (nt, SUBLANES, LANES))
    return chunk_row, n_chunks, tail, block_expert, n_used, loff_col, loff_row


def _start_chunk_copies(table_ref, count_ref, tile, make_copy):
    base = tile * N_CHUNKS

    def one(k, prio):
        make_copy(k, pl.multiple_of(table_ref[base + k], CHUNK_ROWS)).start(priority=prio)

    def pair(k2, carry):
        for prio in range(2):
            one(2 * k2 + prio, prio)
        return carry

    def single(k, carry):
        one(k, 0)
        return carry

    lax.fori_loop(0, MIN_CHUNKS // 2, pair, 0, unroll=4)
    lax.fori_loop(MIN_CHUNKS, count_ref[tile], single, 0)


def _wait_chunk_copies(count_ref, tile, many, one):
    many.wait()

    def single(k, carry):
        one.wait()
        return carry

    lax.fori_loop(MIN_CHUNKS, count_ref[tile], single, 0)


def _dispatch_kernel(row_ref, cnt_ref, tail_ref, h_ref, rl_ref, loff_ref, xb_ref, sbuf, zbuf, sem, zsem):
    i = pl.program_id(0)
    n = SEQ_TILE
    xiota = lax.broadcasted_iota(jnp.int32, (N_EXPERTS, n), 0)
    riota = lax.broadcasted_iota(jnp.int32, (LOCAL_ROWS, n), 0)
    min_rows = MIN_CHUNKS * CHUNK_ROWS

    def wait_all(slot, tile):
        many = pltpu.make_async_copy(sbuf.at[slot, pl.ds(0, min_rows)], xb_ref.at[pl.ds(0, min_rows)], sem.at[slot])
        one = pltpu.make_async_copy(sbuf.at[slot, pl.ds(0, CHUNK_ROWS)], xb_ref.at[pl.ds(0, CHUNK_ROWS)],
                                    sem.at[slot])
        _wait_chunk_copies(cnt_ref, tile, many, one)

    def sorted_rows(t):
        rl = rl_ref[:, t * n:(t + 1) * n]
        loff = loff_ref[t][:, 0:1]
        slots = []
        for k in range(TOP_K):
            e = rl[k:k + 1].astype(jnp.int32)
            base = jnp.sum(jnp.where(xiota == e, loff, 0.0), axis=0, keepdims=True)
            slots.append((base + rl[4 + k:5 + k]).astype(jnp.int32))
        perm = jnp.where(riota == slots[0], 1.0, jnp.where(riota == slots[1], 1.0, 0.0)).astype(MXU_DTYPE)
        return _pack_rows(_dot(perm, h_ref[t * n:(t + 1) * n, :].astype(MXU_DTYPE)))

    packed = [sorted_rows(t) for t in range(DC_SUB)]
    for t in range(DC_SUB):
        tile = i * DC_SUB + t

        @pl.when(i >= 1)
        def _():
            wait_all(t, tile - DC_SUB)

        sbuf[t] = packed[t]

        def chunk_copy(k, row):
            src = sbuf.at[t, pl.ds(pl.multiple_of(k * CHUNK_ROWS, CHUNK_ROWS), CHUNK_ROWS)]
            return pltpu.make_async_copy(src, xb_ref.at[pl.ds(row, CHUNK_ROWS)], sem.at[t])

        _start_chunk_copies(row_ref, cnt_ref, tile, chunk_copy)

    @pl.when(i == pl.num_programs(0) - 1)
    def _():
        zbuf[...] = jnp.zeros_like(zbuf)

        def fill(start_copy):
            def per_expert(e, carry):
                first = pl.multiple_of(tail_ref[e], CHUNK_ROWS)

                def per_chunk(c, inner):
                    row = pl.multiple_of(first + c * CHUNK_ROWS, CHUNK_ROWS)
                    copy = pltpu.make_async_copy(zbuf, xb_ref.at[pl.ds(row, CHUNK_ROWS)], zsem)
                    if start_copy:
                        copy.start()
                    else:
                        copy.wait()
                    return inner

                lax.fori_loop(0, tail_ref[N_EXPERTS + e], per_chunk, 0)
                return carry

            lax.fori_loop(0, N_EXPERTS, per_expert, 0)

        fill(True)
        for t in range(DC_SUB):
            wait_all(t, i * DC_SUB + t)
        fill(False)

        zblock = sbuf.at[0, pl.ds(0, EXPERT_BLOCK)]
        zblock[...] = jnp.zeros_like(zblock)
        n_all = xb_ref.shape[0] // EXPERT_BLOCK

        def fill_blocks(start_copy):
            def per_block(p, carry):
                row = pl.multiple_of(p * EXPERT_BLOCK, EXPERT_BLOCK)
                copy = pltpu.make_async_copy(zblock, xb_ref.at[pl.ds(row, EXPERT_BLOCK)], zsem)
                if start_copy:
                    copy.start()
                else:
                    copy.wait()
                return carry

            lax.fori_loop(tail_ref[2 * N_EXPERTS], n_all, per_block, 0)

        fill_blocks(True)
        fill_blocks(False)


def _dispatch(chunk_row, n_chunks, tail, hb, rl, loff_col, n_blocks):
    T = hb.shape[0]
    n = DC_SUB * SEQ_TILE
    rows = n_blocks * EXPERT_BLOCK
    grid_spec = pltpu.PrefetchScalarGridSpec(
        num_scalar_prefetch=3,
        grid=(T // n,),
        in_specs=[
            pl.BlockSpec((n, D_MODEL), lambda i, r, c, t: (i, 0)),
            pl.BlockSpec((ROUTE_ROWS, n), lambda i, r, c, t: (0, i)),
            pl.BlockSpec((DC_SUB, N_EXPERTS, LANES), lambda i, r, c, t: (i, 0, 0)),
        ],
        out_specs=pl.BlockSpec(memory_space=pl.ANY),
        scratch_shapes=[pltpu.VMEM((DC_SUB, LOCAL_ROWS, PACK_COLS), jnp.uint32),
                        pltpu.VMEM((CHUNK_ROWS, PACK_COLS), jnp.uint32),
                        pltpu.SemaphoreType.DMA((DC_SUB,)), pltpu.SemaphoreType.DMA],
    )
    assert T % n == 0 and LOCAL_ROWS >= EXPERT_BLOCK
    return pl.pallas_call(
        _dispatch_kernel,
        grid_spec=grid_spec,
        out_shape=jax.ShapeDtypeStruct((rows, PACK_COLS), jnp.uint32),
        compiler_params=_params("arbitrary"),
        name="dispatch",
    )(chunk_row, n_chunks, tail, hb, rl, loff_col)


def _expert_kernel(be_ref, nused_ref, x_ref, wg_ref, wu_ref, wd_ref, y_ref, wg_b, wu_b, wd_b):
    p = pl.program_id(0)
    new_expert = jnp.logical_or(p == 0, be_ref[p] != be_ref[jnp.maximum(p - 1, 0)])

    @pl.when(new_expert)
    def _():
        wg_b[...] = wg_ref[...].astype(MXU_DTYPE)
        wu_b[...] = wu_ref[...].astype(MXU_DTYPE)
        wd_b[...] = wd_ref[...].astype(MXU_DTYPE)

    @pl.when(p < nused_ref[0])
    def _():
        lo, hi = _unpack_rows(x_ref[...])
        x = jnp.concatenate([lo.astype(MXU_DTYPE), hi.astype(MXU_DTYPE)], axis=1)
        gate = _dot(x, wg_b[...])
        up = _dot(x, wu_b[...])
        hid = (gate * _sigmoid(gate) * up).astype(MXU_DTYPE)
        y_ref[...] = _pack_rows(_dot(hid, wd_b[...]))

    @pl.when(p >= nused_ref[0])
    def _():
        y_ref[...] = jnp.zeros_like(y_ref)


def _experts(block_expert, n_used, xb, wg, wu, wd, layer):
    nb = block_expert.shape[0]
    n_slots = nb * EXPERT_BLOCK
    grid_spec = pltpu.PrefetchScalarGridSpec(
        num_scalar_prefetch=2,
        grid=(nb,),
        in_specs=[
            pl.BlockSpec((EXPERT_BLOCK, PACK_COLS), lambda p, be, nu: (jnp.minimum(p, jnp.maximum(nu[0] - 1, 0)), 0)),
            pl.BlockSpec((None, None, D_MODEL, D_EXPERT), lambda p, be, nu: (layer, be[p], 0, 0)),
            pl.BlockSpec((None, None, D_MODEL, D_EXPERT), lambda p, be, nu: (layer, be[p], 0, 0)),
            pl.BlockSpec((None, None, D_EXPERT, D_MODEL), lambda p, be, nu: (layer, be[p], 0, 0)),
        ],
        out_specs=pl.BlockSpec((EXPERT_BLOCK, PACK_COLS), lambda p, be, nu: (p, 0)),
        scratch_shapes=[pltpu.VMEM((D_MODEL, D_EXPERT), MXU_DTYPE), pltpu.VMEM((D_MODEL, D_EXPERT), MXU_DTYPE),
                        pltpu.VMEM((D_EXPERT, D_MODEL), MXU_DTYPE)],
    )
    return pl.pallas_call(
        _expert_kernel,
        grid_spec=grid_spec,
        out_shape=jax.ShapeDtypeStruct((n_slots, PACK_COLS), jnp.uint32),
        compiler_params=_params("arbitrary"),
        name="experts",
    )(block_expert, n_used, xb, wg, wu, wd)


def _combine_kernel(row_ref, cnt_ref, yb_ref, rr_ref, loff_ref, h_ref, g_ref, b_ref, o_ref, ybuf, sem):
    i = pl.program_id(0)
    n = SEQ_TILE
    min_rows = MIN_CHUNKS * CHUNK_ROWS

    def fetch(tile, into):
        def chunk_copy(k, row):
            dst = ybuf.at[into, pl.ds(pl.multiple_of(k * CHUNK_ROWS, CHUNK_ROWS), CHUNK_ROWS)]
            return pltpu.make_async_copy(yb_ref.at[pl.ds(row, CHUNK_ROWS)], dst, sem.at[into])

        _start_chunk_copies(row_ref, cnt_ref, tile, chunk_copy)

    @pl.when(i == 0)
    def _():
        ybuf[...] = jnp.zeros_like(ybuf)
        for t in range(DC_SUB):
            fetch(t, t)

    lane = lax.broadcasted_iota(jnp.int32, (n, LANES), 1)
    ciota = lax.broadcasted_iota(jnp.int32, (n, LOCAL_ROWS), 1)

    def weights(t):
        rr = rr_ref[t * n:(t + 1) * n, :]
        loff = loff_ref[t][0:1, :]
        mix = jnp.zeros((n, LOCAL_ROWS), F32)
        for k in range(TOP_K):
            e = rr[:, k:k + 1].astype(jnp.int32)
            base = jnp.sum(jnp.where(lane == e, loff, 0.0), axis=1, keepdims=True)
            sorted_row = (base + rr[:, 4 + k:5 + k]).astype(jnp.int32)
            mix = jnp.where(ciota == sorted_row, rr[:, 2 + k:3 + k], mix)
        return mix.astype(MXU_DTYPE)

    mixes = [weights(t) for t in range(DC_SUB)]
    for t in range(DC_SUB):
        many = pltpu.make_async_copy(yb_ref.at[pl.ds(0, min_rows)], ybuf.at[t, pl.ds(0, min_rows)], sem.at[t])
        one = pltpu.make_async_copy(yb_ref.at[pl.ds(0, CHUNK_ROWS)], ybuf.at[t, pl.ds(0, CHUNK_ROWS)], sem.at[t])
        _wait_chunk_copies(cnt_ref, i * DC_SUB + t, many, one)
        lo, hi = _unpack_rows(ybuf[t])
        y = jnp.concatenate([_dot(mixes[t], lo.astype(MXU_DTYPE)), _dot(mixes[t], hi.astype(MXU_DTYPE))], axis=1)
        rows = slice(t * n, (t + 1) * n)
        o_ref[rows, :] = _layer_norm(DEEPNORM_ALPHA * h_ref[rows, :] + y, g_ref[...], b_ref[...])

        @pl.when(i + 1 < pl.num_programs(0))
        def _():
            fetch((i + 1) * DC_SUB + t, t)


def _combine(chunk_row, n_chunks, yb, rr, loff_row, h2, ln_g, ln_b):
    T = h2.shape[0]
    n = DC_SUB * SEQ_TILE
    row = lambda i, r, c: (i, 0)
    grid_spec = pltpu.PrefetchScalarGridSpec(
        num_scalar_prefetch=2,
        grid=(T // n,),
        in_specs=[
            pl.BlockSpec(memory_space=pl.ANY),
            pl.BlockSpec((n, LANES), row),
            pl.BlockSpec((DC_SUB, SUBLANES, LANES), lambda i, r, c: (i, 0, 0)),
            pl.BlockSpec((n, D_MODEL), row),
            pl.BlockSpec((1, D_MODEL), lambda i, r, c: (0, 0)),
            pl.BlockSpec((1, D_MODEL), lambda i, r, c: (0, 0)),
        ],
        out_specs=pl.BlockSpec((n, D_MODEL), row),
        scratch_shapes=[pltpu.VMEM((DC_SUB, LOCAL_ROWS, PACK_COLS), jnp.uint32),
                        pltpu.SemaphoreType.DMA((DC_SUB,))],
    )
    return pl.pallas_call(
        _combine_kernel,
        grid_spec=grid_spec,
        out_shape=jax.ShapeDtypeStruct((T, D_MODEL), F32),
        compiler_params=_params("arbitrary"),
        name="combine",
    )(chunk_row, n_chunks, yb, rr, loff_row, h2, ln_g, ln_b)


def _router_weights(w_group, b_group, w_route, b_route):
    wr = jnp.zeros((ROUTER_ROWS, D_MODEL), F32)
    wr = wr.at[GROUP_ROW0:GROUP_ROW0 + N_GROUPS].set(w_group.T)
    wr = wr.at[EXPERT_ROW0:EXPERT_ROW0 + N_EXPERTS].set(w_route.T)
    br = jnp.zeros((ROUTER_ROWS,), F32)
    br = br.at[GROUP_ROW0:GROUP_ROW0 + N_GROUPS].set(b_group)
    br = br.at[EXPERT_ROW0:EXPERT_ROW0 + N_EXPERTS].set(b_route.reshape(-1))
    return wr.astype(MXU_DTYPE), br.reshape(ROUTER_ROWS, 1)


def kernel(x, mem, positions, ln_in_g, ln_in_b, w_in, rel_bias, w_proj_ret, w_proj_att, w_out, ln1_g, ln1_b, w_q_mem, w_kv_mem, w_o_mem, ln2_g, ln2_b, w_group, b_group, w_route, b_route, w_gate, w_up, w_down, ln3_g, ln3_b):
    B, S, D = x.shape
    assert D == D_MODEL and S % PROJ_TILE == 0 and S % SEQ_TILE == 0 and S % MEM_TILE == 0
    T = B * S
    A = T * TOP_K
    n_blocks = -(-(A + (T // SEQ_TILE) * N_EXPERTS * (CHUNK_ROWS - 1) + N_EXPERTS * (EXPERT_BLOCK - 1)) // EXPERT_BLOCK)
    bf = MXU_DTYPE

    cos_t, sin_t = _rope_tables(positions)
    tables = _retention_tables()
    pos = np.arange(MEM_TILE)
    tri = jnp.asarray((pos[:, None] < pos[None, :]) & (pos[:, None] // SEQ_TILE == pos[None, :] // SEQ_TILE), bf)
    mem2d = mem.reshape(-1, D)
    row2 = lambda v: v.reshape(1, D)

    h = x.reshape(T, D)
    for l in range(DEPTH):
        w_in_b = _to_mxu(w_in, l)
        if l == 0:
            z, h = _inproj(h, row2(ln_in_g), row2(ln_in_b), w_in_b, cos_t, sin_t, True)
        else:
            (z,) = _inproj(h, row2(ln_in_g), row2(ln_in_b), w_in_b, cos_t, sin_t, False)
        bias = _attention_bias(rel_bias[l])
        h = _mixer(z, h, B, S, tables, bias, _to_mxu(w_proj_ret, l), _to_mxu(w_proj_att, l),
                   _to_mxu(w_out, l), row2(ln1_g[l]), row2(ln1_b[l]))

        kv = _mem_kv(mem2d, _to_mxu(w_kv_mem, l))
        wr_t, br_col = _router_weights(w_group[l], b_group[l], w_route[l], b_route[l])
        h2, hb, rl, rr, counts = _memattn(h, kv, B, S, _to_mxu(w_q_mem, l), _to_mxu(w_o_mem, l),
                                          row2(ln2_g[l]), row2(ln2_b[l]), wr_t, br_col, tri)

        chunk_row, n_chunks, tail, block_expert, n_used, loff_col, loff_row = _chunk_plan(counts[:, :, 0], n_blocks)
        xb = _dispatch(chunk_row, n_chunks, tail, hb, rl, loff_col, n_blocks)
        yb = _experts(block_expert, n_used, xb, w_gate, w_up, w_down, l)
        h = _combine(chunk_row, n_chunks, yb, rr, loff_row, h2, row2(ln3_g[l]), row2(ln3_b[l]))
    return h.reshape(B, S, D)
```

```python
import functools

import jax
import jax.numpy as jnp
import numpy as np
from jax import lax
from jax.experimental import pallas as pl
from jax.experimental.pallas import tpu as pltpu

D_MODEL = 1024
DEPTH = 2
CHUNK = 64
RET_HEADS = 8
RET_DK = 64
RET_DV = 128
RET_QK = RET_HEADS * RET_DK
RET_V = RET_HEADS * RET_DV
ROPE_BASE = 10000.0
ATT_HEADS = 8
ATT_DH = 64
ATT_W = ATT_HEADS * ATT_DH
LEFT_CHUNKS = 8
MAX_REL = 256
MEM_HEADS = 4
MEM_DH = D_MODEL // MEM_HEADS
N_GROUPS = 4
EXPERTS_PER_GROUP = 8
N_EXPERTS = N_GROUPS * EXPERTS_PER_GROUP
TOP_K = 2
D_EXPERT = 512
LN_EPS = 1e-5
DEEPNORM_ALPHA = (2.0 * DEPTH) ** 0.25
IN_TOTAL = 2 * RET_QK + 2 * RET_V + 3 * ATT_W + 2 * D_MODEL

LANES = 128
SUBLANES = 8
VMEM_LIMIT_BYTES = 56 * 1024 * 1024

SEQ_TILE = 256
MEM_TILE = 2 * SEQ_TILE
MIX_SUB = 2
DC_SUB = 2
PROJ_TILE = 512
PROJ_COLS = 512
EXPERT_BLOCK = 512
HEAD_PAIR = 2 * RET_DK
NEG_BIG = -1e30
LOG2E = 1.4426950408889634

F32 = jnp.float32
MXU_DTYPE = jnp.bfloat16
PACK_COLS = D_MODEL // 2

Z_QK = 0
Z_VR = Z_QK + 2 * RET_QK
Z_GR = Z_VR + RET_V
Z_QA = Z_GR + RET_V
Z_KA = Z_QA + ATT_W
Z_VA = Z_KA + ATT_W
Z_GATE_R = Z_VA + ATT_W
Z_GATE_A = Z_GATE_R + D_MODEL
GATE_HALF = D_MODEL // 2


def _layer_norm(x, g, b):
    mu = jnp.mean(x, axis=-1, keepdims=True)
    xc = x - mu
    var = jnp.mean(xc * xc, axis=-1, keepdims=True)
    return xc * lax.rsqrt(var + LN_EPS) * g + b


def _sigmoid(x):
    return 1.0 / (1.0 + jnp.exp(-x))


def _dot(a, b):
    return jnp.dot(a, b, preferred_element_type=F32)


def _dot_nt(a, b):
    return lax.dot_general(a, b, (((1,), (1,)), ((), ())), preferred_element_type=F32)


def _dot_tn(a, b):
    return lax.dot_general(a, b, (((0,), (0,)), ((), ())), preferred_element_type=F32)


def _params(*semantics):
    return pltpu.CompilerParams(dimension_semantics=semantics, vmem_limit_bytes=VMEM_LIMIT_BYTES)


def _cast_kernel(x_ref, o_ref):
    o_ref[...] = x_ref[...].astype(o_ref.dtype)


def _to_mxu(w_stack, layer):
    _, r, c = w_stack.shape
    tr = 256
    return pl.pallas_call(
        _cast_kernel,
        grid=(r // tr,),
        in_specs=[pl.BlockSpec((None, tr, c), lambda i: (layer, i, 0))],
        out_specs=pl.BlockSpec((tr, c), lambda i: (i, 0)),
        out_shape=jax.ShapeDtypeStruct((r, c), MXU_DTYPE),
        compiler_params=_params("parallel"),
        name="cast_weight",
    )(w_stack)


def _rope_kernel(pos_ref, invf_ref, cos_ref, sin_ref):
    ang = pos_ref[...] * invf_ref[...]
    lane = lax.broadcasted_iota(jnp.int32, ang.shape, 1)
    s = jnp.sin(ang)
    cos_ref[...] = jnp.cos(ang)
    sin_ref[...] = jnp.where((lane % RET_DK) < RET_DK // 2, -s, s)


def _rope_tables(positions):
    T = positions.size
    half = RET_DK // 2
    inv_freq = 1.0 / (ROPE_BASE ** jnp.linspace(0.0, 1.0, half, dtype=F32))
    invf = jnp.tile(inv_freq, LANES // half).reshape(1, LANES)
    pos = jnp.broadcast_to(positions.reshape(T, 1).astype(F32), (T, LANES))
    tm = 1024
    spec = pl.BlockSpec((tm, LANES), lambda i: (i, 0))
    return pl.pallas_call(
        _rope_kernel,
        grid=(T // tm,),
        in_specs=[spec, pl.BlockSpec((1, LANES), lambda i: (0, 0))],
        out_specs=[spec, spec],
        out_shape=[jax.ShapeDtypeStruct((T, LANES), F32)] * 2,
        compiler_params=_params("parallel"),
        name="rope_tables",
    )(pos, invf)


def _rotary(a, cos, sin_signed):
    outs = []
    lane = lax.broadcasted_iota(jnp.int32, cos.shape, 1)
    first_half = (lane % RET_DK) < RET_DK // 2
    for g in range(a.shape[1] // LANES):
        x = a[:, g * LANES:(g + 1) * LANES]
        rot = jnp.where(first_half, pltpu.roll(x, LANES - RET_DK // 2, 1), pltpu.roll(x, RET_DK // 2, 1))
        outs.append(x * cos + rot * sin_signed)
    return jnp.concatenate(outs, axis=1)


def _project_columns(xb, w_ref, cos, sin, z_ref, chunks):
    for c in chunks:
        lo = c * PROJ_COLS
        acc = _dot(xb, w_ref[:, lo:lo + PROJ_COLS])
        if lo < 2 * RET_QK:
            acc = _rotary(acc, cos, sin)
            if lo >= RET_QK:
                acc = acc * (RET_DK ** -0.5)
        z_ref[:, lo:lo + PROJ_COLS] = acc.astype(z_ref.dtype)


def _inproj_kernel(apply_ln, h_ref, g_ref, b_ref, w_ref, cos_ref, sin_ref, z_ref, *h_out):
    x = h_ref[...]
    if apply_ln:
        x = _layer_norm(x, g_ref[...], b_ref[...])
        h_out[0][...] = x
    _project_columns(x.astype(MXU_DTYPE), w_ref, cos_ref[...], sin_ref[...], z_ref, range(IN_TOTAL // PROJ_COLS))


def _inproj(h, ln_g, ln_b, w_in_b, cos_t, sin_t, apply_ln):
    T = h.shape[0]
    tm = PROJ_TILE
    row = lambda i: (i, 0)
    const = lambda i: (0, 0)
    out_shape = [jax.ShapeDtypeStruct((T, IN_TOTAL), MXU_DTYPE)]
    out_specs = [pl.BlockSpec((tm, IN_TOTAL), row)]
    if apply_ln:
        out_shape.append(jax.ShapeDtypeStruct((T, D_MODEL), F32))
        out_specs.append(pl.BlockSpec((tm, D_MODEL), row))
    res = pl.pallas_call(
        functools.partial(_inproj_kernel, apply_ln),
        grid=(T // tm,),
        in_specs=[
            pl.BlockSpec((tm, D_MODEL), row),
            pl.BlockSpec((1, D_MODEL), const),
            pl.BlockSpec((1, D_MODEL), const),
            pl.BlockSpec((D_MODEL, IN_TOTAL), const, pipeline_mode=pl.Buffered(1)),
            pl.BlockSpec((tm, LANES), row),
            pl.BlockSpec((tm, LANES), row),
        ],
        out_specs=out_specs,
        out_shape=out_shape,
        compiler_params=_params("parallel"),
        name="inproj",
    )(h, ln_g, ln_b, w_in_b, cos_t, sin_t)
    return res


def _retention_tables():
    n = SEQ_TILE
    log_g = np.log1p(-np.exp2(-5.0 - np.arange(RET_HEADS, dtype=np.float64)))
    i = np.arange(n)
    same = (i[:, None] // CHUNK) == (i[None, :] // CHUNK)
    earlier = (i[None, :] // CHUNK) < (i[:, None] // CHUNK)
    diff = i[:, None] - i[None, :]
    expo = np.where(same, np.abs(diff), np.where(earlier, diff, 0)).astype(np.float64)
    decay = np.exp(log_g[:, None, None] * expo) * (same | earlier)
    q_decay = np.exp(log_g[:, None] * (i + 1.0))
    k_decay = np.exp(log_g[:, None] * (n - 1.0 - i))
    tile_decay = np.exp(log_g * n)
    wide = lambda t: np.broadcast_to(t[:, :, None], t.shape + (HEAD_PAIR,))
    owner = (np.arange(HEAD_PAIR)[None, :] // RET_DK) == (np.arange(RET_HEADS)[:, None] % 2)
    k_table = wide(k_decay) * owner[:, None, :]
    return (jnp.asarray(decay, F32), jnp.asarray(wide(q_decay), F32), jnp.asarray(k_table, F32),
            tuple(float(v) for v in tile_decay))


def _attention_bias(rel_bias):
    n = SEQ_TILE
    width = 3 * n
    pad = LEFT_CHUNKS * CHUNK
    assert pad == 2 * n and n <= MAX_REL
    heads = rel_bias.shape[0]
    length = width + n
    near = rel_bias[:, MAX_REL - n:].astype(F32)
    far = jnp.broadcast_to(rel_bias[:, -1:].astype(F32), (heads, length - near.shape[1]))
    g_rev = jnp.concatenate([near, far], axis=1)[:, ::-1]
    rows = jnp.tile(g_rev, (1, n + 1))[:, :n * (length - 1)].reshape(heads, n, length - 1)
    bias = rows[:, :, n - 1:n - 1 + width]
    qi = np.arange(n)
    kj = np.arange(width)
    q_chunk = (pad + qi) // CHUNK
    k_chunk = kj // CHUNK
    in_band = (k_chunk[None, :] <= q_chunk[:, None]) & (k_chunk[None, :] >= q_chunk[:, None] - LEFT_CHUNKS)
    return jnp.where(jnp.asarray(in_band)[None], bias * LOG2E, NEG_BIG)


def _mixer_kernel(tile_decay,
                  qk_ref, vr_ref, gr_ref, gate_r0_ref, gate_r1_ref, gate_a0_ref, gate_a1_ref, qa_ref,
                  kp_ref, kc_ref, vp_ref, vc_ref,
                  h_ref, decay_ref, qdec_ref, kdec_ref, bias_ref,
                  wpr_ref, wpa_ref, wout_ref, g_ref, b_ref,
                  o_ref, state_ref, yr_ref, ya_ref):
    i = pl.program_id(1)

    @pl.when(i == 0)
    def _():
        state_ref[...] = jnp.zeros_like(state_ref)

    n = SEQ_TILE
    lane = lax.broadcasted_iota(jnp.int32, (n, HEAD_PAIR), 1)
    low_half = lane < RET_DK

    def head_lanes(odd):
        return low_half if odd == 0 else jnp.logical_not(low_half)

    def rows(t):
        return slice(t * n, (t + 1) * n)

    def retention_state_update(t, hd):
        p = hd // 2
        k2 = qk_ref[rows(t), RET_QK + p * HEAD_PAIR:RET_QK + (p + 1) * HEAD_PAIR]
        v = vr_ref[rows(t), hd * RET_DV:(hd + 1) * RET_DV]
        k_dec = (k2.astype(F32) * kdec_ref[hd]).astype(MXU_DTYPE)
        state_ref[hd] = state_ref[hd] * tile_decay[hd] + _dot_tn(k_dec, v)

    def retention_first(t, hd):
        p, odd = divmod(hd, 2)
        q2 = qk_ref[rows(t), p * HEAD_PAIR:(p + 1) * HEAD_PAIR]
        k2 = qk_ref[rows(t), RET_QK + p * HEAD_PAIR:RET_QK + (p + 1) * HEAD_PAIR]
        qm = jnp.where(head_lanes(odd), q2, jnp.zeros_like(q2))
        v = vr_ref[rows(t), hd * RET_DV:(hd + 1) * RET_DV]
        raw = _dot_nt(qm, k2)
        read = _dot(qm, state_ref[hd].astype(MXU_DTYPE))
        if t + 1 < MIX_SUB:
            retention_state_update(t, hd)
        return raw, read, v

    def retention_second(t, hd, raw, read, v):
        intra = _dot((raw * decay_ref[hd]).astype(MXU_DTYPE), v)
        ret = intra + qdec_ref[hd] * read
        mu = jnp.mean(ret, axis=-1, keepdims=True)
        rc = ret - mu
        var = jnp.mean(rc * rc, axis=-1, keepdims=True)
        gn = rc * lax.rsqrt(var + LN_EPS)
        gate = gr_ref[rows(t), hd * RET_DV:(hd + 1) * RET_DV].astype(F32)
        yr_ref[rows(t), hd * RET_DV:(hd + 1) * RET_DV] = (gate * _sigmoid(gate) * gn).astype(MXU_DTYPE)

    pen_prev = jnp.where(i >= 1, 0.0, NEG_BIG).astype(F32)

    def key_block(t, kb):
        j = t + kb + (MIX_SUB - 2)
        if j < MIX_SUB:
            return kp_ref, vp_ref, rows(j), pen_prev
        return kc_ref, vc_ref, rows(j - MIX_SUB), None

    def attention_first(t, hd):
        p, odd = divmod(hd, 2)
        cols = slice(p * HEAD_PAIR, (p + 1) * HEAD_PAIR)
        qs = (qa_ref[rows(t), cols].astype(F32) * (ATT_DH ** -0.5 * LOG2E)).astype(MXU_DTYPE)
        qm = jnp.where(head_lanes(odd), qs, jnp.zeros_like(qs))
        s = []
        for kb in range(3):
            k_ref, _, krows, pen = key_block(t, kb)
            sk = _dot_nt(qm, k_ref[krows, cols]) + bias_ref[hd, :, kb * n:(kb + 1) * n]
            if pen is not None:
                sk = sk + pen
            s.append(sk)
        return (s,)

    def attention_second(t, hd, s):
        p, odd = divmod(hd, 2)
        cols = slice(p * HEAD_PAIR, (p + 1) * HEAD_PAIR)
        m = jnp.maximum(jnp.maximum(jnp.max(s[0], axis=-1, keepdims=True),
                                    jnp.max(s[1], axis=-1, keepdims=True)),
                        jnp.max(s[2], axis=-1, keepdims=True))
        e = [jnp.exp2(sk - m) for sk in s]
        denom = (jnp.sum(e[0], axis=-1, keepdims=True) + jnp.sum(e[1], axis=-1, keepdims=True)
                 + jnp.sum(e[2], axis=-1, keepdims=True))
        pv = None
        for kb in range(3):
            _, v_ref, krows, _ = key_block(t, kb)
            term = _dot(e[kb].astype(MXU_DTYPE), v_ref[krows, cols])
            pv = term if pv is None else pv + term
        out = (pv / denom).astype(MXU_DTYPE)
        lo = odd * ATT_DH
        ya_ref[rows(t), p * HEAD_PAIR + lo:p * HEAD_PAIR + lo + ATT_DH] = out[:, lo:lo + ATT_DH]

    work = []
    for t in range(MIX_SUB):
        for hd in range(max(RET_HEADS, ATT_HEADS)):
            if hd < RET_HEADS:
                work.append((retention_first, retention_second, t, hd))
            if hd < ATT_HEADS:
                work.append((attention_first, attention_second, t, hd))
    ahead = 2
    pending = [first(t, hd) for first, _, t, hd in work[:ahead]]
    for j, (_, second, t, hd) in enumerate(work):
        if j + ahead < len(work):
            first, _, t_next, hd_next = work[j + ahead]
            pending.append(first(t_next, hd_next))
        second(t, hd, *pending.pop(0))

    pr = _dot(yr_ref[...], wpr_ref[...])
    pa = _dot(ya_ref[...], wpa_ref[...])
    gate_r = jnp.concatenate([gate_r0_ref[...], gate_r1_ref[...]], axis=1).astype(F32)
    gate_a = jnp.concatenate([gate_a0_ref[...], gate_a1_ref[...]], axis=1).astype(F32)
    merged = _sigmoid(gate_r) * pr + _sigmoid(gate_a) * pa
    mix = _dot(merged.astype(MXU_DTYPE), wout_ref[...])
    for hd in range(RET_HEADS):
        retention_state_update(MIX_SUB - 1, hd)
    o_ref[...] = _layer_norm(DEEPNORM_ALPHA * h_ref[...] + mix, g_ref[...], b_ref[...])


def _mixer(z, h, B, S, tables, bias, wpr, wpa, wout, ln_g, ln_b):
    decay, q_decay, k_decay, tile_decay = tables
    T = B * S
    n = SEQ_TILE
    step_rows = MIX_SUB * n
    nt = S // step_rows
    assert MIX_SUB >= 2 and S % step_rows == 0

    def zspec(width, col_off, back=0):
        cb = col_off // width
        return pl.BlockSpec((step_rows, width), lambda b, i: (b * nt + jnp.maximum(i - back, 0), cb))

    row = lambda b, i: (b * nt + i, 0)
    c2 = lambda b, i: (0, 0)
    c3 = lambda b, i: (0, 0, 0)
    one = pl.Buffered(1)
    in_specs = [
        zspec(2 * RET_QK, Z_QK), zspec(RET_V, Z_VR), zspec(RET_V, Z_GR),
        zspec(GATE_HALF, Z_GATE_R), zspec(GATE_HALF, Z_GATE_R + GATE_HALF),
        zspec(GATE_HALF, Z_GATE_A), zspec(GATE_HALF, Z_GATE_A + GATE_HALF), zspec(ATT_W, Z_QA),
        zspec(ATT_W, Z_KA, 1), zspec(ATT_W, Z_KA, 0),
        zspec(ATT_W, Z_VA, 1), zspec(ATT_W, Z_VA, 0),
        pl.BlockSpec((step_rows, D_MODEL), row),
        pl.BlockSpec((RET_HEADS, n, n), c3, pipeline_mode=one),
        pl.BlockSpec((RET_HEADS, n, HEAD_PAIR), c3, pipeline_mode=one),
        pl.BlockSpec((RET_HEADS, n, HEAD_PAIR), c3, pipeline_mode=one),
        pl.BlockSpec((ATT_HEADS, n, 3 * n), c3, pipeline_mode=one),
        pl.BlockSpec((RET_V, D_MODEL), c2, pipeline_mode=one),
        pl.BlockSpec((ATT_W, D_MODEL), c2, pipeline_mode=one),
        pl.BlockSpec((D_MODEL, D_MODEL), c2, pipeline_mode=one),
        pl.BlockSpec((1, D_MODEL), c2),
        pl.BlockSpec((1, D_MODEL), c2),
    ]
    return pl.pallas_call(
        functools.partial(_mixer_kernel, tile_decay),
        grid=(B, nt),
        in_specs=in_specs,
        out_specs=pl.BlockSpec((step_rows, D_MODEL), row),
        out_shape=jax.ShapeDtypeStruct((T, D_MODEL), F32),
        scratch_shapes=[
            pltpu.VMEM((RET_HEADS, HEAD_PAIR, RET_DV), F32),
            pltpu.VMEM((step_rows, RET_V), MXU_DTYPE),
            pltpu.VMEM((step_rows, ATT_W), MXU_DTYPE),
        ],
        compiler_params=_params("arbitrary", "arbitrary"),
        name="mixer",
    )(*([z] * 12), h, decay, q_decay, k_decay, bias, wpr, wpa, wout, ln_g, ln_b)


def _kv_kernel(m_ref, w_ref, o_ref):
    o_ref[...] = _dot(m_ref[...].astype(MXU_DTYPE), w_ref[...]).astype(o_ref.dtype)


def _mem_kv(mem2d, w_kv):
    M = mem2d.shape[0]
    tm = 256
    return pl.pallas_call(
        _kv_kernel,
        grid=(M // tm,),
        in_specs=[pl.BlockSpec((tm, D_MODEL), lambda i: (i, 0)),
                  pl.BlockSpec((D_MODEL, 2 * D_MODEL), lambda i: (0, 0), pipeline_mode=pl.Buffered(1))],
        out_specs=pl.BlockSpec((tm, 2 * D_MODEL), lambda i: (i, 0)),
        out_shape=jax.ShapeDtypeStruct((M, 2 * D_MODEL), MXU_DTYPE),
        compiler_params=_params("parallel"),
        name="mem_kv",
    )(mem2d, w_kv)


ROUTE_ROWS = 8
GROUP_ROW0 = 0
EXPERT_ROW0 = 8
ROUTER_ROWS = 128


def _memattn_kernel(h_ref, kv_ref, wq_ref, wo_ref, g_ref, b_ref, wr_ref, br_ref, tri_ref,
                    h2_ref, hb_ref, rl_ref, rr_ref, cnt_ref, o_scr):
    n = MEM_TILE
    h1 = h_ref[...]
    q = (_dot(h1.astype(MXU_DTYPE), wq_ref[...]) * (MEM_DH ** -0.5 * LOG2E)).astype(MXU_DTYPE)
    scores = [_dot_nt(q[:, hd * MEM_DH:(hd + 1) * MEM_DH], kv_ref[:, hd * MEM_DH:(hd + 1) * MEM_DH])
              for hd in range(MEM_HEADS)]
    for hd in range(MEM_HEADS):
        cols = slice(hd * MEM_DH, (hd + 1) * MEM_DH)
        s = scores[hd]
        m = jnp.max(s, axis=-1, keepdims=True)
        e = jnp.exp2(s - m)
        denom = jnp.sum(e, axis=-1, keepdims=True)
        v = kv_ref[:, D_MODEL + hd * MEM_DH:D_MODEL + (hd + 1) * MEM_DH]
        o_scr[:, cols] = (_dot(e.astype(MXU_DTYPE), v) / denom).astype(MXU_DTYPE)
    cross = _dot(o_scr[...], wo_ref[...])
    h2 = _layer_norm(DEEPNORM_ALPHA * h1 + cross, g_ref[...], b_ref[...])
    h2_ref[...] = h2
    hb = h2.astype(MXU_DTYPE)
    hb_ref[...] = hb

    logits = _dot_nt(wr_ref[...], hb) + br_ref[...]
    glog = logits[GROUP_ROW0:GROUP_ROW0 + N_GROUPS]
    gmax = jnp.max(glog, axis=0, keepdims=True)
    g_w = 1.0 / jnp.sum(jnp.exp(glog - gmax), axis=0, keepdims=True)
    giota = lax.broadcasted_iota(jnp.int32, glog.shape, 0)
    g_idx = jnp.min(jnp.where(glog == gmax, giota, N_GROUPS), axis=0, keepdims=True)
    el = jnp.zeros((EXPERTS_PER_GROUP, n), F32)
    for g in range(N_GROUPS):
        r0 = EXPERT_ROW0 + g * EXPERTS_PER_GROUP
        el = jnp.where(g_idx == g, logits[r0:r0 + EXPERTS_PER_GROUP], el)
    eiota = lax.broadcasted_iota(jnp.int32, el.shape, 0)
    m1 = jnp.max(el, axis=0, keepdims=True)
    i1 = jnp.min(jnp.where(el == m1, eiota, EXPERTS_PER_GROUP), axis=0, keepdims=True)
    el2 = jnp.where(eiota == i1, -jnp.inf, el)
    m2 = jnp.max(el2, axis=0, keepdims=True)
    i2 = jnp.min(jnp.where(el2 == m2, eiota, EXPERTS_PER_GROUP), axis=0, keepdims=True)
    r = jnp.exp(m2 - m1)
    w1 = g_w / (1.0 + r)
    w2 = g_w * r / (1.0 + r)
    e1 = g_idx * EXPERTS_PER_GROUP + i1
    e2 = g_idx * EXPERTS_PER_GROUP + i2

    xiota = lax.broadcasted_iota(jnp.int32, (N_EXPERTS, n), 0)
    oh1 = xiota == e1
    oh2 = xiota == e2
    cnt = jnp.where(oh1, 1.0, 0.0) + jnp.where(oh2, 1.0, 0.0)
    before = _dot(cnt.astype(MXU_DTYPE), tri_ref[...])
    rank1 = jnp.sum(jnp.where(oh1, before, 0.0), axis=0, keepdims=True)
    rank2 = jnp.sum(jnp.where(oh2, before, 0.0), axis=0, keepdims=True)
    for j in range(n // SEQ_TILE):
        tile_cnt = jnp.sum(cnt[:, j * SEQ_TILE:(j + 1) * SEQ_TILE], axis=1, keepdims=True)
        cnt_ref[j] = jnp.broadcast_to(tile_cnt, (N_EXPERTS, LANES))

    zero = jnp.zeros((1, n), F32)
    rec = jnp.concatenate([e1.astype(F32), e2.astype(F32), w1, w2, rank1, rank2, zero, zero], axis=0)
    rl_ref[...] = rec
    rec_full = jnp.concatenate([rec, jnp.zeros((LANES - ROUTE_ROWS, n), F32)], axis=0)
    rr_ref[...] = rec_full.T


def _memattn(h1, kv, B, S, wq, wo, ln_g, ln_b, wr_t, br_col, tri):
    T = B * S
    n = MEM_TILE
    nt = S // n
    sub = n // SEQ_TILE
    row = lambda b, i: (b * nt + i, 0)
    c2 = lambda b, i: (0, 0)
    one = pl.Buffered(1)
    return pl.pallas_call(
        _memattn_kernel,
        grid=(B, nt),
        in_specs=[
            pl.BlockSpec((n, D_MODEL), row),
            pl.BlockSpec((kv.shape[0] // B, 2 * D_MODEL), lambda b, i: (b, 0)),
            pl.BlockSpec((D_MODEL, D_MODEL), c2, pipeline_mode=one),
            pl.BlockSpec((D_MODEL, D_MODEL), c2, pipeline_mode=one),
            pl.BlockSpec((1, D_MODEL), c2),
            pl.BlockSpec((1, D_MODEL), c2),
            pl.BlockSpec((ROUTER_ROWS, D_MODEL), c2, pipeline_mode=one),
            pl.BlockSpec((ROUTER_ROWS, 1), c2),
            pl.BlockSpec((n, n), c2, pipeline_mode=one),
        ],
        out_specs=[
            pl.BlockSpec((n, D_MODEL), row),
            pl.BlockSpec((n, D_MODEL), row),
            pl.BlockSpec((ROUTE_ROWS, n), lambda b, i: (0, b * nt + i)),
            pl.BlockSpec((n, LANES), row),
            pl.BlockSpec((sub, N_EXPERTS, LANES), lambda b, i: (b * nt + i, 0, 0)),
        ],
        out_shape=[
            jax.ShapeDtypeStruct((T, D_MODEL), F32),
            jax.ShapeDtypeStruct((T, D_MODEL), MXU_DTYPE),
            jax.ShapeDtypeStruct((ROUTE_ROWS, T), F32),
            jax.ShapeDtypeStruct((T, LANES), F32),
            jax.ShapeDtypeStruct((T // SEQ_TILE, N_EXPERTS, LANES), F32),
        ],
        scratch_shapes=[pltpu.VMEM((n, D_MODEL), MXU_DTYPE)],
        compiler_params=_params("parallel", "parallel"),
        name="memattn_router",
    )(h1, kv, wq, wo, ln_g, ln_b, wr_t, br_col, tri)


CHUNK_ROWS = SUBLANES
LOCAL_ROWS = 768
N_CHUNKS = LOCAL_ROWS // CHUNK_ROWS
MIN_CHUNKS = TOP_K * SEQ_TILE // CHUNK_ROWS
assert TOP_K * SEQ_TILE + N_EXPERTS * (CHUNK_ROWS - 1) <= LOCAL_ROWS


def _unpack_rows(packed):
    lo = lax.bitcast_convert_type(packed << 16, F32)
    hi = lax.bitcast_convert_type(packed & jnp.uint32(0xFFFF0000), F32)
    return lo, hi


def _pack_rows(y):
    bits = lax.bitcast_convert_type(y.astype(jnp.bfloat16).astype(F32), jnp.uint32)
    return (bits[:, :PACK_COLS] >> 16) | bits[:, PACK_COLS:]


def _chunk_plan(counts, n_blocks):
    nt = counts.shape[0]
    c = counts.astype(jnp.int32)
    incl = jnp.asarray(np.tril(np.ones((N_EXPERTS, N_EXPERTS), bool)))
    padc = (c + CHUNK_ROWS - 1) // CHUNK_ROWS * CHUNK_ROWS
    lend = jnp.sum(jnp.where(incl[None], padc[:, None, :], 0), axis=2)
    loff = lend - padc
    earlier = jnp.asarray(np.tril(np.ones((nt, nt), bool), -1))
    cbefore = jnp.sum(jnp.where(earlier[:, :, None], padc[None, :, :], 0), axis=1)
    total = jnp.sum(padc, axis=0)
    region = (total + EXPERT_BLOCK - 1) // EXPERT_BLOCK * EXPERT_BLOCK
    gend = jnp.sum(jnp.where(incl, region[None, :], 0), axis=1)
    gstart = gend - region
    delta = gstart[None, :] + cbefore - loff
    k_row = jnp.arange(N_CHUNKS, dtype=jnp.int32) * CHUNK_ROWS
    e_of_chunk = jnp.sum((lend[:, None, :] <= k_row[None, :, None]).astype(jnp.int32), axis=2)
    onehot = e_of_chunk[:, :, None] == jnp.arange(N_EXPERTS, dtype=jnp.int32)[None, None, :]
    chunk_row = (jnp.sum(jnp.where(onehot, delta[:, None, :], 0), axis=2) + k_row[None, :]).reshape(-1)
    n_chunks = lend[:, -1] // CHUNK_ROWS
    block_first = jnp.arange(n_blocks, dtype=jnp.int32) * EXPERT_BLOCK
    block_expert = jnp.minimum(jnp.sum((gend[None, :] <= block_first[:, None]).astype(jnp.int32), axis=1),
                               N_EXPERTS - 1)
    n_used = (gend[-1:] // EXPERT_BLOCK).astype(jnp.int32)
    tail = jnp.concatenate([gstart + total, (region - total) // CHUNK_ROWS, n_used]).astype(jnp.int32)
    loff_f = loff.astype(F32)
    loff_col = jnp.broadcast_to(loff_f[:, :, None], (nt, N_EXPERTS, LANES))
    loff_row = jnp.broadcast_to(jnp.pad(loff_f, ((0, 0), (0, LANES - N_EXPERTS)))[:, None, :],
                                (nt, SUBLANES, LANES))
    return chunk_row, n_chunks, tail, block_expert, n_used, loff_col, loff_row


def _start_chunk_copies(table_ref, count_ref, tile, make_copy):
    base = tile * N_CHUNKS

    def one(k, prio):
        make_copy(k, pl.multiple_of(table_ref[base + k], CHUNK_ROWS)).start(priority=prio)

    def pair(k2, carry):
        for prio in range(2):
            one(2 * k2 + prio, prio)
        return carry

    def single(k, carry):
        one(k, 0)
        return carry

    lax.fori_loop(0, MIN_CHUNKS // 2, pair, 0, unroll=4)
    lax.fori_loop(MIN_CHUNKS, count_ref[tile], single, 0)


def _wait_chunk_copies(count_ref, tile, many, one):
    many.wait()

    def single(k, carry):
        one.wait()
        return carry

    lax.fori_loop(MIN_CHUNKS, count_ref[tile], single, 0)


def _dispatch_kernel(row_ref, cnt_ref, tail_ref, h_ref, rl_ref, loff_ref, xb_ref, sbuf, zbuf, sem, zsem):
    i = pl.program_id(0)
    n = SEQ_TILE
    xiota = lax.broadcasted_iota(jnp.int32, (N_EXPERTS, n), 0)
    riota = lax.broadcasted_iota(jnp.int32, (LOCAL_ROWS, n), 0)
    min_rows = MIN_CHUNKS * CHUNK_ROWS

    def wait_all(slot, tile):
        many = pltpu.make_async_copy(sbuf.at[slot, pl.ds(0, min_rows)], xb_ref.at[pl.ds(0, min_rows)], sem.at[slot])
        one = pltpu.make_async_copy(sbuf.at[slot, pl.ds(0, CHUNK_ROWS)], xb_ref.at[pl.ds(0, CHUNK_ROWS)],
                                    sem.at[slot])
        _wait_chunk_copies(cnt_ref, tile, many, one)

    def sorted_rows(t):
        rl = rl_ref[:, t * n:(t + 1) * n]
        loff = loff_ref[t][:, 0:1]
        slots = []
        for k in range(TOP_K):
            e = rl[k:k + 1].astype(jnp.int32)
            base = jnp.sum(jnp.where(xiota == e, loff, 0.0), axis=0, keepdims=True)
            slots.append((base + rl[4 + k:5 + k]).astype(jnp.int32))
        perm = jnp.where(riota == slots[0], 1.0, jnp.where(riota == slots[1], 1.0, 0.0)).astype(MXU_DTYPE)
        return _pack_rows(_dot(perm, h_ref[t * n:(t + 1) * n, :].astype(MXU_DTYPE)))

    packed = [sorted_rows(t) for t in range(DC_SUB)]
    for t in range(DC_SUB):
        tile = i * DC_SUB + t

        @pl.when(i >= 1)
        def _():
            wait_all(t, tile - DC_SUB)

        sbuf[t] = packed[t]

        def chunk_copy(k, row):
            src = sbuf.at[t, pl.ds(pl.multiple_of(k * CHUNK_ROWS, CHUNK_ROWS), CHUNK_ROWS)]
            return pltpu.make_async_copy(src, xb_ref.at[pl.ds(row, CHUNK_ROWS)], sem.at[t])

        _start_chunk_copies(row_ref, cnt_ref, tile, chunk_copy)

    @pl.when(i == pl.num_programs(0) - 1)
    def _():
        zbuf[...] = jnp.zeros_like(zbuf)

        def fill(start_copy):
            def per_expert(e, carry):
                first = pl.multiple_of(tail_ref[e], CHUNK_ROWS)

                def per_chunk(c, inner):
                    row = pl.multiple_of(first + c * CHUNK_ROWS, CHUNK_ROWS)
                    copy = pltpu.make_async_copy(zbuf, xb_ref.at[pl.ds(row, CHUNK_ROWS)], zsem)
                    if start_copy:
                        copy.start()
                    else:
                        copy.wait()
                    return inner

                lax.fori_loop(0, tail_ref[N_EXPERTS + e], per_chunk, 0)
                return carry

            lax.fori_loop(0, N_EXPERTS, per_expert, 0)

        fill(True)
        for t in range(DC_SUB):
            wait_all(t, i * DC_SUB + t)
        fill(False)

        zblock = sbuf.at[0, pl.ds(0, EXPERT_BLOCK)]
        zblock[...] = jnp.zeros_like(zblock)
        n_all = xb_ref.shape[0] // EXPERT_BLOCK

        def fill_blocks(start_copy):
            def per_block(p, carry):
                row = pl.multiple_of(p * EXPERT_BLOCK, EXPERT_BLOCK)
                copy = pltpu.make_async_copy(zblock, xb_ref.at[pl.ds(row, EXPERT_BLOCK)], zsem)
                if start_copy:
                    copy.start()
                else:
                    copy.wait()
                return carry

            lax.fori_loop(tail_ref[2 * N_EXPERTS], n_all, per_block, 0)

        fill_blocks(True)
        fill_blocks(False)


def _dispatch(chunk_row, n_chunks, tail, hb, rl, loff_col, n_blocks):
    T = hb.shape[0]
    n = DC_SUB * SEQ_TILE
    rows = n_blocks * EXPERT_BLOCK
    grid_spec = pltpu.PrefetchScalarGridSpec(
        num_scalar_prefetch=3,
        grid=(T // n,),
        in_specs=[
            pl.BlockSpec((n, D_MODEL), lambda i, r, c, t: (i, 0)),
            pl.BlockSpec((ROUTE_ROWS, n), lambda i, r, c, t: (0, i)),
            pl.BlockSpec((DC_SUB, N_EXPERTS, LANES), lambda i, r, c, t: (i, 0, 0)),
        ],
        out_specs=pl.BlockSpec(memory_space=pl.ANY),
        scratch_shapes=[pltpu.VMEM((DC_SUB, LOCAL_ROWS, PACK_COLS), jnp.uint32),
                        pltpu.VMEM((CHUNK_ROWS, PACK_COLS), jnp.uint32),
                        pltpu.SemaphoreType.DMA((DC_SUB,)), pltpu.SemaphoreType.DMA],
    )
    assert T % n == 0 and LOCAL_ROWS >= EXPERT_BLOCK
    return pl.pallas_call(
        _dispatch_kernel,
        grid_spec=grid_spec,
        out_shape=jax.ShapeDtypeStruct((rows, PACK_COLS), jnp.uint32),
        compiler_params=_params("arbitrary"),
        name="dispatch",
    )(chunk_row, n_chunks, tail, hb, rl, loff_col)


def _expert_kernel(be_ref, nused_ref, x_ref, wg_ref, wu_ref, wd_ref, y_ref, wg_b, wu_b, wd_b):
    p = pl.program_id(0)
    new_expert = jnp.logical_or(p == 0, be_ref[p] != be_ref[jnp.maximum(p - 1, 0)])

    @pl.when(new_expert)
    def _():
        wg_b[...] = wg_ref[...].astype(MXU_DTYPE)
        wu_b[...] = wu_ref[...].astype(MXU_DTYPE)
        wd_b[...] = wd_ref[...].astype(MXU_DTYPE)

    @pl.when(p < nused_ref[0])
    def _():
        lo, hi = _unpack_rows(x_ref[...])
        x = jnp.concatenate([lo.astype(MXU_DTYPE), hi.astype(MXU_DTYPE)], axis=1)
        y = None
        half = D_EXPERT // 2
        for j in range(2):
            cols = slice(j * half, (j + 1) * half)
            gate = _dot(x, wg_b[:, cols])
            up = _dot(x, wu_b[:, cols])
            hid = (gate * _sigmoid(gate) * up).astype(MXU_DTYPE)
            part = _dot(hid, wd_b[cols, :])
            y = part if y is None else y + part
        y_ref[...] = _pack_rows(y)

    @pl.when(p >= nused_ref[0])
    def _():
        y_ref[...] = jnp.zeros_like(y_ref)


def _experts(block_expert, n_used, xb, wg, wu, wd, layer):
    nb = block_expert.shape[0]
    n_slots = nb * EXPERT_BLOCK
    grid_spec = pltpu.PrefetchScalarGridSpec(
        num_scalar_prefetch=2,
        grid=(nb,),
        in_specs=[
            pl.BlockSpec((EXPERT_BLOCK, PACK_COLS), lambda p, be, nu: (jnp.minimum(p, jnp.maximum(nu[0] - 1, 0)), 0)),
            pl.BlockSpec((None, None, D_MODEL, D_EXPERT), lambda p, be, nu: (layer, be[p], 0, 0)),
            pl.BlockSpec((None, None, D_MODEL, D_EXPERT), lambda p, be, nu: (layer, be[p], 0, 0)),
            pl.BlockSpec((None, None, D_EXPERT, D_MODEL), lambda p, be, nu: (layer, be[p], 0, 0)),
        ],
        out_specs=pl.BlockSpec((EXPERT_BLOCK, PACK_COLS), lambda p, be, nu: (p, 0)),
        scratch_shapes=[pltpu.VMEM((D_MODEL, D_EXPERT), MXU_DTYPE), pltpu.VMEM((D_MODEL, D_EXPERT), MXU_DTYPE),
                        pltpu.VMEM((D_EXPERT, D_MODEL), MXU_DTYPE)],
    )
    return pl.pallas_call(
        _expert_kernel,
        grid_spec=grid_spec,
        out_shape=jax.ShapeDtypeStruct((n_slots, PACK_COLS), jnp.uint32),
        compiler_params=_params("arbitrary"),
        name="experts",
    )(block_expert, n_used, xb, wg, wu, wd)


def _combine_rows(step, is_first, next_step, has_next,
                  row_ref, cnt_ref, yb_ref, rr_ref, loff_ref, h_ref, g_ref, b_ref, ybuf, sem, emit,
                  filler=lambda stage: None):
    n = SEQ_TILE
    min_rows = MIN_CHUNKS * CHUNK_ROWS

    def fetch(tile, into):
        def chunk_copy(k, row):
            dst = ybuf.at[into, pl.ds(pl.multiple_of(k * CHUNK_ROWS, CHUNK_ROWS), CHUNK_ROWS)]
            return pltpu.make_async_copy(yb_ref.at[pl.ds(row, CHUNK_ROWS)], dst, sem.at[into])

        _start_chunk_copies(row_ref, cnt_ref, tile, chunk_copy)

    @pl.when(is_first)
    def _():
        ybuf[...] = jnp.zeros_like(ybuf)
        for t in range(DC_SUB):
            fetch(step * DC_SUB + t, t)

    lane = lax.broadcasted_iota(jnp.int32, (n, LANES), 1)
    ciota = lax.broadcasted_iota(jnp.int32, (n, LOCAL_ROWS), 1)

    def weights(t):
        rr = rr_ref[t * n:(t + 1) * n, :]
        loff = loff_ref[t][0:1, :]
        mix = jnp.zeros((n, LOCAL_ROWS), F32)
        for k in range(TOP_K):
            e = rr[:, k:k + 1].astype(jnp.int32)
            base = jnp.sum(jnp.where(lane == e, loff, 0.0), axis=1, keepdims=True)
            sorted_row = (base + rr[:, 4 + k:5 + k]).astype(jnp.int32)
            mix = jnp.where(ciota == sorted_row, rr[:, 2 + k:3 + k], mix)
        return mix.astype(MXU_DTYPE)

    mixes = [weights(t) for t in range(DC_SUB)]
    filler(0)
    for t in range(DC_SUB):
        many = pltpu.make_async_copy(yb_ref.at[pl.ds(0, min_rows)], ybuf.at[t, pl.ds(0, min_rows)], sem.at[t])
        one = pltpu.make_async_copy(yb_ref.at[pl.ds(0, CHUNK_ROWS)], ybuf.at[t, pl.ds(0, CHUNK_ROWS)], sem.at[t])
        _wait_chunk_copies(cnt_ref, step * DC_SUB + t, many, one)
        filler(1 + t)
        lo, hi = _unpack_rows(ybuf[t])
        y = jnp.concatenate([_dot(mixes[t], lo.astype(MXU_DTYPE)), _dot(mixes[t], hi.astype(MXU_DTYPE))], axis=1)
        rows = slice(t * n, (t + 1) * n)
        emit(t, _layer_norm(DEEPNORM_ALPHA * h_ref[rows, :] + y, g_ref[...], b_ref[...]))

        @pl.when(has_next)
        def _():
            fetch(next_step * DC_SUB + t, t)


def _combine_kernel(row_ref, cnt_ref, yb_ref, rr_ref, loff_ref, h_ref, g_ref, b_ref, o_ref, ybuf, sem):
    i = pl.program_id(0)

    def emit(t, rows):
        o_ref[t * SEQ_TILE:(t + 1) * SEQ_TILE, :] = rows

    _combine_rows(i, i == 0, i + 1, i + 1 < pl.num_programs(0),
                  row_ref, cnt_ref, yb_ref, rr_ref, loff_ref, h_ref, g_ref, b_ref, ybuf, sem, emit)


def _combine_inproj_kernel(row_ref, cnt_ref, yb_ref, rr_ref, loff_ref, h_ref, g_ref, b_ref,
                           w_ref, cos_ref, sin_ref, o_ref, z_ref, ybuf, hprev, sem):
    i = pl.program_id(0)
    n_real = pl.num_programs(0) - 1
    step = jnp.minimum(i, n_real - 1)

    @pl.when(i == 0)
    def _():
        hprev[...] = jnp.zeros_like(hprev)

    cur = i % 2
    xb = hprev.at[1 - cur][...].astype(MXU_DTYPE)

    def emit(t, rows):
        o_ref[t * SEQ_TILE:(t + 1) * SEQ_TILE, :] = rows
        hprev.at[cur][t * SEQ_TILE:(t + 1) * SEQ_TILE, :] = rows

    n_chunks = IN_TOTAL // PROJ_COLS
    bounds = [0, 5, 9, n_chunks]

    def filler(stage):
        _project_columns(xb, w_ref, cos_ref[...], sin_ref[...], z_ref, range(bounds[stage], bounds[stage + 1]))

    _combine_rows(step, i == 0, jnp.minimum(i + 1, n_real - 1), i < n_real,
                  row_ref, cnt_ref, yb_ref, rr_ref, loff_ref, h_ref, g_ref, b_ref, ybuf, sem, emit, filler)


def _combine(chunk_row, n_chunks, yb, rr, loff_row, h2, ln_g, ln_b):
    T = h2.shape[0]
    n = DC_SUB * SEQ_TILE
    row = lambda i, r, c: (i, 0)
    grid_spec = pltpu.PrefetchScalarGridSpec(
        num_scalar_prefetch=2,
        grid=(T // n,),
        in_specs=[
            pl.BlockSpec(memory_space=pl.ANY),
            pl.BlockSpec((n, LANES), row),
            pl.BlockSpec((DC_SUB, SUBLANES, LANES), lambda i, r, c: (i, 0, 0)),
            pl.BlockSpec((n, D_MODEL), row),
            pl.BlockSpec((1, D_MODEL), lambda i, r, c: (0, 0)),
            pl.BlockSpec((1, D_MODEL), lambda i, r, c: (0, 0)),
        ],
        out_specs=pl.BlockSpec((n, D_MODEL), row),
        scratch_shapes=[pltpu.VMEM((DC_SUB, LOCAL_ROWS, PACK_COLS), jnp.uint32),
                        pltpu.SemaphoreType.DMA((DC_SUB,))],
    )
    return pl.pallas_call(
        _combine_kernel,
        grid_spec=grid_spec,
        out_shape=jax.ShapeDtypeStruct((T, D_MODEL), F32),
        compiler_params=_params("arbitrary"),
        name="combine",
    )(chunk_row, n_chunks, yb, rr, loff_row, h2, ln_g, ln_b)


def _combine_inproj(chunk_row, n_chunks, yb, rr, loff_row, h2, ln_g, ln_b, w_in_b, cos_t, sin_t):
    T = h2.shape[0]
    n = DC_SUB * SEQ_TILE
    assert n == PROJ_TILE
    steps = T // n
    cur = lambda i, r, c: (jnp.minimum(i, steps - 1), 0)
    prev = lambda i, r, c: (jnp.maximum(i - 1, 0), 0)
    const = lambda i, r, c: (0, 0)
    grid_spec = pltpu.PrefetchScalarGridSpec(
        num_scalar_prefetch=2,
        grid=(steps + 1,),
        in_specs=[
            pl.BlockSpec(memory_space=pl.ANY),
            pl.BlockSpec((n, LANES), cur),
            pl.BlockSpec((DC_SUB, SUBLANES, LANES), lambda i, r, c: (jnp.minimum(i, steps - 1), 0, 0)),
            pl.BlockSpec((n, D_MODEL), cur),
            pl.BlockSpec((1, D_MODEL), const),
            pl.BlockSpec((1, D_MODEL), const),
            pl.BlockSpec((D_MODEL, IN_TOTAL), const, pipeline_mode=pl.Buffered(1)),
            pl.BlockSpec((n, LANES), prev),
            pl.BlockSpec((n, LANES), prev),
        ],
        out_specs=[pl.BlockSpec((n, D_MODEL), cur), pl.BlockSpec((n, IN_TOTAL), prev)],
        scratch_shapes=[pltpu.VMEM((DC_SUB, LOCAL_ROWS, PACK_COLS), jnp.uint32),
                        pltpu.VMEM((2, n, D_MODEL), F32),
                        pltpu.SemaphoreType.DMA((DC_SUB,))],
    )
    return pl.pallas_call(
        _combine_inproj_kernel,
        grid_spec=grid_spec,
        out_shape=[jax.ShapeDtypeStruct((T, D_MODEL), F32), jax.ShapeDtypeStruct((T, IN_TOTAL), MXU_DTYPE)],
        compiler_params=_params("arbitrary"),
        name="combine_inproj",
    )(chunk_row, n_chunks, yb, rr, loff_row, h2, ln_g, ln_b, w_in_b, cos_t, sin_t)


def _router_weights(w_group, b_group, w_route, b_route):
    wr = jnp.zeros((ROUTER_ROWS, D_MODEL), F32)
    wr = wr.at[GROUP_ROW0:GROUP_ROW0 + N_GROUPS].set(w_group.T)
    wr = wr.at[EXPERT_ROW0:EXPERT_ROW0 + N_EXPERTS].set(w_route.T)
    br = jnp.zeros((ROUTER_ROWS,), F32)
    br = br.at[GROUP_ROW0:GROUP_ROW0 + N_GROUPS].set(b_group)
    br = br.at[EXPERT_ROW0:EXPERT_ROW0 + N_EXPERTS].set(b_route.reshape(-1))
    return wr.astype(MXU_DTYPE), br.reshape(ROUTER_ROWS, 1)


def kernel(x, mem, positions, ln_in_g, ln_in_b, w_in, rel_bias, w_proj_ret, w_proj_att, w_out, ln1_g, ln1_b, w_q_mem, w_kv_mem, w_o_mem, ln2_g, ln2_b, w_group, b_group, w_route, b_route, w_gate, w_up, w_down, ln3_g, ln3_b):
    B, S, D = x.shape
    assert D == D_MODEL and S % PROJ_TILE == 0 and S % SEQ_TILE == 0 and S % MEM_TILE == 0
    T = B * S
    A = T * TOP_K
    n_blocks = -(-(A + (T // SEQ_TILE) * N_EXPERTS * (CHUNK_ROWS - 1) + N_EXPERTS * (EXPERT_BLOCK - 1)) // EXPERT_BLOCK)
    bf = MXU_DTYPE

    cos_t, sin_t = _rope_tables(positions)
    tables = _retention_tables()
    pos = np.arange(MEM_TILE)
    tri = jnp.asarray((pos[:, None] < pos[None, :]) & (pos[:, None] // SEQ_TILE == pos[None, :] // SEQ_TILE), bf)
    mem2d = mem.reshape(-1, D)
    row2 = lambda v: v.reshape(1, D)

    z, h = _inproj(x.reshape(T, D), row2(ln_in_g), row2(ln_in_b), _to_mxu(w_in, 0), cos_t, sin_t, True)
    for l in range(DEPTH):
        bias = _attention_bias(rel_bias[l])
        h = _mixer(z, h, B, S, tables, bias, _to_mxu(w_proj_ret, l), _to_mxu(w_proj_att, l),
                   _to_mxu(w_out, l), row2(ln1_g[l]), row2(ln1_b[l]))

        kv = _mem_kv(mem2d, _to_mxu(w_kv_mem, l))
        wr_t, br_col = _router_weights(w_group[l], b_group[l], w_route[l], b_route[l])
        h2, hb, rl, rr, counts = _memattn(h, kv, B, S, _to_mxu(w_q_mem, l), _to_mxu(w_o_mem, l),
                                          row2(ln2_g[l]), row2(ln2_b[l]), wr_t, br_col, tri)

        chunk_row, n_chunks, tail, block_expert, n_used, loff_col, loff_row = _chunk_plan(counts[:, :, 0], n_blocks)
        xb = _dispatch(chunk_row, n_chunks, tail, hb, rl, loff_col, n_blocks)
        yb = _experts(block_expert, n_used, xb, w_gate, w_up, w_down, l)
        if l + 1 < DEPTH:
            h, z = _combine_inproj(chunk_row, n_chunks, yb, rr, loff_row, h2, row2(ln3_g[l]), row2(ln3_b[l]),
                                   _to_mxu(w_in, l + 1), cos_t, sin_t)
        else:
            h = _combine(chunk_row, n_chunks, yb, rr, loff_row, h2, row2(ln3_g[l]), row2(ln3_b[l]))
    return h.reshape(B, S, D)
```

```python
import functools

import jax
import jax.numpy as jnp
import numpy as np
from jax import lax
from jax.experimental import pallas as pl
from jax.experimental.pallas import tpu as pltpu

D_MODEL = 1024
DEPTH = 2
CHUNK = 64
RET_HEADS = 8
RET_DK = 64
RET_DV = 128
RET_QK = RET_HEADS * RET_DK
RET_V = RET_HEADS * RET_DV
ROPE_BASE = 10000.0
ATT_HEADS = 8
ATT_DH = 64
ATT_W = ATT_HEADS * ATT_DH
LEFT_CHUNKS = 8
MAX_REL = 256
MEM_HEADS = 4
MEM_DH = D_MODEL // MEM_HEADS
N_GROUPS = 4
EXPERTS_PER_GROUP = 8
N_EXPERTS = N_GROUPS * EXPERTS_PER_GROUP
TOP_K = 2
D_EXPERT = 512
LN_EPS = 1e-5
DEEPNORM_ALPHA = (2.0 * DEPTH) ** 0.25
IN_TOTAL = 2 * RET_QK + 2 * RET_V + 3 * ATT_W + 2 * D_MODEL

LANES = 128
SUBLANES = 8
VMEM_LIMIT_BYTES = 56 * 1024 * 1024

SEQ_TILE = 256
MEM_TILE = 2 * SEQ_TILE
MIX_SUB = 2
DC_SUB = 2
PROJ_TILE = 512
PROJ_COLS = 512
EXPERT_BLOCK = 512
HEAD_PAIR = 2 * RET_DK
NEG_BIG = -1e30
LOG2E = 1.4426950408889634

F32 = jnp.float32
MXU_DTYPE = jnp.bfloat16
PACK_COLS = D_MODEL // 2

Z_QK = 0
Z_VR = Z_QK + 2 * RET_QK
Z_GR = Z_VR + RET_V
Z_QA = Z_GR + RET_V
Z_KA = Z_QA + ATT_W
Z_VA = Z_KA + ATT_W
Z_GATE_R = Z_VA + ATT_W
Z_GATE_A = Z_GATE_R + D_MODEL
GATE_HALF = D_MODEL // 2


def _layer_norm(x, g, b):
    mu = jnp.mean(x, axis=-1, keepdims=True)
    xc = x - mu
    var = jnp.mean(xc * xc, axis=-1, keepdims=True)
    return xc * lax.rsqrt(var + LN_EPS) * g + b


def _sigmoid(x):
    return 1.0 / (1.0 + jnp.exp(-x))


def _dot(a, b):
    return jnp.dot(a, b, preferred_element_type=F32)


def _dot_nt(a, b):
    return lax.dot_general(a, b, (((1,), (1,)), ((), ())), preferred_element_type=F32)


def _dot_tn(a, b):
    return lax.dot_general(a, b, (((0,), (0,)), ((), ())), preferred_element_type=F32)


def _params(*semantics):
    return pltpu.CompilerParams(dimension_semantics=semantics, vmem_limit_bytes=VMEM_LIMIT_BYTES)


def _cast_kernel(x_ref, o_ref):
    o_ref[...] = x_ref[...].astype(o_ref.dtype)


def _to_mxu(w_stack, layer):
    _, r, c = w_stack.shape
    tr = 256
    return pl.pallas_call(
        _cast_kernel,
        grid=(r // tr,),
        in_specs=[pl.BlockSpec((None, tr, c), lambda i: (layer, i, 0))],
        out_specs=pl.BlockSpec((tr, c), lambda i: (i, 0)),
        out_shape=jax.ShapeDtypeStruct((r, c), MXU_DTYPE),
        compiler_params=_params("parallel"),
        name="cast_weight",
    )(w_stack)


def _rope_kernel(pos_ref, invf_ref, cos_ref, sin_ref):
    ang = pos_ref[...] * invf_ref[...]
    lane = lax.broadcasted_iota(jnp.int32, ang.shape, 1)
    s = jnp.sin(ang)
    cos_ref[...] = jnp.cos(ang)
    sin_ref[...] = jnp.where((lane % RET_DK) < RET_DK // 2, -s, s)


def _rope_tables(positions):
    T = positions.size
    half = RET_DK // 2
    inv_freq = 1.0 / (ROPE_BASE ** jnp.linspace(0.0, 1.0, half, dtype=F32))
    invf = jnp.tile(inv_freq, LANES // half).reshape(1, LANES)
    pos = jnp.broadcast_to(positions.reshape(T, 1).astype(F32), (T, LANES))
    tm = 1024
    spec = pl.BlockSpec((tm, LANES), lambda i: (i, 0))
    return pl.pallas_call(
        _rope_kernel,
        grid=(T // tm,),
        in_specs=[spec, pl.BlockSpec((1, LANES), lambda i: (0, 0))],
        out_specs=[spec, spec],
        out_shape=[jax.ShapeDtypeStruct((T, LANES), F32)] * 2,
        compiler_params=_params("parallel"),
        name="rope_tables",
    )(pos, invf)


def _rotary(a, cos, sin_signed):
    outs = []
    lane = lax.broadcasted_iota(jnp.int32, cos.shape, 1)
    first_half = (lane % RET_DK) < RET_DK // 2
    for g in range(a.shape[1] // LANES):
        x = a[:, g * LANES:(g + 1) * LANES]
        rot = jnp.where(first_half, pltpu.roll(x, LANES - RET_DK // 2, 1), pltpu.roll(x, RET_DK // 2, 1))
        outs.append(x * cos + rot * sin_signed)
    return jnp.concatenate(outs, axis=1)


def _project_columns(xb, w_ref, cos, sin, z_ref, chunks):
    for c in chunks:
        lo = c * PROJ_COLS
        acc = _dot(xb, w_ref[:, lo:lo + PROJ_COLS])
        if lo < 2 * RET_QK:
            acc = _rotary(acc, cos, sin)
            if lo >= RET_QK:
                acc = acc * (RET_DK ** -0.5)
        z_ref[:, lo:lo + PROJ_COLS] = acc.astype(z_ref.dtype)


def _inproj_kernel(apply_ln, h_ref, g_ref, b_ref, w_ref, cos_ref, sin_ref, z_ref, *h_out):
    x = h_ref[...]
    if apply_ln:
        x = _layer_norm(x, g_ref[...], b_ref[...])
        h_out[0][...] = x
    _project_columns(x.astype(MXU_DTYPE), w_ref, cos_ref[...], sin_ref[...], z_ref, range(IN_TOTAL // PROJ_COLS))


def _inproj(h, ln_g, ln_b, w_in_b, cos_t, sin_t, apply_ln):
    T = h.shape[0]
    tm = PROJ_TILE
    row = lambda i: (i, 0)
    const = lambda i: (0, 0)
    out_shape = [jax.ShapeDtypeStruct((T, IN_TOTAL), MXU_DTYPE)]
    out_specs = [pl.BlockSpec((tm, IN_TOTAL), row)]
    if apply_ln:
        out_shape.append(jax.ShapeDtypeStruct((T, D_MODEL), F32))
        out_specs.append(pl.BlockSpec((tm, D_MODEL), row))
    res = pl.pallas_call(
        functools.partial(_inproj_kernel, apply_ln),
        grid=(T // tm,),
        in_specs=[
            pl.BlockSpec((tm, D_MODEL), row),
            pl.BlockSpec((1, D_MODEL), const),
            pl.BlockSpec((1, D_MODEL), const),
            pl.BlockSpec((D_MODEL, IN_TOTAL), const, pipeline_mode=pl.Buffered(1)),
            pl.BlockSpec((tm, LANES), row),
            pl.BlockSpec((tm, LANES), row),
        ],
        out_specs=out_specs,
        out_shape=out_shape,
        compiler_params=_params("parallel"),
        name="inproj",
    )(h, ln_g, ln_b, w_in_b, cos_t, sin_t)
    return res


def _retention_tables():
    n = SEQ_TILE
    log_g = np.log1p(-np.exp2(-5.0 - np.arange(RET_HEADS, dtype=np.float64)))
    i = np.arange(n)
    same = (i[:, None] // CHUNK) == (i[None, :] // CHUNK)
    earlier = (i[None, :] // CHUNK) < (i[:, None] // CHUNK)
    diff = i[:, None] - i[None, :]
    expo = np.where(same, np.abs(diff), np.where(earlier, diff, 0)).astype(np.float64)
    decay = np.exp(log_g[:, None, None] * expo) * (same | earlier)
    q_decay = np.exp(log_g[:, None] * (i + 1.0))
    k_decay = np.exp(log_g[:, None] * (n - 1.0 - i))
    tile_decay = np.exp(log_g * n)
    wide = lambda t: np.broadcast_to(t[:, :, None], t.shape + (HEAD_PAIR,))
    owner = (np.arange(HEAD_PAIR)[None, :] // RET_DK) == (np.arange(RET_HEADS)[:, None] % 2)
    k_table = wide(k_decay) * owner[:, None, :]
    return (jnp.asarray(decay, F32), jnp.asarray(wide(q_decay), F32), jnp.asarray(k_table, F32),
            tuple(float(v) for v in tile_decay))


def _attention_bias(rel_bias):
    n = SEQ_TILE
    width = 3 * n
    pad = LEFT_CHUNKS * CHUNK
    assert pad == 2 * n and n <= MAX_REL
    heads = rel_bias.shape[0]
    length = width + n
    near = rel_bias[:, MAX_REL - n:].astype(F32)
    far = jnp.broadcast_to(rel_bias[:, -1:].astype(F32), (heads, length - near.shape[1]))
    g_rev = jnp.concatenate([near, far], axis=1)[:, ::-1]
    rows = jnp.tile(g_rev, (1, n + 1))[:, :n * (length - 1)].reshape(heads, n, length - 1)
    bias = rows[:, :, n - 1:n - 1 + width]
    qi = np.arange(n)
    kj = np.arange(width)
    q_chunk = (pad + qi) // CHUNK
    k_chunk = kj // CHUNK
    in_band = (k_chunk[None, :] <= q_chunk[:, None]) & (k_chunk[None, :] >= q_chunk[:, None] - LEFT_CHUNKS)
    return jnp.where(jnp.asarray(in_band)[None], bias * LOG2E, NEG_BIG)


def _mixer_kernel(tile_decay,
                  qk_ref, vr_ref, gr_ref, gate_r0_ref, gate_r1_ref, gate_a0_ref, gate_a1_ref, qa_ref,
                  kp_ref, kc_ref, vp_ref, vc_ref,
                  h_ref, decay_ref, qdec_ref, kdec_ref, bias_ref,
                  wpr_ref, wpa_ref, wout_ref, g_ref, b_ref,
                  o_ref, state_ref, yr_ref, ya_ref):
    i = pl.program_id(1)

    @pl.when(i == 0)
    def _():
        state_ref[...] = jnp.zeros_like(state_ref)

    n = SEQ_TILE
    lane = lax.broadcasted_iota(jnp.int32, (n, HEAD_PAIR), 1)
    low_half = lane < RET_DK

    def head_lanes(odd):
        return low_half if odd == 0 else jnp.logical_not(low_half)

    def rows(t):
        return slice(t * n, (t + 1) * n)

    def retention_state_update(t, hd):
        p = hd // 2
        k2 = qk_ref[rows(t), RET_QK + p * HEAD_PAIR:RET_QK + (p + 1) * HEAD_PAIR]
        v = vr_ref[rows(t), hd * RET_DV:(hd + 1) * RET_DV]
        k_dec = (k2.astype(F32) * kdec_ref[hd]).astype(MXU_DTYPE)
        state_ref[hd] = state_ref[hd] * tile_decay[hd] + _dot_tn(k_dec, v)

    def retention_first(t, hd):
        p, odd = divmod(hd, 2)
        q2 = qk_ref[rows(t), p * HEAD_PAIR:(p + 1) * HEAD_PAIR]
        k2 = qk_ref[rows(t), RET_QK + p * HEAD_PAIR:RET_QK + (p + 1) * HEAD_PAIR]
        qm = jnp.where(head_lanes(odd), q2, jnp.zeros_like(q2))
        v = vr_ref[rows(t), hd * RET_DV:(hd + 1) * RET_DV]
        raw = _dot_nt(qm, k2)
        read = _dot(qm, state_ref[hd].astype(MXU_DTYPE))
        if t + 1 < MIX_SUB:
            retention_state_update(t, hd)
        return raw, read, v

    def retention_second(t, hd, raw, read, v):
        intra = _dot((raw * decay_ref[hd]).astype(MXU_DTYPE), v)
        ret = intra + qdec_ref[hd] * read
        mu = jnp.mean(ret, axis=-1, keepdims=True)
        rc = ret - mu
        var = jnp.mean(rc * rc, axis=-1, keepdims=True)
        gn = rc * lax.rsqrt(var + LN_EPS)
        gate = gr_ref[rows(t), hd * RET_DV:(hd + 1) * RET_DV].astype(F32)
        yr_ref[rows(t), hd * RET_DV:(hd + 1) * RET_DV] = (gate * _sigmoid(gate) * gn).astype(MXU_DTYPE)

    pen_prev = jnp.where(i >= 1, 0.0, NEG_BIG).astype(F32)

    def key_block(t, kb):
        j = t + kb + (MIX_SUB - 2)
        if j < MIX_SUB:
            return kp_ref, vp_ref, rows(j), pen_prev
        return kc_ref, vc_ref, rows(j - MIX_SUB), None

    def attention_first(t, hd):
        p, odd = divmod(hd, 2)
        cols = slice(p * HEAD_PAIR, (p + 1) * HEAD_PAIR)
        qs = (qa_ref[rows(t), cols].astype(F32) * (ATT_DH ** -0.5 * LOG2E)).astype(MXU_DTYPE)
        qm = jnp.where(head_lanes(odd), qs, jnp.zeros_like(qs))
        s = []
        for kb in range(3):
            k_ref, _, krows, pen = key_block(t, kb)
            sk = _dot_nt(qm, k_ref[krows, cols]) + bias_ref[hd, :, kb * n:(kb + 1) * n]
            if pen is not None:
                sk = sk + pen
            s.append(sk)
        return (s,)

    def attention_second(t, hd, s):
        p, odd = divmod(hd, 2)
        cols = slice(p * HEAD_PAIR, (p + 1) * HEAD_PAIR)
        m = jnp.maximum(jnp.maximum(jnp.max(s[0], axis=-1, keepdims=True),
                                    jnp.max(s[1], axis=-1, keepdims=True)),
                        jnp.max(s[2], axis=-1, keepdims=True))
        e = [jnp.exp2(sk - m) for sk in s]
        denom = (jnp.sum(e[0], axis=-1, keepdims=True) + jnp.sum(e[1], axis=-1, keepdims=True)
                 + jnp.sum(e[2], axis=-1, keepdims=True))
        pv = None
        for kb in range(3):
            _, v_ref, krows, _ = key_block(t, kb)
            term = _dot(e[kb].astype(MXU_DTYPE), v_ref[krows, cols])
            pv = term if pv is None else pv + term
        out = (pv / denom).astype(MXU_DTYPE)
        lo = odd * ATT_DH
        ya_ref[rows(t), p * HEAD_PAIR + lo:p * HEAD_PAIR + lo + ATT_DH] = out[:, lo:lo + ATT_DH]

    work = []
    for t in range(MIX_SUB):
        for hd in range(max(RET_HEADS, ATT_HEADS)):
            if hd < RET_HEADS:
                work.append((retention_first, retention_second, t, hd))
            if hd < ATT_HEADS:
                work.append((attention_first, attention_second, t, hd))
    ahead = 2
    pending = [first(t, hd) for first, _, t, hd in work[:ahead]]
    for j, (_, second, t, hd) in enumerate(work):
        if j + ahead < len(work):
            first, _, t_next, hd_next = work[j + ahead]
            pending.append(first(t_next, hd_next))
        second(t, hd, *pending.pop(0))

    pr = _dot(yr_ref[...], wpr_ref[...])
    pa = _dot(ya_ref[...], wpa_ref[...])
    gate_r = jnp.concatenate([gate_r0_ref[...], gate_r1_ref[...]], axis=1).astype(F32)
    gate_a = jnp.concatenate([gate_a0_ref[...], gate_a1_ref[...]], axis=1).astype(F32)
    merged = _sigmoid(gate_r) * pr + _sigmoid(gate_a) * pa
    mix = _dot(merged.astype(MXU_DTYPE), wout_ref[...])
    for hd in range(RET_HEADS):
        retention_state_update(MIX_SUB - 1, hd)
    o_ref[...] = _layer_norm(DEEPNORM_ALPHA * h_ref[...] + mix, g_ref[...], b_ref[...])


def _mixer(z, h, B, S, tables, bias, wpr, wpa, wout, ln_g, ln_b):
    decay, q_decay, k_decay, tile_decay = tables
    T = B * S
    n = SEQ_TILE
    step_rows = MIX_SUB * n
    nt = S // step_rows
    assert MIX_SUB >= 2 and S % step_rows == 0

    def zspec(width, col_off, back=0):
        cb = col_off // width
        return pl.BlockSpec((step_rows, width), lambda b, i: (b * nt + jnp.maximum(i - back, 0), cb))

    row = lambda b, i: (b * nt + i, 0)
    c2 = lambda b, i: (0, 0)
    c3 = lambda b, i: (0, 0, 0)
    one = pl.Buffered(1)
    in_specs = [
        zspec(2 * RET_QK, Z_QK), zspec(RET_V, Z_VR), zspec(RET_V, Z_GR),
        zspec(GATE_HALF, Z_GATE_R), zspec(GATE_HALF, Z_GATE_R + GATE_HALF),
        zspec(GATE_HALF, Z_GATE_A), zspec(GATE_HALF, Z_GATE_A + GATE_HALF), zspec(ATT_W, Z_QA),
        zspec(ATT_W, Z_KA, 1), zspec(ATT_W, Z_KA, 0),
        zspec(ATT_W, Z_VA, 1), zspec(ATT_W, Z_VA, 0),
        pl.BlockSpec((step_rows, D_MODEL), row),
        pl.BlockSpec((RET_HEADS, n, n), c3, pipeline_mode=one),
        pl.BlockSpec((RET_HEADS, n, HEAD_PAIR), c3, pipeline_mode=one),
        pl.BlockSpec((RET_HEADS, n, HEAD_PAIR), c3, pipeline_mode=one),
        pl.BlockSpec((ATT_HEADS, n, 3 * n), c3, pipeline_mode=one),
        pl.BlockSpec((RET_V, D_MODEL), c2, pipeline_mode=one),
        pl.BlockSpec((ATT_W, D_MODEL), c2, pipeline_mode=one),
        pl.BlockSpec((D_MODEL, D_MODEL), c2, pipeline_mode=one),
        pl.BlockSpec((1, D_MODEL), c2),
        pl.BlockSpec((1, D_MODEL), c2),
    ]
    return pl.pallas_call(
        functools.partial(_mixer_kernel, tile_decay),
        grid=(B, nt),
        in_specs=in_specs,
        out_specs=pl.BlockSpec((step_rows, D_MODEL), row),
        out_shape=jax.ShapeDtypeStruct((T, D_MODEL), F32),
        scratch_shapes=[
            pltpu.VMEM((RET_HEADS, HEAD_PAIR, RET_DV), F32),
            pltpu.VMEM((step_rows, RET_V), MXU_DTYPE),
            pltpu.VMEM((step_rows, ATT_W), MXU_DTYPE),
        ],
        compiler_params=_params("arbitrary", "arbitrary"),
        name="mixer",
    )(*([z] * 12), h, decay, q_decay, k_decay, bias, wpr, wpa, wout, ln_g, ln_b)


def _kv_kernel(m_ref, w_ref, o_ref):
    o_ref[...] = _dot(m_ref[...].astype(MXU_DTYPE), w_ref[...]).astype(o_ref.dtype)


def _mem_kv(mem2d, w_kv):
    M = mem2d.shape[0]
    tm = 256
    return pl.pallas_call(
        _kv_kernel,
        grid=(M // tm,),
        in_specs=[pl.BlockSpec((tm, D_MODEL), lambda i: (i, 0)),
                  pl.BlockSpec((D_MODEL, 2 * D_MODEL), lambda i: (0, 0), pipeline_mode=pl.Buffered(1))],
        out_specs=pl.BlockSpec((tm, 2 * D_MODEL), lambda i: (i, 0)),
        out_shape=jax.ShapeDtypeStruct((M, 2 * D_MODEL), MXU_DTYPE),
        compiler_params=_params("parallel"),
        name="mem_kv",
    )(mem2d, w_kv)


ROUTE_ROWS = 8
GROUP_ROW0 = 0
EXPERT_ROW0 = 8
ROUTER_ROWS = 128


def _memattn_kernel(h_ref, kv_ref, wq_ref, wo_ref, g_ref, b_ref, wr_ref, br_ref, tri_ref,
                    h2_ref, hb_ref, rl_ref, rr_ref, cnt_ref, o_scr):
    n = MEM_TILE
    h1 = h_ref[...]
    q = (_dot(h1.astype(MXU_DTYPE), wq_ref[...]) * (MEM_DH ** -0.5 * LOG2E)).astype(MXU_DTYPE)
    scores = [_dot_nt(q[:, hd * MEM_DH:(hd + 1) * MEM_DH], kv_ref[:, hd * MEM_DH:(hd + 1) * MEM_DH])
              for hd in range(MEM_HEADS)]
    for hd in range(MEM_HEADS):
        cols = slice(hd * MEM_DH, (hd + 1) * MEM_DH)
        s = scores[hd]
        m = jnp.max(s, axis=-1, keepdims=True)
        e = jnp.exp2(s - m)
        denom = jnp.sum(e, axis=-1, keepdims=True)
        v = kv_ref[:, D_MODEL + hd * MEM_DH:D_MODEL + (hd + 1) * MEM_DH]
        o_scr[:, cols] = (_dot(e.astype(MXU_DTYPE), v) / denom).astype(MXU_DTYPE)
    cross = _dot(o_scr[...], wo_ref[...])
    h2 = _layer_norm(DEEPNORM_ALPHA * h1 + cross, g_ref[...], b_ref[...])
    h2_ref[...] = h2
    hb = h2.astype(MXU_DTYPE)
    hb_ref[...] = hb

    logits = _dot_nt(wr_ref[...], hb) + br_ref[...]
    glog = logits[GROUP_ROW0:GROUP_ROW0 + N_GROUPS]
    gmax = jnp.max(glog, axis=0, keepdims=True)
    g_w = 1.0 / jnp.sum(jnp.exp(glog - gmax), axis=0, keepdims=True)
    giota = lax.broadcasted_iota(jnp.int32, glog.shape, 0)
    g_idx = jnp.min(jnp.where(glog == gmax, giota, N_GROUPS), axis=0, keepdims=True)
    el = jnp.zeros((EXPERTS_PER_GROUP, n), F32)
    for g in range(N_GROUPS):
        r0 = EXPERT_ROW0 + g * EXPERTS_PER_GROUP
        el = jnp.where(g_idx == g, logits[r0:r0 + EXPERTS_PER_GROUP], el)
    eiota = lax.broadcasted_iota(jnp.int32, el.shape, 0)
    m1 = jnp.max(el, axis=0, keepdims=True)
    i1 = jnp.min(jnp.where(el == m1, eiota, EXPERTS_PER_GROUP), axis=0, keepdims=True)
    el2 = jnp.where(eiota == i1, -jnp.inf, el)
    m2 = jnp.max(el2, axis=0, keepdims=True)
    i2 = jnp.min(jnp.where(el2 == m2, eiota, EXPERTS_PER_GROUP), axis=0, keepdims=True)
    r = jnp.exp(m2 - m1)
    w1 = g_w / (1.0 + r)
    w2 = g_w * r / (1.0 + r)
    e1 = g_idx * EXPERTS_PER_GROUP + i1
    e2 = g_idx * EXPERTS_PER_GROUP + i2

    xiota = lax.broadcasted_iota(jnp.int32, (N_EXPERTS, n), 0)
    oh1 = xiota == e1
    oh2 = xiota == e2
    cnt = jnp.where(oh1, 1.0, 0.0) + jnp.where(oh2, 1.0, 0.0)
    before = _dot(cnt.astype(MXU_DTYPE), tri_ref[...])
    rank1 = jnp.sum(jnp.where(oh1, before, 0.0), axis=0, keepdims=True)
    rank2 = jnp.sum(jnp.where(oh2, before, 0.0), axis=0, keepdims=True)
    for j in range(n // SEQ_TILE):
        tile_cnt = jnp.sum(cnt[:, j * SEQ_TILE:(j + 1) * SEQ_TILE], axis=1, keepdims=True)
        cnt_ref[j] = jnp.broadcast_to(tile_cnt, (N_EXPERTS, LANES))

    zero = jnp.zeros((1, n), F32)
    rec = jnp.concatenate([e1.astype(F32), e2.astype(F32), w1, w2, rank1, rank2, zero, zero], axis=0)
    rl_ref[...] = rec
    rec_full = jnp.concatenate([rec, jnp.zeros((LANES - ROUTE_ROWS, n), F32)], axis=0)
    rr_ref[...] = rec_full.T


def _memattn(h1, kv, B, S, wq, wo, ln_g, ln_b, wr_t, br_col, tri):
    T = B * S
    n = MEM_TILE
    nt = S // n
    sub = n // SEQ_TILE
    row = lambda b, i: (b * nt + i, 0)
    c2 = lambda b, i: (0, 0)
    one = pl.Buffered(1)
    return pl.pallas_call(
        _memattn_kernel,
        grid=(B, nt),
        in_specs=[
            pl.BlockSpec((n, D_MODEL), row),
            pl.BlockSpec((kv.shape[0] // B, 2 * D_MODEL), lambda b, i: (b, 0)),
            pl.BlockSpec((D_MODEL, D_MODEL), c2, pipeline_mode=one),
            pl.BlockSpec((D_MODEL, D_MODEL), c2, pipeline_mode=one),
            pl.BlockSpec((1, D_MODEL), c2),
            pl.BlockSpec((1, D_MODEL), c2),
            pl.BlockSpec((ROUTER_ROWS, D_MODEL), c2, pipeline_mode=one),
            pl.BlockSpec((ROUTER_ROWS, 1), c2),
            pl.BlockSpec((n, n), c2, pipeline_mode=one),
        ],
        out_specs=[
            pl.BlockSpec((n, D_MODEL), row),
            pl.BlockSpec((n, D_MODEL), row),
            pl.BlockSpec((ROUTE_ROWS, n), lambda b, i: (0, b * nt + i)),
            pl.BlockSpec((n, LANES), row),
            pl.BlockSpec((sub, N_EXPERTS, LANES), lambda b, i: (b * nt + i, 0, 0)),
        ],
        out_shape=[
            jax.ShapeDtypeStruct((T, D_MODEL), F32),
            jax.ShapeDtypeStruct((T, D_MODEL), MXU_DTYPE),
            jax.ShapeDtypeStruct((ROUTE_ROWS, T), F32),
            jax.ShapeDtypeStruct((T, LANES), F32),
            jax.ShapeDtypeStruct((T // SEQ_TILE, N_EXPERTS, LANES), F32),
        ],
        scratch_shapes=[pltpu.VMEM((n, D_MODEL), MXU_DTYPE)],
        compiler_params=_params("parallel", "parallel"),
        name="memattn_router",
    )(h1, kv, wq, wo, ln_g, ln_b, wr_t, br_col, tri)


CHUNK_ROWS = SUBLANES
LOCAL_ROWS = 768
N_CHUNKS = LOCAL_ROWS // CHUNK_ROWS
MIN_CHUNKS = TOP_K * SEQ_TILE // CHUNK_ROWS
assert TOP_K * SEQ_TILE + N_EXPERTS * (CHUNK_ROWS - 1) <= LOCAL_ROWS


def _unpack_rows(packed):
    lo = lax.bitcast_convert_type(packed << 16, F32)
    hi = lax.bitcast_convert_type(packed & jnp.uint32(0xFFFF0000), F32)
    return lo, hi


def _pack_rows(y):
    bits = lax.bitcast_convert_type(y.astype(jnp.bfloat16).astype(F32), jnp.uint32)
    return (bits[:, :PACK_COLS] >> 16) | bits[:, PACK_COLS:]


def _chunk_plan(counts, n_blocks):
    nt = counts.shape[0]
    c = counts.astype(jnp.int32)
    incl = jnp.asarray(np.tril(np.ones((N_EXPERTS, N_EXPERTS), bool)))
    padc = (c + CHUNK_ROWS - 1) // CHUNK_ROWS * CHUNK_ROWS
    lend = jnp.sum(jnp.where(incl[None], padc[:, None, :], 0), axis=2)
    loff = lend - padc
    earlier = jnp.asarray(np.tril(np.ones((nt, nt), bool), -1))
    cbefore = jnp.sum(jnp.where(earlier[:, :, None], padc[None, :, :], 0), axis=1)
    total = jnp.sum(padc, axis=0)
    region = (total + EXPERT_BLOCK - 1) // EXPERT_BLOCK * EXPERT_BLOCK
    gend = jnp.sum(jnp.where(incl, region[None, :], 0), axis=1)
    gstart = gend - region
    delta = gstart[None, :] + cbefore - loff
    k_row = jnp.arange(N_CHUNKS, dtype=jnp.int32) * CHUNK_ROWS
    e_of_chunk = jnp.sum((lend[:, None, :] <= k_row[None, :, None]).astype(jnp.int32), axis=2)
    onehot = e_of_chunk[:, :, None] == jnp.arange(N_EXPERTS, dtype=jnp.int32)[None, None, :]
    chunk_row = (jnp.sum(jnp.where(onehot, delta[:, None, :], 0), axis=2) + k_row[None, :]).reshape(-1)
    n_chunks = lend[:, -1] // CHUNK_ROWS
    block_first = jnp.arange(n_blocks, dtype=jnp.int32) * EXPERT_BLOCK
    block_expert = jnp.minimum(jnp.sum((gend[None, :] <= block_first[:, None]).astype(jnp.int32), axis=1),
                               N_EXPERTS - 1)
    n_used = (gend[-1:] // EXPERT_BLOCK).astype(jnp.int32)
    tail = jnp.concatenate([gstart + total, (region - total) // CHUNK_ROWS, n_used]).astype(jnp.int32)
    loff_f = loff.astype(F32)
    loff_col = jnp.broadcast_to(loff_f[:, :, None], (nt, N_EXPERTS, LANES))
    loff_row = jnp.broadcast_to(jnp.pad(loff_f, ((0, 0), (0, LANES - N_EXPERTS)))[:, None, :],
                                (nt, SUBLANES, LANES))
    return chunk_row, n_chunks, tail, block_expert, n_used, loff_col, loff_row


def _start_chunk_copies(table_ref, count_ref, tile, make_copy):
    base = tile * N_CHUNKS

    def one(k, prio):
        make_copy(k, pl.multiple_of(table_ref[base + k], CHUNK_ROWS)).start(priority=prio)

    def pair(k2, carry):
        for prio in range(2):
            one(2 * k2 + prio, prio)
        return carry

    def single(k, carry):
        one(k, 0)
        return carry

    lax.fori_loop(0, MIN_CHUNKS // 2, pair, 0, unroll=4)
    lax.fori_loop(MIN_CHUNKS, count_ref[tile], single, 0)


def _wait_chunk_copies(count_ref, tile, many, one):
    many.wait()

    def single(k, carry):
        one.wait()
        return carry

    lax.fori_loop(MIN_CHUNKS, count_ref[tile], single, 0)


def _dispatch_kernel(row_ref, cnt_ref, tail_ref, h_ref, rl_ref, loff_ref, xb_ref, sbuf, zbuf, sem, zsem):
    i = pl.program_id(0)
    n = SEQ_TILE
    xiota = lax.broadcasted_iota(jnp.int32, (N_EXPERTS, n), 0)
    riota = lax.broadcasted_iota(jnp.int32, (LOCAL_ROWS, n), 0)
    min_rows = MIN_CHUNKS * CHUNK_ROWS

    def wait_all(slot, tile):
        many = pltpu.make_async_copy(sbuf.at[slot, pl.ds(0, min_rows)], xb_ref.at[pl.ds(0, min_rows)], sem.at[slot])
        one = pltpu.make_async_copy(sbuf.at[slot, pl.ds(0, CHUNK_ROWS)], xb_ref.at[pl.ds(0, CHUNK_ROWS)],
                                    sem.at[slot])
        _wait_chunk_copies(cnt_ref, tile, many, one)

    def sorted_rows(t):
        rl = rl_ref[:, t * n:(t + 1) * n]
        loff = loff_ref[t][:, 0:1]
        slots = []
        for k in range(TOP_K):
            e = rl[k:k + 1].astype(jnp.int32)
            base = jnp.sum(jnp.where(xiota == e, loff, 0.0), axis=0, keepdims=True)
            slots.append((base + rl[4 + k:5 + k]).astype(jnp.int32))
        perm = jnp.where(riota == slots[0], 1.0, jnp.where(riota == slots[1], 1.0, 0.0)).astype(MXU_DTYPE)
        return _pack_rows(_dot(perm, h_ref[t * n:(t + 1) * n, :].astype(MXU_DTYPE)))

    packed = [sorted_rows(t) for t in range(DC_SUB)]
    parity = i % 2
    for t in range(DC_SUB):
        tile = i * DC_SUB + t
        buf = parity * DC_SUB + t

        @pl.when(i >= 2)
        def _():
            wait_all(buf, tile - 2 * DC_SUB)

        sbuf.at[buf][...] = packed[t]

        def chunk_copy(k, row):
            src = sbuf.at[buf, pl.ds(pl.multiple_of(k * CHUNK_ROWS, CHUNK_ROWS), CHUNK_ROWS)]
            return pltpu.make_async_copy(src, xb_ref.at[pl.ds(row, CHUNK_ROWS)], sem.at[buf])

        _start_chunk_copies(row_ref, cnt_ref, tile, chunk_copy)

    @pl.when(i == pl.num_programs(0) - 1)
    def _():
        zbuf[...] = jnp.zeros_like(zbuf)

        def fill(start_copy):
            def per_expert(e, carry):
                first = pl.multiple_of(tail_ref[e], CHUNK_ROWS)

                def per_chunk(c, inner):
                    row = pl.multiple_of(first + c * CHUNK_ROWS, CHUNK_ROWS)
                    copy = pltpu.make_async_copy(zbuf, xb_ref.at[pl.ds(row, CHUNK_ROWS)], zsem)
                    if start_copy:
                        copy.start()
                    else:
                        copy.wait()
                    return inner

                lax.fori_loop(0, tail_ref[N_EXPERTS + e], per_chunk, 0)
                return carry

            lax.fori_loop(0, N_EXPERTS, per_expert, 0)

        fill(True)
        for t in range(DC_SUB):
            wait_all((1 - parity) * DC_SUB + t, (i - 1) * DC_SUB + t)
            wait_all(parity * DC_SUB + t, i * DC_SUB + t)
        fill(False)

        zblock = sbuf.at[0, pl.ds(0, EXPERT_BLOCK)]
        zblock[...] = jnp.zeros_like(zblock)
        n_all = xb_ref.shape[0] // EXPERT_BLOCK

        def fill_blocks(start_copy):
            def per_block(p, carry):
                row = pl.multiple_of(p * EXPERT_BLOCK, EXPERT_BLOCK)
                copy = pltpu.make_async_copy(zblock, xb_ref.at[pl.ds(row, EXPERT_BLOCK)], zsem)
                if start_copy:
                    copy.start()
                else:
                    copy.wait()
                return carry

            lax.fori_loop(tail_ref[2 * N_EXPERTS], n_all, per_block, 0)

        fill_blocks(True)
        fill_blocks(False)


def _dispatch(chunk_row, n_chunks, tail, hb, rl, loff_col, n_blocks):
    T = hb.shape[0]
    n = DC_SUB * SEQ_TILE
    rows = n_blocks * EXPERT_BLOCK
    grid_spec = pltpu.PrefetchScalarGridSpec(
        num_scalar_prefetch=3,
        grid=(T // n,),
        in_specs=[
            pl.BlockSpec((n, D_MODEL), lambda i, r, c, t: (i, 0)),
            pl.BlockSpec((ROUTE_ROWS, n), lambda i, r, c, t: (0, i)),
            pl.BlockSpec((DC_SUB, N_EXPERTS, LANES), lambda i, r, c, t: (i, 0, 0)),
        ],
        out_specs=pl.BlockSpec(memory_space=pl.ANY),
        scratch_shapes=[pltpu.VMEM((2 * DC_SUB, LOCAL_ROWS, PACK_COLS), jnp.uint32),
                        pltpu.VMEM((CHUNK_ROWS, PACK_COLS), jnp.uint32),
                        pltpu.SemaphoreType.DMA((2 * DC_SUB,)), pltpu.SemaphoreType.DMA],
    )
    assert T % n == 0 and T // n >= 2 and LOCAL_ROWS >= EXPERT_BLOCK
    return pl.pallas_call(
        _dispatch_kernel,
        grid_spec=grid_spec,
        out_shape=jax.ShapeDtypeStruct((rows, PACK_COLS), jnp.uint32),
        compiler_params=_params("arbitrary"),
        name="dispatch",
    )(chunk_row, n_chunks, tail, hb, rl, loff_col)


def _expert_kernel(be_ref, nused_ref, x_ref, wg_ref, wu_ref, wd_ref, y_ref, wg_b, wu_b, wd_b):
    p = pl.program_id(0)
    new_expert = jnp.logical_or(p == 0, be_ref[p] != be_ref[jnp.maximum(p - 1, 0)])

    @pl.when(new_expert)
    def _():
        wg_b[...] = wg_ref[...].astype(MXU_DTYPE)
        wu_b[...] = wu_ref[...].astype(MXU_DTYPE)
        wd_b[...] = wd_ref[...].astype(MXU_DTYPE)

    @pl.when(p < nused_ref[0])
    def _():
        lo, hi = _unpack_rows(x_ref[...])
        x = jnp.concatenate([lo.astype(MXU_DTYPE), hi.astype(MXU_DTYPE)], axis=1)
        y = None
        half = D_EXPERT // 2
        for j in range(2):
            cols = slice(j * half, (j + 1) * half)
            gate = _dot(x, wg_b[:, cols])
            up = _dot(x, wu_b[:, cols])
            hid = (gate * _sigmoid(gate) * up).astype(MXU_DTYPE)
            part = _dot(hid, wd_b[cols, :])
            y = part if y is None else y + part
        y_ref[...] = _pack_rows(y)

    @pl.when(p >= nused_ref[0])
    def _():
        y_ref[...] = jnp.zeros_like(y_ref)


def _experts(block_expert, n_used, xb, wg, wu, wd, layer):
    nb = block_expert.shape[0]
    n_slots = nb * EXPERT_BLOCK
    grid_spec = pltpu.PrefetchScalarGridSpec(
        num_scalar_prefetch=2,
        grid=(nb,),
        in_specs=[
            pl.BlockSpec((EXPERT_BLOCK, PACK_COLS), lambda p, be, nu: (jnp.minimum(p, jnp.maximum(nu[0] - 1, 0)), 0)),
            pl.BlockSpec((None, None, D_MODEL, D_EXPERT), lambda p, be, nu: (layer, be[p], 0, 0)),
            pl.BlockSpec((None, None, D_MODEL, D_EXPERT), lambda p, be, nu: (layer, be[p], 0, 0)),
            pl.BlockSpec((None, None, D_EXPERT, D_MODEL), lambda p, be, nu: (layer, be[p], 0, 0)),
        ],
        out_specs=pl.BlockSpec((EXPERT_BLOCK, PACK_COLS), lambda p, be, nu: (p, 0)),
        scratch_shapes=[pltpu.VMEM((D_MODEL, D_EXPERT), MXU_DTYPE), pltpu.VMEM((D_MODEL, D_EXPERT), MXU_DTYPE),
                        pltpu.VMEM((D_EXPERT, D_MODEL), MXU_DTYPE)],
    )
    return pl.pallas_call(
        _expert_kernel,
        grid_spec=grid_spec,
        out_shape=jax.ShapeDtypeStruct((n_slots, PACK_COLS), jnp.uint32),
        compiler_params=_params("arbitrary"),
        name="experts",
    )(block_expert, n_used, xb, wg, wu, wd)


def _combine_rows(step, parity, is_first, next_step, has_next,
                  row_ref, cnt_ref, yb_ref, rr_ref, loff_ref, h_ref, g_ref, b_ref, ybuf, sem, emit,
                  filler=lambda stage: None):
    n = SEQ_TILE
    min_rows = MIN_CHUNKS * CHUNK_ROWS

    def fetch(tile, into):
        def chunk_copy(k, row):
            dst = ybuf.at[into, pl.ds(pl.multiple_of(k * CHUNK_ROWS, CHUNK_ROWS), CHUNK_ROWS)]
            return pltpu.make_async_copy(yb_ref.at[pl.ds(row, CHUNK_ROWS)], dst, sem.at[into])

        _start_chunk_copies(row_ref, cnt_ref, tile, chunk_copy)

    mine = parity * DC_SUB
    theirs = (1 - parity) * DC_SUB

    @pl.when(is_first)
    def _():
        ybuf[...] = jnp.zeros_like(ybuf)
        for t in range(DC_SUB):
            fetch(step * DC_SUB + t, mine + t)

    @pl.when(has_next)
    def _():
        for t in range(DC_SUB):
            fetch(next_step * DC_SUB + t, theirs + t)

    lane = lax.broadcasted_iota(jnp.int32, (n, LANES), 1)
    ciota = lax.broadcasted_iota(jnp.int32, (n, LOCAL_ROWS), 1)

    def weights(t):
        rr = rr_ref[t * n:(t + 1) * n, :]
        loff = loff_ref[t][0:1, :]
        mix = jnp.zeros((n, LOCAL_ROWS), F32)
        for k in range(TOP_K):
            e = rr[:, k:k + 1].astype(jnp.int32)
            base = jnp.sum(jnp.where(lane == e, loff, 0.0), axis=1, keepdims=True)
            sorted_row = (base + rr[:, 4 + k:5 + k]).astype(jnp.int32)
            mix = jnp.where(ciota == sorted_row, rr[:, 2 + k:3 + k], mix)
        return mix.astype(MXU_DTYPE)

    mixes = [weights(t) for t in range(DC_SUB)]
    filler(0)
    for t in range(DC_SUB):
        buf = mine + t
        many = pltpu.make_async_copy(yb_ref.at[pl.ds(0, min_rows)], ybuf.at[buf, pl.ds(0, min_rows)], sem.at[buf])
        one = pltpu.make_async_copy(yb_ref.at[pl.ds(0, CHUNK_ROWS)], ybuf.at[buf, pl.ds(0, CHUNK_ROWS)],
                                    sem.at[buf])
        _wait_chunk_copies(cnt_ref, step * DC_SUB + t, many, one)
        filler(1 + t)
        lo, hi = _unpack_rows(ybuf.at[buf][...])
        y = jnp.concatenate([_dot(mixes[t], lo.astype(MXU_DTYPE)), _dot(mixes[t], hi.astype(MXU_DTYPE))], axis=1)
        rows = slice(t * n, (t + 1) * n)
        emit(t, _layer_norm(DEEPNORM_ALPHA * h_ref[rows, :] + y, g_ref[...], b_ref[...]))


def _combine_kernel(row_ref, cnt_ref, yb_ref, rr_ref, loff_ref, h_ref, g_ref, b_ref, o_ref, ybuf, sem):
    i = pl.program_id(0)

    def emit(t, rows):
        o_ref[t * SEQ_TILE:(t + 1) * SEQ_TILE, :] = rows

    _combine_rows(i, i % 2, i == 0, i + 1, i + 1 < pl.num_programs(0),
                  row_ref, cnt_ref, yb_ref, rr_ref, loff_ref, h_ref, g_ref, b_ref, ybuf, sem, emit)


def _combine_inproj_kernel(row_ref, cnt_ref, yb_ref, rr_ref, loff_ref, h_ref, g_ref, b_ref,
                           w_ref, cos_ref, sin_ref, o_ref, z_ref, ybuf, hprev, sem):
    i = pl.program_id(0)
    n_real = pl.num_programs(0) - 1
    step = jnp.minimum(i, n_real - 1)

    @pl.when(i == 0)
    def _():
        hprev[...] = jnp.zeros_like(hprev)

    cur = i % 2
    xb = hprev.at[1 - cur][...].astype(MXU_DTYPE)

    def emit(t, rows):
        o_ref[t * SEQ_TILE:(t + 1) * SEQ_TILE, :] = rows
        hprev.at[cur][t * SEQ_TILE:(t + 1) * SEQ_TILE, :] = rows

    n_chunks = IN_TOTAL // PROJ_COLS
    bounds = [0, 5, 9, n_chunks]

    def filler(stage):
        _project_columns(xb, w_ref, cos_ref[...], sin_ref[...], z_ref, range(bounds[stage], bounds[stage + 1]))

    _combine_rows(step, cur, i == 0, jnp.minimum(i + 1, n_real - 1), i < n_real,
                  row_ref, cnt_ref, yb_ref, rr_ref, loff_ref, h_ref, g_ref, b_ref, ybuf, sem, emit, filler)


def _combine(chunk_row, n_chunks, yb, rr, loff_row, h2, ln_g, ln_b):
    T = h2.shape[0]
    n = DC_SUB * SEQ_TILE
    row = lambda i, r, c: (i, 0)
    grid_spec = pltpu.PrefetchScalarGridSpec(
        num_scalar_prefetch=2,
        grid=(T // n,),
        in_specs=[
            pl.BlockSpec(memory_space=pl.ANY),
            pl.BlockSpec((n, LANES), row),
            pl.BlockSpec((DC_SUB, SUBLANES, LANES), lambda i, r, c: (i, 0, 0)),
            pl.BlockSpec((n, D_MODEL), row),
            pl.BlockSpec((1, D_MODEL), lambda i, r, c: (0, 0)),
            pl.BlockSpec((1, D_MODEL), lambda i, r, c: (0, 0)),
        ],
        out_specs=pl.BlockSpec((n, D_MODEL), row),
        scratch_shapes=[pltpu.VMEM((2 * DC_SUB, LOCAL_ROWS, PACK_COLS), jnp.uint32),
                        pltpu.SemaphoreType.DMA((2 * DC_SUB,))],
    )
    return pl.pallas_call(
        _combine_kernel,
        grid_spec=grid_spec,
        out_shape=jax.ShapeDtypeStruct((T, D_MODEL), F32),
        compiler_params=_params("arbitrary"),
        name="combine",
    )(chunk_row, n_chunks, yb, rr, loff_row, h2, ln_g, ln_b)


def _combine_inproj(chunk_row, n_chunks, yb, rr, loff_row, h2, ln_g, ln_b, w_in_b, cos_t, sin_t):
    T = h2.shape[0]
    n = DC_SUB * SEQ_TILE
    assert n == PROJ_TILE
    steps = T // n
    cur = lambda i, r, c: (jnp.minimum(i, steps - 1), 0)
    prev = lambda i, r, c: (jnp.maximum(i - 1, 0), 0)
    const = lambda i, r, c: (0, 0)
    grid_spec = pltpu.PrefetchScalarGridSpec(
        num_scalar_prefetch=2,
        grid=(steps + 1,),
        in_specs=[
            pl.BlockSpec(memory_space=pl.ANY),
            pl.BlockSpec((n, LANES), cur),
            pl.BlockSpec((DC_SUB, SUBLANES, LANES), lambda i, r, c: (jnp.minimum(i, steps - 1), 0, 0)),
            pl.BlockSpec((n, D_MODEL), cur),
            pl.BlockSpec((1, D_MODEL), const),
            pl.BlockSpec((1, D_MODEL), const),
            pl.BlockSpec((D_MODEL, IN_TOTAL), const, pipeline_mode=pl.Buffered(1)),
            pl.BlockSpec((n, LANES), prev),
            pl.BlockSpec((n, LANES), prev),
        ],
        out_specs=[pl.BlockSpec((n, D_MODEL), cur), pl.BlockSpec((n, IN_TOTAL), prev)],
        scratch_shapes=[pltpu.VMEM((2 * DC_SUB, LOCAL_ROWS, PACK_COLS), jnp.uint32),
                        pltpu.VMEM((2, n, D_MODEL), F32),
                        pltpu.SemaphoreType.DMA((2 * DC_SUB,))],
    )
    return pl.pallas_call(
        _combine_inproj_kernel,
        grid_spec=grid_spec,
        out_shape=[jax.ShapeDtypeStruct((T, D_MODEL), F32), jax.ShapeDtypeStruct((T, IN_TOTAL), MXU_DTYPE)],
        compiler_params=_params("arbitrary"),
        name="combine_inproj",
    )(chunk_row, n_chunks, yb, rr, loff_row, h2, ln_g, ln_b, w_in_b, cos_t, sin_t)


def _router_weights(w_group, b_group, w_route, b_route):
    wr = jnp.zeros((ROUTER_ROWS, D_MODEL), F32)
    wr = wr.at[GROUP_ROW0:GROUP_ROW0 + N_GROUPS].set(w_group.T)
    wr = wr.at[EXPERT_ROW0:EXPERT_ROW0 + N_EXPERTS].set(w_route.T)
    br = jnp.zeros((ROUTER_ROWS,), F32)
    br = br.at[GROUP_ROW0:GROUP_ROW0 + N_GROUPS].set(b_group)
    br = br.at[EXPERT_ROW0:EXPERT_ROW0 + N_EXPERTS].set(b_route.reshape(-1))
    return wr.astype(MXU_DTYPE), br.reshape(ROUTER_ROWS, 1)


def kernel(x, mem, positions, ln_in_g, ln_in_b, w_in, rel_bias, w_proj_ret, w_proj_att, w_out, ln1_g, ln1_b, w_q_mem, w_kv_mem, w_o_mem, ln2_g, ln2_b, w_group, b_group, w_route, b_route, w_gate, w_up, w_down, ln3_g, ln3_b):
    B, S, D = x.shape
    assert D == D_MODEL and S % PROJ_TILE == 0 and S % SEQ_TILE == 0 and S % MEM_TILE == 0
    T = B * S
    A = T * TOP_K
    n_blocks = -(-(A + (T // SEQ_TILE) * N_EXPERTS * (CHUNK_ROWS - 1) + N_EXPERTS * (EXPERT_BLOCK - 1)) // EXPERT_BLOCK)
    bf = MXU_DTYPE

    cos_t, sin_t = _rope_tables(positions)
    tables = _retention_tables()
    pos = np.arange(MEM_TILE)
    tri = jnp.asarray((pos[:, None] < pos[None, :]) & (pos[:, None] // SEQ_TILE == pos[None, :] // SEQ_TILE), bf)
    mem2d = mem.reshape(-1, D)
    row2 = lambda v: v.reshape(1, D)

    z, h = _inproj(x.reshape(T, D), row2(ln_in_g), row2(ln_in_b), _to_mxu(w_in, 0), cos_t, sin_t, True)
    for l in range(DEPTH):
        bias = _attention_bias(rel_bias[l])
        h = _mixer(z, h, B, S, tables, bias, _to_mxu(w_proj_ret, l), _to_mxu(w_proj_att, l),
                   _to_mxu(w_out, l), row2(ln1_g[l]), row2(ln1_b[l]))

        kv = _mem_kv(mem2d, _to_mxu(w_kv_mem, l))
        wr_t, br_col = _router_weights(w_group[l], b_group[l], w_route[l], b_route[l])
        h2, hb, rl, rr, counts = _memattn(h, kv, B, S, _to_mxu(w_q_mem, l), _to_mxu(w_o_mem, l),
                                          row2(ln2_g[l]), row2(ln2_b[l]), wr_t, br_col, tri)

        chunk_row, n_chunks, tail, block_expert, n_used, loff_col, loff_row = _chunk_plan(counts[:, :, 0], n_blocks)
        xb = _dispatch(chunk_row, n_chunks, tail, hb, rl, loff_col, n_blocks)
        yb = _experts(block_expert, n_used, xb, w_gate, w_up, w_down, l)
        if l + 1 < DEPTH:
            h, z = _combine_inproj(chunk_row, n_chunks, yb, rr, loff_row, h2, row2(ln3_g[l]), row2(ln3_b[l]),
                                   _to_mxu(w_in, l + 1), cos_t, sin_t)
        else:
            h = _combine(chunk_row, n_chunks, yb, rr, loff_row, h2, row2(ln3_g[l]), row2(ln3_b[l]))
    return h.reshape(B, S, D)
```

```python
import functools

import jax
import jax.numpy as jnp
import numpy as np
from jax import lax
from jax.experimental import pallas as pl
from jax.experimental.pallas import tpu as pltpu

D_MODEL = 1024
DEPTH = 2
CHUNK = 64
RET_HEADS = 8
RET_DK = 64
RET_DV = 128
RET_QK = RET_HEADS * RET_DK
RET_V = RET_HEADS * RET_DV
ROPE_BASE = 10000.0
ATT_HEADS = 8
ATT_DH = 64
ATT_W = ATT_HEADS * ATT_DH
LEFT_CHUNKS = 8
MAX_REL = 256
MEM_HEADS = 4
MEM_DH = D_MODEL // MEM_HEADS
N_GROUPS = 4
EXPERTS_PER_GROUP = 8
N_EXPERTS = N_GROUPS * EXPERTS_PER_GROUP
TOP_K = 2
D_EXPERT = 512
LN_EPS = 1e-5
DEEPNORM_ALPHA = (2.0 * DEPTH) ** 0.25
IN_TOTAL = 2 * RET_QK + 2 * RET_V + 3 * ATT_W + 2 * D_MODEL

LANES = 128
SUBLANES = 8
VMEM_LIMIT_BYTES = 56 * 1024 * 1024

SEQ_TILE = 256
MEM_TILE = 2 * SEQ_TILE
MIX_SUB = 2
DC_SUB = 2
PROJ_TILE = 512
PROJ_COLS = 512
EXPERT_BLOCK = 512
HEAD_PAIR = 2 * RET_DK
NEG_BIG = -1e30
LOG2E = 1.4426950408889634

F32 = jnp.float32
MXU_DTYPE = jnp.bfloat16
PACK_COLS = D_MODEL // 2

Z_QK = 0
Z_VR = Z_QK + 2 * RET_QK
Z_GR = Z_VR + RET_V
Z_QA = Z_GR + RET_V
Z_KA = Z_QA + ATT_W
Z_VA = Z_KA + ATT_W
Z_GATE_R = Z_VA + ATT_W
Z_GATE_A = Z_GATE_R + D_MODEL
GATE_HALF = D_MODEL // 2


def _layer_norm(x, g, b):
    mu = jnp.mean(x, axis=-1, keepdims=True)
    xc = x - mu
    var = jnp.mean(xc * xc, axis=-1, keepdims=True)
    return xc * lax.rsqrt(var + LN_EPS) * g + b


def _sigmoid(x):
    return 1.0 / (1.0 + jnp.exp(-x))


def _dot(a, b):
    return jnp.dot(a, b, preferred_element_type=F32)


def _dot_nt(a, b):
    return lax.dot_general(a, b, (((1,), (1,)), ((), ())), preferred_element_type=F32)


def _dot_tn(a, b):
    return lax.dot_general(a, b, (((0,), (0,)), ((), ())), preferred_element_type=F32)


def _params(*semantics):
    return pltpu.CompilerParams(dimension_semantics=semantics, vmem_limit_bytes=VMEM_LIMIT_BYTES)


def _cast_kernel(x_ref, o_ref):
    o_ref[...] = x_ref[...].astype(o_ref.dtype)


def _to_mxu(w_stack, layer):
    _, r, c = w_stack.shape
    tr = 256
    return pl.pallas_call(
        _cast_kernel,
        grid=(r // tr,),
        in_specs=[pl.BlockSpec((None, tr, c), lambda i: (layer, i, 0))],
        out_specs=pl.BlockSpec((tr, c), lambda i: (i, 0)),
        out_shape=jax.ShapeDtypeStruct((r, c), MXU_DTYPE),
        compiler_params=_params("parallel"),
        name="cast_weight",
    )(w_stack)


def _rope_kernel(pos_ref, invf_ref, cos_ref, sin_ref):
    ang = pos_ref[...] * invf_ref[...]
    cos_ref[...] = jnp.cos(ang)
    sin_ref[...] = jnp.sin(ang)


def _rope_tables(positions):
    T = positions.size
    half = RET_DK // 2
    per_row = LANES // half
    inv_freq = 1.0 / (ROPE_BASE ** jnp.linspace(0.0, 1.0, half, dtype=F32))
    invf = jnp.tile(inv_freq, per_row).reshape(1, LANES)
    pos = jnp.repeat(positions.reshape(T // per_row, per_row).astype(F32), half, axis=1)
    rows = T // per_row
    tm = min(1024, rows)
    assert T % per_row == 0 and rows % tm == 0
    spec = pl.BlockSpec((tm, LANES), lambda i: (i, 0))
    cos_p, sin_p = pl.pallas_call(
        _rope_kernel,
        grid=(rows // tm,),
        in_specs=[spec, pl.BlockSpec((1, LANES), lambda i: (0, 0))],
        out_specs=[spec, spec],
        out_shape=[jax.ShapeDtypeStruct((rows, LANES), F32)] * 2,
        compiler_params=_params("parallel"),
        name="rope_tables",
    )(pos, invf)
    cos = cos_p.reshape(T, half)
    sin = sin_p.reshape(T, half)
    return jnp.tile(cos, (1, per_row)), jnp.tile(jnp.concatenate([-sin, sin], axis=1), (1, per_row // 2))


def _rotary(a, cos, sin_signed):
    outs = []
    lane = lax.broadcasted_iota(jnp.int32, cos.shape, 1)
    first_half = (lane % RET_DK) < RET_DK // 2
    for g in range(a.shape[1] // LANES):
        x = a[:, g * LANES:(g + 1) * LANES]
        rot = jnp.where(first_half, pltpu.roll(x, LANES - RET_DK // 2, 1), pltpu.roll(x, RET_DK // 2, 1))
        outs.append(x * cos + rot * sin_signed)
    return jnp.concatenate(outs, axis=1)


def _project_columns(xb, w_ref, cos, sin, z_ref, chunks):
    for c in chunks:
        lo = c * PROJ_COLS
        acc = _dot(xb, w_ref[:, lo:lo + PROJ_COLS])
        if lo < 2 * RET_QK:
            acc = _rotary(acc, cos, sin)
            if lo >= RET_QK:
                acc = acc * (RET_DK ** -0.5)
        z_ref[:, lo:lo + PROJ_COLS] = acc.astype(z_ref.dtype)


def _inproj_kernel(h_ref, g_ref, b_ref, w_ref, cos_ref, sin_ref, z_ref, h_out):
    x = _layer_norm(h_ref[...], g_ref[...], b_ref[...])
    h_out[...] = x
    _project_columns(x.astype(MXU_DTYPE), w_ref, cos_ref[...], sin_ref[...], z_ref, range(IN_TOTAL // PROJ_COLS))


def _inproj(h, ln_g, ln_b, w_in_b, cos_t, sin_t):
    T = h.shape[0]
    tm = PROJ_TILE
    row = lambda i: (i, 0)
    const = lambda i: (0, 0)
    out_shape = [jax.ShapeDtypeStruct((T, IN_TOTAL), MXU_DTYPE), jax.ShapeDtypeStruct((T, D_MODEL), F32)]
    out_specs = [pl.BlockSpec((tm, IN_TOTAL), row), pl.BlockSpec((tm, D_MODEL), row)]
    res = pl.pallas_call(
        _inproj_kernel,
        grid=(T // tm,),
        in_specs=[
            pl.BlockSpec((tm, D_MODEL), row),
            pl.BlockSpec((1, D_MODEL), const),
            pl.BlockSpec((1, D_MODEL), const),
            pl.BlockSpec((D_MODEL, IN_TOTAL), const, pipeline_mode=pl.Buffered(1)),
            pl.BlockSpec((tm, LANES), row),
            pl.BlockSpec((tm, LANES), row),
        ],
        out_specs=out_specs,
        out_shape=out_shape,
        compiler_params=_params("parallel"),
        name="inproj",
    )(h, ln_g, ln_b, w_in_b, cos_t, sin_t)
    return res


def _retention_tables():
    n = SEQ_TILE
    log_g = np.log1p(-np.exp2(-5.0 - np.arange(RET_HEADS, dtype=np.float64)))
    i = np.arange(n)
    same = (i[:, None] // CHUNK) == (i[None, :] // CHUNK)
    earlier = (i[None, :] // CHUNK) < (i[:, None] // CHUNK)
    diff = i[:, None] - i[None, :]
    expo = np.where(same, np.abs(diff), np.where(earlier, diff, 0)).astype(np.float64)
    decay = np.exp(log_g[:, None, None] * expo) * (same | earlier)
    q_decay = np.exp(log_g[:, None] * (i + 1.0))
    k_decay = np.exp(log_g[:, None] * (n - 1.0 - i))
    tile_decay = np.exp(log_g * n)
    wide = lambda t: np.broadcast_to(t[:, :, None], t.shape + (HEAD_PAIR,))
    owner = (np.arange(HEAD_PAIR)[None, :] // RET_DK) == (np.arange(RET_HEADS)[:, None] % 2)
    k_table = wide(k_decay) * owner[:, None, :]
    return (jnp.asarray(decay, F32), jnp.asarray(wide(q_decay), F32), jnp.asarray(k_table, F32),
            tuple(float(v) for v in tile_decay))


def _attention_bias(rel_bias):
    n = SEQ_TILE
    width = 3 * n
    pad = LEFT_CHUNKS * CHUNK
    assert pad == 2 * n and n <= MAX_REL
    heads = rel_bias.shape[0]
    length = width + n
    near = rel_bias[:, MAX_REL - n:].astype(F32)
    far = jnp.broadcast_to(rel_bias[:, -1:].astype(F32), (heads, length - near.shape[1]))
    g_rev = jnp.concatenate([near, far], axis=1)[:, ::-1]
    rows = jnp.tile(g_rev, (1, n + 1))[:, :n * (length - 1)].reshape(heads, n, length - 1)
    bias = rows[:, :, n - 1:n - 1 + width]
    qi = np.arange(n)
    kj = np.arange(width)
    q_chunk = (pad + qi) // CHUNK
    k_chunk = kj // CHUNK
    in_band = (k_chunk[None, :] <= q_chunk[:, None]) & (k_chunk[None, :] >= q_chunk[:, None] - LEFT_CHUNKS)
    return jnp.where(jnp.asarray(in_band)[None], bias * LOG2E, NEG_BIG)


def _mixer_kernel(tile_decay,
                  qk_ref, vr_ref, gr_ref, gate_r0_ref, gate_r1_ref, gate_a0_ref, gate_a1_ref, qa_ref,
                  kp_ref, kc_ref, vp_ref, vc_ref,
                  h_ref, decay_ref, qdec_ref, kdec_ref, bias_ref,
                  wpr_ref, wpa_ref, wout_ref, g_ref, b_ref,
                  o_ref, state_ref, yr_ref, ya_ref):
    i = pl.program_id(1)

    @pl.when(i == 0)
    def _():
        state_ref[...] = jnp.zeros_like(state_ref)

    n = SEQ_TILE
    lane = lax.broadcasted_iota(jnp.int32, (n, HEAD_PAIR), 1)
    low_half = lane < RET_DK

    def head_lanes(odd):
        return low_half if odd == 0 else jnp.logical_not(low_half)

    def rows(t):
        return slice(t * n, (t + 1) * n)

    def retention_state_update(t, hd):
        p = hd // 2
        k2 = qk_ref[rows(t), RET_QK + p * HEAD_PAIR:RET_QK + (p + 1) * HEAD_PAIR]
        v = vr_ref[rows(t), hd * RET_DV:(hd + 1) * RET_DV]
        k_dec = (k2.astype(F32) * kdec_ref[hd]).astype(MXU_DTYPE)
        state_ref[hd] = state_ref[hd] * tile_decay[hd] + _dot_tn(k_dec, v)

    def retention_first(t, hd):
        p, odd = divmod(hd, 2)
        q2 = qk_ref[rows(t), p * HEAD_PAIR:(p + 1) * HEAD_PAIR]
        k2 = qk_ref[rows(t), RET_QK + p * HEAD_PAIR:RET_QK + (p + 1) * HEAD_PAIR]
        qm = jnp.where(head_lanes(odd), q2, jnp.zeros_like(q2))
        v = vr_ref[rows(t), hd * RET_DV:(hd + 1) * RET_DV]
        raw = _dot_nt(qm, k2)
        read = _dot(qm, state_ref[hd].astype(MXU_DTYPE))
        if t + 1 < MIX_SUB:
            retention_state_update(t, hd)
        return raw, read, v

    def retention_second(t, hd, raw, read, v):
        intra = _dot((raw * decay_ref[hd]).astype(MXU_DTYPE), v)
        ret = intra + qdec_ref[hd] * read
        mu = jnp.mean(ret, axis=-1, keepdims=True)
        rc = ret - mu
        var = jnp.mean(rc * rc, axis=-1, keepdims=True)
        gn = rc * lax.rsqrt(var + LN_EPS)
        gate = gr_ref[rows(t), hd * RET_DV:(hd + 1) * RET_DV].astype(F32)
        yr_ref[rows(t), hd * RET_DV:(hd + 1) * RET_DV] = (gate * _sigmoid(gate) * gn).astype(MXU_DTYPE)

    pen_prev = jnp.where(i >= 1, 0.0, NEG_BIG).astype(F32)

    def key_block(t, kb):
        j = t + kb + (MIX_SUB - 2)
        if j < MIX_SUB:
            return kp_ref, vp_ref, rows(j), pen_prev
        return kc_ref, vc_ref, rows(j - MIX_SUB), None

    def attention_first(t, hd):
        p, odd = divmod(hd, 2)
        cols = slice(p * HEAD_PAIR, (p + 1) * HEAD_PAIR)
        qs = (qa_ref[rows(t), cols].astype(F32) * (ATT_DH ** -0.5 * LOG2E)).astype(MXU_DTYPE)
        qm = jnp.where(head_lanes(odd), qs, jnp.zeros_like(qs))
        s = []
        for kb in range(3):
            k_ref, _, krows, pen = key_block(t, kb)
            sk = _dot_nt(qm, k_ref[krows, cols]) + bias_ref[hd, :, kb * n:(kb + 1) * n]
            if pen is not None:
                sk = sk + pen
            s.append(sk)
        return (s,)

    def attention_second(t, hd, s):
        p, odd = divmod(hd, 2)
        cols = slice(p * HEAD_PAIR, (p + 1) * HEAD_PAIR)
        m = jnp.maximum(jnp.maximum(jnp.max(s[0], axis=-1, keepdims=True),
                                    jnp.max(s[1], axis=-1, keepdims=True)),
                        jnp.max(s[2], axis=-1, keepdims=True))
        e = [jnp.exp2(sk - m) for sk in s]
        denom = (jnp.sum(e[0], axis=-1, keepdims=True) + jnp.sum(e[1], axis=-1, keepdims=True)
                 + jnp.sum(e[2], axis=-1, keepdims=True))
        pv = None
        for kb in range(3):
            _, v_ref, krows, _ = key_block(t, kb)
            term = _dot(e[kb].astype(MXU_DTYPE), v_ref[krows, cols])
            pv = term if pv is None else pv + term
        out = (pv / denom).astype(MXU_DTYPE)
        lo = odd * ATT_DH
        ya_ref[rows(t), p * HEAD_PAIR + lo:p * HEAD_PAIR + lo + ATT_DH] = out[:, lo:lo + ATT_DH]

    work = []
    for t in range(MIX_SUB):
        for hd in range(max(RET_HEADS, ATT_HEADS)):
            if hd < RET_HEADS:
                work.append((retention_first, retention_second, t, hd))
            if hd < ATT_HEADS:
                work.append((attention_first, attention_second, t, hd))
    ahead = 2
    pending = [first(t, hd) for first, _, t, hd in work[:ahead]]
    for j, (_, second, t, hd) in enumerate(work):
        if j + ahead < len(work):
            first, _, t_next, hd_next = work[j + ahead]
            pending.append(first(t_next, hd_next))
        second(t, hd, *pending.pop(0))

    pr = _dot(yr_ref[...], wpr_ref[...])
    pa = _dot(ya_ref[...], wpa_ref[...])
    gate_r = jnp.concatenate([gate_r0_ref[...], gate_r1_ref[...]], axis=1).astype(F32)
    gate_a = jnp.concatenate([gate_a0_ref[...], gate_a1_ref[...]], axis=1).astype(F32)
    merged = _sigmoid(gate_r) * pr + _sigmoid(gate_a) * pa
    mix = _dot(merged.astype(MXU_DTYPE), wout_ref[...])
    for hd in range(RET_HEADS):
        retention_state_update(MIX_SUB - 1, hd)
    o_ref[...] = _layer_norm(DEEPNORM_ALPHA * h_ref[...] + mix, g_ref[...], b_ref[...])


def _mixer(z, h, B, S, tables, bias, wpr, wpa, wout, ln_g, ln_b):
    decay, q_decay, k_decay, tile_decay = tables
    T = B * S
    n = SEQ_TILE
    step_rows = MIX_SUB * n
    nt = S // step_rows
    assert MIX_SUB >= 2 and S % step_rows == 0

    def zspec(width, col_off, back=0):
        cb = col_off // width
        return pl.BlockSpec((step_rows, width), lambda b, i: (b * nt + jnp.maximum(i - back, 0), cb))

    row = lambda b, i: (b * nt + i, 0)
    c2 = lambda b, i: (0, 0)
    c3 = lambda b, i: (0, 0, 0)
    one = pl.Buffered(1)
    in_specs = [
        zspec(2 * RET_QK, Z_QK), zspec(RET_V, Z_VR), zspec(RET_V, Z_GR),
        zspec(GATE_HALF, Z_GATE_R), zspec(GATE_HALF, Z_GATE_R + GATE_HALF),
        zspec(GATE_HALF, Z_GATE_A), zspec(GATE_HALF, Z_GATE_A + GATE_HALF), zspec(ATT_W, Z_QA),
        zspec(ATT_W, Z_KA, 1), zspec(ATT_W, Z_KA, 0),
        zspec(ATT_W, Z_VA, 1), zspec(ATT_W, Z_VA, 0),
        pl.BlockSpec((step_rows, D_MODEL), row),
        pl.BlockSpec((RET_HEADS, n, n), c3, pipeline_mode=one),
        pl.BlockSpec((RET_HEADS, n, HEAD_PAIR), c3, pipeline_mode=one),
        pl.BlockSpec((RET_HEADS, n, HEAD_PAIR), c3, pipeline_mode=one),
        pl.BlockSpec((ATT_HEADS, n, 3 * n), c3, pipeline_mode=one),
        pl.BlockSpec((RET_V, D_MODEL), c2, pipeline_mode=one),
        pl.BlockSpec((ATT_W, D_MODEL), c2, pipeline_mode=one),
        pl.BlockSpec((D_MODEL, D_MODEL), c2, pipeline_mode=one),
        pl.BlockSpec((1, D_MODEL), c2),
        pl.BlockSpec((1, D_MODEL), c2),
    ]
    return pl.pallas_call(
        functools.partial(_mixer_kernel, tile_decay),
        grid=(B, nt),
        in_specs=in_specs,
        out_specs=pl.BlockSpec((step_rows, D_MODEL), row),
        out_shape=jax.ShapeDtypeStruct((T, D_MODEL), F32),
        scratch_shapes=[
            pltpu.VMEM((RET_HEADS, HEAD_PAIR, RET_DV), F32),
            pltpu.VMEM((step_rows, RET_V), MXU_DTYPE),
            pltpu.VMEM((step_rows, ATT_W), MXU_DTYPE),
        ],
        compiler_params=_params("arbitrary", "arbitrary"),
        name="mixer",
    )(*([z] * 12), h, decay, q_decay, k_decay, bias, wpr, wpa, wout, ln_g, ln_b)


def _kv_kernel(m_ref, w_ref, o_ref):
    o_ref[...] = _dot(m_ref[...].astype(MXU_DTYPE), w_ref[...]).astype(o_ref.dtype)


def _mem_kv(mem2d, w_kv):
    M = mem2d.shape[0]
    tm = 256
    return pl.pallas_call(
        _kv_kernel,
        grid=(M // tm,),
        in_specs=[pl.BlockSpec((tm, D_MODEL), lambda i: (i, 0)),
                  pl.BlockSpec((D_MODEL, 2 * D_MODEL), lambda i: (0, 0), pipeline_mode=pl.Buffered(1))],
        out_specs=pl.BlockSpec((tm, 2 * D_MODEL), lambda i: (i, 0)),
        out_shape=jax.ShapeDtypeStruct((M, 2 * D_MODEL), MXU_DTYPE),
        compiler_params=_params("parallel"),
        name="mem_kv",
    )(mem2d, w_kv)


ROUTE_ROWS = 8
GROUP_ROW0 = 0
EXPERT_ROW0 = 8
ROUTER_ROWS = 128


def _memattn_kernel(h_ref, kv_ref, wq_ref, wo_ref, g_ref, b_ref, wr_ref, br_ref, tri_ref,
                    h2_ref, hb_ref, rl_ref, rr_ref, cnt_ref, o_scr):
    n = MEM_TILE
    h1 = h_ref[...]
    q = (_dot(h1.astype(MXU_DTYPE), wq_ref[...]) * (MEM_DH ** -0.5 * LOG2E)).astype(MXU_DTYPE)
    scores = [_dot_nt(q[:, hd * MEM_DH:(hd + 1) * MEM_DH], kv_ref[:, hd * MEM_DH:(hd + 1) * MEM_DH])
              for hd in range(MEM_HEADS)]
    for hd in range(MEM_HEADS):
        cols = slice(hd * MEM_DH, (hd + 1) * MEM_DH)
        s = scores[hd]
        m = jnp.max(s, axis=-1, keepdims=True)
        e = jnp.exp2(s - m)
        denom = jnp.sum(e, axis=-1, keepdims=True)
        v = kv_ref[:, D_MODEL + hd * MEM_DH:D_MODEL + (hd + 1) * MEM_DH]
        o_scr[:, cols] = (_dot(e.astype(MXU_DTYPE), v) / denom).astype(MXU_DTYPE)
    cross = _dot(o_scr[...], wo_ref[...])
    h2 = _layer_norm(DEEPNORM_ALPHA * h1 + cross, g_ref[...], b_ref[...])
    h2_ref[...] = h2
    hb = h2.astype(MXU_DTYPE)
    hb_ref[...] = hb

    logits = _dot_nt(wr_ref[...], hb) + br_ref[...]
    glog = logits[GROUP_ROW0:GROUP_ROW0 + N_GROUPS]
    gmax = jnp.max(glog, axis=0, keepdims=True)
    g_w = 1.0 / jnp.sum(jnp.exp(glog - gmax), axis=0, keepdims=True)
    giota = lax.broadcasted_iota(jnp.int32, glog.shape, 0)
    g_idx = jnp.min(jnp.where(glog == gmax, giota, N_GROUPS), axis=0, keepdims=True)
    el = jnp.zeros((EXPERTS_PER_GROUP, n), F32)
    for g in range(N_GROUPS):
        r0 = EXPERT_ROW0 + g * EXPERTS_PER_GROUP
        el = jnp.where(g_idx == g, logits[r0:r0 + EXPERTS_PER_GROUP], el)
    eiota = lax.broadcasted_iota(jnp.int32, el.shape, 0)
    m1 = jnp.max(el, axis=0, keepdims=True)
    i1 = jnp.min(jnp.where(el == m1, eiota, EXPERTS_PER_GROUP), axis=0, keepdims=True)
    el2 = jnp.where(eiota == i1, -jnp.inf, el)
    m2 = jnp.max(el2, axis=0, keepdims=True)
    i2 = jnp.min(jnp.where(el2 == m2, eiota, EXPERTS_PER_GROUP), axis=0, keepdims=True)
    r = jnp.exp(m2 - m1)
    w1 = g_w / (1.0 + r)
    w2 = g_w * r / (1.0 + r)
    e1 = g_idx * EXPERTS_PER_GROUP + i1
    e2 = g_idx * EXPERTS_PER_GROUP + i2

    xiota = lax.broadcasted_iota(jnp.int32, (N_EXPERTS, n), 0)
    oh1 = xiota == e1
    oh2 = xiota == e2
    cnt = jnp.where(oh1, 1.0, 0.0) + jnp.where(oh2, 1.0, 0.0)
    before = _dot(cnt.astype(MXU_DTYPE), tri_ref[...])
    rank1 = jnp.sum(jnp.where(oh1, before, 0.0), axis=0, keepdims=True)
    rank2 = jnp.sum(jnp.where(oh2, before, 0.0), axis=0, keepdims=True)
    for j in range(n // SEQ_TILE):
        tile_cnt = jnp.sum(cnt[:, j * SEQ_TILE:(j + 1) * SEQ_TILE], axis=1, keepdims=True)
        cnt_ref[j] = jnp.broadcast_to(tile_cnt, (N_EXPERTS, LANES))

    zero = jnp.zeros((1, n), F32)
    rec = jnp.concatenate([e1.astype(F32), e2.astype(F32), w1, w2, rank1, rank2, zero, zero], axis=0)
    rl_ref[...] = rec
    rec_full = jnp.concatenate([rec, jnp.zeros((LANES - ROUTE_ROWS, n), F32)], axis=0)
    rr_ref[...] = rec_full.T


def _memattn(h1, kv, B, S, wq, wo, ln_g, ln_b, wr_t, br_col, tri):
    T = B * S
    n = MEM_TILE
    nt = S // n
    sub = n // SEQ_TILE
    row = lambda b, i: (b * nt + i, 0)
    c2 = lambda b, i: (0, 0)
    one = pl.Buffered(1)
    return pl.pallas_call(
        _memattn_kernel,
        grid=(B, nt),
        in_specs=[
            pl.BlockSpec((n, D_MODEL), row),
            pl.BlockSpec((kv.shape[0] // B, 2 * D_MODEL), lambda b, i: (b, 0)),
            pl.BlockSpec((D_MODEL, D_MODEL), c2, pipeline_mode=one),
            pl.BlockSpec((D_MODEL, D_MODEL), c2, pipeline_mode=one),
            pl.BlockSpec((1, D_MODEL), c2),
            pl.BlockSpec((1, D_MODEL), c2),
            pl.BlockSpec((ROUTER_ROWS, D_MODEL), c2, pipeline_mode=one),
            pl.BlockSpec((ROUTER_ROWS, 1), c2),
            pl.BlockSpec((n, n), c2, pipeline_mode=one),
        ],
        out_specs=[
            pl.BlockSpec((n, D_MODEL), row),
            pl.BlockSpec((n, D_MODEL), row),
            pl.BlockSpec((ROUTE_ROWS, n), lambda b, i: (0, b * nt + i)),
            pl.BlockSpec((n, LANES), row),
            pl.BlockSpec((sub, N_EXPERTS, LANES), lambda b, i: (b * nt + i, 0, 0)),
        ],
        out_shape=[
            jax.ShapeDtypeStruct((T, D_MODEL), F32),
            jax.ShapeDtypeStruct((T, D_MODEL), MXU_DTYPE),
            jax.ShapeDtypeStruct((ROUTE_ROWS, T), F32),
            jax.ShapeDtypeStruct((T, LANES), F32),
            jax.ShapeDtypeStruct((T // SEQ_TILE, N_EXPERTS, LANES), F32),
        ],
        scratch_shapes=[pltpu.VMEM((n, D_MODEL), MXU_DTYPE)],
        compiler_params=_params("parallel", "parallel"),
        name="memattn_router",
    )(h1, kv, wq, wo, ln_g, ln_b, wr_t, br_col, tri)


CHUNK_ROWS = SUBLANES
LOCAL_ROWS = 768
N_CHUNKS = LOCAL_ROWS // CHUNK_ROWS
MIN_CHUNKS = TOP_K * SEQ_TILE // CHUNK_ROWS
assert TOP_K * SEQ_TILE + N_EXPERTS * (CHUNK_ROWS - 1) <= LOCAL_ROWS


def _unpack_rows(packed):
    lo = lax.bitcast_convert_type(packed << 16, F32)
    hi = lax.bitcast_convert_type(packed & jnp.uint32(0xFFFF0000), F32)
    return lo, hi


def _pack_rows(y):
    bits = lax.bitcast_convert_type(y.astype(jnp.bfloat16).astype(F32), jnp.uint32)
    return (bits[:, :PACK_COLS] >> 16) | bits[:, PACK_COLS:]


def _chunk_plan(counts, n_blocks):
    nt = counts.shape[0]
    c = counts.astype(jnp.int32)
    incl = jnp.asarray(np.tril(np.ones((N_EXPERTS, N_EXPERTS), bool)))
    padc = (c + CHUNK_ROWS - 1) // CHUNK_ROWS * CHUNK_ROWS
    lend = jnp.sum(jnp.where(incl[None], padc[:, None, :], 0), axis=2)
    loff = lend - padc
    earlier = jnp.asarray(np.tril(np.ones((nt, nt), bool), -1))
    cbefore = jnp.sum(jnp.where(earlier[:, :, None], padc[None, :, :], 0), axis=1)
    total = jnp.sum(padc, axis=0)
    region = (total + EXPERT_BLOCK - 1) // EXPERT_BLOCK * EXPERT_BLOCK
    gend = jnp.sum(jnp.where(incl, region[None, :], 0), axis=1)
    gstart = gend - region
    delta = gstart[None, :] + cbefore - loff
    k_row = jnp.arange(N_CHUNKS, dtype=jnp.int32) * CHUNK_ROWS
    e_of_chunk = jnp.sum((lend[:, None, :] <= k_row[None, :, None]).astype(jnp.int32), axis=2)
    onehot = e_of_chunk[:, :, None] == jnp.arange(N_EXPERTS, dtype=jnp.int32)[None, None, :]
    chunk_row = (jnp.sum(jnp.where(onehot, delta[:, None, :], 0), axis=2) + k_row[None, :]).reshape(-1)
    n_chunks = lend[:, -1] // CHUNK_ROWS
    block_first = jnp.arange(n_blocks, dtype=jnp.int32) * EXPERT_BLOCK
    block_expert = jnp.minimum(jnp.sum((gend[None, :] <= block_first[:, None]).astype(jnp.int32), axis=1),
                               N_EXPERTS - 1)
    n_used = (gend[-1:] // EXPERT_BLOCK).astype(jnp.int32)
    tail = jnp.concatenate([gstart + total, (region - total) // CHUNK_ROWS, n_used]).astype(jnp.int32)
    loff_f = loff.astype(F32)
    loff_col = jnp.broadcast_to(loff_f[:, :, None], (nt, N_EXPERTS, LANES))
    loff_row = jnp.broadcast_to(jnp.pad(loff_f, ((0, 0), (0, LANES - N_EXPERTS)))[:, None, :],
                                (nt, SUBLANES, LANES))
    return chunk_row, n_chunks, tail, block_expert, n_used, loff_col, loff_row


def _start_chunk_copies(table_ref, count_ref, tile, make_copy):
    base = tile * N_CHUNKS

    def one(k, prio):
        make_copy(k, pl.multiple_of(table_ref[base + k], CHUNK_ROWS)).start(priority=prio)

    def pair(k2, carry):
        for prio in range(2):
            one(2 * k2 + prio, prio)
        return carry

    def single(k, carry):
        one(k, 0)
        return carry

    lax.fori_loop(0, MIN_CHUNKS // 2, pair, 0, unroll=4)
    lax.fori_loop(MIN_CHUNKS, count_ref[tile], single, 0)


def _wait_chunk_copies(count_ref, tile, many, one):
    many.wait()

    def single(k, carry):
        one.wait()
        return carry

    lax.fori_loop(MIN_CHUNKS, count_ref[tile], single, 0)


def _dispatch_kernel(row_ref, cnt_ref, tail_ref, h_ref, rl_ref, loff_ref, xb_ref, sbuf, zbuf, sem, zsem):
    i = pl.program_id(0)
    n = SEQ_TILE
    xiota = lax.broadcasted_iota(jnp.int32, (N_EXPERTS, n), 0)
    riota = lax.broadcasted_iota(jnp.int32, (LOCAL_ROWS, n), 0)
    min_rows = MIN_CHUNKS * CHUNK_ROWS

    def wait_all(slot, tile):
        many = pltpu.make_async_copy(sbuf.at[slot, pl.ds(0, min_rows)], xb_ref.at[pl.ds(0, min_rows)], sem.at[slot])
        one = pltpu.make_async_copy(sbuf.at[slot, pl.ds(0, CHUNK_ROWS)], xb_ref.at[pl.ds(0, CHUNK_ROWS)],
                                    sem.at[slot])
        _wait_chunk_copies(cnt_ref, tile, many, one)

    def sorted_rows(t):
        rl = rl_ref[:, t * n:(t + 1) * n]
        loff = loff_ref[t][:, 0:1]
        slots = []
        for k in range(TOP_K):
            e = rl[k:k + 1].astype(jnp.int32)
            base = jnp.sum(jnp.where(xiota == e, loff, 0.0), axis=0, keepdims=True)
            slots.append((base + rl[4 + k:5 + k]).astype(jnp.int32))
        perm = jnp.where(riota == slots[0], 1.0, jnp.where(riota == slots[1], 1.0, 0.0)).astype(MXU_DTYPE)
        return _pack_rows(_dot(perm, h_ref[t * n:(t + 1) * n, :].astype(MXU_DTYPE)))

    packed = [sorted_rows(t) for t in range(DC_SUB)]
    parity = i % 2
    for t in range(DC_SUB):
        tile = i * DC_SUB + t
        buf = parity * DC_SUB + t

        @pl.when(i >= 2)
        def _():
            wait_all(buf, tile - 2 * DC_SUB)

        sbuf.at[buf][...] = packed[t]

        def chunk_copy(k, row):
            src = sbuf.at[buf, pl.ds(pl.multiple_of(k * CHUNK_ROWS, CHUNK_ROWS), CHUNK_ROWS)]
            return pltpu.make_async_copy(src, xb_ref.at[pl.ds(row, CHUNK_ROWS)], sem.at[buf])

        _start_chunk_copies(row_ref, cnt_ref, tile, chunk_copy)

    @pl.when(i == pl.num_programs(0) - 1)
    def _():
        zbuf[...] = jnp.zeros_like(zbuf)

        def fill(start_copy):
            def per_expert(e, carry):
                first = pl.multiple_of(tail_ref[e], CHUNK_ROWS)

                def per_chunk(c, inner):
                    row = pl.multiple_of(first + c * CHUNK_ROWS, CHUNK_ROWS)
                    copy = pltpu.make_async_copy(zbuf, xb_ref.at[pl.ds(row, CHUNK_ROWS)], zsem)
                    if start_copy:
                        copy.start()
                    else:
                        copy.wait()
                    return inner

                lax.fori_loop(0, tail_ref[N_EXPERTS + e], per_chunk, 0)
                return carry

            lax.fori_loop(0, N_EXPERTS, per_expert, 0)

        fill(True)
        for t in range(DC_SUB):
            wait_all((1 - parity) * DC_SUB + t, (i - 1) * DC_SUB + t)
            wait_all(parity * DC_SUB + t, i * DC_SUB + t)
        fill(False)

        zblock = sbuf.at[0, pl.ds(0, EXPERT_BLOCK)]
        zblock[...] = jnp.zeros_like(zblock)
        n_all = xb_ref.shape[0] // EXPERT_BLOCK

        def fill_blocks(start_copy):
            def per_block(p, carry):
                row = pl.multiple_of(p * EXPERT_BLOCK, EXPERT_BLOCK)
                copy = pltpu.make_async_copy(zblock, xb_ref.at[pl.ds(row, EXPERT_BLOCK)], zsem)
                if start_copy:
                    copy.start()
                else:
                    copy.wait()
                return carry

            lax.fori_loop(tail_ref[2 * N_EXPERTS], n_all, per_block, 0)

        fill_blocks(True)
        fill_blocks(False)


def _dispatch(chunk_row, n_chunks, tail, hb, rl, loff_col, n_blocks):
    T = hb.shape[0]
    n = DC_SUB * SEQ_TILE
    rows = n_blocks * EXPERT_BLOCK
    grid_spec = pltpu.PrefetchScalarGridSpec(
        num_scalar_prefetch=3,
        grid=(T // n,),
        in_specs=[
            pl.BlockSpec((n, D_MODEL), lambda i, r, c, t: (i, 0)),
            pl.BlockSpec((ROUTE_ROWS, n), lambda i, r, c, t: (0, i)),
            pl.BlockSpec((DC_SUB, N_EXPERTS, LANES), lambda i, r, c, t: (i, 0, 0)),
        ],
        out_specs=pl.BlockSpec(memory_space=pl.ANY),
        scratch_shapes=[pltpu.VMEM((2 * DC_SUB, LOCAL_ROWS, PACK_COLS), jnp.uint32),
                        pltpu.VMEM((CHUNK_ROWS, PACK_COLS), jnp.uint32),
                        pltpu.SemaphoreType.DMA((2 * DC_SUB,)), pltpu.SemaphoreType.DMA],
    )
    assert T % n == 0 and T // n >= 2 and LOCAL_ROWS >= EXPERT_BLOCK
    return pl.pallas_call(
        _dispatch_kernel,
        grid_spec=grid_spec,
        out_shape=jax.ShapeDtypeStruct((rows, PACK_COLS), jnp.uint32),
        compiler_params=_params("arbitrary"),
        name="dispatch",
    )(chunk_row, n_chunks, tail, hb, rl, loff_col)


def _expert_kernel(be_ref, nused_ref, x_ref, wg_ref, wu_ref, wd_ref, y_ref, wg_b, wu_b, wd_b):
    p = pl.program_id(0)
    new_expert = jnp.logical_or(p == 0, be_ref[p] != be_ref[jnp.maximum(p - 1, 0)])

    @pl.when(new_expert)
    def _():
        wg_b[...] = wg_ref[...].astype(MXU_DTYPE)
        wu_b[...] = wu_ref[...].astype(MXU_DTYPE)
        wd_b[...] = wd_ref[...].astype(MXU_DTYPE)

    @pl.when(p < nused_ref[0])
    def _():
        lo, hi = _unpack_rows(x_ref[...])
        x = jnp.concatenate([lo.astype(MXU_DTYPE), hi.astype(MXU_DTYPE)], axis=1)
        y = None
        half = D_EXPERT // 2
        for j in range(2):
            cols = slice(j * half, (j + 1) * half)
            gate = _dot(x, wg_b[:, cols])
            up = _dot(x, wu_b[:, cols])
            hid = (gate * _sigmoid(gate) * up).astype(MXU_DTYPE)
            part = _dot(hid, wd_b[cols, :])
            y = part if y is None else y + part
        y_ref[...] = _pack_rows(y)

    @pl.when(p >= nused_ref[0])
    def _():
        y_ref[...] = jnp.zeros_like(y_ref)


def _experts(block_expert, n_used, xb, wg, wu, wd, layer):
    nb = block_expert.shape[0]
    n_slots = nb * EXPERT_BLOCK
    grid_spec = pltpu.PrefetchScalarGridSpec(
        num_scalar_prefetch=2,
        grid=(nb,),
        in_specs=[
            pl.BlockSpec((EXPERT_BLOCK, PACK_COLS), lambda p, be, nu: (jnp.minimum(p, jnp.maximum(nu[0] - 1, 0)), 0)),
            pl.BlockSpec((None, None, D_MODEL, D_EXPERT), lambda p, be, nu: (layer, be[p], 0, 0)),
            pl.BlockSpec((None, None, D_MODEL, D_EXPERT), lambda p, be, nu: (layer, be[p], 0, 0)),
            pl.BlockSpec((None, None, D_EXPERT, D_MODEL), lambda p, be, nu: (layer, be[p], 0, 0)),
        ],
        out_specs=pl.BlockSpec((EXPERT_BLOCK, PACK_COLS), lambda p, be, nu: (p, 0)),
        scratch_shapes=[pltpu.VMEM((D_MODEL, D_EXPERT), MXU_DTYPE), pltpu.VMEM((D_MODEL, D_EXPERT), MXU_DTYPE),
                        pltpu.VMEM((D_EXPERT, D_MODEL), MXU_DTYPE)],
    )
    return pl.pallas_call(
        _expert_kernel,
        grid_spec=grid_spec,
        out_shape=jax.ShapeDtypeStruct((n_slots, PACK_COLS), jnp.uint32),
        compiler_params=_params("arbitrary"),
        name="experts",
    )(block_expert, n_used, xb, wg, wu, wd)


def _combine_rows(step, parity, is_first, next_step, has_next,
                  row_ref, cnt_ref, yb_ref, rr_ref, loff_ref, h_ref, g_ref, b_ref, ybuf, sem, emit,
                  filler=lambda stage: None):
    n = SEQ_TILE
    min_rows = MIN_CHUNKS * CHUNK_ROWS

    def fetch(tile, into):
        def chunk_copy(k, row):
            dst = ybuf.at[into, pl.ds(pl.multiple_of(k * CHUNK_ROWS, CHUNK_ROWS), CHUNK_ROWS)]
            return pltpu.make_async_copy(yb_ref.at[pl.ds(row, CHUNK_ROWS)], dst, sem.at[into])

        _start_chunk_copies(row_ref, cnt_ref, tile, chunk_copy)

    mine = parity * DC_SUB
    theirs = (1 - parity) * DC_SUB

    @pl.when(is_first)
    def _():
        ybuf[...] = jnp.zeros_like(ybuf)
        for t in range(DC_SUB):
            fetch(step * DC_SUB + t, mine + t)

    @pl.when(has_next)
    def _():
        for t in range(DC_SUB):
            fetch(next_step * DC_SUB + t, theirs + t)

    lane = lax.broadcasted_iota(jnp.int32, (n, LANES), 1)
    ciota = lax.broadcasted_iota(jnp.int32, (n, LOCAL_ROWS), 1)

    def weights(t):
        rr = rr_ref[t * n:(t + 1) * n, :]
        loff = loff_ref[t][0:1, :]
        mix = jnp.zeros((n, LOCAL_ROWS), F32)
        for k in range(TOP_K):
            e = rr[:, k:k + 1].astype(jnp.int32)
            base = jnp.sum(jnp.where(lane == e, loff, 0.0), axis=1, keepdims=True)
            sorted_row = (base + rr[:, 4 + k:5 + k]).astype(jnp.int32)
            mix = jnp.where(ciota == sorted_row, rr[:, 2 + k:3 + k], mix)
        return mix.astype(MXU_DTYPE)

    mixes = [weights(t) for t in range(DC_SUB)]
    filler(0)
    for t in range(DC_SUB):
        buf = mine + t
        many = pltpu.make_async_copy(yb_ref.at[pl.ds(0, min_rows)], ybuf.at[buf, pl.ds(0, min_rows)], sem.at[buf])
        one = pltpu.make_async_copy(yb_ref.at[pl.ds(0, CHUNK_ROWS)], ybuf.at[buf, pl.ds(0, CHUNK_ROWS)],
                                    sem.at[buf])
        _wait_chunk_copies(cnt_ref, step * DC_SUB + t, many, one)
        filler(1 + t)
        lo, hi = _unpack_rows(ybuf.at[buf][...])
        y = jnp.concatenate([_dot(mixes[t], lo.astype(MXU_DTYPE)), _dot(mixes[t], hi.astype(MXU_DTYPE))], axis=1)
        rows = slice(t * n, (t + 1) * n)
        emit(t, _layer_norm(DEEPNORM_ALPHA * h_ref[rows, :] + y, g_ref[...], b_ref[...]))


def _combine_kernel(row_ref, cnt_ref, yb_ref, rr_ref, loff_ref, h_ref, g_ref, b_ref, o_ref, ybuf, sem):
    i = pl.program_id(0)

    def emit(t, rows):
        o_ref[t * SEQ_TILE:(t + 1) * SEQ_TILE, :] = rows

    _combine_rows(i, i % 2, i == 0, i + 1, i + 1 < pl.num_programs(0),
                  row_ref, cnt_ref, yb_ref, rr_ref, loff_ref, h_ref, g_ref, b_ref, ybuf, sem, emit)


def _combine_inproj_kernel(row_ref, cnt_ref, yb_ref, rr_ref, loff_ref, h_ref, g_ref, b_ref,
                           w_ref, cos_ref, sin_ref, o_ref, z_ref, ybuf, hprev, sem):
    i = pl.program_id(0)
    n_real = pl.num_programs(0) - 1
    step = jnp.minimum(i, n_real - 1)

    @pl.when(i == 0)
    def _():
        hprev[...] = jnp.zeros_like(hprev)

    cur = i % 2
    xb = hprev.at[1 - cur][...].astype(MXU_DTYPE)

    def emit(t, rows):
        o_ref[t * SEQ_TILE:(t + 1) * SEQ_TILE, :] = rows
        hprev.at[cur][t * SEQ_TILE:(t + 1) * SEQ_TILE, :] = rows

    n_chunks = IN_TOTAL // PROJ_COLS
    bounds = [0, 5, 9, n_chunks]

    def filler(stage):
        _project_columns(xb, w_ref, cos_ref[...], sin_ref[...], z_ref, range(bounds[stage], bounds[stage + 1]))

    _combine_rows(step, cur, i == 0, jnp.minimum(i + 1, n_real - 1), i < n_real,
                  row_ref, cnt_ref, yb_ref, rr_ref, loff_ref, h_ref, g_ref, b_ref, ybuf, sem, emit, filler)


def _combine(chunk_row, n_chunks, yb, rr, loff_row, h2, ln_g, ln_b):
    T = h2.shape[0]
    n = DC_SUB * SEQ_TILE
    row = lambda i, r, c: (i, 0)
    grid_spec = pltpu.PrefetchScalarGridSpec(
        num_scalar_prefetch=2,
        grid=(T // n,),
        in_specs=[
            pl.BlockSpec(memory_space=pl.ANY),
            pl.BlockSpec((n, LANES), row),
            pl.BlockSpec((DC_SUB, SUBLANES, LANES), lambda i, r, c: (i, 0, 0)),
            pl.BlockSpec((n, D_MODEL), row),
            pl.BlockSpec((1, D_MODEL), lambda i, r, c: (0, 0)),
            pl.BlockSpec((1, D_MODEL), lambda i, r, c: (0, 0)),
        ],
        out_specs=pl.BlockSpec((n, D_MODEL), row),
        scratch_shapes=[pltpu.VMEM((2 * DC_SUB, LOCAL_ROWS, PACK_COLS), jnp.uint32),
                        pltpu.SemaphoreType.DMA((2 * DC_SUB,))],
    )
    return pl.pallas_call(
        _combine_kernel,
        grid_spec=grid_spec,
        out_shape=jax.ShapeDtypeStruct((T, D_MODEL), F32),
        compiler_params=_params("arbitrary"),
        name="combine",
    )(chunk_row, n_chunks, yb, rr, loff_row, h2, ln_g, ln_b)


def _combine_inproj(chunk_row, n_chunks, yb, rr, loff_row, h2, ln_g, ln_b, w_in_b, cos_t, sin_t):
    T = h2.shape[0]
    n = DC_SUB * SEQ_TILE
    assert n == PROJ_TILE
    steps = T // n
    cur = lambda i, r, c: (jnp.minimum(i, steps - 1), 0)
    prev = lambda i, r, c: (jnp.maximum(i - 1, 0), 0)
    const = lambda i, r, c: (0, 0)
    grid_spec = pltpu.PrefetchScalarGridSpec(
        num_scalar_prefetch=2,
        grid=(steps + 1,),
        in_specs=[
            pl.BlockSpec(memory_space=pl.ANY),
            pl.BlockSpec((n, LANES), cur),
            pl.BlockSpec((DC_SUB, SUBLANES, LANES), lambda i, r, c: (jnp.minimum(i, steps - 1), 0, 0)),
            pl.BlockSpec((n, D_MODEL), cur),
            pl.BlockSpec((1, D_MODEL), const),
            pl.BlockSpec((1, D_MODEL), const),
            pl.BlockSpec((D_MODEL, IN_TOTAL), const, pipeline_mode=pl.Buffered(1)),
            pl.BlockSpec((n, LANES), prev),
            pl.BlockSpec((n, LANES), prev),
        ],
        out_specs=[pl.BlockSpec((n, D_MODEL), cur), pl.BlockSpec((n, IN_TOTAL), prev)],
        scratch_shapes=[pltpu.VMEM((2 * DC_SUB, LOCAL_ROWS, PACK_COLS), jnp.uint32),
                        pltpu.VMEM((2, n, D_MODEL), F32),
                        pltpu.SemaphoreType.DMA((2 * DC_SUB,))],
    )
    return pl.pallas_call(
        _combine_inproj_kernel,
        grid_spec=grid_spec,
        out_shape=[jax.ShapeDtypeStruct((T, D_MODEL), F32), jax.ShapeDtypeStruct((T, IN_TOTAL), MXU_DTYPE)],
        compiler_params=_params("arbitrary"),
        name="combine_inproj",
    )(chunk_row, n_chunks, yb, rr, loff_row, h2, ln_g, ln_b, w_in_b, cos_t, sin_t)


def _router_weights(w_group, b_group, w_route, b_route):
    wr = jnp.zeros((ROUTER_ROWS, D_MODEL), F32)
    wr = wr.at[GROUP_ROW0:GROUP_ROW0 + N_GROUPS].set(w_group.T)
    wr = wr.at[EXPERT_ROW0:EXPERT_ROW0 + N_EXPERTS].set(w_route.T)
    br = jnp.zeros((ROUTER_ROWS,), F32)
    br = br.at[GROUP_ROW0:GROUP_ROW0 + N_GROUPS].set(b_group)
    br = br.at[EXPERT_ROW0:EXPERT_ROW0 + N_EXPERTS].set(b_route.reshape(-1))
    return wr.astype(MXU_DTYPE), br.reshape(ROUTER_ROWS, 1)


def kernel(x, mem, positions, ln_in_g, ln_in_b, w_in, rel_bias, w_proj_ret, w_proj_att, w_out, ln1_g, ln1_b, w_q_mem, w_kv_mem, w_o_mem, ln2_g, ln2_b, w_group, b_group, w_route, b_route, w_gate, w_up, w_down, ln3_g, ln3_b):
    B, S, D = x.shape
    assert D == D_MODEL and S % PROJ_TILE == 0 and S % SEQ_TILE == 0 and S % MEM_TILE == 0
    T = B * S
    A = T * TOP_K
    n_blocks = -(-(A + (T // SEQ_TILE) * N_EXPERTS * (CHUNK_ROWS - 1) + N_EXPERTS * (EXPERT_BLOCK - 1)) // EXPERT_BLOCK)
    bf = MXU_DTYPE

    cos_t, sin_t = _rope_tables(positions)
    tables = _retention_tables()
    pos = np.arange(MEM_TILE)
    tri = jnp.asarray((pos[:, None] < pos[None, :]) & (pos[:, None] // SEQ_TILE == pos[None, :] // SEQ_TILE), bf)
    mem2d = mem.reshape(-1, D)
    row2 = lambda v: v.reshape(1, D)

    z, h = _inproj(x.reshape(T, D), row2(ln_in_g), row2(ln_in_b), _to_mxu(w_in, 0), cos_t, sin_t)
    for l in range(DEPTH):
        bias = _attention_bias(rel_bias[l])
        h = _mixer(z, h, B, S, tables, bias, _to_mxu(w_proj_ret, l), _to_mxu(w_proj_att, l),
                   _to_mxu(w_out, l), row2(ln1_g[l]), row2(ln1_b[l]))

        kv = _mem_kv(mem2d, _to_mxu(w_kv_mem, l))
        wr_t, br_col = _router_weights(w_group[l], b_group[l], w_route[l], b_route[l])
        h2, hb, rl, rr, counts = _memattn(h, kv, B, S, _to_mxu(w_q_mem, l), _to_mxu(w_o_mem, l),
                                          row2(ln2_g[l]), row2(ln2_b[l]), wr_t, br_col, tri)

        chunk_row, n_chunks, tail, block_expert, n_used, loff_col, loff_row = _chunk_plan(counts[:, :, 0], n_blocks)
        xb = _dispatch(chunk_row, n_chunks, tail, hb, rl, loff_col, n_blocks)
        yb = _experts(block_expert, n_used, xb, w_gate, w_up, w_down, l)
        if l + 1 < DEPTH:
            h, z = _combine_inproj(chunk_row, n_chunks, yb, rr, loff_row, h2, row2(ln3_g[l]), row2(ln3_b[l]),
                                   _to_mxu(w_in, l + 1), cos_t, sin_t)
        else:
            h = _combine(chunk_row, n_chunks, yb, rr, loff_row, h2, row2(ln3_g[l]), row2(ln3_b[l]))
    return h.reshape(B, S, D)
```

```python
import functools

import jax
import jax.numpy as jnp
import numpy as np
from jax import lax
from jax.experimental import pallas as pl
from jax.experimental.pallas import tpu as pltpu

D_MODEL = 1024
DEPTH = 2
CHUNK = 64
RET_HEADS = 8
RET_DK = 64
RET_DV = 128
RET_QK = RET_HEADS * RET_DK
RET_V = RET_HEADS * RET_DV
ROPE_BASE = 10000.0
ATT_HEADS = 8
ATT_DH = 64
ATT_W = ATT_HEADS * ATT_DH
LEFT_CHUNKS = 8
MAX_REL = 256
MEM_HEADS = 4
MEM_DH = D_MODEL // MEM_HEADS
N_GROUPS = 4
EXPERTS_PER_GROUP = 8
N_EXPERTS = N_GROUPS * EXPERTS_PER_GROUP
TOP_K = 2
D_EXPERT = 512
LN_EPS = 1e-5
DEEPNORM_ALPHA = (2.0 * DEPTH) ** 0.25
IN_TOTAL = 2 * RET_QK + 2 * RET_V + 3 * ATT_W + 2 * D_MODEL

LANES = 128
SUBLANES = 8
VMEM_LIMIT_BYTES = 56 * 1024 * 1024

SEQ_TILE = 256
MEM_TILE = 4 * SEQ_TILE
MIX_SUB = 2
DC_SUB = 2
PROJ_TILE = 512
PROJ_COLS = 512
EXPERT_BLOCK = 512
HEAD_PAIR = 2 * RET_DK
NEG_BIG = -1e30
LOG2E = 1.4426950408889634

F32 = jnp.float32
MXU_DTYPE = jnp.bfloat16
PACK_COLS = D_MODEL // 2

Z_QK = 0
Z_VR = Z_QK + 2 * RET_QK
Z_GR = Z_VR + RET_V
Z_QA = Z_GR + RET_V
Z_KA = Z_QA + ATT_W
Z_VA = Z_KA + ATT_W
Z_GATE_R = Z_VA + ATT_W
Z_GATE_A = Z_GATE_R + D_MODEL
GATE_HALF = D_MODEL // 2


def _layer_norm(x, g, b):
    mu = jnp.mean(x, axis=-1, keepdims=True)
    xc = x - mu
    var = jnp.mean(xc * xc, axis=-1, keepdims=True)
    return xc * lax.rsqrt(var + LN_EPS) * g + b


def _sigmoid(x):
    return 1.0 / (1.0 + jnp.exp(-x))


def _dot(a, b):
    return jnp.dot(a, b, preferred_element_type=F32)


def _dot_nt(a, b):
    return lax.dot_general(a, b, (((1,), (1,)), ((), ())), preferred_element_type=F32)


def _dot_tn(a, b):
    return lax.dot_general(a, b, (((0,), (0,)), ((), ())), preferred_element_type=F32)


def _params(*semantics):
    return pltpu.CompilerParams(dimension_semantics=semantics, vmem_limit_bytes=VMEM_LIMIT_BYTES)


def _cast_kernel(x_ref, o_ref):
    o_ref[...] = x_ref[...].astype(o_ref.dtype)


def _to_mxu(w_stack, layer):
    _, r, c = w_stack.shape
    tr = 256
    return pl.pallas_call(
        _cast_kernel,
        grid=(r // tr,),
        in_specs=[pl.BlockSpec((None, tr, c), lambda i: (layer, i, 0))],
        out_specs=pl.BlockSpec((tr, c), lambda i: (i, 0)),
        out_shape=jax.ShapeDtypeStruct((r, c), MXU_DTYPE),
        compiler_params=_params("parallel"),
        name="cast_weight",
    )(w_stack)


def _rope_kernel(pos_ref, invf_ref, cos_ref, sin_ref):
    ang = pos_ref[...] * invf_ref[...]
    lane = lax.broadcasted_iota(jnp.int32, ang.shape, 1)
    s = jnp.sin(ang)
    cos_ref[...] = jnp.cos(ang)
    sin_ref[...] = jnp.where((lane % RET_DK) < RET_DK // 2, -s, s)


def _rope_tables(positions):
    T = positions.size
    half = RET_DK // 2
    inv_freq = 1.0 / (ROPE_BASE ** jnp.linspace(0.0, 1.0, half, dtype=F32))
    invf = jnp.tile(inv_freq, LANES // half).reshape(1, LANES)
    pos = jnp.broadcast_to(positions.reshape(T, 1).astype(F32), (T, LANES))
    tm = 1024
    spec = pl.BlockSpec((tm, LANES), lambda i: (i, 0))
    return pl.pallas_call(
        _rope_kernel,
        grid=(T // tm,),
        in_specs=[spec, pl.BlockSpec((1, LANES), lambda i: (0, 0))],
        out_specs=[spec, spec],
        out_shape=[jax.ShapeDtypeStruct((T, LANES), F32)] * 2,
        compiler_params=_params("parallel"),
        name="rope_tables",
    )(pos, invf)


def _rotary(a, cos, sin_signed):
    outs = []
    lane = lax.broadcasted_iota(jnp.int32, cos.shape, 1)
    first_half = (lane % RET_DK) < RET_DK // 2
    for g in range(a.shape[1] // LANES):
        x = a[:, g * LANES:(g + 1) * LANES]
        rot = jnp.where(first_half, pltpu.roll(x, LANES - RET_DK // 2, 1), pltpu.roll(x, RET_DK // 2, 1))
        outs.append(x * cos + rot * sin_signed)
    return jnp.concatenate(outs, axis=1)


def _project_columns(xb, w_ref, cos, sin, z_ref, chunks):
    for c in chunks:
        lo = c * PROJ_COLS
        acc = _dot(xb, w_ref[:, lo:lo + PROJ_COLS])
        if lo < 2 * RET_QK:
            acc = _rotary(acc, cos, sin)
            if lo >= RET_QK:
                acc = acc * (RET_DK ** -0.5)
        z_ref[:, lo:lo + PROJ_COLS] = acc.astype(z_ref.dtype)


def _inproj_kernel(h_ref, g_ref, b_ref, w_ref, cos_ref, sin_ref, z_ref, h_out):
    x = _layer_norm(h_ref[...], g_ref[...], b_ref[...])
    h_out[...] = x
    _project_columns(x.astype(MXU_DTYPE), w_ref, cos_ref[...], sin_ref[...], z_ref, range(IN_TOTAL // PROJ_COLS))


def _inproj(h, ln_g, ln_b, w_in_b, cos_t, sin_t):
    T = h.shape[0]
    tm = PROJ_TILE
    row = lambda i: (i, 0)
    const = lambda i: (0, 0)
    out_shape = [jax.ShapeDtypeStruct((T, IN_TOTAL), MXU_DTYPE), jax.ShapeDtypeStruct((T, D_MODEL), F32)]
    out_specs = [pl.BlockSpec((tm, IN_TOTAL), row), pl.BlockSpec((tm, D_MODEL), row)]
    res = pl.pallas_call(
        _inproj_kernel,
        grid=(T // tm,),
        in_specs=[
            pl.BlockSpec((tm, D_MODEL), row),
            pl.BlockSpec((1, D_MODEL), const),
            pl.BlockSpec((1, D_MODEL), const),
            pl.BlockSpec((D_MODEL, IN_TOTAL), const, pipeline_mode=pl.Buffered(1)),
            pl.BlockSpec((tm, LANES), row),
            pl.BlockSpec((tm, LANES), row),
        ],
        out_specs=out_specs,
        out_shape=out_shape,
        compiler_params=_params("parallel"),
        name="inproj",
    )(h, ln_g, ln_b, w_in_b, cos_t, sin_t)
    return res


def _retention_tables():
    n = SEQ_TILE
    log_g = np.log1p(-np.exp2(-5.0 - np.arange(RET_HEADS, dtype=np.float64)))
    i = np.arange(n)
    same = (i[:, None] // CHUNK) == (i[None, :] // CHUNK)
    earlier = (i[None, :] // CHUNK) < (i[:, None] // CHUNK)
    diff = i[:, None] - i[None, :]
    expo = np.where(same, np.abs(diff), np.where(earlier, diff, 0)).astype(np.float64)
    decay = np.exp(log_g[:, None, None] * expo) * (same | earlier)
    q_decay = np.exp(log_g[:, None] * (i + 1.0))
    k_decay = np.exp(log_g[:, None] * (n - 1.0 - i))
    tile_decay = np.exp(log_g * n)
    wide = lambda t: np.broadcast_to(t[:, :, None], t.shape + (HEAD_PAIR,))
    owner = (np.arange(HEAD_PAIR)[None, :] // RET_DK) == (np.arange(RET_HEADS)[:, None] % 2)
    k_table = wide(k_decay) * owner[:, None, :]
    return (jnp.asarray(decay, F32), jnp.asarray(wide(q_decay), F32), jnp.asarray(k_table, F32),
            tuple(float(v) for v in tile_decay))


def _attention_bias(rel_bias):
    n = SEQ_TILE
    width = 3 * n
    pad = LEFT_CHUNKS * CHUNK
    assert pad == 2 * n and n <= MAX_REL
    heads = rel_bias.shape[0]
    length = width + n
    near = rel_bias[:, MAX_REL - n:].astype(F32)
    far = jnp.broadcast_to(rel_bias[:, -1:].astype(F32), (heads, length - near.shape[1]))
    g_rev = jnp.concatenate([near, far], axis=1)[:, ::-1]
    rows = jnp.tile(g_rev, (1, n + 1))[:, :n * (length - 1)].reshape(heads, n, length - 1)
    bias = rows[:, :, n - 1:n - 1 + width]
    qi = np.arange(n)
    kj = np.arange(width)
    q_chunk = (pad + qi) // CHUNK
    k_chunk = kj // CHUNK
    in_band = (k_chunk[None, :] <= q_chunk[:, None]) & (k_chunk[None, :] >= q_chunk[:, None] - LEFT_CHUNKS)
    return jnp.where(jnp.asarray(in_band)[None], bias * LOG2E, NEG_BIG)


def _mixer_kernel(tile_decay,
                  qk_ref, vr_ref, gr_ref, gate_r0_ref, gate_r1_ref, gate_a0_ref, gate_a1_ref, qa_ref,
                  kp_ref, kc_ref, vp_ref, vc_ref,
                  h_ref, decay_ref, qdec_ref, kdec_ref, bias_ref,
                  wpr_ref, wpa_ref, wout_ref, g_ref, b_ref,
                  o_ref, state_ref, yr_ref, ya_ref):
    i = pl.program_id(1)

    @pl.when(i == 0)
    def _():
        state_ref[...] = jnp.zeros_like(state_ref)

    n = SEQ_TILE
    lane = lax.broadcasted_iota(jnp.int32, (n, HEAD_PAIR), 1)
    low_half = lane < RET_DK

    def head_lanes(odd):
        return low_half if odd == 0 else jnp.logical_not(low_half)

    def rows(t):
        return slice(t * n, (t + 1) * n)

    def retention_state_update(t, hd):
        p = hd // 2
        k2 = qk_ref[rows(t), RET_QK + p * HEAD_PAIR:RET_QK + (p + 1) * HEAD_PAIR]
        v = vr_ref[rows(t), hd * RET_DV:(hd + 1) * RET_DV]
        k_dec = (k2.astype(F32) * kdec_ref[hd]).astype(MXU_DTYPE)
        state_ref[hd] = state_ref[hd] * tile_decay[hd] + _dot_tn(k_dec, v)

    def retention_first(t, hd):
        p, odd = divmod(hd, 2)
        q2 = qk_ref[rows(t), p * HEAD_PAIR:(p + 1) * HEAD_PAIR]
        k2 = qk_ref[rows(t), RET_QK + p * HEAD_PAIR:RET_QK + (p + 1) * HEAD_PAIR]
        qm = jnp.where(head_lanes(odd), q2, jnp.zeros_like(q2))
        v = vr_ref[rows(t), hd * RET_DV:(hd + 1) * RET_DV]
        raw = _dot_nt(qm, k2)
        read = _dot(qm, state_ref[hd].astype(MXU_DTYPE))
        if t + 1 < MIX_SUB:
            retention_state_update(t, hd)
        return raw, read, v

    def retention_second(t, hd, raw, read, v):
        intra = _dot((raw * decay_ref[hd]).astype(MXU_DTYPE), v)
        ret = intra + qdec_ref[hd] * read
        mu = jnp.mean(ret, axis=-1, keepdims=True)
        rc = ret - mu
        var = jnp.mean(rc * rc, axis=-1, keepdims=True)
        gn = rc * lax.rsqrt(var + LN_EPS)
        gate = gr_ref[rows(t), hd * RET_DV:(hd + 1) * RET_DV].astype(F32)
        yr_ref[rows(t), hd * RET_DV:(hd + 1) * RET_DV] = (gate * _sigmoid(gate) * gn).astype(MXU_DTYPE)

    pen_prev = jnp.where(i >= 1, 0.0, NEG_BIG).astype(F32)

    def key_block(t, kb):
        j = t + kb + (MIX_SUB - 2)
        if j < MIX_SUB:
            return kp_ref, vp_ref, rows(j), pen_prev
        return kc_ref, vc_ref, rows(j - MIX_SUB), None

    def attention_first(t, hd):
        p, odd = divmod(hd, 2)
        cols = slice(p * HEAD_PAIR, (p + 1) * HEAD_PAIR)
        qs = (qa_ref[rows(t), cols].astype(F32) * (ATT_DH ** -0.5 * LOG2E)).astype(MXU_DTYPE)
        qm = jnp.where(head_lanes(odd), qs, jnp.zeros_like(qs))
        s = []
        for kb in range(3):
            k_ref, _, krows, pen = key_block(t, kb)
            sk = _dot_nt(qm, k_ref[krows, cols]) + bias_ref[hd, :, kb * n:(kb + 1) * n]
            if pen is not None:
                sk = sk + pen
            s.append(sk)
        return (s,)

    def attention_second(t, hd, s):
        p, odd = divmod(hd, 2)
        cols = slice(p * HEAD_PAIR, (p + 1) * HEAD_PAIR)
        m = jnp.maximum(jnp.maximum(jnp.max(s[0], axis=-1, keepdims=True),
                                    jnp.max(s[1], axis=-1, keepdims=True)),
                        jnp.max(s[2], axis=-1, keepdims=True))
        e = [jnp.exp2(sk - m) for sk in s]
        denom = (jnp.sum(e[0], axis=-1, keepdims=True) + jnp.sum(e[1], axis=-1, keepdims=True)
                 + jnp.sum(e[2], axis=-1, keepdims=True))
        pv = None
        for kb in range(3):
            _, v_ref, krows, _ = key_block(t, kb)
            term = _dot(e[kb].astype(MXU_DTYPE), v_ref[krows, cols])
            pv = term if pv is None else pv + term
        out = (pv / denom).astype(MXU_DTYPE)
        lo = odd * ATT_DH
        ya_ref[rows(t), p * HEAD_PAIR + lo:p * HEAD_PAIR + lo + ATT_DH] = out[:, lo:lo + ATT_DH]

    work = []
    for t in range(MIX_SUB):
        for hd in range(max(RET_HEADS, ATT_HEADS)):
            if hd < RET_HEADS:
                work.append((retention_first, retention_second, t, hd))
            if hd < ATT_HEADS:
                work.append((attention_first, attention_second, t, hd))
    ahead = 2
    pending = [first(t, hd) for first, _, t, hd in work[:ahead]]
    for j, (_, second, t, hd) in enumerate(work):
        if j + ahead < len(work):
            first, _, t_next, hd_next = work[j + ahead]
            pending.append(first(t_next, hd_next))
        second(t, hd, *pending.pop(0))

    pr = _dot(yr_ref[...], wpr_ref[...])
    pa = _dot(ya_ref[...], wpa_ref[...])
    gate_r = jnp.concatenate([gate_r0_ref[...], gate_r1_ref[...]], axis=1).astype(F32)
    gate_a = jnp.concatenate([gate_a0_ref[...], gate_a1_ref[...]], axis=1).astype(F32)
    merged = _sigmoid(gate_r) * pr + _sigmoid(gate_a) * pa
    mix = _dot(merged.astype(MXU_DTYPE), wout_ref[...])
    for hd in range(RET_HEADS):
        retention_state_update(MIX_SUB - 1, hd)
    o_ref[...] = _layer_norm(DEEPNORM_ALPHA * h_ref[...] + mix, g_ref[...], b_ref[...])


def _mixer(z, h, B, S, tables, bias, wpr, wpa, wout, ln_g, ln_b):
    decay, q_decay, k_decay, tile_decay = tables
    T = B * S
    n = SEQ_TILE
    step_rows = MIX_SUB * n
    nt = S // step_rows
    assert MIX_SUB >= 2 and S % step_rows == 0

    def zspec(width, col_off, back=0):
        cb = col_off // width
        return pl.BlockSpec((step_rows, width), lambda b, i: (b * nt + jnp.maximum(i - back, 0), cb))

    row = lambda b, i: (b * nt + i, 0)
    c2 = lambda b, i: (0, 0)
    c3 = lambda b, i: (0, 0, 0)
    one = pl.Buffered(1)
    in_specs = [
        zspec(2 * RET_QK, Z_QK), zspec(RET_V, Z_VR), zspec(RET_V, Z_GR),
        zspec(GATE_HALF, Z_GATE_R), zspec(GATE_HALF, Z_GATE_R + GATE_HALF),
        zspec(GATE_HALF, Z_GATE_A), zspec(GATE_HALF, Z_GATE_A + GATE_HALF), zspec(ATT_W, Z_QA),
        zspec(ATT_W, Z_KA, 1), zspec(ATT_W, Z_KA, 0),
        zspec(ATT_W, Z_VA, 1), zspec(ATT_W, Z_VA, 0),
        pl.BlockSpec((step_rows, D_MODEL), row),
        pl.BlockSpec((RET_HEADS, n, n), c3, pipeline_mode=one),
        pl.BlockSpec((RET_HEADS, n, HEAD_PAIR), c3, pipeline_mode=one),
        pl.BlockSpec((RET_HEADS, n, HEAD_PAIR), c3, pipeline_mode=one),
        pl.BlockSpec((ATT_HEADS, n, 3 * n), c3, pipeline_mode=one),
        pl.BlockSpec((RET_V, D_MODEL), c2, pipeline_mode=one),
        pl.BlockSpec((ATT_W, D_MODEL), c2, pipeline_mode=one),
        pl.BlockSpec((D_MODEL, D_MODEL), c2, pipeline_mode=one),
        pl.BlockSpec((1, D_MODEL), c2),
        pl.BlockSpec((1, D_MODEL), c2),
    ]
    return pl.pallas_call(
        functools.partial(_mixer_kernel, tile_decay),
        grid=(B, nt),
        in_specs=in_specs,
        out_specs=pl.BlockSpec((step_rows, D_MODEL), row),
        out_shape=jax.ShapeDtypeStruct((T, D_MODEL), F32),
        scratch_shapes=[
            pltpu.VMEM((RET_HEADS, HEAD_PAIR, RET_DV), F32),
            pltpu.VMEM((step_rows, RET_V), MXU_DTYPE),
            pltpu.VMEM((step_rows, ATT_W), MXU_DTYPE),
        ],
        compiler_params=_params("arbitrary", "arbitrary"),
        name="mixer",
    )(*([z] * 12), h, decay, q_decay, k_decay, bias, wpr, wpa, wout, ln_g, ln_b)


def _kv_kernel(m_ref, w_ref, o_ref):
    o_ref[...] = _dot(m_ref[...].astype(MXU_DTYPE), w_ref[...]).astype(o_ref.dtype)


def _mem_kv(mem2d, w_kv):
    M = mem2d.shape[0]
    tm = 256
    return pl.pallas_call(
        _kv_kernel,
        grid=(M // tm,),
        in_specs=[pl.BlockSpec((tm, D_MODEL), lambda i: (i, 0)),
                  pl.BlockSpec((D_MODEL, 2 * D_MODEL), lambda i: (0, 0), pipeline_mode=pl.Buffered(1))],
        out_specs=pl.BlockSpec((tm, 2 * D_MODEL), lambda i: (i, 0)),
        out_shape=jax.ShapeDtypeStruct((M, 2 * D_MODEL), MXU_DTYPE),
        compiler_params=_params("parallel"),
        name="mem_kv",
    )(mem2d, w_kv)


ROUTE_ROWS = 8
GROUP_ROW0 = 0
EXPERT_ROW0 = 8
ROUTER_ROWS = 128


def _memattn_kernel(h_ref, kv_ref, wq_ref, wo_ref, g_ref, b_ref, wr_ref, br_ref, tri_ref,
                    h2_ref, hb_ref, rl_ref, rr_ref, cnt_ref, o_scr):
    n = MEM_TILE
    h1 = h_ref[...]
    q = (_dot(h1.astype(MXU_DTYPE), wq_ref[...]) * (MEM_DH ** -0.5 * LOG2E)).astype(MXU_DTYPE)
    scores = [_dot_nt(q[:, hd * MEM_DH:(hd + 1) * MEM_DH], kv_ref[:, hd * MEM_DH:(hd + 1) * MEM_DH])
              for hd in range(MEM_HEADS)]
    for hd in range(MEM_HEADS):
        cols = slice(hd * MEM_DH, (hd + 1) * MEM_DH)
        s = scores[hd]
        m = jnp.max(s, axis=-1, keepdims=True)
        e = jnp.exp2(s - m)
        denom = jnp.sum(e, axis=-1, keepdims=True)
        v = kv_ref[:, D_MODEL + hd * MEM_DH:D_MODEL + (hd + 1) * MEM_DH]
        o_scr[:, cols] = (_dot(e.astype(MXU_DTYPE), v) / denom).astype(MXU_DTYPE)
    cross = _dot(o_scr[...], wo_ref[...])
    h2 = _layer_norm(DEEPNORM_ALPHA * h1 + cross, g_ref[...], b_ref[...])
    h2_ref[...] = h2
    hb = h2.astype(MXU_DTYPE)
    hb_ref[...] = hb

    logits = _dot_nt(wr_ref[...], hb) + br_ref[...]
    glog = logits[GROUP_ROW0:GROUP_ROW0 + N_GROUPS]
    gmax = jnp.max(glog, axis=0, keepdims=True)
    g_w = 1.0 / jnp.sum(jnp.exp(glog - gmax), axis=0, keepdims=True)
    giota = lax.broadcasted_iota(jnp.int32, glog.shape, 0)
    g_idx = jnp.min(jnp.where(glog == gmax, giota, N_GROUPS), axis=0, keepdims=True)
    el = jnp.zeros((EXPERTS_PER_GROUP, n), F32)
    for g in range(N_GROUPS):
        r0 = EXPERT_ROW0 + g * EXPERTS_PER_GROUP
        el = jnp.where(g_idx == g, logits[r0:r0 + EXPERTS_PER_GROUP], el)
    eiota = lax.broadcasted_iota(jnp.int32, el.shape, 0)
    m1 = jnp.max(el, axis=0, keepdims=True)
    i1 = jnp.min(jnp.where(el == m1, eiota, EXPERTS_PER_GROUP), axis=0, keepdims=True)
    el2 = jnp.where(eiota == i1, -jnp.inf, el)
    m2 = jnp.max(el2, axis=0, keepdims=True)
    i2 = jnp.min(jnp.where(el2 == m2, eiota, EXPERTS_PER_GROUP), axis=0, keepdims=True)
    r = jnp.exp(m2 - m1)
    w1 = g_w / (1.0 + r)
    w2 = g_w * r / (1.0 + r)
    e1 = g_idx * EXPERTS_PER_GROUP + i1
    e2 = g_idx * EXPERTS_PER_GROUP + i2

    xiota = lax.broadcasted_iota(jnp.int32, (N_EXPERTS, n), 0)
    oh1 = xiota == e1
    oh2 = xiota == e2
    cnt = jnp.where(oh1, 1.0, 0.0) + jnp.where(oh2, 1.0, 0.0)
    before = _dot(cnt.astype(MXU_DTYPE), tri_ref[...])
    rank1 = jnp.sum(jnp.where(oh1, before, 0.0), axis=0, keepdims=True)
    rank2 = jnp.sum(jnp.where(oh2, before, 0.0), axis=0, keepdims=True)
    for j in range(n // SEQ_TILE):
        tile_cnt = jnp.sum(cnt[:, j * SEQ_TILE:(j + 1) * SEQ_TILE], axis=1, keepdims=True)
        cnt_ref[j] = jnp.broadcast_to(tile_cnt, (N_EXPERTS, LANES))

    zero = jnp.zeros((1, n), F32)
    rec = jnp.concatenate([e1.astype(F32), e2.astype(F32), w1, w2, rank1, rank2, zero, zero], axis=0)
    rl_ref[...] = rec
    rec_full = jnp.concatenate([rec, jnp.zeros((LANES - ROUTE_ROWS, n), F32)], axis=0)
    rr_ref[...] = rec_full.T


def _memattn(h1, kv, B, S, wq, wo, ln_g, ln_b, wr_t, br_col, tri):
    T = B * S
    n = MEM_TILE
    nt = S // n
    sub = n // SEQ_TILE
    row = lambda b, i: (b * nt + i, 0)
    c2 = lambda b, i: (0, 0)
    one = pl.Buffered(1)
    return pl.pallas_call(
        _memattn_kernel,
        grid=(B, nt),
        in_specs=[
            pl.BlockSpec((n, D_MODEL), row),
            pl.BlockSpec((kv.shape[0] // B, 2 * D_MODEL), lambda b, i: (b, 0)),
            pl.BlockSpec((D_MODEL, D_MODEL), c2, pipeline_mode=one),
            pl.BlockSpec((D_MODEL, D_MODEL), c2, pipeline_mode=one),
            pl.BlockSpec((1, D_MODEL), c2),
            pl.BlockSpec((1, D_MODEL), c2),
            pl.BlockSpec((ROUTER_ROWS, D_MODEL), c2, pipeline_mode=one),
            pl.BlockSpec((ROUTER_ROWS, 1), c2),
            pl.BlockSpec((n, n), c2, pipeline_mode=one),
        ],
        out_specs=[
            pl.BlockSpec((n, D_MODEL), row),
            pl.BlockSpec((n, D_MODEL), row),
            pl.BlockSpec((ROUTE_ROWS, n), lambda b, i: (0, b * nt + i)),
            pl.BlockSpec((n, LANES), row),
            pl.BlockSpec((sub, N_EXPERTS, LANES), lambda b, i: (b * nt + i, 0, 0)),
        ],
        out_shape=[
            jax.ShapeDtypeStruct((T, D_MODEL), F32),
            jax.ShapeDtypeStruct((T, D_MODEL), MXU_DTYPE),
            jax.ShapeDtypeStruct((ROUTE_ROWS, T), F32),
            jax.ShapeDtypeStruct((T, LANES), F32),
            jax.ShapeDtypeStruct((T // SEQ_TILE, N_EXPERTS, LANES), F32),
        ],
        scratch_shapes=[pltpu.VMEM((n, D_MODEL), MXU_DTYPE)],
        compiler_params=_params("parallel", "parallel"),
        name="memattn_router",
    )(h1, kv, wq, wo, ln_g, ln_b, wr_t, br_col, tri)


CHUNK_ROWS = SUBLANES
LOCAL_ROWS = 768
N_CHUNKS = LOCAL_ROWS // CHUNK_ROWS
MIN_CHUNKS = TOP_K * SEQ_TILE // CHUNK_ROWS
assert TOP_K * SEQ_TILE + N_EXPERTS * (CHUNK_ROWS - 1) <= LOCAL_ROWS


def _unpack_rows(packed):
    lo = lax.bitcast_convert_type(packed << 16, F32)
    hi = lax.bitcast_convert_type(packed & jnp.uint32(0xFFFF0000), F32)
    return lo, hi


def _pack_rows(y):
    bits = lax.bitcast_convert_type(y.astype(jnp.bfloat16).astype(F32), jnp.uint32)
    return (bits[:, :PACK_COLS] >> 16) | bits[:, PACK_COLS:]


def _chunk_plan(counts, n_blocks):
    nt = counts.shape[0]
    c = counts.astype(jnp.int32)
    incl = jnp.asarray(np.tril(np.ones((N_EXPERTS, N_EXPERTS), bool)))
    padc = (c + CHUNK_ROWS - 1) // CHUNK_ROWS * CHUNK_ROWS
    lend = jnp.sum(jnp.where(incl[None], padc[:, None, :], 0), axis=2)
    loff = lend - padc
    earlier = jnp.asarray(np.tril(np.ones((nt, nt), bool), -1))
    cbefore = jnp.sum(jnp.where(earlier[:, :, None], padc[None, :, :], 0), axis=1)
    total = jnp.sum(padc, axis=0)
    region = (total + EXPERT_BLOCK - 1) // EXPERT_BLOCK * EXPERT_BLOCK
    gend = jnp.sum(jnp.where(incl, region[None, :], 0), axis=1)
    gstart = gend - region
    delta = gstart[None, :] + cbefore - loff
    k_row = jnp.arange(N_CHUNKS, dtype=jnp.int32) * CHUNK_ROWS
    e_of_chunk = jnp.sum((lend[:, None, :] <= k_row[None, :, None]).astype(jnp.int32), axis=2)
    onehot = e_of_chunk[:, :, None] == jnp.arange(N_EXPERTS, dtype=jnp.int32)[None, None, :]
    chunk_row = (jnp.sum(jnp.where(onehot, delta[:, None, :], 0), axis=2) + k_row[None, :]).reshape(-1)
    n_chunks = lend[:, -1] // CHUNK_ROWS
    block_first = jnp.arange(n_blocks, dtype=jnp.int32) * EXPERT_BLOCK
    block_expert = jnp.minimum(jnp.sum((gend[None, :] <= block_first[:, None]).astype(jnp.int32), axis=1),
                               N_EXPERTS - 1)
    n_used = (gend[-1:] // EXPERT_BLOCK).astype(jnp.int32)
    tail = jnp.concatenate([gstart + total, (region - total) // CHUNK_ROWS, n_used]).astype(jnp.int32)
    loff_f = loff.astype(F32)
    loff_col = jnp.broadcast_to(loff_f[:, :, None], (nt, N_EXPERTS, LANES))
    loff_row = jnp.broadcast_to(jnp.pad(loff_f, ((0, 0), (0, LANES - N_EXPERTS)))[:, None, :],
                                (nt, SUBLANES, LANES))
    return chunk_row, n_chunks, tail, block_expert, n_used, loff_col, loff_row


def _start_chunk_copies(table_ref, count_ref, tile, make_copy):
    base = tile * N_CHUNKS

    def one(k, prio):
        make_copy(k, pl.multiple_of(table_ref[base + k], CHUNK_ROWS)).start(priority=prio)

    def pair(k2, carry):
        for prio in range(2):
            one(2 * k2 + prio, prio)
        return carry

    def single(k, carry):
        one(k, 0)
        return carry

    lax.fori_loop(0, MIN_CHUNKS // 2, pair, 0, unroll=4)
    lax.fori_loop(MIN_CHUNKS, count_ref[tile], single, 0)


def _wait_chunk_copies(count_ref, tile, many, one):
    many.wait()

    def single(k, carry):
        one.wait()
        return carry

    lax.fori_loop(MIN_CHUNKS, count_ref[tile], single, 0)


def _dispatch_kernel(row_ref, cnt_ref, tail_ref, h_ref, rl_ref, loff_ref, xb_ref, sbuf, zbuf, sem, zsem):
    i = pl.program_id(0)
    n = SEQ_TILE
    xiota = lax.broadcasted_iota(jnp.int32, (N_EXPERTS, n), 0)
    riota = lax.broadcasted_iota(jnp.int32, (LOCAL_ROWS, n), 0)
    min_rows = MIN_CHUNKS * CHUNK_ROWS

    def wait_all(slot, tile):
        many = pltpu.make_async_copy(sbuf.at[slot, pl.ds(0, min_rows)], xb_ref.at[pl.ds(0, min_rows)], sem.at[slot])
        one = pltpu.make_async_copy(sbuf.at[slot, pl.ds(0, CHUNK_ROWS)], xb_ref.at[pl.ds(0, CHUNK_ROWS)],
                                    sem.at[slot])
        _wait_chunk_copies(cnt_ref, tile, many, one)

    def sorted_rows(t):
        rl = rl_ref[:, t * n:(t + 1) * n]
        loff = loff_ref[t][:, 0:1]
        slots = []
        for k in range(TOP_K):
            e = rl[k:k + 1].astype(jnp.int32)
            base = jnp.sum(jnp.where(xiota == e, loff, 0.0), axis=0, keepdims=True)
            slots.append((base + rl[4 + k:5 + k]).astype(jnp.int32))
        perm = jnp.where(riota == slots[0], 1.0, jnp.where(riota == slots[1], 1.0, 0.0)).astype(MXU_DTYPE)
        return _pack_rows(_dot(perm, h_ref[t * n:(t + 1) * n, :].astype(MXU_DTYPE)))

    packed = [sorted_rows(t) for t in range(DC_SUB)]
    parity = i % 2
    for t in range(DC_SUB):
        tile = i * DC_SUB + t
        buf = parity * DC_SUB + t

        @pl.when(i >= 2)
        def _():
            wait_all(buf, tile - 2 * DC_SUB)

        sbuf.at[buf][...] = packed[t]

        def chunk_copy(k, row):
            src = sbuf.at[buf, pl.ds(pl.multiple_of(k * CHUNK_ROWS, CHUNK_ROWS), CHUNK_ROWS)]
            return pltpu.make_async_copy(src, xb_ref.at[pl.ds(row, CHUNK_ROWS)], sem.at[buf])

        _start_chunk_copies(row_ref, cnt_ref, tile, chunk_copy)

    @pl.when(i == pl.num_programs(0) - 1)
    def _():
        zbuf[...] = jnp.zeros_like(zbuf)

        def fill(start_copy):
            def per_expert(e, carry):
                first = pl.multiple_of(tail_ref[e], CHUNK_ROWS)

                def per_chunk(c, inner):
                    row = pl.multiple_of(first + c * CHUNK_ROWS, CHUNK_ROWS)
                    copy = pltpu.make_async_copy(zbuf, xb_ref.at[pl.ds(row, CHUNK_ROWS)], zsem)
                    if start_copy:
                        copy.start()
                    else:
                        copy.wait()
                    return inner

                lax.fori_loop(0, tail_ref[N_EXPERTS + e], per_chunk, 0)
                return carry

            lax.fori_loop(0, N_EXPERTS, per_expert, 0)

        fill(True)
        for t in range(DC_SUB):
            wait_all((1 - parity) * DC_SUB + t, (i - 1) * DC_SUB + t)
            wait_all(parity * DC_SUB + t, i * DC_SUB + t)
        fill(False)

        zblock = sbuf.at[0, pl.ds(0, EXPERT_BLOCK)]
        zblock[...] = jnp.zeros_like(zblock)
        n_all = xb_ref.shape[0] // EXPERT_BLOCK

        def fill_blocks(start_copy):
            def per_block(p, carry):
                row = pl.multiple_of(p * EXPERT_BLOCK, EXPERT_BLOCK)
                copy = pltpu.make_async_copy(zblock, xb_ref.at[pl.ds(row, EXPERT_BLOCK)], zsem)
                if start_copy:
                    copy.start()
                else:
                    copy.wait()
                return carry

            lax.fori_loop(tail_ref[2 * N_EXPERTS], n_all, per_block, 0)

        fill_blocks(True)
        fill_blocks(False)


def _dispatch(chunk_row, n_chunks, tail, hb, rl, loff_col, n_blocks):
    T = hb.shape[0]
    n = DC_SUB * SEQ_TILE
    rows = n_blocks * EXPERT_BLOCK
    grid_spec = pltpu.PrefetchScalarGridSpec(
        num_scalar_prefetch=3,
        grid=(T // n,),
        in_specs=[
            pl.BlockSpec((n, D_MODEL), lambda i, r, c, t: (i, 0)),
            pl.BlockSpec((ROUTE_ROWS, n), lambda i, r, c, t: (0, i)),
            pl.BlockSpec((DC_SUB, N_EXPERTS, LANES), lambda i, r, c, t: (i, 0, 0)),
        ],
        out_specs=pl.BlockSpec(memory_space=pl.ANY),
        scratch_shapes=[pltpu.VMEM((2 * DC_SUB, LOCAL_ROWS, PACK_COLS), jnp.uint32),
                        pltpu.VMEM((CHUNK_ROWS, PACK_COLS), jnp.uint32),
                        pltpu.SemaphoreType.DMA((2 * DC_SUB,)), pltpu.SemaphoreType.DMA],
    )
    assert T % n == 0 and T // n >= 2 and LOCAL_ROWS >= EXPERT_BLOCK
    return pl.pallas_call(
        _dispatch_kernel,
        grid_spec=grid_spec,
        out_shape=jax.ShapeDtypeStruct((rows, PACK_COLS), jnp.uint32),
        compiler_params=_params("arbitrary"),
        name="dispatch",
    )(chunk_row, n_chunks, tail, hb, rl, loff_col)


def _expert_kernel(be_ref, nused_ref, x_ref, wg_ref, wu_ref, wd_ref, y_ref, wg_b, wu_b, wd_b):
    p = pl.program_id(0)
    new_expert = jnp.logical_or(p == 0, be_ref[p] != be_ref[jnp.maximum(p - 1, 0)])

    @pl.when(new_expert)
    def _():
        wg_b[...] = wg_ref[...].astype(MXU_DTYPE)
        wu_b[...] = wu_ref[...].astype(MXU_DTYPE)
        wd_b[...] = wd_ref[...].astype(MXU_DTYPE)

    @pl.when(p < nused_ref[0])
    def _():
        lo, hi = _unpack_rows(x_ref[...])
        x = jnp.concatenate([lo.astype(MXU_DTYPE), hi.astype(MXU_DTYPE)], axis=1)
        y = None
        half = D_EXPERT // 2
        for j in range(2):
            cols = slice(j * half, (j + 1) * half)
            gate = _dot(x, wg_b[:, cols])
            up = _dot(x, wu_b[:, cols])
            hid = (gate * _sigmoid(gate) * up).astype(MXU_DTYPE)
            part = _dot(hid, wd_b[cols, :])
            y = part if y is None else y + part
        y_ref[...] = _pack_rows(y)

    @pl.when(p >= nused_ref[0])
    def _():
        y_ref[...] = jnp.zeros_like(y_ref)


def _experts(block_expert, n_used, xb, wg, wu, wd, layer):
    nb = block_expert.shape[0]
    n_slots = nb * EXPERT_BLOCK
    grid_spec = pltpu.PrefetchScalarGridSpec(
        num_scalar_prefetch=2,
        grid=(nb,),
        in_specs=[
            pl.BlockSpec((EXPERT_BLOCK, PACK_COLS), lambda p, be, nu: (jnp.minimum(p, jnp.maximum(nu[0] - 1, 0)), 0)),
            pl.BlockSpec((None, None, D_MODEL, D_EXPERT), lambda p, be, nu: (layer, be[p], 0, 0)),
            pl.BlockSpec((None, None, D_MODEL, D_EXPERT), lambda p, be, nu: (layer, be[p], 0, 0)),
            pl.BlockSpec((None, None, D_EXPERT, D_MODEL), lambda p, be, nu: (layer, be[p], 0, 0)),
        ],
        out_specs=pl.BlockSpec((EXPERT_BLOCK, PACK_COLS), lambda p, be, nu: (p, 0)),
        scratch_shapes=[pltpu.VMEM((D_MODEL, D_EXPERT), MXU_DTYPE), pltpu.VMEM((D_MODEL, D_EXPERT), MXU_DTYPE),
                        pltpu.VMEM((D_EXPERT, D_MODEL), MXU_DTYPE)],
    )
    return pl.pallas_call(
        _expert_kernel,
        grid_spec=grid_spec,
        out_shape=jax.ShapeDtypeStruct((n_slots, PACK_COLS), jnp.uint32),
        compiler_params=_params("arbitrary"),
        name="experts",
    )(block_expert, n_used, xb, wg, wu, wd)


def _combine_rows(step, parity, is_first, next_step, has_next,
                  row_ref, cnt_ref, yb_ref, rr_ref, loff_ref, h_ref, g_ref, b_ref, ybuf, sem, emit,
                  filler=lambda stage: None):
    n = SEQ_TILE
    min_rows = MIN_CHUNKS * CHUNK_ROWS

    def fetch(tile, into):
        def chunk_copy(k, row):
            dst = ybuf.at[into, pl.ds(pl.multiple_of(k * CHUNK_ROWS, CHUNK_ROWS), CHUNK_ROWS)]
            return pltpu.make_async_copy(yb_ref.at[pl.ds(row, CHUNK_ROWS)], dst, sem.at[into])

        _start_chunk_copies(row_ref, cnt_ref, tile, chunk_copy)

    mine = parity * DC_SUB
    theirs = (1 - parity) * DC_SUB

    @pl.when(is_first)
    def _():
        ybuf[...] = jnp.zeros_like(ybuf)
        for t in range(DC_SUB):
            fetch(step * DC_SUB + t, mine + t)

    @pl.when(has_next)
    def _():
        for t in range(DC_SUB):
            fetch(next_step * DC_SUB + t, theirs + t)

    lane = lax.broadcasted_iota(jnp.int32, (n, LANES), 1)
    ciota = lax.broadcasted_iota(jnp.int32, (n, LOCAL_ROWS), 1)

    def weights(t):
        rr = rr_ref[t * n:(t + 1) * n, :]
        loff = loff_ref[t][0:1, :]
        mix = jnp.zeros((n, LOCAL_ROWS), F32)
        for k in range(TOP_K):
            e = rr[:, k:k + 1].astype(jnp.int32)
            base = jnp.sum(jnp.where(lane == e, loff, 0.0), axis=1, keepdims=True)
            sorted_row = (base + rr[:, 4 + k:5 + k]).astype(jnp.int32)
            mix = jnp.where(ciota == sorted_row, rr[:, 2 + k:3 + k], mix)
        return mix.astype(MXU_DTYPE)

    mixes = [weights(t) for t in range(DC_SUB)]
    filler(0)
    for t in range(DC_SUB):
        buf = mine + t
        many = pltpu.make_async_copy(yb_ref.at[pl.ds(0, min_rows)], ybuf.at[buf, pl.ds(0, min_rows)], sem.at[buf])
        one = pltpu.make_async_copy(yb_ref.at[pl.ds(0, CHUNK_ROWS)], ybuf.at[buf, pl.ds(0, CHUNK_ROWS)],
                                    sem.at[buf])
        _wait_chunk_copies(cnt_ref, step * DC_SUB + t, many, one)
        filler(1 + t)
        lo, hi = _unpack_rows(ybuf.at[buf][...])
        y = jnp.concatenate([_dot(mixes[t], lo.astype(MXU_DTYPE)), _dot(mixes[t], hi.astype(MXU_DTYPE))], axis=1)
        rows = slice(t * n, (t + 1) * n)
        emit(t, _layer_norm(DEEPNORM_ALPHA * h_ref[rows, :] + y, g_ref[...], b_ref[...]))


def _combine_kernel(row_ref, cnt_ref, yb_ref, rr_ref, loff_ref, h_ref, g_ref, b_ref, o_ref, ybuf, sem):
    i = pl.program_id(0)

    def emit(t, rows):
        o_ref[t * SEQ_TILE:(t + 1) * SEQ_TILE, :] = rows

    _combine_rows(i, i % 2, i == 0, i + 1, i + 1 < pl.num_programs(0),
                  row_ref, cnt_ref, yb_ref, rr_ref, loff_ref, h_ref, g_ref, b_ref, ybuf, sem, emit)


def _combine_inproj_kernel(row_ref, cnt_ref, yb_ref, rr_ref, loff_ref, h_ref, g_ref, b_ref,
                           w_ref, cos_ref, sin_ref, o_ref, z_ref, ybuf, hprev, sem):
    i = pl.program_id(0)
    n_real = pl.num_programs(0) - 1
    step = jnp.minimum(i, n_real - 1)

    @pl.when(i == 0)
    def _():
        hprev[...] = jnp.zeros_like(hprev)

    cur = i % 2
    xb = hprev.at[1 - cur][...].astype(MXU_DTYPE)

    def emit(t, rows):
        o_ref[t * SEQ_TILE:(t + 1) * SEQ_TILE, :] = rows
        hprev.at[cur][t * SEQ_TILE:(t + 1) * SEQ_TILE, :] = rows

    n_chunks = IN_TOTAL // PROJ_COLS
    bounds = [0, 5, 9, n_chunks]

    def filler(stage):
        _project_columns(xb, w_ref, cos_ref[...], sin_ref[...], z_ref, range(bounds[stage], bounds[stage + 1]))

    _combine_rows(step, cur, i == 0, jnp.minimum(i + 1, n_real - 1), i < n_real,
                  row_ref, cnt_ref, yb_ref, rr_ref, loff_ref, h_ref, g_ref, b_ref, ybuf, sem, emit, filler)


def _combine(chunk_row, n_chunks, yb, rr, loff_row, h2, ln_g, ln_b):
    T = h2.shape[0]
    n = DC_SUB * SEQ_TILE
    row = lambda i, r, c: (i, 0)
    grid_spec = pltpu.PrefetchScalarGridSpec(
        num_scalar_prefetch=2,
        grid=(T // n,),
        in_specs=[
            pl.BlockSpec(memory_space=pl.ANY),
            pl.BlockSpec((n, LANES), row),
            pl.BlockSpec((DC_SUB, SUBLANES, LANES), lambda i, r, c: (i, 0, 0)),
            pl.BlockSpec((n, D_MODEL), row),
            pl.BlockSpec((1, D_MODEL), lambda i, r, c: (0, 0)),
            pl.BlockSpec((1, D_MODEL), lambda i, r, c: (0, 0)),
        ],
        out_specs=pl.BlockSpec((n, D_MODEL), row),
        scratch_shapes=[pltpu.VMEM((2 * DC_SUB, LOCAL_ROWS, PACK_COLS), jnp.uint32),
                        pltpu.SemaphoreType.DMA((2 * DC_SUB,))],
    )
    return pl.pallas_call(
        _combine_kernel,
        grid_spec=grid_spec,
        out_shape=jax.ShapeDtypeStruct((T, D_MODEL), F32),
        compiler_params=_params("arbitrary"),
        name="combine",
    )(chunk_row, n_chunks, yb, rr, loff_row, h2, ln_g, ln_b)


def _combine_inproj(chunk_row, n_chunks, yb, rr, loff_row, h2, ln_g, ln_b, w_in_b, cos_t, sin_t):
    T = h2.shape[0]
    n = DC_SUB * SEQ_TILE
    assert n == PROJ_TILE
    steps = T // n
    cur = lambda i, r, c: (jnp.minimum(i, steps - 1), 0)
    prev = lambda i, r, c: (jnp.maximum(i - 1, 0), 0)
    const = lambda i, r, c: (0, 0)
    grid_spec = pltpu.PrefetchScalarGridSpec(
        num_scalar_prefetch=2,
        grid=(steps + 1,),
        in_specs=[
            pl.BlockSpec(memory_space=pl.ANY),
            pl.BlockSpec((n, LANES), cur),
            pl.BlockSpec((DC_SUB, SUBLANES, LANES), lambda i, r, c: (jnp.minimum(i, steps - 1), 0, 0)),
            pl.BlockSpec((n, D_MODEL), cur),
            pl.BlockSpec((1, D_MODEL), const),
            pl.BlockSpec((1, D_MODEL), const),
            pl.BlockSpec((D_MODEL, IN_TOTAL), const, pipeline_mode=pl.Buffered(1)),
            pl.BlockSpec((n, LANES), prev),
            pl.BlockSpec((n, LANES), prev),
        ],
        out_specs=[pl.BlockSpec((n, D_MODEL), cur), pl.BlockSpec((n, IN_TOTAL), prev)],
        scratch_shapes=[pltpu.VMEM((2 * DC_SUB, LOCAL_ROWS, PACK_COLS), jnp.uint32),
                        pltpu.VMEM((2, n, D_MODEL), F32),
                        pltpu.SemaphoreType.DMA((2 * DC_SUB,))],
    )
    return pl.pallas_call(
        _combine_inproj_kernel,
        grid_spec=grid_spec,
        out_shape=[jax.ShapeDtypeStruct((T, D_MODEL), F32), jax.ShapeDtypeStruct((T, IN_TOTAL), MXU_DTYPE)],
        compiler_params=_params("arbitrary"),
        name="combine_inproj",
    )(chunk_row, n_chunks, yb, rr, loff_row, h2, ln_g, ln_b, w_in_b, cos_t, sin_t)


def _router_weights(w_group, b_group, w_route, b_route):
    wr = jnp.zeros((ROUTER_ROWS, D_MODEL), F32)
    wr = wr.at[GROUP_ROW0:GROUP_ROW0 + N_GROUPS].set(w_group.T)
    wr = wr.at[EXPERT_ROW0:EXPERT_ROW0 + N_EXPERTS].set(w_route.T)
    br = jnp.zeros((ROUTER_ROWS,), F32)
    br = br.at[GROUP_ROW0:GROUP_ROW0 + N_GROUPS].set(b_group)
    br = br.at[EXPERT_ROW0:EXPERT_ROW0 + N_EXPERTS].set(b_route.reshape(-1))
    return wr.astype(MXU_DTYPE), br.reshape(ROUTER_ROWS, 1)


def kernel(x, mem, positions, ln_in_g, ln_in_b, w_in, rel_bias, w_proj_ret, w_proj_att, w_out, ln1_g, ln1_b, w_q_mem, w_kv_mem, w_o_mem, ln2_g, ln2_b, w_group, b_group, w_route, b_route, w_gate, w_up, w_down, ln3_g, ln3_b):
    B, S, D = x.shape
    assert D == D_MODEL and S % PROJ_TILE == 0 and S % SEQ_TILE == 0 and S % MEM_TILE == 0
    T = B * S
    A = T * TOP_K
    n_blocks = -(-(A + (T // SEQ_TILE) * N_EXPERTS * (CHUNK_ROWS - 1) + N_EXPERTS * (EXPERT_BLOCK - 1)) // EXPERT_BLOCK)
    bf = MXU_DTYPE

    cos_t, sin_t = _rope_tables(positions)
    tables = _retention_tables()
    pos = np.arange(MEM_TILE)
    tri = jnp.asarray((pos[:, None] < pos[None, :]) & (pos[:, None] // SEQ_TILE == pos[None, :] // SEQ_TILE), bf)
    mem2d = mem.reshape(-1, D)
    row2 = lambda v: v.reshape(1, D)

    z, h = _inproj(x.reshape(T, D), row2(ln_in_g), row2(ln_in_b), _to_mxu(w_in, 0), cos_t, sin_t)
    for l in range(DEPTH):
        bias = _attention_bias(rel_bias[l])
        h = _mixer(z, h, B, S, tables, bias, _to_mxu(w_proj_ret, l), _to_mxu(w_proj_att, l),
                   _to_mxu(w_out, l), row2(ln1_g[l]), row2(ln1_b[l]))

        kv = _mem_kv(mem2d, _to_mxu(w_kv_mem, l))
        wr_t, br_col = _router_weights(w_group[l], b_group[l], w_route[l], b_route[l])
        h2, hb, rl, rr, counts = _memattn(h, kv, B, S, _to_mxu(w_q_mem, l), _to_mxu(w_o_mem, l),
                                          row2(ln2_g[l]), row2(ln2_b[l]), wr_t, br_col, tri)

        chunk_row, n_chunks, tail, block_expert, n_used, loff_col, loff_row = _chunk_plan(counts[:, :, 0], n_blocks)
        xb = _dispatch(chunk_row, n_chunks, tail, hb, rl, loff_col, n_blocks)
        yb = _experts(block_expert, n_used, xb, w_gate, w_up, w_down, l)
        if l + 1 < DEPTH:
            h, z = _combine_inproj(chunk_row, n_chunks, yb, rr, loff_row, h2, row2(ln3_g[l]), row2(ln3_b[l]),
                                   _to_mxu(w_in, l + 1), cos_t, sin_t)
        else:
            h = _combine(chunk_row, n_chunks, yb, rr, loff_row, h2, row2(ln3_g[l]), row2(ln3_b[l]))
    return h.reshape(B, S, D)
```

```python
import functools

import jax
import jax.numpy as jnp
import numpy as np
from jax import lax
from jax.experimental import pallas as pl
from jax.experimental.pallas import tpu as pltpu

D_MODEL = 1024
DEPTH = 2
CHUNK = 64
RET_HEADS = 8
RET_DK = 64
RET_DV = 128
RET_QK = RET_HEADS * RET_DK
RET_V = RET_HEADS * RET_DV
ROPE_BASE = 10000.0
ATT_HEADS = 8
ATT_DH = 64
ATT_W = ATT_HEADS * ATT_DH
LEFT_CHUNKS = 8
MAX_REL = 256
MEM_HEADS = 4
MEM_DH = D_MODEL // MEM_HEADS
N_GROUPS = 4
EXPERTS_PER_GROUP = 8
N_EXPERTS = N_GROUPS * EXPERTS_PER_GROUP
TOP_K = 2
D_EXPERT = 512
LN_EPS = 1e-5
DEEPNORM_ALPHA = (2.0 * DEPTH) ** 0.25
IN_TOTAL = 2 * RET_QK + 2 * RET_V + 3 * ATT_W + 2 * D_MODEL

LANES = 128
SUBLANES = 8
VMEM_LIMIT_BYTES = 56 * 1024 * 1024

SEQ_TILE = 256
MEM_TILE = 4 * SEQ_TILE
MIX_SUB = 2
DC_SUB = 2
PROJ_TILE = 512
PROJ_COLS = 512
EXPERT_BLOCK = 512
HEAD_PAIR = 2 * RET_DK
NEG_BIG = -1e30
LOG2E = 1.4426950408889634

F32 = jnp.float32
MXU_DTYPE = jnp.bfloat16
PACK_COLS = D_MODEL // 2

Z_QK = 0
Z_VR = Z_QK + 2 * RET_QK
Z_GR = Z_VR + RET_V
Z_QA = Z_GR + RET_V
Z_KA = Z_QA + ATT_W
Z_VA = Z_KA + ATT_W
Z_GATE_R = Z_VA + ATT_W
Z_GATE_A = Z_GATE_R + D_MODEL
GATE_HALF = D_MODEL // 2


def _layer_norm(x, g, b):
    mu = jnp.mean(x, axis=-1, keepdims=True)
    xc = x - mu
    var = jnp.mean(xc * xc, axis=-1, keepdims=True)
    return xc * lax.rsqrt(var + LN_EPS) * g + b


def _sigmoid(x):
    return 1.0 / (1.0 + jnp.exp(-x))


def _dot(a, b):
    return jnp.dot(a, b, preferred_element_type=F32)


def _dot_nt(a, b):
    return lax.dot_general(a, b, (((1,), (1,)), ((), ())), preferred_element_type=F32)


def _dot_tn(a, b):
    return lax.dot_general(a, b, (((0,), (0,)), ((), ())), preferred_element_type=F32)


def _params(*semantics):
    return pltpu.CompilerParams(dimension_semantics=semantics, vmem_limit_bytes=VMEM_LIMIT_BYTES)


def _cast_kernel(x_ref, o_ref):
    o_ref[...] = x_ref[...].astype(o_ref.dtype)


def _to_mxu(w_stack, layer):
    _, r, c = w_stack.shape
    tr = 256
    return pl.pallas_call(
        _cast_kernel,
        grid=(r // tr,),
        in_specs=[pl.BlockSpec((None, tr, c), lambda i: (layer, i, 0))],
        out_specs=pl.BlockSpec((tr, c), lambda i: (i, 0)),
        out_shape=jax.ShapeDtypeStruct((r, c), MXU_DTYPE),
        compiler_params=_params("parallel"),
        name="cast_weight",
    )(w_stack)


ROPE_HALF = RET_DK // 2
ROPE_PACK = LANES // ROPE_HALF


def _rope_kernel(pos_ref, invf_ref, cos_ref, sin_ref):
    ang = pos_ref[...] * invf_ref[...]
    cos = jnp.cos(ang)
    sin = jnp.sin(ang)
    rows = ang.shape[0]
    lane = lax.broadcasted_iota(jnp.int32, ang.shape, 1)
    group = lane // ROPE_HALF
    for q in range(ROPE_PACK):
        outs = []
        for x in (cos, sin):
            first = x if q == 0 else pltpu.roll(x, LANES - q * ROPE_HALF, 1)
            rep = first
            for g in range(1, ROPE_PACK):
                rep = jnp.where(group == g, pltpu.roll(first, g * ROPE_HALF, 1), rep)
            outs.append(rep)
        cos_ref[pl.ds(q, rows, stride=ROPE_PACK), :] = outs[0]
        sin_ref[pl.ds(q, rows, stride=ROPE_PACK), :] = jnp.where((lane % RET_DK) < ROPE_HALF, -outs[1], outs[1])


def _rope_tables(positions):
    T = positions.size
    inv_freq = 1.0 / (ROPE_BASE ** jnp.linspace(0.0, 1.0, ROPE_HALF, dtype=F32))
    invf = jnp.tile(inv_freq, ROPE_PACK).reshape(1, LANES)
    packed_rows = T // ROPE_PACK
    pos = jnp.repeat(positions.reshape(packed_rows, ROPE_PACK).astype(F32), ROPE_HALF, axis=1)
    tp = 256
    assert T % ROPE_PACK == 0 and packed_rows % tp == 0
    out_spec = pl.BlockSpec((tp * ROPE_PACK, LANES), lambda i: (i, 0))
    return pl.pallas_call(
        _rope_kernel,
        grid=(packed_rows // tp,),
        in_specs=[pl.BlockSpec((tp, LANES), lambda i: (i, 0)), pl.BlockSpec((1, LANES), lambda i: (0, 0))],
        out_specs=[out_spec, out_spec],
        out_shape=[jax.ShapeDtypeStruct((T, LANES), F32)] * 2,
        compiler_params=_params("parallel"),
        name="rope_tables",
    )(pos, invf)


def _rotary(a, cos, sin_signed):
    outs = []
    lane = lax.broadcasted_iota(jnp.int32, cos.shape, 1)
    first_half = (lane % RET_DK) < RET_DK // 2
    for g in range(a.shape[1] // LANES):
        x = a[:, g * LANES:(g + 1) * LANES]
        rot = jnp.where(first_half, pltpu.roll(x, LANES - RET_DK // 2, 1), pltpu.roll(x, RET_DK // 2, 1))
        outs.append(x * cos + rot * sin_signed)
    return jnp.concatenate(outs, axis=1)


def _project_columns(xb, w_ref, cos, sin, z_ref, chunks):
    for c in chunks:
        lo = c * PROJ_COLS
        acc = _dot(xb, w_ref[:, lo:lo + PROJ_COLS])
        if lo < 2 * RET_QK:
            acc = _rotary(acc, cos, sin)
            if lo >= RET_QK:
                acc = acc * (RET_DK ** -0.5)
        z_ref[:, lo:lo + PROJ_COLS] = acc.astype(z_ref.dtype)


def _inproj_kernel(h_ref, g_ref, b_ref, w_ref, cos_ref, sin_ref, z_ref, h_out):
    x = _layer_norm(h_ref[...], g_ref[...], b_ref[...])
    h_out[...] = x
    _project_columns(x.astype(MXU_DTYPE), w_ref, cos_ref[...], sin_ref[...], z_ref, range(IN_TOTAL // PROJ_COLS))


def _inproj(h, ln_g, ln_b, w_in_b, cos_t, sin_t):
    T = h.shape[0]
    tm = PROJ_TILE
    row = lambda i: (i, 0)
    const = lambda i: (0, 0)
    out_shape = [jax.ShapeDtypeStruct((T, IN_TOTAL), MXU_DTYPE), jax.ShapeDtypeStruct((T, D_MODEL), F32)]
    out_specs = [pl.BlockSpec((tm, IN_TOTAL), row), pl.BlockSpec((tm, D_MODEL), row)]
    res = pl.pallas_call(
        _inproj_kernel,
        grid=(T // tm,),
        in_specs=[
            pl.BlockSpec((tm, D_MODEL), row),
            pl.BlockSpec((1, D_MODEL), const),
            pl.BlockSpec((1, D_MODEL), const),
            pl.BlockSpec((D_MODEL, IN_TOTAL), const, pipeline_mode=pl.Buffered(1)),
            pl.BlockSpec((tm, LANES), row),
            pl.BlockSpec((tm, LANES), row),
        ],
        out_specs=out_specs,
        out_shape=out_shape,
        compiler_params=_params("parallel"),
        name="inproj",
    )(h, ln_g, ln_b, w_in_b, cos_t, sin_t)
    return res


def _retention_tables():
    n = SEQ_TILE
    log_g = np.log1p(-np.exp2(-5.0 - np.arange(RET_HEADS, dtype=np.float64)))
    i = np.arange(n)
    same = (i[:, None] // CHUNK) == (i[None, :] // CHUNK)
    earlier = (i[None, :] // CHUNK) < (i[:, None] // CHUNK)
    diff = i[:, None] - i[None, :]
    expo = np.where(same, np.abs(diff), np.where(earlier, diff, 0)).astype(np.float64)
    decay = np.exp(log_g[:, None, None] * expo) * (same | earlier)
    q_decay = np.exp(log_g[:, None] * (i + 1.0))
    k_decay = np.exp(log_g[:, None] * (n - 1.0 - i))
    tile_decay = np.exp(log_g * n)
    wide = lambda t: np.broadcast_to(t[:, :, None], t.shape + (HEAD_PAIR,))
    owner = (np.arange(HEAD_PAIR)[None, :] // RET_DK) == (np.arange(RET_HEADS)[:, None] % 2)
    k_table = wide(k_decay) * owner[:, None, :]
    return (jnp.asarray(decay, F32), jnp.asarray(wide(q_decay), F32), jnp.asarray(k_table, F32),
            tuple(float(v) for v in tile_decay))


def _attention_bias(rel_bias):
    n = SEQ_TILE
    width = 3 * n
    pad = LEFT_CHUNKS * CHUNK
    assert pad == 2 * n and n <= MAX_REL
    heads = rel_bias.shape[0]
    length = width + n
    near = rel_bias[:, MAX_REL - n:].astype(F32)
    far = jnp.broadcast_to(rel_bias[:, -1:].astype(F32), (heads, length - near.shape[1]))
    g_rev = jnp.concatenate([near, far], axis=1)[:, ::-1]
    rows = jnp.tile(g_rev, (1, n + 1))[:, :n * (length - 1)].reshape(heads, n, length - 1)
    bias = rows[:, :, n - 1:n - 1 + width]
    qi = np.arange(n)
    kj = np.arange(width)
    q_chunk = (pad + qi) // CHUNK
    k_chunk = kj // CHUNK
    in_band = (k_chunk[None, :] <= q_chunk[:, None]) & (k_chunk[None, :] >= q_chunk[:, None] - LEFT_CHUNKS)
    return jnp.where(jnp.asarray(in_band)[None], bias * LOG2E, NEG_BIG)


def _mixer_kernel(tile_decay,
                  qk_ref, vr_ref, gr_ref, gate_r0_ref, gate_r1_ref, gate_a0_ref, gate_a1_ref, qa_ref,
                  kp_ref, kc_ref, vp_ref, vc_ref,
                  h_ref, decay_ref, qdec_ref, kdec_ref, bias_ref,
                  wpr_ref, wpa_ref, wout_ref, g_ref, b_ref,
                  o_ref, state_ref, yr_ref, ya_ref):
    i = pl.program_id(1)

    @pl.when(i == 0)
    def _():
        state_ref[...] = jnp.zeros_like(state_ref)

    n = SEQ_TILE
    lane = lax.broadcasted_iota(jnp.int32, (n, HEAD_PAIR), 1)
    low_half = lane < RET_DK

    def head_lanes(odd):
        return low_half if odd == 0 else jnp.logical_not(low_half)

    def rows(t):
        return slice(t * n, (t + 1) * n)

    def retention_state_update(t, hd):
        p = hd // 2
        k2 = qk_ref[rows(t), RET_QK + p * HEAD_PAIR:RET_QK + (p + 1) * HEAD_PAIR]
        v = vr_ref[rows(t), hd * RET_DV:(hd + 1) * RET_DV]
        k_dec = (k2.astype(F32) * kdec_ref[hd]).astype(MXU_DTYPE)
        state_ref[hd] = state_ref[hd] * tile_decay[hd] + _dot_tn(k_dec, v)

    def retention_first(t, hd):
        p, odd = divmod(hd, 2)
        q2 = qk_ref[rows(t), p * HEAD_PAIR:(p + 1) * HEAD_PAIR]
        k2 = qk_ref[rows(t), RET_QK + p * HEAD_PAIR:RET_QK + (p + 1) * HEAD_PAIR]
        qm = jnp.where(head_lanes(odd), q2, jnp.zeros_like(q2))
        v = vr_ref[rows(t), hd * RET_DV:(hd + 1) * RET_DV]
        raw = _dot_nt(qm, k2)
        read = _dot(qm, state_ref[hd].astype(MXU_DTYPE))
        if t + 1 < MIX_SUB:
            retention_state_update(t, hd)
        return raw, read, v

    def retention_second(t, hd, raw, read, v):
        intra = _dot((raw * decay_ref[hd]).astype(MXU_DTYPE), v)
        ret = intra + qdec_ref[hd] * read
        mu = jnp.mean(ret, axis=-1, keepdims=True)
        rc = ret - mu
        var = jnp.mean(rc * rc, axis=-1, keepdims=True)
        gn = rc * lax.rsqrt(var + LN_EPS)
        gate = gr_ref[rows(t), hd * RET_DV:(hd + 1) * RET_DV].astype(F32)
        yr_ref[rows(t), hd * RET_DV:(hd + 1) * RET_DV] = (gate * _sigmoid(gate) * gn).astype(MXU_DTYPE)

    pen_prev = jnp.where(i >= 1, 0.0, NEG_BIG).astype(F32)

    def key_block(t, kb):
        j = t + kb + (MIX_SUB - 2)
        if j < MIX_SUB:
            return kp_ref, vp_ref, rows(j), pen_prev
        return kc_ref, vc_ref, rows(j - MIX_SUB), None

    def attention_first(t, hd):
        p, odd = divmod(hd, 2)
        cols = slice(p * HEAD_PAIR, (p + 1) * HEAD_PAIR)
        qs = (qa_ref[rows(t), cols].astype(F32) * (ATT_DH ** -0.5 * LOG2E)).astype(MXU_DTYPE)
        qm = jnp.where(head_lanes(odd), qs, jnp.zeros_like(qs))
        s = []
        for kb in range(3):
            k_ref, _, krows, pen = key_block(t, kb)
            sk = _dot_nt(qm, k_ref[krows, cols]) + bias_ref[hd, :, kb * n:(kb + 1) * n]
            if pen is not None:
                sk = sk + pen
            s.append(sk)
        return (s,)

    def attention_second(t, hd, s):
        p, odd = divmod(hd, 2)
        cols = slice(p * HEAD_PAIR, (p + 1) * HEAD_PAIR)
        m = jnp.maximum(jnp.maximum(jnp.max(s[0], axis=-1, keepdims=True),
                                    jnp.max(s[1], axis=-1, keepdims=True)),
                        jnp.max(s[2], axis=-1, keepdims=True))
        e = [jnp.exp2(sk - m) for sk in s]
        denom = (jnp.sum(e[0], axis=-1, keepdims=True) + jnp.sum(e[1], axis=-1, keepdims=True)
                 + jnp.sum(e[2], axis=-1, keepdims=True))
        pv = None
        for kb in range(3):
            _, v_ref, krows, _ = key_block(t, kb)
            term = _dot(e[kb].astype(MXU_DTYPE), v_ref[krows, cols])
            pv = term if pv is None else pv + term
        out = (pv / denom).astype(MXU_DTYPE)
        lo = odd * ATT_DH
        ya_ref[rows(t), p * HEAD_PAIR + lo:p * HEAD_PAIR + lo + ATT_DH] = out[:, lo:lo + ATT_DH]

    work = []
    for t in range(MIX_SUB):
        for hd in range(max(RET_HEADS, ATT_HEADS)):
            if hd < RET_HEADS:
                work.append((retention_first, retention_second, t, hd))
            if hd < ATT_HEADS:
                work.append((attention_first, attention_second, t, hd))
    ahead = 2
    pending = [first(t, hd) for first, _, t, hd in work[:ahead]]
    for j, (_, second, t, hd) in enumerate(work):
        if j + ahead < len(work):
            first, _, t_next, hd_next = work[j + ahead]
            pending.append(first(t_next, hd_next))
        second(t, hd, *pending.pop(0))

    pr = _dot(yr_ref[...], wpr_ref[...])
    pa = _dot(ya_ref[...], wpa_ref[...])
    gate_r = jnp.concatenate([gate_r0_ref[...], gate_r1_ref[...]], axis=1).astype(F32)
    gate_a = jnp.concatenate([gate_a0_ref[...], gate_a1_ref[...]], axis=1).astype(F32)
    merged = _sigmoid(gate_r) * pr + _sigmoid(gate_a) * pa
    mix = _dot(merged.astype(MXU_DTYPE), wout_ref[...])
    for hd in range(RET_HEADS):
        retention_state_update(MIX_SUB - 1, hd)
    o_ref[...] = _layer_norm(DEEPNORM_ALPHA * h_ref[...] + mix, g_ref[...], b_ref[...])


def _mixer(z, h, B, S, tables, bias, wpr, wpa, wout, ln_g, ln_b):
    decay, q_decay, k_decay, tile_decay = tables
    T = B * S
    n = SEQ_TILE
    step_rows = MIX_SUB * n
    nt = S // step_rows
    assert MIX_SUB >= 2 and S % step_rows == 0

    def zspec(width, col_off, back=0):
        cb = col_off // width
        return pl.BlockSpec((step_rows, width), lambda b, i: (b * nt + jnp.maximum(i - back, 0), cb))

    row = lambda b, i: (b * nt + i, 0)
    c2 = lambda b, i: (0, 0)
    c3 = lambda b, i: (0, 0, 0)
    one = pl.Buffered(1)
    in_specs = [
        zspec(2 * RET_QK, Z_QK), zspec(RET_V, Z_VR), zspec(RET_V, Z_GR),
        zspec(GATE_HALF, Z_GATE_R), zspec(GATE_HALF, Z_GATE_R + GATE_HALF),
        zspec(GATE_HALF, Z_GATE_A), zspec(GATE_HALF, Z_GATE_A + GATE_HALF), zspec(ATT_W, Z_QA),
        zspec(ATT_W, Z_KA, 1), zspec(ATT_W, Z_KA, 0),
        zspec(ATT_W, Z_VA, 1), zspec(ATT_W, Z_VA, 0),
        pl.BlockSpec((step_rows, D_MODEL), row),
        pl.BlockSpec((RET_HEADS, n, n), c3, pipeline_mode=one),
        pl.BlockSpec((RET_HEADS, n, HEAD_PAIR), c3, pipeline_mode=one),
        pl.BlockSpec((RET_HEADS, n, HEAD_PAIR), c3, pipeline_mode=one),
        pl.BlockSpec((ATT_HEADS, n, 3 * n), c3, pipeline_mode=one),
        pl.BlockSpec((RET_V, D_MODEL), c2, pipeline_mode=one),
        pl.BlockSpec((ATT_W, D_MODEL), c2, pipeline_mode=one),
        pl.BlockSpec((D_MODEL, D_MODEL), c2, pipeline_mode=one),
        pl.BlockSpec((1, D_MODEL), c2),
        pl.BlockSpec((1, D_MODEL), c2),
    ]
    return pl.pallas_call(
        functools.partial(_mixer_kernel, tile_decay),
        grid=(B, nt),
        in_specs=in_specs,
        out_specs=pl.BlockSpec((step_rows, D_MODEL), row),
        out_shape=jax.ShapeDtypeStruct((T, D_MODEL), F32),
        scratch_shapes=[
            pltpu.VMEM((RET_HEADS, HEAD_PAIR, RET_DV), F32),
            pltpu.VMEM((step_rows, RET_V), MXU_DTYPE),
            pltpu.VMEM((step_rows, ATT_W), MXU_DTYPE),
        ],
        compiler_params=_params("arbitrary", "arbitrary"),
        name="mixer",
    )(*([z] * 12), h, decay, q_decay, k_decay, bias, wpr, wpa, wout, ln_g, ln_b)


def _kv_kernel(m_ref, w_ref, o_ref):
    o_ref[...] = _dot(m_ref[...].astype(MXU_DTYPE), w_ref[...]).astype(o_ref.dtype)


def _mem_kv(mem2d, w_kv):
    M = mem2d.shape[0]
    tm = 256
    return pl.pallas_call(
        _kv_kernel,
        grid=(M // tm,),
        in_specs=[pl.BlockSpec((tm, D_MODEL), lambda i: (i, 0)),
                  pl.BlockSpec((D_MODEL, 2 * D_MODEL), lambda i: (0, 0), pipeline_mode=pl.Buffered(1))],
        out_specs=pl.BlockSpec((tm, 2 * D_MODEL), lambda i: (i, 0)),
        out_shape=jax.ShapeDtypeStruct((M, 2 * D_MODEL), MXU_DTYPE),
        compiler_params=_params("parallel"),
        name="mem_kv",
    )(mem2d, w_kv)


ROUTE_ROWS = 8
GROUP_ROW0 = 0
EXPERT_ROW0 = 8
ROUTER_ROWS = 128


def _memattn_kernel(h_ref, kv_ref, wq_ref, wo_ref, g_ref, b_ref, wr_ref, br_ref, tri_ref,
                    h2_ref, hb_ref, rl_ref, rr_ref, cnt_ref, o_scr):
    n = MEM_TILE
    h1 = h_ref[...]
    q = (_dot(h1.astype(MXU_DTYPE), wq_ref[...]) * (MEM_DH ** -0.5 * LOG2E)).astype(MXU_DTYPE)
    scores = [_dot_nt(q[:, hd * MEM_DH:(hd + 1) * MEM_DH], kv_ref[:, hd * MEM_DH:(hd + 1) * MEM_DH])
              for hd in range(MEM_HEADS)]
    for hd in range(MEM_HEADS):
        cols = slice(hd * MEM_DH, (hd + 1) * MEM_DH)
        s = scores[hd]
        m = jnp.max(s, axis=-1, keepdims=True)
        e = jnp.exp2(s - m)
        denom = jnp.sum(e, axis=-1, keepdims=True)
        v = kv_ref[:, D_MODEL + hd * MEM_DH:D_MODEL + (hd + 1) * MEM_DH]
        o_scr[:, cols] = (_dot(e.astype(MXU_DTYPE), v) / denom).astype(MXU_DTYPE)
    cross = _dot(o_scr[...], wo_ref[...])
    h2 = _layer_norm(DEEPNORM_ALPHA * h1 + cross, g_ref[...], b_ref[...])
    h2_ref[...] = h2
    hb = h2.astype(MXU_DTYPE)
    hb_ref[...] = hb

    logits = _dot_nt(wr_ref[...], hb) + br_ref[...]
    glog = logits[GROUP_ROW0:GROUP_ROW0 + N_GROUPS]
    gmax = jnp.max(glog, axis=0, keepdims=True)
    g_w = 1.0 / jnp.sum(jnp.exp(glog - gmax), axis=0, keepdims=True)
    giota = lax.broadcasted_iota(jnp.int32, glog.shape, 0)
    g_idx = jnp.min(jnp.where(glog == gmax, giota, N_GROUPS), axis=0, keepdims=True)
    el = jnp.zeros((EXPERTS_PER_GROUP, n), F32)
    for g in range(N_GROUPS):
        r0 = EXPERT_ROW0 + g * EXPERTS_PER_GROUP
        el = jnp.where(g_idx == g, logits[r0:r0 + EXPERTS_PER_GROUP], el)
    eiota = lax.broadcasted_iota(jnp.int32, el.shape, 0)
    m1 = jnp.max(el, axis=0, keepdims=True)
    i1 = jnp.min(jnp.where(el == m1, eiota, EXPERTS_PER_GROUP), axis=0, keepdims=True)
    el2 = jnp.where(eiota == i1, -jnp.inf, el)
    m2 = jnp.max(el2, axis=0, keepdims=True)
    i2 = jnp.min(jnp.where(el2 == m2, eiota, EXPERTS_PER_GROUP), axis=0, keepdims=True)
    r = jnp.exp(m2 - m1)
    w1 = g_w / (1.0 + r)
    w2 = g_w * r / (1.0 + r)
    e1 = g_idx * EXPERTS_PER_GROUP + i1
    e2 = g_idx * EXPERTS_PER_GROUP + i2

    xiota = lax.broadcasted_iota(jnp.int32, (N_EXPERTS, n), 0)
    oh1 = xiota == e1
    oh2 = xiota == e2
    cnt = jnp.where(oh1, 1.0, 0.0) + jnp.where(oh2, 1.0, 0.0)
    before = _dot(cnt.astype(MXU_DTYPE), tri_ref[...])
    rank1 = jnp.sum(jnp.where(oh1, before, 0.0), axis=0, keepdims=True)
    rank2 = jnp.sum(jnp.where(oh2, before, 0.0), axis=0, keepdims=True)
    for j in range(n // SEQ_TILE):
        tile_cnt = jnp.sum(cnt[:, j * SEQ_TILE:(j + 1) * SEQ_TILE], axis=1, keepdims=True)
        cnt_ref[j] = jnp.broadcast_to(tile_cnt, (N_EXPERTS, LANES))

    zero = jnp.zeros((1, n), F32)
    rec = jnp.concatenate([e1.astype(F32), e2.astype(F32), w1, w2, rank1, rank2, zero, zero], axis=0)
    rl_ref[...] = rec
    rec_full = jnp.concatenate([rec, jnp.zeros((LANES - ROUTE_ROWS, n), F32)], axis=0)
    rr_ref[...] = rec_full.T


def _memattn(h1, kv, B, S, wq, wo, ln_g, ln_b, wr_t, br_col, tri):
    T = B * S
    n = MEM_TILE
    nt = S // n
    sub = n // SEQ_TILE
    row = lambda b, i: (b * nt + i, 0)
    c2 = lambda b, i: (0, 0)
    one = pl.Buffered(1)
    return pl.pallas_call(
        _memattn_kernel,
        grid=(B, nt),
        in_specs=[
            pl.BlockSpec((n, D_MODEL), row),
            pl.BlockSpec((kv.shape[0] // B, 2 * D_MODEL), lambda b, i: (b, 0)),
            pl.BlockSpec((D_MODEL, D_MODEL), c2, pipeline_mode=one),
            pl.BlockSpec((D_MODEL, D_MODEL), c2, pipeline_mode=one),
            pl.BlockSpec((1, D_MODEL), c2),
            pl.BlockSpec((1, D_MODEL), c2),
            pl.BlockSpec((ROUTER_ROWS, D_MODEL), c2, pipeline_mode=one),
            pl.BlockSpec((ROUTER_ROWS, 1), c2),
            pl.BlockSpec((n, n), c2, pipeline_mode=one),
        ],
        out_specs=[
            pl.BlockSpec((n, D_MODEL), row),
            pl.BlockSpec((n, D_MODEL), row),
            pl.BlockSpec((ROUTE_ROWS, n), lambda b, i: (0, b * nt + i)),
            pl.BlockSpec((n, LANES), row),
            pl.BlockSpec((sub, N_EXPERTS, LANES), lambda b, i: (b * nt + i, 0, 0)),
        ],
        out_shape=[
            jax.ShapeDtypeStruct((T, D_MODEL), F32),
            jax.ShapeDtypeStruct((T, D_MODEL), MXU_DTYPE),
            jax.ShapeDtypeStruct((ROUTE_ROWS, T), F32),
            jax.ShapeDtypeStruct((T, LANES), F32),
            jax.ShapeDtypeStruct((T // SEQ_TILE, N_EXPERTS, LANES), F32),
        ],
        scratch_shapes=[pltpu.VMEM((n, D_MODEL), MXU_DTYPE)],
        compiler_params=_params("parallel", "parallel"),
        name="memattn_router",
    )(h1, kv, wq, wo, ln_g, ln_b, wr_t, br_col, tri)


CHUNK_ROWS = SUBLANES
LOCAL_ROWS = 768
N_CHUNKS = LOCAL_ROWS // CHUNK_ROWS
MIN_CHUNKS = TOP_K * SEQ_TILE // CHUNK_ROWS
assert TOP_K * SEQ_TILE + N_EXPERTS * (CHUNK_ROWS - 1) <= LOCAL_ROWS


def _unpack_rows(packed):
    lo = lax.bitcast_convert_type(packed << 16, F32)
    hi = lax.bitcast_convert_type(packed & jnp.uint32(0xFFFF0000), F32)
    return lo, hi


def _pack_rows(y):
    bits = lax.bitcast_convert_type(y.astype(jnp.bfloat16).astype(F32), jnp.uint32)
    return (bits[:, :PACK_COLS] >> 16) | bits[:, PACK_COLS:]


def _chunk_plan(counts, n_blocks):
    nt = counts.shape[0]
    c = counts.astype(jnp.int32)
    incl = jnp.asarray(np.tril(np.ones((N_EXPERTS, N_EXPERTS), bool)))
    padc = (c + CHUNK_ROWS - 1) // CHUNK_ROWS * CHUNK_ROWS
    lend = jnp.sum(jnp.where(incl[None], padc[:, None, :], 0), axis=2)
    loff = lend - padc
    earlier = jnp.asarray(np.tril(np.ones((nt, nt), bool), -1))
    cbefore = jnp.sum(jnp.where(earlier[:, :, None], padc[None, :, :], 0), axis=1)
    total = jnp.sum(padc, axis=0)
    region = (total + EXPERT_BLOCK - 1) // EXPERT_BLOCK * EXPERT_BLOCK
    gend = jnp.sum(jnp.where(incl, region[None, :], 0), axis=1)
    gstart = gend - region
    delta = gstart[None, :] + cbefore - loff
    k_row = jnp.arange(N_CHUNKS, dtype=jnp.int32) * CHUNK_ROWS
    e_of_chunk = jnp.sum((lend[:, None, :] <= k_row[None, :, None]).astype(jnp.int32), axis=2)
    onehot = e_of_chunk[:, :, None] == jnp.arange(N_EXPERTS, dtype=jnp.int32)[None, None, :]
    chunk_row = (jnp.sum(jnp.where(onehot, delta[:, None, :], 0), axis=2) + k_row[None, :]).reshape(-1)
    n_chunks = lend[:, -1] // CHUNK_ROWS
    block_first = jnp.arange(n_blocks, dtype=jnp.int32) * EXPERT_BLOCK
    block_expert = jnp.minimum(jnp.sum((gend[None, :] <= block_first[:, None]).astype(jnp.int32), axis=1),
                               N_EXPERTS - 1)
    n_used = (gend[-1:] // EXPERT_BLOCK).astype(jnp.int32)
    tail = jnp.concatenate([gstart + total, (region - total) // CHUNK_ROWS, n_used]).astype(jnp.int32)
    loff_f = loff.astype(F32)
    loff_col = jnp.broadcast_to(loff_f[:, :, None], (nt, N_EXPERTS, LANES))
    loff_row = jnp.broadcast_to(jnp.pad(loff_f, ((0, 0), (0, LANES - N_EXPERTS)))[:, None, :],
                                (nt, SUBLANES, LANES))
    return chunk_row, n_chunks, tail, block_expert, n_used, loff_col, loff_row


def _start_chunk_copies(table_ref, count_ref, tile, make_copy):
    base = tile * N_CHUNKS

    def one(k, prio):
        make_copy(k, pl.multiple_of(table_ref[base + k], CHUNK_ROWS)).start(priority=prio)

    def pair(k2, carry):
        for prio in range(2):
            one(2 * k2 + prio, prio)
        return carry

    def single(k, carry):
        one(k, 0)
        return carry

    lax.fori_loop(0, MIN_CHUNKS // 2, pair, 0, unroll=4)
    lax.fori_loop(MIN_CHUNKS, count_ref[tile], single, 0)


def _wait_chunk_copies(count_ref, tile, many, one):
    many.wait()

    def single(k, carry):
        one.wait()
        return carry

    lax.fori_loop(MIN_CHUNKS, count_ref[tile], single, 0)


def _dispatch_kernel(row_ref, cnt_ref, tail_ref, h_ref, rl_ref, loff_ref, xb_ref, sbuf, zbuf, sem, zsem):
    i = pl.program_id(0)
    n = SEQ_TILE
    xiota = lax.broadcasted_iota(jnp.int32, (N_EXPERTS, n), 0)
    riota = lax.broadcasted_iota(jnp.int32, (LOCAL_ROWS, n), 0)
    min_rows = MIN_CHUNKS * CHUNK_ROWS

    def wait_all(slot, tile):
        many = pltpu.make_async_copy(sbuf.at[slot, pl.ds(0, min_rows)], xb_ref.at[pl.ds(0, min_rows)], sem.at[slot])
        one = pltpu.make_async_copy(sbuf.at[slot, pl.ds(0, CHUNK_ROWS)], xb_ref.at[pl.ds(0, CHUNK_ROWS)],
                                    sem.at[slot])
        _wait_chunk_copies(cnt_ref, tile, many, one)

    def sorted_rows(t):
        rl = rl_ref[:, t * n:(t + 1) * n]
        loff = loff_ref[t][:, 0:1]
        slots = []
        for k in range(TOP_K):
            e = rl[k:k + 1].astype(jnp.int32)
            base = jnp.sum(jnp.where(xiota == e, loff, 0.0), axis=0, keepdims=True)
            slots.append((base + rl[4 + k:5 + k]).astype(jnp.int32))
        perm = jnp.where(riota == slots[0], 1.0, jnp.where(riota == slots[1], 1.0, 0.0)).astype(MXU_DTYPE)
        return _pack_rows(_dot(perm, h_ref[t * n:(t + 1) * n, :].astype(MXU_DTYPE)))

    packed = [sorted_rows(t) for t in range(DC_SUB)]
    parity = i % 2
    for t in range(DC_SUB):
        tile = i * DC_SUB + t
        buf = parity * DC_SUB + t

        @pl.when(i >= 2)
        def _():
            wait_all(buf, tile - 2 * DC_SUB)

        sbuf.at[buf][...] = packed[t]

        def chunk_copy(k, row):
            src = sbuf.at[buf, pl.ds(pl.multiple_of(k * CHUNK_ROWS, CHUNK_ROWS), CHUNK_ROWS)]
            return pltpu.make_async_copy(src, xb_ref.at[pl.ds(row, CHUNK_ROWS)], sem.at[buf])

        _start_chunk_copies(row_ref, cnt_ref, tile, chunk_copy)

    @pl.when(i == pl.num_programs(0) - 1)
    def _():
        zbuf[...] = jnp.zeros_like(zbuf)

        def fill(start_copy):
            def per_expert(e, carry):
                first = pl.multiple_of(tail_ref[e], CHUNK_ROWS)

                def per_chunk(c, inner):
                    row = pl.multiple_of(first + c * CHUNK_ROWS, CHUNK_ROWS)
                    copy = pltpu.make_async_copy(zbuf, xb_ref.at[pl.ds(row, CHUNK_ROWS)], zsem)
                    if start_copy:
                        copy.start()
                    else:
                        copy.wait()
                    return inner

                lax.fori_loop(0, tail_ref[N_EXPERTS + e], per_chunk, 0)
                return carry

            lax.fori_loop(0, N_EXPERTS, per_expert, 0)

        fill(True)
        for t in range(DC_SUB):
            wait_all((1 - parity) * DC_SUB + t, (i - 1) * DC_SUB + t)
            wait_all(parity * DC_SUB + t, i * DC_SUB + t)
        fill(False)

        zblock = sbuf.at[0, pl.ds(0, EXPERT_BLOCK)]
        zblock[...] = jnp.zeros_like(zblock)
        n_all = xb_ref.shape[0] // EXPERT_BLOCK

        def fill_blocks(start_copy):
            def per_block(p, carry):
                row = pl.multiple_of(p * EXPERT_BLOCK, EXPERT_BLOCK)
                copy = pltpu.make_async_copy(zblock, xb_ref.at[pl.ds(row, EXPERT_BLOCK)], zsem)
                if start_copy:
                    copy.start()
                else:
                    copy.wait()
                return carry

            lax.fori_loop(tail_ref[2 * N_EXPERTS], n_all, per_block, 0)

        fill_blocks(True)
        fill_blocks(False)


def _dispatch(chunk_row, n_chunks, tail, hb, rl, loff_col, n_blocks):
    T = hb.shape[0]
    n = DC_SUB * SEQ_TILE
    rows = n_blocks * EXPERT_BLOCK
    grid_spec = pltpu.PrefetchScalarGridSpec(
        num_scalar_prefetch=3,
        grid=(T // n,),
        in_specs=[
            pl.BlockSpec((n, D_MODEL), lambda i, r, c, t: (i, 0)),
            pl.BlockSpec((ROUTE_ROWS, n), lambda i, r, c, t: (0, i)),
            pl.BlockSpec((DC_SUB, N_EXPERTS, LANES), lambda i, r, c, t: (i, 0, 0)),
        ],
        out_specs=pl.BlockSpec(memory_space=pl.ANY),
        scratch_shapes=[pltpu.VMEM((2 * DC_SUB, LOCAL_ROWS, PACK_COLS), jnp.uint32),
                        pltpu.VMEM((CHUNK_ROWS, PACK_COLS), jnp.uint32),
                        pltpu.SemaphoreType.DMA((2 * DC_SUB,)), pltpu.SemaphoreType.DMA],
    )
    assert T % n == 0 and T // n >= 2 and LOCAL_ROWS >= EXPERT_BLOCK
    return pl.pallas_call(
        _dispatch_kernel,
        grid_spec=grid_spec,
        out_shape=jax.ShapeDtypeStruct((rows, PACK_COLS), jnp.uint32),
        compiler_params=_params("arbitrary"),
        name="dispatch",
    )(chunk_row, n_chunks, tail, hb, rl, loff_col)


def _expert_kernel(be_ref, nused_ref, x_ref, wg_ref, wu_ref, wd_ref, y_ref, wg_b, wu_b, wd_b):
    p = pl.program_id(0)
    new_expert = jnp.logical_or(p == 0, be_ref[p] != be_ref[jnp.maximum(p - 1, 0)])

    @pl.when(new_expert)
    def _():
        wg_b[...] = wg_ref[...].astype(MXU_DTYPE)
        wu_b[...] = wu_ref[...].astype(MXU_DTYPE)
        wd_b[...] = wd_ref[...].astype(MXU_DTYPE)

    @pl.when(p < nused_ref[0])
    def _():
        lo, hi = _unpack_rows(x_ref[...])
        x = jnp.concatenate([lo.astype(MXU_DTYPE), hi.astype(MXU_DTYPE)], axis=1)
        y = None
        half = D_EXPERT // 2
        for j in range(2):
            cols = slice(j * half, (j + 1) * half)
            gate = _dot(x, wg_b[:, cols])
            up = _dot(x, wu_b[:, cols])
            hid = (gate * _sigmoid(gate) * up).astype(MXU_DTYPE)
            part = _dot(hid, wd_b[cols, :])
            y = part if y is None else y + part
        y_ref[...] = _pack_rows(y)

    @pl.when(p >= nused_ref[0])
    def _():
        y_ref[...] = jnp.zeros_like(y_ref)


def _experts(block_expert, n_used, xb, wg, wu, wd, layer):
    nb = block_expert.shape[0]
    n_slots = nb * EXPERT_BLOCK
    grid_spec = pltpu.PrefetchScalarGridSpec(
        num_scalar_prefetch=2,
        grid=(nb,),
        in_specs=[
            pl.BlockSpec((EXPERT_BLOCK, PACK_COLS), lambda p, be, nu: (jnp.minimum(p, jnp.maximum(nu[0] - 1, 0)), 0)),
            pl.BlockSpec((None, None, D_MODEL, D_EXPERT), lambda p, be, nu: (layer, be[p], 0, 0)),
            pl.BlockSpec((None, None, D_MODEL, D_EXPERT), lambda p, be, nu: (layer, be[p], 0, 0)),
            pl.BlockSpec((None, None, D_EXPERT, D_MODEL), lambda p, be, nu: (layer, be[p], 0, 0)),
        ],
        out_specs=pl.BlockSpec((EXPERT_BLOCK, PACK_COLS), lambda p, be, nu: (p, 0)),
        scratch_shapes=[pltpu.VMEM((D_MODEL, D_EXPERT), MXU_DTYPE), pltpu.VMEM((D_MODEL, D_EXPERT), MXU_DTYPE),
                        pltpu.VMEM((D_EXPERT, D_MODEL), MXU_DTYPE)],
    )
    return pl.pallas_call(
        _expert_kernel,
        grid_spec=grid_spec,
        out_shape=jax.ShapeDtypeStruct((n_slots, PACK_COLS), jnp.uint32),
        compiler_params=_params("arbitrary"),
        name="experts",
    )(block_expert, n_used, xb, wg, wu, wd)


def _combine_rows(step, parity, is_first, next_step, has_next,
                  row_ref, cnt_ref, yb_ref, rr_ref, loff_ref, h_ref, g_ref, b_ref, ybuf, sem, emit,
                  filler=lambda stage: None):
    n = SEQ_TILE
    min_rows = MIN_CHUNKS * CHUNK_ROWS

    def fetch(tile, into):
        def chunk_copy(k, row):
            dst = ybuf.at[into, pl.ds(pl.multiple_of(k * CHUNK_ROWS, CHUNK_ROWS), CHUNK_ROWS)]
            return pltpu.make_async_copy(yb_ref.at[pl.ds(row, CHUNK_ROWS)], dst, sem.at[into])

        _start_chunk_copies(row_ref, cnt_ref, tile, chunk_copy)

    mine = parity * DC_SUB
    theirs = (1 - parity) * DC_SUB

    @pl.when(is_first)
    def _():
        ybuf[...] = jnp.zeros_like(ybuf)
        for t in range(DC_SUB):
            fetch(step * DC_SUB + t, mine + t)

    @pl.when(has_next)
    def _():
        for t in range(DC_SUB):
            fetch(next_step * DC_SUB + t, theirs + t)

    lane = lax.broadcasted_iota(jnp.int32, (n, LANES), 1)
    ciota = lax.broadcasted_iota(jnp.int32, (n, LOCAL_ROWS), 1)

    def weights(t):
        rr = rr_ref[t * n:(t + 1) * n, :]
        loff = loff_ref[t][0:1, :]
        mix = jnp.zeros((n, LOCAL_ROWS), F32)
        for k in range(TOP_K):
            e = rr[:, k:k + 1].astype(jnp.int32)
            base = jnp.sum(jnp.where(lane == e, loff, 0.0), axis=1, keepdims=True)
            sorted_row = (base + rr[:, 4 + k:5 + k]).astype(jnp.int32)
            mix = jnp.where(ciota == sorted_row, rr[:, 2 + k:3 + k], mix)
        return mix.astype(MXU_DTYPE)

    mixes = [weights(t) for t in range(DC_SUB)]
    filler(0)
    for t in range(DC_SUB):
        buf = mine + t
        many = pltpu.make_async_copy(yb_ref.at[pl.ds(0, min_rows)], ybuf.at[buf, pl.ds(0, min_rows)], sem.at[buf])
        one = pltpu.make_async_copy(yb_ref.at[pl.ds(0, CHUNK_ROWS)], ybuf.at[buf, pl.ds(0, CHUNK_ROWS)],
                                    sem.at[buf])
        _wait_chunk_copies(cnt_ref, step * DC_SUB + t, many, one)
        filler(1 + t)
        lo, hi = _unpack_rows(ybuf.at[buf][...])
        y = jnp.concatenate([_dot(mixes[t], lo.astype(MXU_DTYPE)), _dot(mixes[t], hi.astype(MXU_DTYPE))], axis=1)
        rows = slice(t * n, (t + 1) * n)
        emit(t, _layer_norm(DEEPNORM_ALPHA * h_ref[rows, :] + y, g_ref[...], b_ref[...]))


def _combine_kernel(row_ref, cnt_ref, yb_ref, rr_ref, loff_ref, h_ref, g_ref, b_ref, o_ref, ybuf, sem):
    i = pl.program_id(0)

    def emit(t, rows):
        o_ref[t * SEQ_TILE:(t + 1) * SEQ_TILE, :] = rows

    _combine_rows(i, i % 2, i == 0, i + 1, i + 1 < pl.num_programs(0),
                  row_ref, cnt_ref, yb_ref, rr_ref, loff_ref, h_ref, g_ref, b_ref, ybuf, sem, emit)


def _combine_inproj_kernel(row_ref, cnt_ref, yb_ref, rr_ref, loff_ref, h_ref, g_ref, b_ref,
                           w_ref, cos_ref, sin_ref, o_ref, z_ref, ybuf, hprev, sem):
    i = pl.program_id(0)
    n_real = pl.num_programs(0) - 1
    step = jnp.minimum(i, n_real - 1)

    @pl.when(i == 0)
    def _():
        hprev[...] = jnp.zeros_like(hprev)

    cur = i % 2
    xb = hprev.at[1 - cur][...].astype(MXU_DTYPE)

    def emit(t, rows):
        o_ref[t * SEQ_TILE:(t + 1) * SEQ_TILE, :] = rows
        hprev.at[cur][t * SEQ_TILE:(t + 1) * SEQ_TILE, :] = rows

    n_chunks = IN_TOTAL // PROJ_COLS
    bounds = [0, 5, 9, n_chunks]

    def filler(stage):
        _project_columns(xb, w_ref, cos_ref[...], sin_ref[...], z_ref, range(bounds[stage], bounds[stage + 1]))

    _combine_rows(step, cur, i == 0, jnp.minimum(i + 1, n_real - 1), i < n_real,
                  row_ref, cnt_ref, yb_ref, rr_ref, loff_ref, h_ref, g_ref, b_ref, ybuf, sem, emit, filler)


def _combine(chunk_row, n_chunks, yb, rr, loff_row, h2, ln_g, ln_b):
    T = h2.shape[0]
    n = DC_SUB * SEQ_TILE
    row = lambda i, r, c: (i, 0)
    grid_spec = pltpu.PrefetchScalarGridSpec(
        num_scalar_prefetch=2,
        grid=(T // n,),
        in_specs=[
            pl.BlockSpec(memory_space=pl.ANY),
            pl.BlockSpec((n, LANES), row),
            pl.BlockSpec((DC_SUB, SUBLANES, LANES), lambda i, r, c: (i, 0, 0)),
            pl.BlockSpec((n, D_MODEL), row),
            pl.BlockSpec((1, D_MODEL), lambda i, r, c: (0, 0)),
            pl.BlockSpec((1, D_MODEL), lambda i, r, c: (0, 0)),
        ],
        out_specs=pl.BlockSpec((n, D_MODEL), row),
        scratch_shapes=[pltpu.VMEM((2 * DC_SUB, LOCAL_ROWS, PACK_COLS), jnp.uint32),
                        pltpu.SemaphoreType.DMA((2 * DC_SUB,))],
    )
    return pl.pallas_call(
        _combine_kernel,
        grid_spec=grid_spec,
        out_shape=jax.ShapeDtypeStruct((T, D_MODEL), F32),
        compiler_params=_params("arbitrary"),
        name="combine",
    )(chunk_row, n_chunks, yb, rr, loff_row, h2, ln_g, ln_b)


def _combine_inproj(chunk_row, n_chunks, yb, rr, loff_row, h2, ln_g, ln_b, w_in_b, cos_t, sin_t):
    T = h2.shape[0]
    n = DC_SUB * SEQ_TILE
    assert n == PROJ_TILE
    steps = T // n
    cur = lambda i, r, c: (jnp.minimum(i, steps - 1), 0)
    prev = lambda i, r, c: (jnp.maximum(i - 1, 0), 0)
    const = lambda i, r, c: (0, 0)
    grid_spec = pltpu.PrefetchScalarGridSpec(
        num_scalar_prefetch=2,
        grid=(steps + 1,),
        in_specs=[
            pl.BlockSpec(memory_space=pl.ANY),
            pl.BlockSpec((n, LANES), cur),
            pl.BlockSpec((DC_SUB, SUBLANES, LANES), lambda i, r, c: (jnp.minimum(i, steps - 1), 0, 0)),
            pl.BlockSpec((n, D_MODEL), cur),
            pl.BlockSpec((1, D_MODEL), const),
            pl.BlockSpec((1, D_MODEL), const),
            pl.BlockSpec((D_MODEL, IN_TOTAL), const, pipeline_mode=pl.Buffered(1)),
            pl.BlockSpec((n, LANES), prev),
            pl.BlockSpec((n, LANES), prev),
        ],
        out_specs=[pl.BlockSpec((n, D_MODEL), cur), pl.BlockSpec((n, IN_TOTAL), prev)],
        scratch_shapes=[pltpu.VMEM((2 * DC_SUB, LOCAL_ROWS, PACK_COLS), jnp.uint32),
                        pltpu.VMEM((2, n, D_MODEL), F32),
                        pltpu.SemaphoreType.DMA((2 * DC_SUB,))],
    )
    return pl.pallas_call(
        _combine_inproj_kernel,
        grid_spec=grid_spec,
        out_shape=[jax.ShapeDtypeStruct((T, D_MODEL), F32), jax.ShapeDtypeStruct((T, IN_TOTAL), MXU_DTYPE)],
        compiler_params=_params("arbitrary"),
        name="combine_inproj",
    )(chunk_row, n_chunks, yb, rr, loff_row, h2, ln_g, ln_b, w_in_b, cos_t, sin_t)


def _router_weights(w_group, b_group, w_route, b_route):
    wr = jnp.zeros((ROUTER_ROWS, D_MODEL), F32)
    wr = wr.at[GROUP_ROW0:GROUP_ROW0 + N_GROUPS].set(w_group.T)
    wr = wr.at[EXPERT_ROW0:EXPERT_ROW0 + N_EXPERTS].set(w_route.T)
    br = jnp.zeros((ROUTER_ROWS,), F32)
    br = br.at[GROUP_ROW0:GROUP_ROW0 + N_GROUPS].set(b_group)
    br = br.at[EXPERT_ROW0:EXPERT_ROW0 + N_EXPERTS].set(b_route.reshape(-1))
    return wr.astype(MXU_DTYPE), br.reshape(ROUTER_ROWS, 1)


def kernel(x, mem, positions, ln_in_g, ln_in_b, w_in, rel_bias, w_proj_ret, w_proj_att, w_out, ln1_g, ln1_b, w_q_mem, w_kv_mem, w_o_mem, ln2_g, ln2_b, w_group, b_group, w_route, b_route, w_gate, w_up, w_down, ln3_g, ln3_b):
    B, S, D = x.shape
    assert D == D_MODEL and S % PROJ_TILE == 0 and S % SEQ_TILE == 0 and S % MEM_TILE == 0
    T = B * S
    A = T * TOP_K
    n_blocks = -(-(A + (T // SEQ_TILE) * N_EXPERTS * (CHUNK_ROWS - 1) + N_EXPERTS * (EXPERT_BLOCK - 1)) // EXPERT_BLOCK)
    bf = MXU_DTYPE

    cos_t, sin_t = _rope_tables(positions)
    tables = _retention_tables()
    pos = np.arange(MEM_TILE)
    tri = jnp.asarray((pos[:, None] < pos[None, :]) & (pos[:, None] // SEQ_TILE == pos[None, :] // SEQ_TILE), bf)
    mem2d = mem.reshape(-1, D)
    row2 = lambda v: v.reshape(1, D)

    z, h = _inproj(x.reshape(T, D), row2(ln_in_g), row2(ln_in_b), _to_mxu(w_in, 0), cos_t, sin_t)
    for l in range(DEPTH):
        bias = _attention_bias(rel_bias[l])
        h = _mixer(z, h, B, S, tables, bias, _to_mxu(w_proj_ret, l), _to_mxu(w_proj_att, l),
                   _to_mxu(w_out, l), row2(ln1_g[l]), row2(ln1_b[l]))

        kv = _mem_kv(mem2d, _to_mxu(w_kv_mem, l))
        wr_t, br_col = _router_weights(w_group[l], b_group[l], w_route[l], b_route[l])
        h2, hb, rl, rr, counts = _memattn(h, kv, B, S, _to_mxu(w_q_mem, l), _to_mxu(w_o_mem, l),
                                          row2(ln2_g[l]), row2(ln2_b[l]), wr_t, br_col, tri)

        chunk_row, n_chunks, tail, block_expert, n_used, loff_col, loff_row = _chunk_plan(counts[:, :, 0], n_blocks)
        xb = _dispatch(chunk_row, n_chunks, tail, hb, rl, loff_col, n_blocks)
        yb = _experts(block_expert, n_used, xb, w_gate, w_up, w_down, l)
        if l + 1 < DEPTH:
            h, z = _combine_inproj(chunk_row, n_chunks, yb, rr, loff_row, h2, row2(ln3_g[l]), row2(ln3_b[l]),
                                   _to_mxu(w_in, l + 1), cos_t, sin_t)
        else:
            h = _combine(chunk_row, n_chunks, yb, rr, loff_row, h2, row2(ln3_g[l]), row2(ln3_b[l]))
    return h.reshape(B, S, D)
```

```python
import functools

import jax
import jax.numpy as jnp
import numpy as np
from jax import lax
from jax.experimental import pallas as pl
from jax.experimental.pallas import tpu as pltpu

D_MODEL = 1024
DEPTH = 2
CHUNK = 64
RET_HEADS = 8
RET_DK = 64
RET_DV = 128
RET_QK = RET_HEADS * RET_DK
RET_V = RET_HEADS * RET_DV
ROPE_BASE = 10000.0
ATT_HEADS = 8
ATT_DH = 64
ATT_W = ATT_HEADS * ATT_DH
LEFT_CHUNKS = 8
MAX_REL = 256
MEM_HEADS = 4
MEM_DH = D_MODEL // MEM_HEADS
N_GROUPS = 4
EXPERTS_PER_GROUP = 8
N_EXPERTS = N_GROUPS * EXPERTS_PER_GROUP
TOP_K = 2
D_EXPERT = 512
LN_EPS = 1e-5
DEEPNORM_ALPHA = (2.0 * DEPTH) ** 0.25
IN_TOTAL = 2 * RET_QK + 2 * RET_V + 3 * ATT_W + 2 * D_MODEL

LANES = 128
SUBLANES = 8
VMEM_LIMIT_BYTES = 56 * 1024 * 1024

SEQ_TILE = 256
MEM_TILE = 4 * SEQ_TILE
MIX_SUB = 2
DC_SUB = 2
PROJ_TILE = 512
PROJ_COLS = 512
EXPERT_BLOCK = 512
HEAD_PAIR = 2 * RET_DK
NEG_BIG = -1e30
LOG2E = 1.4426950408889634

F32 = jnp.float32
MXU_DTYPE = jnp.bfloat16
PACK_COLS = D_MODEL // 2

Z_QK = 0
Z_VR = Z_QK + 2 * RET_QK
Z_GR = Z_VR + RET_V
Z_QA = Z_GR + RET_V
Z_KA = Z_QA + ATT_W
Z_VA = Z_KA + ATT_W
Z_GATE_R = Z_VA + ATT_W
Z_GATE_A = Z_GATE_R + D_MODEL
GATE_HALF = D_MODEL // 2


def _layer_norm(x, g, b):
    mu = jnp.mean(x, axis=-1, keepdims=True)
    xc = x - mu
    var = jnp.mean(xc * xc, axis=-1, keepdims=True)
    return xc * lax.rsqrt(var + LN_EPS) * g + b


def _sigmoid(x):
    return 1.0 / (1.0 + jnp.exp(-x))


def _dot(a, b):
    return jnp.dot(a, b, preferred_element_type=F32)


def _dot_nt(a, b):
    return lax.dot_general(a, b, (((1,), (1,)), ((), ())), preferred_element_type=F32)


def _dot_tn(a, b):
    return lax.dot_general(a, b, (((0,), (0,)), ((), ())), preferred_element_type=F32)


def _params(*semantics):
    return pltpu.CompilerParams(dimension_semantics=semantics, vmem_limit_bytes=VMEM_LIMIT_BYTES)


def _cast_kernel(x_ref, o_ref):
    o_ref[...] = x_ref[...].astype(o_ref.dtype)


def _to_mxu(w_stack, layer):
    _, r, c = w_stack.shape
    tr = 256
    return pl.pallas_call(
        _cast_kernel,
        grid=(r // tr,),
        in_specs=[pl.BlockSpec((None, tr, c), lambda i: (layer, i, 0))],
        out_specs=pl.BlockSpec((tr, c), lambda i: (i, 0)),
        out_shape=jax.ShapeDtypeStruct((r, c), MXU_DTYPE),
        compiler_params=_params("parallel"),
        name="cast_weight",
    )(w_stack)


ROPE_HALF = RET_DK // 2
ROPE_PACK = LANES // ROPE_HALF


def _rope_kernel(pos_ref, invf_ref, cos_ref, sin_ref):
    ang = pos_ref[...] * invf_ref[...]
    cos = jnp.cos(ang)
    sin = jnp.sin(ang)
    rows = ang.shape[0]
    lane = lax.broadcasted_iota(jnp.int32, ang.shape, 1)
    group = lane // ROPE_HALF
    for q in range(ROPE_PACK):
        outs = []
        for x in (cos, sin):
            first = x if q == 0 else pltpu.roll(x, LANES - q * ROPE_HALF, 1)
            rep = first
            for g in range(1, ROPE_PACK):
                rep = jnp.where(group == g, pltpu.roll(first, g * ROPE_HALF, 1), rep)
            outs.append(rep)
        cos_ref[pl.ds(q, rows, stride=ROPE_PACK), :] = outs[0]
        sin_ref[pl.ds(q, rows, stride=ROPE_PACK), :] = jnp.where((lane % RET_DK) < ROPE_HALF, -outs[1], outs[1])


def _rope_tables(positions):
    T = positions.size
    inv_freq = 1.0 / (ROPE_BASE ** jnp.linspace(0.0, 1.0, ROPE_HALF, dtype=F32))
    invf = jnp.tile(inv_freq, ROPE_PACK).reshape(1, LANES)
    packed_rows = T // ROPE_PACK
    pos = jnp.repeat(positions.reshape(packed_rows, ROPE_PACK).astype(F32), ROPE_HALF, axis=1)
    tp = min(1024, packed_rows)
    assert T % ROPE_PACK == 0 and packed_rows % tp == 0
    out_spec = pl.BlockSpec((tp * ROPE_PACK, LANES), lambda i: (i, 0))
    return pl.pallas_call(
        _rope_kernel,
        grid=(packed_rows // tp,),
        in_specs=[pl.BlockSpec((tp, LANES), lambda i: (i, 0)), pl.BlockSpec((1, LANES), lambda i: (0, 0))],
        out_specs=[out_spec, out_spec],
        out_shape=[jax.ShapeDtypeStruct((T, LANES), F32)] * 2,
        compiler_params=_params("parallel"),
        name="rope_tables",
    )(pos, invf)


def _rotary(a, cos, sin_signed):
    outs = []
    lane = lax.broadcasted_iota(jnp.int32, cos.shape, 1)
    first_half = (lane % RET_DK) < RET_DK // 2
    for g in range(a.shape[1] // LANES):
        x = a[:, g * LANES:(g + 1) * LANES]
        rot = jnp.where(first_half, pltpu.roll(x, LANES - RET_DK // 2, 1), pltpu.roll(x, RET_DK // 2, 1))
        outs.append(x * cos + rot * sin_signed)
    return jnp.concatenate(outs, axis=1)


def _project_columns(xb, w_ref, cos, sin, z_ref, chunks):
    for c in chunks:
        lo = c * PROJ_COLS
        acc = _dot(xb, w_ref[:, lo:lo + PROJ_COLS])
        if lo < 2 * RET_QK:
            acc = _rotary(acc, cos, sin)
            if lo >= RET_QK:
                acc = acc * (RET_DK ** -0.5)
        z_ref[:, lo:lo + PROJ_COLS] = acc.astype(z_ref.dtype)


def _inproj_kernel(h_ref, g_ref, b_ref, w_ref, cos_ref, sin_ref, z_ref, h_out):
    x = _layer_norm(h_ref[...], g_ref[...], b_ref[...])
    h_out[...] = x
    _project_columns(x.astype(MXU_DTYPE), w_ref, cos_ref[...], sin_ref[...], z_ref, range(IN_TOTAL // PROJ_COLS))


def _inproj(h, ln_g, ln_b, w_in_b, cos_t, sin_t):
    T = h.shape[0]
    tm = PROJ_TILE
    row = lambda i: (i, 0)
    const = lambda i: (0, 0)
    out_shape = [jax.ShapeDtypeStruct((T, IN_TOTAL), MXU_DTYPE), jax.ShapeDtypeStruct((T, D_MODEL), F32)]
    out_specs = [pl.BlockSpec((tm, IN_TOTAL), row), pl.BlockSpec((tm, D_MODEL), row)]
    res = pl.pallas_call(
        _inproj_kernel,
        grid=(T // tm,),
        in_specs=[
            pl.BlockSpec((tm, D_MODEL), row),
            pl.BlockSpec((1, D_MODEL), const),
            pl.BlockSpec((1, D_MODEL), const),
            pl.BlockSpec((D_MODEL, IN_TOTAL), const, pipeline_mode=pl.Buffered(1)),
            pl.BlockSpec((tm, LANES), row),
            pl.BlockSpec((tm, LANES), row),
        ],
        out_specs=out_specs,
        out_shape=out_shape,
        compiler_params=_params("parallel"),
        name="inproj",
    )(h, ln_g, ln_b, w_in_b, cos_t, sin_t)
    return res


def _retention_tables():
    n = SEQ_TILE
    log_g = np.log1p(-np.exp2(-5.0 - np.arange(RET_HEADS, dtype=np.float64)))
    i = np.arange(n)
    same = (i[:, None] // CHUNK) == (i[None, :] // CHUNK)
    earlier = (i[None, :] // CHUNK) < (i[:, None] // CHUNK)
    diff = i[:, None] - i[None, :]
    expo = np.where(same, np.abs(diff), np.where(earlier, diff, 0)).astype(np.float64)
    decay = np.exp(log_g[:, None, None] * expo) * (same | earlier)
    q_decay = np.exp(log_g[:, None] * (i + 1.0))
    k_decay = np.exp(log_g[:, None] * (n - 1.0 - i))
    tile_decay = np.exp(log_g * n)
    wide = lambda t: np.broadcast_to(t[:, :, None], t.shape + (HEAD_PAIR,))
    owner = (np.arange(HEAD_PAIR)[None, :] // RET_DK) == (np.arange(RET_HEADS)[:, None] % 2)
    k_table = wide(k_decay) * owner[:, None, :]
    return (jnp.asarray(decay, F32), jnp.asarray(wide(q_decay), F32), jnp.asarray(k_table, F32),
            tuple(float(v) for v in tile_decay))


def _attention_bias(rel_bias):
    n = SEQ_TILE
    width = 3 * n
    pad = LEFT_CHUNKS * CHUNK
    assert pad == 2 * n and n <= MAX_REL
    heads = rel_bias.shape[0]
    length = width + n
    near = rel_bias[:, MAX_REL - n:].astype(F32)
    far = jnp.broadcast_to(rel_bias[:, -1:].astype(F32), (heads, length - near.shape[1]))
    g_rev = jnp.concatenate([near, far], axis=1)[:, ::-1]
    rows = jnp.tile(g_rev, (1, n + 1))[:, :n * (length - 1)].reshape(heads, n, length - 1)
    bias = rows[:, :, n - 1:n - 1 + width]
    qi = np.arange(n)
    kj = np.arange(width)
    q_chunk = (pad + qi) // CHUNK
    k_chunk = kj // CHUNK
    in_band = (k_chunk[None, :] <= q_chunk[:, None]) & (k_chunk[None, :] >= q_chunk[:, None] - LEFT_CHUNKS)
    return jnp.where(jnp.asarray(in_band)[None], bias * LOG2E, NEG_BIG)


def _mixer_kernel(tile_decay,
                  qk_ref, vr_ref, gr_ref, gate_r0_ref, gate_r1_ref, gate_a0_ref, gate_a1_ref, qa_ref,
                  kp_ref, kc_ref, vp_ref, vc_ref,
                  h_ref, decay_ref, qdec_ref, kdec_ref, bias_ref,
                  wpr_ref, wpa_ref, wout_ref, g_ref, b_ref,
                  o_ref, state_ref, yr_ref, ya_ref):
    i = pl.program_id(1)

    @pl.when(i == 0)
    def _():
        state_ref[...] = jnp.zeros_like(state_ref)

    n = SEQ_TILE
    lane = lax.broadcasted_iota(jnp.int32, (n, HEAD_PAIR), 1)
    low_half = lane < RET_DK

    def head_lanes(odd):
        return low_half if odd == 0 else jnp.logical_not(low_half)

    def rows(t):
        return slice(t * n, (t + 1) * n)

    def retention_state_update(t, hd):
        p = hd // 2
        k2 = qk_ref[rows(t), RET_QK + p * HEAD_PAIR:RET_QK + (p + 1) * HEAD_PAIR]
        v = vr_ref[rows(t), hd * RET_DV:(hd + 1) * RET_DV]
        k_dec = (k2.astype(F32) * kdec_ref[hd]).astype(MXU_DTYPE)
        state_ref[hd] = state_ref[hd] * tile_decay[hd] + _dot_tn(k_dec, v)

    def retention_first(t, hd):
        p, odd = divmod(hd, 2)
        q2 = qk_ref[rows(t), p * HEAD_PAIR:(p + 1) * HEAD_PAIR]
        k2 = qk_ref[rows(t), RET_QK + p * HEAD_PAIR:RET_QK + (p + 1) * HEAD_PAIR]
        qm = jnp.where(head_lanes(odd), q2, jnp.zeros_like(q2))
        v = vr_ref[rows(t), hd * RET_DV:(hd + 1) * RET_DV]
        raw = _dot_nt(qm, k2)
        read = _dot(qm, state_ref[hd].astype(MXU_DTYPE))
        if t + 1 < MIX_SUB:
            retention_state_update(t, hd)
        return raw, read, v

    def retention_second(t, hd, raw, read, v):
        intra = _dot((raw * decay_ref[hd]).astype(MXU_DTYPE), v)
        ret = intra + qdec_ref[hd] * read
        mu = jnp.mean(ret, axis=-1, keepdims=True)
        rc = ret - mu
        var = jnp.mean(rc * rc, axis=-1, keepdims=True)
        gn = rc * lax.rsqrt(var + LN_EPS)
        gate = gr_ref[rows(t), hd * RET_DV:(hd + 1) * RET_DV].astype(F32)
        yr_ref[rows(t), hd * RET_DV:(hd + 1) * RET_DV] = (gate * _sigmoid(gate) * gn).astype(MXU_DTYPE)

    pen_prev = jnp.where(i >= 1, 0.0, NEG_BIG).astype(F32)

    def key_block(t, kb):
        j = t + kb + (MIX_SUB - 2)
        if j < MIX_SUB:
            return kp_ref, vp_ref, rows(j), pen_prev
        return kc_ref, vc_ref, rows(j - MIX_SUB), None

    def attention_first(t, hd):
        p, odd = divmod(hd, 2)
        cols = slice(p * HEAD_PAIR, (p + 1) * HEAD_PAIR)
        qs = (qa_ref[rows(t), cols].astype(F32) * (ATT_DH ** -0.5 * LOG2E)).astype(MXU_DTYPE)
        qm = jnp.where(head_lanes(odd), qs, jnp.zeros_like(qs))
        s = []
        for kb in range(3):
            k_ref, _, krows, pen = key_block(t, kb)
            sk = _dot_nt(qm, k_ref[krows, cols]) + bias_ref[hd, :, kb * n:(kb + 1) * n]
            if pen is not None:
                sk = sk + pen
            s.append(sk)
        return (s,)

    def attention_second(t, hd, s):
        p, odd = divmod(hd, 2)
        cols = slice(p * HEAD_PAIR, (p + 1) * HEAD_PAIR)
        m = jnp.maximum(jnp.maximum(jnp.max(s[0], axis=-1, keepdims=True),
                                    jnp.max(s[1], axis=-1, keepdims=True)),
                        jnp.max(s[2], axis=-1, keepdims=True))
        e = [jnp.exp2(sk - m) for sk in s]
        denom = (jnp.sum(e[0], axis=-1, keepdims=True) + jnp.sum(e[1], axis=-1, keepdims=True)
                 + jnp.sum(e[2], axis=-1, keepdims=True))
        pv = None
        for kb in range(3):
            _, v_ref, krows, _ = key_block(t, kb)
            term = _dot(e[kb].astype(MXU_DTYPE), v_ref[krows, cols])
            pv = term if pv is None else pv + term
        out = (pv / denom).astype(MXU_DTYPE)
        lo = odd * ATT_DH
        ya_ref[rows(t), p * HEAD_PAIR + lo:p * HEAD_PAIR + lo + ATT_DH] = out[:, lo:lo + ATT_DH]

    work = []
    for t in range(MIX_SUB):
        for hd in range(max(RET_HEADS, ATT_HEADS)):
            if hd < RET_HEADS:
                work.append((retention_first, retention_second, t, hd))
            if hd < ATT_HEADS:
                work.append((attention_first, attention_second, t, hd))
    ahead = 2
    pending = [first(t, hd) for first, _, t, hd in work[:ahead]]
    for j, (_, second, t, hd) in enumerate(work):
        if j + ahead < len(work):
            first, _, t_next, hd_next = work[j + ahead]
            pending.append(first(t_next, hd_next))
        second(t, hd, *pending.pop(0))

    pr = _dot(yr_ref[...], wpr_ref[...])
    pa = _dot(ya_ref[...], wpa_ref[...])
    gate_r = jnp.concatenate([gate_r0_ref[...], gate_r1_ref[...]], axis=1).astype(F32)
    gate_a = jnp.concatenate([gate_a0_ref[...], gate_a1_ref[...]], axis=1).astype(F32)
    merged = _sigmoid(gate_r) * pr + _sigmoid(gate_a) * pa
    mix = _dot(merged.astype(MXU_DTYPE), wout_ref[...])
    for hd in range(RET_HEADS):
        retention_state_update(MIX_SUB - 1, hd)
    o_ref[...] = _layer_norm(DEEPNORM_ALPHA * h_ref[...] + mix, g_ref[...], b_ref[...])


def _mixer(z, h, B, S, tables, bias, wpr, wpa, wout, ln_g, ln_b):
    decay, q_decay, k_decay, tile_decay = tables
    T = B * S
    n = SEQ_TILE
    step_rows = MIX_SUB * n
    nt = S // step_rows
    assert MIX_SUB >= 2 and S % step_rows == 0

    def zspec(width, col_off, back=0):
        cb = col_off // width
        return pl.BlockSpec((step_rows, width), lambda b, i: (b * nt + jnp.maximum(i - back, 0), cb))

    row = lambda b, i: (b * nt + i, 0)
    c2 = lambda b, i: (0, 0)
    c3 = lambda b, i: (0, 0, 0)
    one = pl.Buffered(1)
    in_specs = [
        zspec(2 * RET_QK, Z_QK), zspec(RET_V, Z_VR), zspec(RET_V, Z_GR),
        zspec(GATE_HALF, Z_GATE_R), zspec(GATE_HALF, Z_GATE_R + GATE_HALF),
        zspec(GATE_HALF, Z_GATE_A), zspec(GATE_HALF, Z_GATE_A + GATE_HALF), zspec(ATT_W, Z_QA),
        zspec(ATT_W, Z_KA, 1), zspec(ATT_W, Z_KA, 0),
        zspec(ATT_W, Z_VA, 1), zspec(ATT_W, Z_VA, 0),
        pl.BlockSpec((step_rows, D_MODEL), row),
        pl.BlockSpec((RET_HEADS, n, n), c3, pipeline_mode=one),
        pl.BlockSpec((RET_HEADS, n, HEAD_PAIR), c3, pipeline_mode=one),
        pl.BlockSpec((RET_HEADS, n, HEAD_PAIR), c3, pipeline_mode=one),
        pl.BlockSpec((ATT_HEADS, n, 3 * n), c3, pipeline_mode=one),
        pl.BlockSpec((RET_V, D_MODEL), c2, pipeline_mode=one),
        pl.BlockSpec((ATT_W, D_MODEL), c2, pipeline_mode=one),
        pl.BlockSpec((D_MODEL, D_MODEL), c2, pipeline_mode=one),
        pl.BlockSpec((1, D_MODEL), c2),
        pl.BlockSpec((1, D_MODEL), c2),
    ]
    return pl.pallas_call(
        functools.partial(_mixer_kernel, tile_decay),
        grid=(B, nt),
        in_specs=in_specs,
        out_specs=pl.BlockSpec((step_rows, D_MODEL), row),
        out_shape=jax.ShapeDtypeStruct((T, D_MODEL), F32),
        scratch_shapes=[
            pltpu.VMEM((RET_HEADS, HEAD_PAIR, RET_DV), F32),
            pltpu.VMEM((step_rows, RET_V), MXU_DTYPE),
            pltpu.VMEM((step_rows, ATT_W), MXU_DTYPE),
        ],
        compiler_params=_params("arbitrary", "arbitrary"),
        name="mixer",
    )(*([z] * 12), h, decay, q_decay, k_decay, bias, wpr, wpa, wout, ln_g, ln_b)


def _kv_kernel(m_ref, w_ref, o_ref):
    o_ref[...] = _dot(m_ref[...].astype(MXU_DTYPE), w_ref[...]).astype(o_ref.dtype)


def _mem_kv(mem2d, w_kv):
    M = mem2d.shape[0]
    tm = 256
    return pl.pallas_call(
        _kv_kernel,
        grid=(M // tm,),
        in_specs=[pl.BlockSpec((tm, D_MODEL), lambda i: (i, 0)),
                  pl.BlockSpec((D_MODEL, 2 * D_MODEL), lambda i: (0, 0), pipeline_mode=pl.Buffered(1))],
        out_specs=pl.BlockSpec((tm, 2 * D_MODEL), lambda i: (i, 0)),
        out_shape=jax.ShapeDtypeStruct((M, 2 * D_MODEL), MXU_DTYPE),
        compiler_params=_params("parallel"),
        name="mem_kv",
    )(mem2d, w_kv)


ROUTE_ROWS = 8
GROUP_ROW0 = 0
EXPERT_ROW0 = 8
ROUTER_ROWS = 128


def _memattn_kernel(h_ref, kv_ref, wq_ref, wo_ref, g_ref, b_ref, wr_ref, br_ref, tri_ref,
                    h2_ref, hb_ref, rl_ref, rr_ref, cnt_ref, o_scr):
    n = MEM_TILE
    h1 = h_ref[...]
    q = (_dot(h1.astype(MXU_DTYPE), wq_ref[...]) * (MEM_DH ** -0.5 * LOG2E)).astype(MXU_DTYPE)
    scores = [_dot_nt(q[:, hd * MEM_DH:(hd + 1) * MEM_DH], kv_ref[:, hd * MEM_DH:(hd + 1) * MEM_DH])
              for hd in range(MEM_HEADS)]
    for hd in range(MEM_HEADS):
        cols = slice(hd * MEM_DH, (hd + 1) * MEM_DH)
        s = scores[hd]
        m = jnp.max(s, axis=-1, keepdims=True)
        e = jnp.exp2(s - m)
        denom = jnp.sum(e, axis=-1, keepdims=True)
        v = kv_ref[:, D_MODEL + hd * MEM_DH:D_MODEL + (hd + 1) * MEM_DH]
        o_scr[:, cols] = (_dot(e.astype(MXU_DTYPE), v) / denom).astype(MXU_DTYPE)
    cross = _dot(o_scr[...], wo_ref[...])
    h2 = _layer_norm(DEEPNORM_ALPHA * h1 + cross, g_ref[...], b_ref[...])
    h2_ref[...] = h2
    hb = h2.astype(MXU_DTYPE)
    hb_ref[...] = hb

    logits = _dot_nt(wr_ref[...], hb) + br_ref[...]
    glog = logits[GROUP_ROW0:GROUP_ROW0 + N_GROUPS]
    gmax = jnp.max(glog, axis=0, keepdims=True)
    g_w = 1.0 / jnp.sum(jnp.exp(glog - gmax), axis=0, keepdims=True)
    giota = lax.broadcasted_iota(jnp.int32, glog.shape, 0)
    g_idx = jnp.min(jnp.where(glog == gmax, giota, N_GROUPS), axis=0, keepdims=True)
    el = jnp.zeros((EXPERTS_PER_GROUP, n), F32)
    for g in range(N_GROUPS):
        r0 = EXPERT_ROW0 + g * EXPERTS_PER_GROUP
        el = jnp.where(g_idx == g, logits[r0:r0 + EXPERTS_PER_GROUP], el)
    eiota = lax.broadcasted_iota(jnp.int32, el.shape, 0)
    m1 = jnp.max(el, axis=0, keepdims=True)
    i1 = jnp.min(jnp.where(el == m1, eiota, EXPERTS_PER_GROUP), axis=0, keepdims=True)
    el2 = jnp.where(eiota == i1, -jnp.inf, el)
    m2 = jnp.max(el2, axis=0, keepdims=True)
    i2 = jnp.min(jnp.where(el2 == m2, eiota, EXPERTS_PER_GROUP), axis=0, keepdims=True)
    r = jnp.exp(m2 - m1)
    w1 = g_w / (1.0 + r)
    w2 = g_w * r / (1.0 + r)
    e1 = g_idx * EXPERTS_PER_GROUP + i1
    e2 = g_idx * EXPERTS_PER_GROUP + i2

    xiota = lax.broadcasted_iota(jnp.int32, (N_EXPERTS, n), 0)
    oh1 = xiota == e1
    oh2 = xiota == e2
    cnt = jnp.where(oh1, 1.0, 0.0) + jnp.where(oh2, 1.0, 0.0)
    before = _dot(cnt.astype(MXU_DTYPE), tri_ref[...])
    rank1 = jnp.sum(jnp.where(oh1, before, 0.0), axis=0, keepdims=True)
    rank2 = jnp.sum(jnp.where(oh2, before, 0.0), axis=0, keepdims=True)
    for j in range(n // SEQ_TILE):
        tile_cnt = jnp.sum(cnt[:, j * SEQ_TILE:(j + 1) * SEQ_TILE], axis=1, keepdims=True)
        cnt_ref[j] = jnp.broadcast_to(tile_cnt, (N_EXPERTS, LANES))

    zero = jnp.zeros((1, n), F32)
    rec = jnp.concatenate([e1.astype(F32), e2.astype(F32), w1, w2, rank1, rank2, zero, zero], axis=0)
    rl_ref[...] = rec
    rec_full = jnp.concatenate([rec, jnp.zeros((LANES - ROUTE_ROWS, n), F32)], axis=0)
    rr_ref[...] = rec_full.T


def _memattn(h1, kv, B, S, wq, wo, ln_g, ln_b, wr_t, br_col, tri):
    T = B * S
    n = MEM_TILE
    nt = S // n
    sub = n // SEQ_TILE
    row = lambda b, i: (b * nt + i, 0)
    c2 = lambda b, i: (0, 0)
    one = pl.Buffered(1)
    return pl.pallas_call(
        _memattn_kernel,
        grid=(B, nt),
        in_specs=[
            pl.BlockSpec((n, D_MODEL), row),
            pl.BlockSpec((kv.shape[0] // B, 2 * D_MODEL), lambda b, i: (b, 0)),
            pl.BlockSpec((D_MODEL, D_MODEL), c2, pipeline_mode=one),
            pl.BlockSpec((D_MODEL, D_MODEL), c2, pipeline_mode=one),
            pl.BlockSpec((1, D_MODEL), c2),
            pl.BlockSpec((1, D_MODEL), c2),
            pl.BlockSpec((ROUTER_ROWS, D_MODEL), c2, pipeline_mode=one),
            pl.BlockSpec((ROUTER_ROWS, 1), c2),
            pl.BlockSpec((n, n), c2, pipeline_mode=one),
        ],
        out_specs=[
            pl.BlockSpec((n, D_MODEL), row),
            pl.BlockSpec((n, D_MODEL), row),
            pl.BlockSpec((ROUTE_ROWS, n), lambda b, i: (0, b * nt + i)),
            pl.BlockSpec((n, LANES), row),
            pl.BlockSpec((sub, N_EXPERTS, LANES), lambda b, i: (b * nt + i, 0, 0)),
        ],
        out_shape=[
            jax.ShapeDtypeStruct((T, D_MODEL), F32),
            jax.ShapeDtypeStruct((T, D_MODEL), MXU_DTYPE),
            jax.ShapeDtypeStruct((ROUTE_ROWS, T), F32),
            jax.ShapeDtypeStruct((T, LANES), F32),
            jax.ShapeDtypeStruct((T // SEQ_TILE, N_EXPERTS, LANES), F32),
        ],
        scratch_shapes=[pltpu.VMEM((n, D_MODEL), MXU_DTYPE)],
        compiler_params=_params("parallel", "parallel"),
        name="memattn_router",
    )(h1, kv, wq, wo, ln_g, ln_b, wr_t, br_col, tri)


CHUNK_ROWS = SUBLANES
LOCAL_ROWS = 768
N_CHUNKS = LOCAL_ROWS // CHUNK_ROWS
MIN_CHUNKS = TOP_K * SEQ_TILE // CHUNK_ROWS
assert TOP_K * SEQ_TILE + N_EXPERTS * (CHUNK_ROWS - 1) <= LOCAL_ROWS


def _unpack_rows(packed):
    lo = lax.bitcast_convert_type(packed << 16, F32)
    hi = lax.bitcast_convert_type(packed & jnp.uint32(0xFFFF0000), F32)
    return lo, hi


def _pack_rows(y):
    bits = lax.bitcast_convert_type(y.astype(jnp.bfloat16).astype(F32), jnp.uint32)
    return (bits[:, :PACK_COLS] >> 16) | bits[:, PACK_COLS:]


def _chunk_plan(counts, n_blocks):
    nt = counts.shape[0]
    c = counts.astype(jnp.int32)
    incl = jnp.asarray(np.tril(np.ones((N_EXPERTS, N_EXPERTS), bool)))
    padc = (c + CHUNK_ROWS - 1) // CHUNK_ROWS * CHUNK_ROWS
    lend = jnp.sum(jnp.where(incl[None], padc[:, None, :], 0), axis=2)
    loff = lend - padc
    earlier = jnp.asarray(np.tril(np.ones((nt, nt), bool), -1))
    cbefore = jnp.sum(jnp.where(earlier[:, :, None], padc[None, :, :], 0), axis=1)
    total = jnp.sum(padc, axis=0)
    region = (total + EXPERT_BLOCK - 1) // EXPERT_BLOCK * EXPERT_BLOCK
    gend = jnp.sum(jnp.where(incl, region[None, :], 0), axis=1)
    gstart = gend - region
    delta = gstart[None, :] + cbefore - loff
    k_row = jnp.arange(N_CHUNKS, dtype=jnp.int32) * CHUNK_ROWS
    e_of_chunk = jnp.sum((lend[:, None, :] <= k_row[None, :, None]).astype(jnp.int32), axis=2)
    onehot = e_of_chunk[:, :, None] == jnp.arange(N_EXPERTS, dtype=jnp.int32)[None, None, :]
    chunk_row = (jnp.sum(jnp.where(onehot, delta[:, None, :], 0), axis=2) + k_row[None, :]).reshape(-1)
    n_chunks = lend[:, -1] // CHUNK_ROWS
    block_first = jnp.arange(n_blocks, dtype=jnp.int32) * EXPERT_BLOCK
    block_expert = jnp.minimum(jnp.sum((gend[None, :] <= block_first[:, None]).astype(jnp.int32), axis=1),
                               N_EXPERTS - 1)
    n_used = (gend[-1:] // EXPERT_BLOCK).astype(jnp.int32)
    tail = jnp.concatenate([gstart + total, (region - total) // CHUNK_ROWS, n_used]).astype(jnp.int32)
    loff_f = loff.astype(F32)
    loff_col = jnp.broadcast_to(loff_f[:, :, None], (nt, N_EXPERTS, LANES))
    loff_row = jnp.broadcast_to(jnp.pad(loff_f, ((0, 0), (0, LANES - N_EXPERTS)))[:, None, :],
                                (nt, SUBLANES, LANES))
    return chunk_row, n_chunks, tail, block_expert, n_used, loff_col, loff_row


def _start_chunk_copies(table_ref, count_ref, tile, make_copy):
    base = tile * N_CHUNKS

    def one(k, prio):
        make_copy(k, pl.multiple_of(table_ref[base + k], CHUNK_ROWS)).start(priority=prio)

    def pair(k2, carry):
        for prio in range(2):
            one(2 * k2 + prio, prio)
        return carry

    def single(k, carry):
        one(k, 0)
        return carry

    lax.fori_loop(0, MIN_CHUNKS // 2, pair, 0, unroll=4)
    lax.fori_loop(MIN_CHUNKS, count_ref[tile], single, 0)


def _wait_chunk_copies(count_ref, tile, many, one):
    many.wait()

    def single(k, carry):
        one.wait()
        return carry

    lax.fori_loop(MIN_CHUNKS, count_ref[tile], single, 0)


def _dispatch_kernel(row_ref, cnt_ref, tail_ref, h_ref, rl_ref, loff_ref, xb_ref, sbuf, zbuf, sem, zsem):
    i = pl.program_id(0)
    n = SEQ_TILE
    xiota = lax.broadcasted_iota(jnp.int32, (N_EXPERTS, n), 0)
    riota = lax.broadcasted_iota(jnp.int32, (LOCAL_ROWS, n), 0)
    min_rows = MIN_CHUNKS * CHUNK_ROWS

    def wait_all(slot, tile):
        many = pltpu.make_async_copy(sbuf.at[slot, pl.ds(0, min_rows)], xb_ref.at[pl.ds(0, min_rows)], sem.at[slot])
        one = pltpu.make_async_copy(sbuf.at[slot, pl.ds(0, CHUNK_ROWS)], xb_ref.at[pl.ds(0, CHUNK_ROWS)],
                                    sem.at[slot])
        _wait_chunk_copies(cnt_ref, tile, many, one)

    def sorted_rows(t):
        rl = rl_ref[:, t * n:(t + 1) * n]
        loff = loff_ref[t][:, 0:1]
        slots = []
        for k in range(TOP_K):
            e = rl[k:k + 1].astype(jnp.int32)
            base = jnp.sum(jnp.where(xiota == e, loff, 0.0), axis=0, keepdims=True)
            slots.append((base + rl[4 + k:5 + k]).astype(jnp.int32))
        perm = jnp.where(riota == slots[0], 1.0, jnp.where(riota == slots[1], 1.0, 0.0)).astype(MXU_DTYPE)
        return _pack_rows(_dot(perm, h_ref[t * n:(t + 1) * n, :].astype(MXU_DTYPE)))

    packed = [sorted_rows(t) for t in range(DC_SUB)]
    parity = i % 2
    for t in range(DC_SUB):
        tile = i * DC_SUB + t
        buf = parity * DC_SUB + t

        @pl.when(i >= 2)
        def _():
            wait_all(buf, tile - 2 * DC_SUB)

        sbuf.at[buf][...] = packed[t]

        def chunk_copy(k, row):
            src = sbuf.at[buf, pl.ds(pl.multiple_of(k * CHUNK_ROWS, CHUNK_ROWS), CHUNK_ROWS)]
            return pltpu.make_async_copy(src, xb_ref.at[pl.ds(row, CHUNK_ROWS)], sem.at[buf])

        _start_chunk_copies(row_ref, cnt_ref, tile, chunk_copy)

    @pl.when(i == pl.num_programs(0) - 1)
    def _():
        zbuf[...] = jnp.zeros_like(zbuf)

        def fill(start_copy):
            def per_expert(e, carry):
                first = pl.multiple_of(tail_ref[e], CHUNK_ROWS)

                def per_chunk(c, inner):
                    row = pl.multiple_of(first + c * CHUNK_ROWS, CHUNK_ROWS)
                    copy = pltpu.make_async_copy(zbuf, xb_ref.at[pl.ds(row, CHUNK_ROWS)], zsem)
                    if start_copy:
                        copy.start()
                    else:
                        copy.wait()
                    return inner

                lax.fori_loop(0, tail_ref[N_EXPERTS + e], per_chunk, 0)
                return carry

            lax.fori_loop(0, N_EXPERTS, per_expert, 0)

        fill(True)
        for t in range(DC_SUB):
            wait_all((1 - parity) * DC_SUB + t, (i - 1) * DC_SUB + t)
            wait_all(parity * DC_SUB + t, i * DC_SUB + t)
        fill(False)

        zblock = sbuf.at[0, pl.ds(0, EXPERT_BLOCK)]
        zblock[...] = jnp.zeros_like(zblock)
        n_all = xb_ref.shape[0] // EXPERT_BLOCK

        def fill_blocks(start_copy):
            def per_block(p, carry):
                row = pl.multiple_of(p * EXPERT_BLOCK, EXPERT_BLOCK)
                copy = pltpu.make_async_copy(zblock, xb_ref.at[pl.ds(row, EXPERT_BLOCK)], zsem)
                if start_copy:
                    copy.start()
                else:
                    copy.wait()
                return carry

            lax.fori_loop(tail_ref[2 * N_EXPERTS], n_all, per_block, 0)

        fill_blocks(True)
        fill_blocks(False)


def _dispatch(chunk_row, n_chunks, tail, hb, rl, loff_col, n_blocks):
    T = hb.shape[0]
    n = DC_SUB * SEQ_TILE
    rows = n_blocks * EXPERT_BLOCK
    grid_spec = pltpu.PrefetchScalarGridSpec(
        num_scalar_prefetch=3,
        grid=(T // n,),
        in_specs=[
            pl.BlockSpec((n, D_MODEL), lambda i, r, c, t: (i, 0)),
            pl.BlockSpec((ROUTE_ROWS, n), lambda i, r, c, t: (0, i)),
            pl.BlockSpec((DC_SUB, N_EXPERTS, LANES), lambda i, r, c, t: (i, 0, 0)),
        ],
        out_specs=pl.BlockSpec(memory_space=pl.ANY),
        scratch_shapes=[pltpu.VMEM((2 * DC_SUB, LOCAL_ROWS, PACK_COLS), jnp.uint32),
                        pltpu.VMEM((CHUNK_ROWS, PACK_COLS), jnp.uint32),
                        pltpu.SemaphoreType.DMA((2 * DC_SUB,)), pltpu.SemaphoreType.DMA],
    )
    assert T % n == 0 and T // n >= 2 and LOCAL_ROWS >= EXPERT_BLOCK
    return pl.pallas_call(
        _dispatch_kernel,
        grid_spec=grid_spec,
        out_shape=jax.ShapeDtypeStruct((rows, PACK_COLS), jnp.uint32),
        compiler_params=_params("arbitrary"),
        name="dispatch",
    )(chunk_row, n_chunks, tail, hb, rl, loff_col)


def _expert_kernel(be_ref, nused_ref, x_ref, wg_ref, wu_ref, wd_ref, y_ref, wg_b, wu_b, wd_b):
    p = pl.program_id(0)
    new_expert = jnp.logical_or(p == 0, be_ref[p] != be_ref[jnp.maximum(p - 1, 0)])

    @pl.when(new_expert)
    def _():
        wg_b[...] = wg_ref[...].astype(MXU_DTYPE)
        wu_b[...] = wu_ref[...].astype(MXU_DTYPE)
        wd_b[...] = wd_ref[...].astype(MXU_DTYPE)

    @pl.when(p < nused_ref[0])
    def _():
        lo, hi = _unpack_rows(x_ref[...])
        x = jnp.concatenate([lo.astype(MXU_DTYPE), hi.astype(MXU_DTYPE)], axis=1)
        y = None
        half = D_EXPERT // 2
        for j in range(2):
            cols = slice(j * half, (j + 1) * half)
            gate = _dot(x, wg_b[:, cols])
            up = _dot(x, wu_b[:, cols])
            hid = (gate * _sigmoid(gate) * up).astype(MXU_DTYPE)
            part = _dot(hid, wd_b[cols, :])
            y = part if y is None else y + part
        y_ref[...] = _pack_rows(y)

    @pl.when(p >= nused_ref[0])
    def _():
        y_ref[...] = jnp.zeros_like(y_ref)


def _experts(block_expert, n_used, xb, wg, wu, wd, layer):
    nb = block_expert.shape[0]
    n_slots = nb * EXPERT_BLOCK
    grid_spec = pltpu.PrefetchScalarGridSpec(
        num_scalar_prefetch=2,
        grid=(nb,),
        in_specs=[
            pl.BlockSpec((EXPERT_BLOCK, PACK_COLS), lambda p, be, nu: (jnp.minimum(p, jnp.maximum(nu[0] - 1, 0)), 0)),
            pl.BlockSpec((None, None, D_MODEL, D_EXPERT), lambda p, be, nu: (layer, be[p], 0, 0)),
            pl.BlockSpec((None, None, D_MODEL, D_EXPERT), lambda p, be, nu: (layer, be[p], 0, 0)),
            pl.BlockSpec((None, None, D_EXPERT, D_MODEL), lambda p, be, nu: (layer, be[p], 0, 0)),
        ],
        out_specs=pl.BlockSpec((EXPERT_BLOCK, PACK_COLS), lambda p, be, nu: (p, 0)),
        scratch_shapes=[pltpu.VMEM((D_MODEL, D_EXPERT), MXU_DTYPE), pltpu.VMEM((D_MODEL, D_EXPERT), MXU_DTYPE),
                        pltpu.VMEM((D_EXPERT, D_MODEL), MXU_DTYPE)],
    )
    return pl.pallas_call(
        _expert_kernel,
        grid_spec=grid_spec,
        out_shape=jax.ShapeDtypeStruct((n_slots, PACK_COLS), jnp.uint32),
        compiler_params=_params("arbitrary"),
        name="experts",
    )(block_expert, n_used, xb, wg, wu, wd)


def _combine_rows(step, parity, is_first, next_step, has_next,
                  row_ref, cnt_ref, yb_ref, rr_ref, loff_ref, h_ref, g_ref, b_ref, ybuf, sem, emit,
                  filler=lambda stage: None):
    n = SEQ_TILE
    min_rows = MIN_CHUNKS * CHUNK_ROWS

    def fetch(tile, into):
        def chunk_copy(k, row):
            dst = ybuf.at[into, pl.ds(pl.multiple_of(k * CHUNK_ROWS, CHUNK_ROWS), CHUNK_ROWS)]
            return pltpu.make_async_copy(yb_ref.at[pl.ds(row, CHUNK_ROWS)], dst, sem.at[into])

        _start_chunk_copies(row_ref, cnt_ref, tile, chunk_copy)

    mine = parity * DC_SUB
    theirs = (1 - parity) * DC_SUB

    @pl.when(is_first)
    def _():
        ybuf[...] = jnp.zeros_like(ybuf)
        for t in range(DC_SUB):
            fetch(step * DC_SUB + t, mine + t)

    @pl.when(has_next)
    def _():
        for t in range(DC_SUB):
            fetch(next_step * DC_SUB + t, theirs + t)

    lane = lax.broadcasted_iota(jnp.int32, (n, LANES), 1)
    ciota = lax.broadcasted_iota(jnp.int32, (n, LOCAL_ROWS), 1)

    def weights(t):
        rr = rr_ref[t * n:(t + 1) * n, :]
        loff = loff_ref[t][0:1, :]
        mix = jnp.zeros((n, LOCAL_ROWS), F32)
        for k in range(TOP_K):
            e = rr[:, k:k + 1].astype(jnp.int32)
            base = jnp.sum(jnp.where(lane == e, loff, 0.0), axis=1, keepdims=True)
            sorted_row = (base + rr[:, 4 + k:5 + k]).astype(jnp.int32)
            mix = jnp.where(ciota == sorted_row, rr[:, 2 + k:3 + k], mix)
        return mix.astype(MXU_DTYPE)

    mixes = [weights(t) for t in range(DC_SUB)]
    filler(0)
    for t in range(DC_SUB):
        buf = mine + t
        many = pltpu.make_async_copy(yb_ref.at[pl.ds(0, min_rows)], ybuf.at[buf, pl.ds(0, min_rows)], sem.at[buf])
        one = pltpu.make_async_copy(yb_ref.at[pl.ds(0, CHUNK_ROWS)], ybuf.at[buf, pl.ds(0, CHUNK_ROWS)],
                                    sem.at[buf])
        _wait_chunk_copies(cnt_ref, step * DC_SUB + t, many, one)
        filler(1 + t)
        lo, hi = _unpack_rows(ybuf.at[buf][...])
        y = jnp.concatenate([_dot(mixes[t], lo.astype(MXU_DTYPE)), _dot(mixes[t], hi.astype(MXU_DTYPE))], axis=1)
        rows = slice(t * n, (t + 1) * n)
        emit(t, _layer_norm(DEEPNORM_ALPHA * h_ref[rows, :] + y, g_ref[...], b_ref[...]))


def _combine_kernel(row_ref, cnt_ref, yb_ref, rr_ref, loff_ref, h_ref, g_ref, b_ref, o_ref, ybuf, sem):
    i = pl.program_id(0)

    def emit(t, rows):
        o_ref[t * SEQ_TILE:(t + 1) * SEQ_TILE, :] = rows

    _combine_rows(i, i % 2, i == 0, i + 1, i + 1 < pl.num_programs(0),
                  row_ref, cnt_ref, yb_ref, rr_ref, loff_ref, h_ref, g_ref, b_ref, ybuf, sem, emit)


def _combine_inproj_kernel(row_ref, cnt_ref, yb_ref, rr_ref, loff_ref, h_ref, g_ref, b_ref,
                           w_ref, cos_ref, sin_ref, o_ref, z_ref, ybuf, hprev, sem):
    i = pl.program_id(0)
    n_real = pl.num_programs(0) - 1
    step = jnp.minimum(i, n_real - 1)

    @pl.when(i == 0)
    def _():
        hprev[...] = jnp.zeros_like(hprev)

    cur = i % 2
    xb = hprev.at[1 - cur][...].astype(MXU_DTYPE)

    def emit(t, rows):
        o_ref[t * SEQ_TILE:(t + 1) * SEQ_TILE, :] = rows
        hprev.at[cur][t * SEQ_TILE:(t + 1) * SEQ_TILE, :] = rows

    n_chunks = IN_TOTAL // PROJ_COLS
    bounds = [0, 5, 9, n_chunks]

    def filler(stage):
        _project_columns(xb, w_ref, cos_ref[...], sin_ref[...], z_ref, range(bounds[stage], bounds[stage + 1]))

    _combine_rows(step, cur, i == 0, jnp.minimum(i + 1, n_real - 1), i < n_real,
                  row_ref, cnt_ref, yb_ref, rr_ref, loff_ref, h_ref, g_ref, b_ref, ybuf, sem, emit, filler)


def _combine(chunk_row, n_chunks, yb, rr, loff_row, h2, ln_g, ln_b):
    T = h2.shape[0]
    n = DC_SUB * SEQ_TILE
    row = lambda i, r, c: (i, 0)
    grid_spec = pltpu.PrefetchScalarGridSpec(
        num_scalar_prefetch=2,
        grid=(T // n,),
        in_specs=[
            pl.BlockSpec(memory_space=pl.ANY),
            pl.BlockSpec((n, LANES), row),
            pl.BlockSpec((DC_SUB, SUBLANES, LANES), lambda i, r, c: (i, 0, 0)),
            pl.BlockSpec((n, D_MODEL), row),
            pl.BlockSpec((1, D_MODEL), lambda i, r, c: (0, 0)),
            pl.BlockSpec((1, D_MODEL), lambda i, r, c: (0, 0)),
        ],
        out_specs=pl.BlockSpec((n, D_MODEL), row),
        scratch_shapes=[pltpu.VMEM((2 * DC_SUB, LOCAL_ROWS, PACK_COLS), jnp.uint32),
                        pltpu.SemaphoreType.DMA((2 * DC_SUB,))],
    )
    return pl.pallas_call(
        _combine_kernel,
        grid_spec=grid_spec,
        out_shape=jax.ShapeDtypeStruct((T, D_MODEL), F32),
        compiler_params=_params("arbitrary"),
        name="combine",
    )(chunk_row, n_chunks, yb, rr, loff_row, h2, ln_g, ln_b)


def _combine_inproj(chunk_row, n_chunks, yb, rr, loff_row, h2, ln_g, ln_b, w_in_b, cos_t, sin_t):
    T = h2.shape[0]
    n = DC_SUB * SEQ_TILE
    assert n == PROJ_TILE
    steps = T // n
    cur = lambda i, r, c: (jnp.minimum(i, steps - 1), 0)
    prev = lambda i, r, c: (jnp.maximum(i - 1, 0), 0)
    const = lambda i, r, c: (0, 0)
    grid_spec = pltpu.PrefetchScalarGridSpec(
        num_scalar_prefetch=2,
        grid=(steps + 1,),
        in_specs=[
            pl.BlockSpec(memory_space=pl.ANY),
            pl.BlockSpec((n, LANES), cur),
            pl.BlockSpec((DC_SUB, SUBLANES, LANES), lambda i, r, c: (jnp.minimum(i, steps - 1), 0, 0)),
            pl.BlockSpec((n, D_MODEL), cur),
            pl.BlockSpec((1, D_MODEL), const),
            pl.BlockSpec((1, D_MODEL), const),
            pl.BlockSpec((D_MODEL, IN_TOTAL), const, pipeline_mode=pl.Buffered(1)),
            pl.BlockSpec((n, LANES), prev),
            pl.BlockSpec((n, LANES), prev),
        ],
        out_specs=[pl.BlockSpec((n, D_MODEL), cur), pl.BlockSpec((n, IN_TOTAL), prev)],
        scratch_shapes=[pltpu.VMEM((2 * DC_SUB, LOCAL_ROWS, PACK_COLS), jnp.uint32),
                        pltpu.VMEM((2, n, D_MODEL), F32),
                        pltpu.SemaphoreType.DMA((2 * DC_SUB,))],
    )
    return pl.pallas_call(
        _combine_inproj_kernel,
        grid_spec=grid_spec,
        out_shape=[jax.ShapeDtypeStruct((T, D_MODEL), F32), jax.ShapeDtypeStruct((T, IN_TOTAL), MXU_DTYPE)],
        compiler_params=_params("arbitrary"),
        name="combine_inproj",
    )(chunk_row, n_chunks, yb, rr, loff_row, h2, ln_g, ln_b, w_in_b, cos_t, sin_t)


def _router_weights(w_group, b_group, w_route, b_route):
    wr = jnp.zeros((ROUTER_ROWS, D_MODEL), F32)
    wr = wr.at[GROUP_ROW0:GROUP_ROW0 + N_GROUPS].set(w_group.T)
    wr = wr.at[EXPERT_ROW0:EXPERT_ROW0 + N_EXPERTS].set(w_route.T)
    br = jnp.zeros((ROUTER_ROWS,), F32)
    br = br.at[GROUP_ROW0:GROUP_ROW0 + N_GROUPS].set(b_group)
    br = br.at[EXPERT_ROW0:EXPERT_ROW0 + N_EXPERTS].set(b_route.reshape(-1))
    return wr.astype(MXU_DTYPE), br.reshape(ROUTER_ROWS, 1)


def kernel(x, mem, positions, ln_in_g, ln_in_b, w_in, rel_bias, w_proj_ret, w_proj_att, w_out, ln1_g, ln1_b, w_q_mem, w_kv_mem, w_o_mem, ln2_g, ln2_b, w_group, b_group, w_route, b_route, w_gate, w_up, w_down, ln3_g, ln3_b):
    B, S, D = x.shape
    assert D == D_MODEL and S % PROJ_TILE == 0 and S % SEQ_TILE == 0 and S % MEM_TILE == 0
    T = B * S
    A = T * TOP_K
    n_blocks = -(-(A + (T // SEQ_TILE) * N_EXPERTS * (CHUNK_ROWS - 1) + N_EXPERTS * (EXPERT_BLOCK - 1)) // EXPERT_BLOCK)
    bf = MXU_DTYPE

    cos_t, sin_t = _rope_tables(positions)
    tables = _retention_tables()
    pos = np.arange(MEM_TILE)
    tri = jnp.asarray((pos[:, None] < pos[None, :]) & (pos[:, None] // SEQ_TILE == pos[None, :] // SEQ_TILE), bf)
    mem2d = mem.reshape(-1, D)
    row2 = lambda v: v.reshape(1, D)

    z, h = _inproj(x.reshape(T, D), row2(ln_in_g), row2(ln_in_b), _to_mxu(w_in, 0), cos_t, sin_t)
    for l in range(DEPTH):
        bias = _attention_bias(rel_bias[l])
        h = _mixer(z, h, B, S, tables, bias, _to_mxu(w_proj_ret, l), _to_mxu(w_proj_att, l),
                   _to_mxu(w_out, l), row2(ln1_g[l]), row2(ln1_b[l]))

        kv = _mem_kv(mem2d, _to_mxu(w_kv_mem, l))
        wr_t, br_col = _router_weights(w_group[l], b_group[l], w_route[l], b_route[l])
        h2, hb, rl, rr, counts = _memattn(h, kv, B, S, _to_mxu(w_q_mem, l), _to_mxu(w_o_mem, l),
                                          row2(ln2_g[l]), row2(ln2_b[l]), wr_t, br_col, tri)

        chunk_row, n_chunks, tail, block_expert, n_used, loff_col, loff_row = _chunk_plan(counts[:, :, 0], n_blocks)
        xb = _dispatch(chunk_row, n_chunks, tail, hb, rl, loff_col, n_blocks)
        yb = _experts(block_expert, n_used, xb, w_gate, w_up, w_down, l)
        if l + 1 < DEPTH:
            h, z = _combine_inproj(chunk_row, n_chunks, yb, rr, loff_row, h2, row2(ln3_g[l]), row2(ln3_b[l]),
                                   _to_mxu(w_in, l + 1), cos_t, sin_t)
        else:
            h = _combine(chunk_row, n_chunks, yb, rr, loff_row, h2, row2(ln3_g[l]), row2(ln3_b[l]))
    return h.reshape(B, S, D)
```
